```python
import math
import jax, jax.numpy as jnp
from jax import lax
import numpy as np

D_MODEL = 1024
BATCH = 8
SEQ = 8192
DEPTH = 2

Q_BLOCK = 128
A_HEADS = 4
A_DH = 64
A_DV = 2 * A_DH
B_HEADS = 8
B_DH = 64
B_PATTERNS = ((128, 1), (512, 4), (2048, 16))
C_HEADS = 8
C_Q_RANK = 384
C_KV_RANK = 256
C_NOPE = 64
C_ROPE = 32
C_DV = 64
ROPE_THETA = 10000.0
N_BRANCHES = 3
BRANCH_WIDTH = A_HEADS * A_DV
MEM_TOKENS = 256
X_HEADS = 4
X_DH = 64
N_GROUPS = 4
EXPERTS_PER_GROUP = 8
N_EXPERTS = N_GROUPS * EXPERTS_PER_GROUP
TOP_K = 2
D_EXPERT = 512
EXPERT_BLOCK = 128
NORM_EPS = 1e-6
COL_SIZES = (A_HEADS * 2 * A_DH, A_HEADS * 2 * A_DH, A_HEADS * A_DV,
             B_HEADS * B_DH, B_HEADS * B_DH, B_HEADS * B_DH,
             C_Q_RANK, C_KV_RANK, C_ROPE, N_BRANCHES * D_MODEL)
IN_COLS = sum(COL_SIZES)

kernel_name = 'hybrid_gated_encoder_block'


def _rms_norm(x, g):
    xf = x.astype(jnp.float32)
    y = xf * lax.rsqrt(jnp.mean(xf * xf, axis=-1, keepdims=True) + NORM_EPS)
    return (y * g.astype(jnp.float32)).astype(x.dtype)


def _to_blocks(a):
    b, s = a.shape[:2]
    a = a.reshape((b, s // Q_BLOCK, Q_BLOCK) + a.shape[2:])
    return jnp.moveaxis(a, 1, 0)


def _from_blocks(a):
    a = jnp.moveaxis(a, 0, 1)
    return a.reshape((a.shape[0], a.shape[1] * a.shape[2]) + a.shape[3:])


def _alibi_slopes(n):
    return jnp.exp2(-8.0 * jnp.arange(1, n + 1, dtype=jnp.float32) / n)


def _rope_angles(pos, dim):
    inv = ROPE_THETA ** (-jnp.arange(0, dim, 2, dtype=jnp.float32) / dim)
    return pos.astype(jnp.float32)[..., None] * inv


def _apply_rope(x, ang):
    half = x.shape[-1] // 2
    cos = jnp.cos(ang).astype(x.dtype)
    sin = jnp.sin(ang).astype(x.dtype)
    x1, x2 = x[..., :half], x[..., half:]
    return jnp.concatenate([x1 * cos - x2 * sin, x2 * cos + x1 * sin], axis=-1)


def _split_cols(proj):
    out, start = [], 0
    for n in COL_SIZES:
        out.append(proj[..., start:start + n])
        start += n
    return out


def _diff_attention(q, k, v, pos, lam, slopes):
    scale = A_DH ** -0.5

    def block(args):
        qb, pb = args
        sc = jnp.einsum('bqhcd,bkhcd->bhcqk', qb, k).astype(jnp.float32) * scale
        dist = jnp.abs(pb[:, :, None] - pos[:, None, :]).astype(jnp.float32)
        sc = sc - slopes[None, :, None, None, None] * dist[:, None, None]
        p = jax.nn.softmax(sc, axis=-1)
        attn = p[:, :, 0] - lam * p[:, :, 1]
        return jnp.einsum('bhqk,bkhd->bqhd', attn.astype(v.dtype), v)

    return _from_blocks(lax.map(block, (_to_blocks(q), _to_blocks(pos))))


def _dilated_pattern(q, k, v, slopes, window, dilation):
    b, s, h, dh = q.shape
    half = window // (2 * dilation)
    pad = half * dilation
    kp = jnp.pad(k, ((0, 0), (pad, pad), (0, 0), (0, 0)))
    vp = jnp.pad(v, ((0, 0), (pad, pad), (0, 0), (0, 0)))
    offs = (jnp.arange(2 * half + 1) - half) * dilation
    bias = -slopes[:, None] * jnp.abs(offs).astype(jnp.float32)[None, :]
    scale = dh ** -0.5

    def block(args):
        qb, t0 = args
        kpos = t0 + jnp.arange(Q_BLOCK)[:, None] + offs[None, :]
        valid = (kpos >= 0) & (kpos < s)
        kb = jnp.take(kp, kpos + pad, axis=1)
        vb = jnp.take(vp, kpos + pad, axis=1)
        sc = jnp.einsum('bqhd,bqkhd->bhqk', qb, kb).astype(jnp.float32) * scale
        sc = jnp.where(valid[None, None], sc + bias[None, :, None, :], -jnp.inf)
        lse = jax.nn.logsumexp(sc, axis=-1)
        p = jnp.exp(sc - lse[..., None])
        o = jnp.einsum('bhqk,bqkhd->bqhd', p.astype(v.dtype), vb)
        return o, jnp.swapaxes(lse, 1, 2)

    starts = jnp.arange(s // Q_BLOCK, dtype=jnp.int32) * Q_BLOCK
    o, lse = lax.map(block, (_to_blocks(q), starts))
    return _from_blocks(o), _from_blocks(lse)


def _dilated_attention(q, k, v, slopes):
    outs, lses = [], []
    for window, dilation in B_PATTERNS:
        o, l = _dilated_pattern(q, k, v, slopes, window, dilation)
        outs.append(o)
        lses.append(l)
    w = jax.nn.softmax(jnp.stack(lses), axis=0)
    return jnp.sum(w[..., None].astype(q.dtype) * jnp.stack(outs), axis=0)


def _mla_attention(cq, ckv, ckr, pos, q_norm_g, w_uq, kv_norm_g, w_ukv):
    b, s, _ = cq.shape
    q = (_rms_norm(cq, q_norm_g) @ w_uq).reshape(b, s, C_HEADS, C_NOPE + C_ROPE)
    kv = (_rms_norm(ckv, kv_norm_g) @ w_ukv).reshape(b, s, C_HEADS, C_NOPE + C_DV)
    q_nope, q_rope = q[..., :C_NOPE], q[..., C_NOPE:]
    k_nope, v = kv[..., :C_NOPE], kv[..., C_NOPE:]
    ang = _rope_angles(pos, C_ROPE)
    q_rope = _apply_rope(q_rope, ang[:, :, None, :])
    k_rope = _apply_rope(ckr, ang)
    scale = (C_NOPE + C_ROPE) ** -0.5

    def block(args):
        qn, qr = args
        sc = (jnp.einsum('bqhd,bkhd->bhqk', qn, k_nope)
              + jnp.einsum('bqhr,bkr->bhqk', qr, k_rope)).astype(jnp.float32) * scale
        p = jax.nn.softmax(sc, axis=-1)
        return jnp.einsum('bhqk,bkhd->bqhd', p.astype(v.dtype), v)

    o = _from_blocks(lax.map(block, (_to_blocks(q_nope), _to_blocks(q_rope))))
    return o.reshape(b, s, C_HEADS * C_DV)


def _cross_attention(h, m, w_xq, w_xkv, w_xo):
    b, s, _ = h.shape
    q = (h @ w_xq).reshape(b, s, X_HEADS, X_DH)
    kv = (m @ w_xkv).reshape(b, m.shape[1], 2, X_HEADS, X_DH)
    k, v = kv[:, :, 0], kv[:, :, 1]
    sc = jnp.einsum('bqhd,bkhd->bhqk', q, k).astype(jnp.float32) * (X_DH ** -0.5)
    p = jax.nn.softmax(sc, axis=-1)
    o = jnp.einsum('bhqk,bkhd->bqhd', p.astype(v.dtype), v).reshape(b, s, X_HEADS * X_DH)
    return o @ w_xo


def _hier_moe(h, w_group, b_group, w_router, b_router, w1, w3, w2):
    b, s, d = h.shape
    xt = h.reshape(-1, d)
    t = xt.shape[0]
    g_prob = jax.nn.softmax((xt @ w_group + b_group).astype(jnp.float32), axis=-1)
    p_g, g_sel = lax.top_k(g_prob, 1)
    e_logits = (xt @ w_router + b_router).astype(jnp.float32).reshape(t, N_GROUPS, EXPERTS_PER_GROUP)
    e_in = jnp.take_along_axis(e_logits, g_sel[:, :, None], axis=1)[:, 0]
    top_v, top_i = lax.top_k(e_in, TOP_K)
    gate = p_g * jax.nn.softmax(top_v, axis=-1)
    eid = (g_sel * EXPERTS_PER_GROUP + top_i).reshape(-1)
    tok = jnp.repeat(jnp.arange(t, dtype=jnp.int32), TOP_K)
    wts = gate.reshape(-1)
    n = eid.shape[0]
    order = jnp.argsort(eid)
    se, stok, sw = eid[order], tok[order], wts[order]
    counts = jnp.bincount(eid, length=N_EXPERTS)
    start = jnp.cumsum(counts) - counts
    padded = (counts + EXPERT_BLOCK - 1) // EXPERT_BLOCK * EXPERT_BLOCK
    pend = jnp.cumsum(padded)
    pstart = pend - padded
    dest = pstart[se] + (jnp.arange(n) - start[se])
    n_blocks = (n + EXPERT_BLOCK - 1) // EXPERT_BLOCK + N_EXPERTS
    rows = n_blocks * EXPERT_BLOCK
    row_tok = jnp.zeros((rows,), jnp.int32).at[dest].set(stok)
    row_w = jnp.zeros((rows,), h.dtype).at[dest].set(sw.astype(h.dtype))
    blk_e = jnp.minimum(jnp.searchsorted(pend, jnp.arange(n_blocks) * EXPERT_BLOCK, side='right'),
                        N_EXPERTS - 1)
    xr = xt[row_tok].reshape(n_blocks, EXPERT_BLOCK, d)

    def expert_block(args):
        xb, e = args
        return (jax.nn.silu(xb @ w1[e]) * (xb @ w3[e])) @ w2[e]

    yr = lax.map(expert_block, (xr, blk_e)).reshape(rows, d)
    y = jax.ops.segment_sum(yr * row_w[:, None], row_tok, num_segments=t)
    return y.reshape(b, s, d)


def setup_inputs(seed: int = 0) -> dict:
    key = jax.random.key(seed)
    ks = jax.random.split(key, 32)
    f32 = jnp.float32

    def nrm(k, shape, scale):
        return jax.random.normal(k, shape, f32) * scale

    def gain(k, shape):
        return 1.0 + 0.05 * jax.random.normal(k, shape, f32)

    L, D = DEPTH, D_MODEL
    return {
        'x': nrm(ks[0], (BATCH, SEQ, D), 1.0),
        'mem': nrm(ks[1], (BATCH, MEM_TOKENS, D), 1.0),
        'positions': jnp.broadcast_to(jnp.arange(SEQ, dtype=jnp.int32), (BATCH, SEQ)),
        'mix_norm_g': gain(ks[2], (L, D)),
        'w_in': nrm(ks[3], (L, D, IN_COLS), D ** -0.5),
        'diff_lambda': nrm(ks[4], (L, 4, A_DH), 0.1),
        'diff_subln_g': gain(ks[5], (L, A_DV)),
        'mla_q_norm_g': gain(ks[6], (L, C_Q_RANK)),
        'w_uq': nrm(ks[7], (L, C_Q_RANK, C_HEADS * (C_NOPE + C_ROPE)), C_Q_RANK ** -0.5),
        'mla_kv_norm_g': gain(ks[8], (L, C_KV_RANK)),
        'w_ukv': nrm(ks[9], (L, C_KV_RANK, C_HEADS * (C_NOPE + C_DV)), C_KV_RANK ** -0.5),
        'w_branch': nrm(ks[10], (L, N_BRANCHES, BRANCH_WIDTH, D), BRANCH_WIDTH ** -0.5),
        'w_out': nrm(ks[11], (L, D, D), D ** -0.5),
        'cross_norm_g': gain(ks[12], (L, D)),
        'mem_norm_g': gain(ks[13], (L, D)),
        'w_xq': nrm(ks[14], (L, D, X_HEADS * X_DH), D ** -0.5),
        'w_xkv': nrm(ks[15], (L, D, 2 * X_HEADS * X_DH), D ** -0.5),
        'w_xo': nrm(ks[16], (L, X_HEADS * X_DH, D), (X_HEADS * X_DH) ** -0.5),
        'ffn_norm_g': gain(ks[17], (L, D)),
        'w_group': nrm(ks[18], (L, D, N_GROUPS), D ** -0.5),
        'b_group': nrm(ks[19], (L, N_GROUPS), 0.01),
        'w_router': nrm(ks[20], (L, D, N_EXPERTS), D ** -0.5),
        'b_router': nrm(ks[21], (L, N_EXPERTS), 0.01),
        'w1': nrm(ks[22], (L, N_EXPERTS, D, D_EXPERT), D ** -0.5),
        'w3': nrm(ks[23], (L, N_EXPERTS, D, D_EXPERT), D ** -0.5),
        'w2': nrm(ks[24], (L, N_EXPERTS, D_EXPERT, D), D_EXPERT ** -0.5),
        'final_norm_g': gain(ks[25], (D,)),
    }


def reference(x, mem, positions, mix_norm_g, w_in, diff_lambda, diff_subln_g,
              mla_q_norm_g, w_uq, mla_kv_norm_g, w_ukv, w_branch, w_out,
              cross_norm_g, mem_norm_g, w_xq, w_xkv, w_xo,
              ffn_norm_g, w_group, b_group, w_router, b_router, w1, w3, w2,
              final_norm_g):
    b, s, d = x.shape
    slopes_a = _alibi_slopes(A_HEADS)
    slopes_b = _alibi_slopes(B_HEADS)
    for l in range(DEPTH):
        h = _rms_norm(x, mix_norm_g[l])
        aq, ak, av, bq, bk, bv, cq, ckv, ckr, gates = _split_cols(h @ w_in[l])
        lq = diff_lambda[l].astype(jnp.float32)
        lam_init = 0.8 - 0.6 * math.exp(-0.3 * l)
        lam = jnp.exp(jnp.sum(lq[0] * lq[1])) - jnp.exp(jnp.sum(lq[2] * lq[3])) + lam_init
        oa = _diff_attention(aq.reshape(b, s, A_HEADS, 2, A_DH), ak.reshape(b, s, A_HEADS, 2, A_DH),
                             av.reshape(b, s, A_HEADS, A_DV), positions, lam, slopes_a)
        oa = (_rms_norm(oa, diff_subln_g[l]) * (1.0 - lam_init)).reshape(b, s, BRANCH_WIDTH)
        ob = _dilated_attention(bq.reshape(b, s, B_HEADS, B_DH), bk.reshape(b, s, B_HEADS, B_DH),
                                bv.reshape(b, s, B_HEADS, B_DH), slopes_b).reshape(b, s, BRANCH_WIDTH)
        oc = _mla_attention(cq, ckv, ckr, positions, mla_q_norm_g[l], w_uq[l],
                            mla_kv_norm_g[l], w_ukv[l])
        br = jnp.einsum('bsnc,ncd->bsnd', jnp.stack([oa, ob, oc], axis=2), w_branch[l])
        g = jax.nn.sigmoid(gates.reshape(b, s, N_BRANCHES, d))
        x = x + jnp.sum(g * br, axis=2) @ w_out[l]
        x = x + _cross_attention(_rms_norm(x, cross_norm_g[l]), _rms_norm(mem, mem_norm_g[l]),
                                 w_xq[l], w_xkv[l], w_xo[l])
        x = x + _hier_moe(_rms_norm(x, ffn_norm_g[l]), w_group[l], b_group[l], w_router[l],
                          b_router[l], w1[l], w3[l], w2[l])
    return _rms_norm(x, final_norm_g)
```

```python
import functools
import math

import jax
import jax.numpy as jnp
from jax import lax
from jax.experimental import pallas as pl
from jax.experimental.pallas import tpu as pltpu

F32 = jnp.float32
BF16 = jnp.bfloat16

LANES = 128
NORM_EPS = 1e-6
LOG2E = math.log2(math.e)
NEG_BIG = -1e30

A_HEADS, A_DH = 4, 64
B_HEADS, B_DH = 8, 64
B_PATTERNS = ((128, 1), (512, 4), (2048, 16))
C_HEADS, C_Q_RANK, C_KV_RANK, C_NOPE, C_ROPE, C_DV = 8, 384, 256, 64, 32, 64
ROPE_THETA = 10000.0
N_BRANCHES = 3
X_HEADS, X_DH = 4, 64
N_GROUPS, EXPERTS_PER_GROUP, TOP_K = 4, 8, 2
N_EXPERTS = N_GROUPS * EXPERTS_PER_GROUP

COL_A = 0
COL_B = 12
COL_CQ = 24
COL_CKV = 28
COL_R1 = 30
COL_R2 = 31
COL_G = 32
PROJ_COLS = 56 * LANES

VMEM_LIMIT = 48 * 1024 * 1024


def _tiles(seq):
    return dict(
        tm_proj=min(1024, seq), tn_proj=1024,
        tm_tok=min(512, seq),
        tq_a=min(512, seq), tk_a=min(512, seq),
        t_b=256,
        tq_c=min(512, seq), tk_c=min(512, seq),
        moe_rows=512,
    )


def _cp(sem):
    return pltpu.CompilerParams(dimension_semantics=sem, vmem_limit_bytes=VMEM_LIMIT)


def _rms(x, g, inv_n):
    ms = jnp.sum(x * x, axis=-1, keepdims=True) * inv_n
    return x * lax.rsqrt(ms + NORM_EPS) * g


def _norm_matmul_kernel(x_ref, g_ref, w_ref, o_ref, h_ref, *, inv_n):
    @pl.when(pl.program_id(1) == 0)
    def _():
        h_ref[...] = _rms(x_ref[...].astype(F32), g_ref[...], inv_n).astype(BF16)

    o_ref[...] = jnp.dot(h_ref[...], w_ref[...], preferred_element_type=F32).astype(o_ref.dtype)


def _norm_matmul(x, g, w, tm, tn, out_dtype=BF16):
    m, k = x.shape
    n = w.shape[1]
    return pl.pallas_call(
        functools.partial(_norm_matmul_kernel, inv_n=1.0 / k),
        grid=(m // tm, n // tn),
        in_specs=[pl.BlockSpec((tm, k), lambda i, j: (i, 0)),
                  pl.BlockSpec((1, k), lambda i, j: (0, 0)),
                  pl.BlockSpec((k, tn), lambda i, j: (0, j))],
        out_specs=pl.BlockSpec((tm, tn), lambda i, j: (i, j)),
        out_shape=jax.ShapeDtypeStruct((m, n), out_dtype),
        scratch_shapes=[pltpu.VMEM((tm, k), BF16)],
        compiler_params=_cp(("parallel", "arbitrary")),
        name="norm_matmul",
    )(x, g.reshape(1, k), w)


def _softmax_step(s, v, m_ref, l_ref, acc_ref):
    m_prev = m_ref[...]
    m_new = jnp.maximum(m_prev, jnp.max(s, axis=-1, keepdims=True))
    alpha = jnp.exp2(m_prev - m_new)
    p = jnp.exp2(s - m_new)
    l_ref[...] = alpha * l_ref[...] + jnp.sum(p, axis=-1, keepdims=True)
    acc_ref[...] = alpha * acc_ref[...] + jnp.dot(p.astype(BF16), v, preferred_element_type=F32)
    m_ref[...] = m_new


def _init_state(m_ref, l_ref, acc_ref):
    m_ref[...] = jnp.full(m_ref.shape, NEG_BIG, F32)
    l_ref[...] = jnp.zeros(l_ref.shape, F32)
    acc_ref[...] = jnp.zeros(acc_ref.shape, F32)


def _split_q(q_ref, qs_ref, scale):
    q = q_ref[...].astype(F32) * scale
    lane = lax.broadcasted_iota(jnp.int32, q.shape, 1)
    qs_ref[0] = jnp.where(lane < LANES // 2, q, 0.0).astype(BF16)
    qs_ref[1] = jnp.where(lane >= LANES // 2, q, 0.0).astype(BF16)


def _qk(q, k):
    return lax.dot_general(q, k, (((1,), (1,)), ((), ())), preferred_element_type=F32)


def _state_scratch(tq):
    return [pltpu.VMEM((2, tq, 1), F32), pltpu.VMEM((2, tq, 1), F32), pltpu.VMEM((2, tq, LANES), F32)]


def _attn_a_kernel(lam_ref, slope_ref, q_ref, k_ref, v_ref, pq_ref, pk_ref, g_ref, o_ref,
                   qs_ref, m_ref, l_ref, acc_ref, *, tk, post_scale):
    seq = k_ref.shape[0]
    sl2 = slope_ref[pl.program_id(1)]
    _split_q(q_ref, qs_ref, A_DH ** -0.5 * LOG2E)
    _init_state(m_ref, l_ref, acc_ref)
    pq = pq_ref[...] * sl2

    def body(j, carry):
        ks = pl.multiple_of(j * tk, tk)
        k = k_ref[pl.ds(ks, tk), :]
        v = v_ref[pl.ds(ks, tk), :]
        bias = jnp.abs(pq - pk_ref[:, pl.ds(ks, tk)] * sl2)
        for c in range(2):
            s = _qk(qs_ref[c], k) - bias
            _softmax_step(s, v, m_ref.at[c], l_ref.at[c], acc_ref.at[c])
        return carry

    lax.fori_loop(0, seq // tk, body, 0)
    o = acc_ref[0] / l_ref[0] - lam_ref[0] * (acc_ref[1] / l_ref[1])
    y = _rms(o, g_ref[...], 1.0 / o.shape[-1]) * post_scale
    o_ref[...] = y.astype(o_ref.dtype)


def _attn_a(proj, posq, posk, lam, slopes2, g, post_scale, tq, tk):
    b, s, _ = proj.shape
    smem = pl.BlockSpec(memory_space=pltpu.SMEM)
    return pl.pallas_call(
        functools.partial(_attn_a_kernel, tk=tk, post_scale=post_scale),
        grid=(b, A_HEADS, s // tq),
        in_specs=[smem, smem,
                  pl.BlockSpec((None, tq, LANES), lambda bb, h, i: (bb, i, COL_A + h)),
                  pl.BlockSpec((None, s, LANES), lambda bb, h, i: (bb, 0, COL_A + A_HEADS + h)),
                  pl.BlockSpec((None, s, LANES), lambda bb, h, i: (bb, 0, COL_A + 2 * A_HEADS + h)),
                  pl.BlockSpec((None, tq, 1), lambda bb, h, i: (bb, i, 0)),
                  pl.BlockSpec((None, 1, s), lambda bb, h, i: (bb, 0, 0)),
                  pl.BlockSpec((1, LANES), lambda bb, h, i: (0, 0))],
        out_specs=pl.BlockSpec((None, tq, LANES), lambda bb, h, i: (bb, i, h)),
        out_shape=jax.ShapeDtypeStruct((b, s, A_HEADS * LANES), BF16),
        scratch_shapes=[pltpu.VMEM((2, tq, LANES), BF16)] + _state_scratch(tq),
        compiler_params=_cp(("parallel", "parallel", "arbitrary")),
        name="attn_diff",
    )(lam, slopes2, proj, proj, proj, posq, posk, g.reshape(1, LANES))


def _attn_b_kernel(q_ref, k_ref, v_ref, bias_ref, o_ref, qs_ref, m_ref, l_ref, acc_ref, *, nband):
    tq = q_ref.shape[0]
    nkb = k_ref.shape[0] // tq
    i = pl.program_id(2)
    half = nband // 2
    _split_q(q_ref, qs_ref, B_DH ** -0.5 * LOG2E)
    _init_state(m_ref, l_ref, acc_ref)

    def body(jj, carry):
        ks = pl.multiple_of((i + jj - half) * tq, tq)
        k = k_ref[pl.ds(ks, tq), :]
        v = v_ref[pl.ds(ks, tq), :]
        for c in range(2):
            s = _qk(qs_ref[c], k) + bias_ref[c, jj]
            _softmax_step(s, v, m_ref.at[c], l_ref.at[c], acc_ref.at[c])
        return carry

    lax.fori_loop(jnp.maximum(0, half - i), jnp.minimum(nband, nkb + half - i), body, 0)
    lane = lax.broadcasted_iota(jnp.int32, (tq, LANES), 1)
    o = jnp.where(lane < LANES // 2, acc_ref[0] / l_ref[0], acc_ref[1] / l_ref[1])
    o_ref[...] = o.astype(o_ref.dtype)


def _b_bias_table(t, nband):
    half = nband // 2
    r = jnp.arange(t, dtype=jnp.int32)[None, :, None]
    c = jnp.arange(t, dtype=jnp.int32)[None, None, :]
    jj = jnp.arange(nband, dtype=jnp.int32)[:, None, None]
    ao = jnp.abs((jj - half) * t + c - r)
    mult = jnp.zeros(ao.shape, jnp.int32)
    for window, dilation in B_PATTERNS:
        reach = (window // (2 * dilation)) * dilation
        mult = mult + ((ao % dilation == 0) & (ao <= reach)).astype(jnp.int32)
    slopes = jnp.exp2(-8.0 * jnp.arange(1, B_HEADS + 1, dtype=F32) / B_HEADS)
    bias = jnp.log2(jnp.maximum(mult, 1).astype(F32))[None] - (slopes * LOG2E)[:, None, None, None] * ao.astype(F32)[None]
    bias = jnp.where((mult > 0)[None], bias, NEG_BIG)
    return bias.reshape(B_HEADS // 2, 2, nband, t, t)


def _attn_b(proj, bias, t):
    b, s, _ = proj.shape
    nband = bias.shape[2]
    npair = B_HEADS // 2
    return pl.pallas_call(
        functools.partial(_attn_b_kernel, nband=nband),
        grid=(b, npair, s // t),
        in_specs=[pl.BlockSpec((None, t, LANES), lambda bb, p, i: (bb, i, COL_B + p)),
                  pl.BlockSpec((None, s, LANES), lambda bb, p, i: (bb, 0, COL_B + npair + p)),
                  pl.BlockSpec((None, s, LANES), lambda bb, p, i: (bb, 0, COL_B + 2 * npair + p)),
                  pl.BlockSpec((None, 2, nband, t, t), lambda bb, p, i: (p, 0, 0, 0, 0))],
        out_specs=pl.BlockSpec((None, t, LANES), lambda bb, p, i: (bb, i, p)),
        out_shape=jax.ShapeDtypeStruct((b, s, npair * LANES), BF16),
        scratch_shapes=[pltpu.VMEM((2, t, LANES), BF16)] + _state_scratch(t),
        compiler_params=_cp(("parallel", "parallel", "arbitrary")),
        name="attn_dilated",
    )(proj, proj, proj, bias)


def _rope_table_kernel(pos_ref, invf_ref, c_ref, s_ref):
    ang = pos_ref[...] * invf_ref[...]
    c_ref[...] = jnp.cos(ang)
    s_ref[...] = jnp.sin(ang)


def _rope_tables(pos_col, tm):
    t = pos_col.shape[0]
    inv = ROPE_THETA ** (-jnp.arange(0, C_ROPE, 2, dtype=F32) / C_ROPE)
    invf = jnp.concatenate([jnp.zeros((C_NOPE,), F32), inv, inv,
                            jnp.zeros((LANES - C_NOPE - C_ROPE,), F32)]).reshape(1, LANES)
    spec = pl.BlockSpec((tm, LANES), lambda i: (i, 0))
    return pl.pallas_call(
        _rope_table_kernel,
        grid=(t // tm,),
        in_specs=[pl.BlockSpec((tm, 1), lambda i: (i, 0)), pl.BlockSpec((1, LANES), lambda i: (0, 0))],
        out_specs=[spec, spec],
        out_shape=[jax.ShapeDtypeStruct((t, LANES), F32)] * 2,
        compiler_params=_cp(("parallel",)),
        name="rope_tables",
    )(pos_col, invf)


def _mla_prep_kernel(cq_ref, ckv_ref, r1_ref, r2_ref, c_ref, s_ref, gq_ref, gkv_ref,
                     wq1_ref, wq2_ref, wk_ref, wv_ref, q_out, k_out, v_out):
    qn = _rms(cq_ref[...].astype(F32), gq_ref[...], 1.0 / C_Q_RANK).astype(BF16)
    kvn = _rms(ckv_ref[...].astype(F32), gkv_ref[...], 1.0 / C_KV_RANK).astype(BF16)
    q1 = jnp.dot(qn, wq1_ref[...], preferred_element_type=F32)
    q2 = jnp.dot(qn, wq2_ref[...], preferred_element_type=F32)
    k1 = jnp.dot(kvn, wk_ref[...], preferred_element_type=F32)
    v_out[...] = jnp.dot(kvn, wv_ref[...], preferred_element_type=F32).astype(v_out.dtype)
    cos = c_ref[...]
    sin = s_ref[...]
    k_rope = r1_ref[...].astype(F32) * cos + r2_ref[...].astype(F32) * sin
    scale = (C_NOPE + C_ROPE) ** -0.5 * LOG2E
    for h in range(C_HEADS):
        sl = slice(h * LANES, (h + 1) * LANES)
        q_out[:, sl] = ((q1[:, sl] * cos + q2[:, sl] * sin) * scale).astype(q_out.dtype)
        k_out[:, sl] = (k1[:, sl] + k_rope).astype(k_out.dtype)


def _mla_prep(proj2, cos, sin, gq, gkv, wq1, wq2, wk, wv, tm):
    t = proj2.shape[0]
    full = lambda a: pl.BlockSpec(a.shape, lambda i: (0, 0))
    row = lambda w: pl.BlockSpec((tm, w), lambda i: (i, 0))
    return pl.pallas_call(
        _mla_prep_kernel,
        grid=(t // tm,),
        in_specs=[pl.BlockSpec((tm, 4 * LANES), lambda i: (i, COL_CQ // 4)),
                  pl.BlockSpec((tm, 2 * LANES), lambda i: (i, COL_CKV // 2)),
                  pl.BlockSpec((tm, LANES), lambda i: (i, COL_R1)),
                  pl.BlockSpec((tm, LANES), lambda i: (i, COL_R2)),
                  row(LANES), row(LANES), full(gq), full(gkv), full(wq1), full(wq2), full(wk), full(wv)],
        out_specs=[row(C_HEADS * LANES), row(C_HEADS * LANES), row(C_HEADS * C_DV)],
        out_shape=[jax.ShapeDtypeStruct((t, C_HEADS * LANES), BF16),
                   jax.ShapeDtypeStruct((t, C_HEADS * LANES), BF16),
                   jax.ShapeDtypeStruct((t, C_HEADS * C_DV), BF16)],
        compiler_params=_cp(("parallel",)),
        name="mla_prep",
    )(proj2, proj2, proj2, proj2, cos, sin, gq, gkv, wq1, wq2, wk, wv)


def _attn_c_kernel(q_ref, k_ref, v_ref, o_ref, m_ref, l_ref, acc_ref, *, tk):
    seq = k_ref.shape[0]
    tq = q_ref.shape[0]
    _init_state(m_ref, l_ref, acc_ref)

    def body(j, carry):
        ks = pl.multiple_of(j * tk, tk)
        v = v_ref[pl.ds(ks, tk), :]
        for c in range(2):
            sl = slice(c * LANES, (c + 1) * LANES)
            s = _qk(q_ref[:, sl], k_ref[pl.ds(ks, tk), sl])
            _softmax_step(s, v, m_ref.at[c], l_ref.at[c], acc_ref.at[c])
        return carry

    lax.fori_loop(0, seq // tk, body, 0)
    lane = lax.broadcasted_iota(jnp.int32, (tq, LANES), 1)
    o = jnp.where(lane < LANES // 2, acc_ref[0] / l_ref[0], acc_ref[1] / l_ref[1])
    o_ref[...] = o.astype(o_ref.dtype)


def _attn_c(q, k, v, tq, tk):
    b, s, _ = q.shape
    npair = C_HEADS // 2
    return pl.pallas_call(
        functools.partial(_attn_c_kernel, tk=tk),
        grid=(b, npair, s // tq),
        in_specs=[pl.BlockSpec((None, tq, 2 * LANES), lambda bb, p, i: (bb, i, p)),
                  pl.BlockSpec((None, s, 2 * LANES), lambda bb, p, i: (bb, 0, p)),
                  pl.BlockSpec((None, s, LANES), lambda bb, p, i: (bb, 0, p))],
        out_specs=pl.BlockSpec((None, tq, LANES), lambda bb, p, i: (bb, i, p)),
        out_shape=jax.ShapeDtypeStruct((b, s, npair * LANES), BF16),
        scratch_shapes=_state_scratch(tq),
        compiler_params=_cp(("parallel", "parallel", "arbitrary")),
        name="attn_latent",
    )(q, k, v)


def _merge_kernel(oa_ref, ob_ref, oc_ref, g0_ref, g1_ref, g2_ref, x_ref, wb_ref, wo_ref, o_ref):
    z = None
    for n, (o_r, g_r) in enumerate(((oa_ref, g0_ref), (ob_ref, g1_ref), (oc_ref, g2_ref))):
        br = jnp.dot(o_r[...], wb_ref[n], preferred_element_type=F32)
        gate = 1.0 / (1.0 + jnp.exp(-g_r[...].astype(F32)))
        z = gate * br if z is None else z + gate * br
    o_ref[...] = x_ref[...] + jnp.dot(z.astype(BF16), wo_ref[...], preferred_element_type=F32)


def _merge(oa, ob, oc, proj2, x2, wb, wo, tm):
    t, d = x2.shape
    bw = oa.shape[1]
    row = lambda w: pl.BlockSpec((tm, w), lambda i: (i, 0))
    gate = lambda n: pl.BlockSpec((tm, d), lambda i: (i, COL_G * LANES // d + n))
    return pl.pallas_call(
        _merge_kernel,
        grid=(t // tm,),
        in_specs=[row(bw), row(bw), row(bw), gate(0), gate(1), gate(2), row(d),
                  pl.BlockSpec(wb.shape, lambda i: (0, 0, 0)), pl.BlockSpec(wo.shape, lambda i: (0, 0))],
        out_specs=row(d),
        out_shape=jax.ShapeDtypeStruct((t, d), F32),
        compiler_params=_cp(("parallel",)),
        name="branch_merge",
    )(oa, ob, oc, proj2, proj2, proj2, x2, wb, wo)


def _cross_kernel(x_ref, g_ref, wq_ref, kbd_ref, vbd_ref, wo_ref, o_ref, *, n_mem):
    x = x_ref[...]
    h = _rms(x, g_ref[...], 1.0 / x.shape[-1]).astype(BF16)
    q = (jnp.dot(h, wq_ref[...], preferred_element_type=F32) * (X_DH ** -0.5 * LOG2E)).astype(BF16)
    s = jnp.dot(q, kbd_ref[...], preferred_element_type=F32)
    ps = []
    for hh in range(X_HEADS):
        sh = s[:, hh * n_mem:(hh + 1) * n_mem]
        p = jnp.exp2(sh - jnp.max(sh, axis=-1, keepdims=True))
        ps.append((p / jnp.sum(p, axis=-1, keepdims=True)).astype(BF16))
    o = jnp.dot(jnp.concatenate(ps, axis=1), vbd_ref[...], preferred_element_type=F32)
    o_ref[...] = x + jnp.dot(o.astype(BF16), wo_ref[...], preferred_element_type=F32)


def _cross(x3, g, wq, kbd, vbd, wo, tm):
    b, s, d = x3.shape
    n_mem = kbd.shape[2] // X_HEADS
    full = lambda a: pl.BlockSpec(a.shape, lambda bb, i: (0, 0))
    return pl.pallas_call(
        functools.partial(_cross_kernel, n_mem=n_mem),
        grid=(b, s // tm),
        in_specs=[pl.BlockSpec((None, tm, d), lambda bb, i: (bb, i, 0)), full(g), full(wq),
                  pl.BlockSpec((None,) + kbd.shape[1:], lambda bb, i: (bb, 0, 0)),
                  pl.BlockSpec((None,) + vbd.shape[1:], lambda bb, i: (bb, 0, 0)), full(wo)],
        out_specs=pl.BlockSpec((None, tm, d), lambda bb, i: (bb, i, 0)),
        out_shape=jax.ShapeDtypeStruct((b, s, d), F32),
        compiler_params=_cp(("parallel", "parallel")),
        name="cross_attn",
    )(x3, g, wq, kbd, vbd, wo)


def _block_diag_kv(kv):
    b, m, _ = kv.shape
    kv = kv.reshape(b, m, 2, X_HEADS, X_DH)
    eye = jnp.eye(X_HEADS, dtype=kv.dtype)
    kt = kv[:, :, 0].transpose(0, 2, 3, 1)
    kbd = (kt[:, :, :, None, :] * eye[None, :, None, :, None]).reshape(b, X_HEADS * X_DH, X_HEADS * m)
    vt = kv[:, :, 1].transpose(0, 2, 1, 3)
    vbd = (vt[:, :, :, None, :] * eye[None, :, None, :, None]).reshape(b, X_HEADS * m, X_HEADS * X_DH)
    return kbd, vbd


def _router_kernel(x_ref, g_ref, w_ref, b_ref, h_out, r_out):
    x = x_ref[...]
    h = _rms(x, g_ref[...], 1.0 / x.shape[-1])
    h_out[...] = h.astype(h_out.dtype)
    logits = jnp.dot(h, w_ref[...], preferred_element_type=F32, precision=lax.Precision.HIGHEST) + b_ref[...]
    lane = lax.broadcasted_iota(jnp.int32, logits.shape, 1)
    lane_f = lane.astype(F32)
    big = jnp.float32(4 * LANES)

    def top(vals, mask):
        mv = jnp.max(jnp.where(mask, vals, -jnp.inf), axis=-1, keepdims=True)
        idx = jnp.min(jnp.where(mask & (vals == mv), lane_f, big), axis=-1, keepdims=True)
        return mv, idx

    g_mask = lane < N_GROUPS
    g_max, g_idx = top(logits, g_mask)
    p_g = 1.0 / jnp.sum(jnp.where(g_mask, jnp.exp(logits - g_max), 0.0), axis=-1, keepdims=True)
    first = N_GROUPS + g_idx * EXPERTS_PER_GROUP
    e_mask = (lane_f >= first) & (lane_f < first + EXPERTS_PER_GROUP)
    v0, i0 = top(logits, e_mask)
    v1, i1 = top(logits, e_mask & (lane_f != i0))
    e1 = jnp.exp(v1 - v0)
    w0 = p_g / (1.0 + e1)
    w1 = p_g * e1 / (1.0 + e1)
    out = jnp.where(lane == 0, i0 - N_GROUPS, 0.0)
    out = jnp.where(lane == 1, i1 - N_GROUPS, out)
    out = jnp.where(lane == 2, w0, out)
    out = jnp.where(lane == 3, w1, out)
    r_out[...] = out


def _router(x2, g, w, bias, tm):
    t, d = x2.shape
    full = lambda a: pl.BlockSpec(a.shape, lambda i: (0, 0))
    return pl.pallas_call(
        _router_kernel,
        grid=(t // tm,),
        in_specs=[pl.BlockSpec((tm, d), lambda i: (i, 0)), full(g), full(w), full(bias)],
        out_specs=[pl.BlockSpec((tm, d), lambda i: (i, 0)), pl.BlockSpec((tm, LANES), lambda i: (i, 0))],
        out_shape=[jax.ShapeDtypeStruct((t, d), BF16), jax.ShapeDtypeStruct((t, LANES), F32)],
        compiler_params=_cp(("parallel",)),
        name="moe_router",
    )(x2, g, w, bias)


def _expert_kernel(blk_e_ref, n_used_ref, x_ref, w13_ref, w2_ref, o_ref):
    del blk_e_ref
    used = pl.program_id(0) < n_used_ref[0]

    @pl.when(used)
    def _():
        hid = jnp.dot(x_ref[...], w13_ref[...], preferred_element_type=F32)
        de = hid.shape[1] // 2
        a = hid[:, :de]
        act = (a / (1.0 + jnp.exp(-a))) * hid[:, de:]
        o_ref[...] = jnp.dot(act.astype(BF16), w2_ref[...], preferred_element_type=F32).astype(o_ref.dtype)

    @pl.when(jnp.logical_not(used))
    def _():
        o_ref[...] = jnp.zeros(o_ref.shape, o_ref.dtype)


def _experts(blk_e, n_used, xr, w13, w2, rows_per_block):
    rows, d = xr.shape
    grid_spec = pltpu.PrefetchScalarGridSpec(
        num_scalar_prefetch=2,
        grid=(rows // rows_per_block,),
        in_specs=[pl.BlockSpec((rows_per_block, d), lambda i, be, nu: (i, 0)),
                  pl.BlockSpec((None,) + w13.shape[1:], lambda i, be, nu: (be[i], 0, 0)),
                  pl.BlockSpec((None,) + w2.shape[1:], lambda i, be, nu: (be[i], 0, 0))],
        out_specs=pl.BlockSpec((rows_per_block, d), lambda i, be, nu: (i, 0)),
    )
    return pl.pallas_call(
        _expert_kernel,
        grid_spec=grid_spec,
        out_shape=jax.ShapeDtypeStruct((rows, d), BF16),
        compiler_params=_cp(("arbitrary",)),
        name="moe_experts",
    )(blk_e, n_used, xr, w13, w2)


def _combine_kernel(x_ref, y0_ref, y1_ref, r_ref, g_ref, o_ref, *, final_norm):
    r = r_ref[...]
    y = x_ref[...] + r[:, 2:3] * y0_ref[...].astype(F32) + r[:, 3:4] * y1_ref[...].astype(F32)
    if final_norm:
        y = _rms(y, g_ref[...], 1.0 / y.shape[-1])
    o_ref[...] = y


def _combine(x2, y0, y1, route, g, final_norm, tm):
    t, d = x2.shape
    row = lambda w: pl.BlockSpec((tm, w), lambda i: (i, 0))
    return pl.pallas_call(
        functools.partial(_combine_kernel, final_norm=final_norm),
        grid=(t // tm,),
        in_specs=[row(d), row(d), row(d), row(LANES), pl.BlockSpec((1, d), lambda i: (0, 0))],
        out_specs=row(d),
        out_shape=jax.ShapeDtypeStruct((t, d), F32),
        compiler_params=_cp(("parallel",)),
        name="moe_combine",
    )(x2, y0, y1, route, g)


def _dispatch(route, rows_per_block):
    t = route.shape[0]
    eid = route[:, :TOP_K].astype(jnp.int32).reshape(-1)
    n = eid.shape[0]
    order = jnp.argsort(eid)
    se = eid[order]
    counts = jnp.bincount(eid, length=N_EXPERTS)
    start = jnp.cumsum(counts) - counts
    padded = (counts + rows_per_block - 1) // rows_per_block * rows_per_block
    pend = jnp.cumsum(padded)
    pstart = pend - padded
    dest_sorted = (pstart[se] + (jnp.arange(n) - start[se])).astype(jnp.int32)
    n_blocks = n // rows_per_block + N_EXPERTS
    row_tok = jnp.zeros((n_blocks * rows_per_block,), jnp.int32).at[dest_sorted].set((order // TOP_K).astype(jnp.int32))
    dest = jnp.zeros((n,), jnp.int32).at[order].set(dest_sorted).reshape(t, TOP_K)
    blk_e = jnp.minimum(jnp.searchsorted(pend, jnp.arange(n_blocks) * rows_per_block, side='right'),
                        N_EXPERTS - 1).astype(jnp.int32)
    n_used = (pend[-1] // rows_per_block).astype(jnp.int32).reshape(1)
    return row_tok, dest, blk_e, n_used


def _rot_cols(w):
    half = w.shape[-1] // 2
    return jnp.concatenate([-w[..., half:], w[..., :half]], axis=-1)


def _pack_w_in(w):
    d = w.shape[0]
    n_ab = 6 * A_HEADS * 2 * A_DH
    cq = w[:, n_ab:n_ab + C_Q_RANK]
    ckv = w[:, n_ab + C_Q_RANK:n_ab + C_Q_RANK + C_KV_RANK]
    ckr = w[:, n_ab + C_Q_RANK + C_KV_RANK:n_ab + C_Q_RANK + C_KV_RANK + C_ROPE]
    gates = w[:, n_ab + C_Q_RANK + C_KV_RANK + C_ROPE:]
    z = lambda n: jnp.zeros((d, n), w.dtype)
    tail = LANES - C_NOPE - C_ROPE
    packed = jnp.concatenate([w[:, :n_ab], cq, z(LANES), ckv,
                              z(C_NOPE), ckr, z(tail), z(C_NOPE), _rot_cols(ckr), z(tail), gates], axis=1)
    return packed.astype(BF16)


def _pack_w_uq(w):
    wq = w.reshape(C_Q_RANK, C_HEADS, C_NOPE + C_ROPE)
    pad_rows = 4 * LANES - C_Q_RANK
    tail = LANES - C_NOPE - C_ROPE
    q1 = jnp.pad(wq, ((0, pad_rows), (0, 0), (0, tail))).reshape(4 * LANES, C_HEADS * LANES)
    q2 = jnp.pad(_rot_cols(wq[:, :, C_NOPE:]), ((0, pad_rows), (0, 0), (C_NOPE, tail))).reshape(4 * LANES, C_HEADS * LANES)
    return q1.astype(BF16), q2.astype(BF16)


def _pack_w_ukv(w):
    wkv = w.reshape(C_KV_RANK, C_HEADS, C_NOPE + C_DV)
    wk = jnp.pad(wkv[:, :, :C_NOPE], ((0, 0), (0, 0), (0, LANES - C_NOPE))).reshape(C_KV_RANK, C_HEADS * LANES)
    wv = wkv[:, :, C_NOPE:].reshape(C_KV_RANK, C_HEADS * C_DV)
    return wk.astype(BF16), wv.astype(BF16)


def kernel(x, mem, positions, mix_norm_g, w_in, diff_lambda, diff_subln_g, mla_q_norm_g, w_uq, mla_kv_norm_g, w_ukv, w_branch, w_out, cross_norm_g, mem_norm_g, w_xq, w_xkv, w_xo, ffn_norm_g, w_group, b_group, w_router, b_router, w1, w3, w2, final_norm_g):
    b, s, d = x.shape
    depth = w_in.shape[0]
    t = b * s
    n_mem = mem.shape[1]
    tl = _tiles(s)
    assert PROJ_COLS == COL_G * LANES + N_BRANCHES * d and s % tl["t_b"] == 0

    pos_f = positions.astype(F32)
    posq = pos_f.reshape(b, s, 1)
    posk = pos_f.reshape(b, 1, s)
    cos, sin = _rope_tables(pos_f.reshape(t, 1), tl["tm_tok"])
    slopes_a2 = jnp.exp2(-8.0 * jnp.arange(1, A_HEADS + 1, dtype=F32) / A_HEADS) * LOG2E
    reach = max((w // (2 * dl)) * dl for w, dl in B_PATTERNS)
    nband = 2 * (-(-reach // tl["t_b"])) + 1
    b_bias = _b_bias_table(tl["t_b"], nband)
    mem2 = mem.reshape(b * n_mem, d)

    x2 = x.reshape(t, d)
    for l in range(depth):
        proj2 = _norm_matmul(x2, mix_norm_g[l], _pack_w_in(w_in[l]), tl["tm_proj"], tl["tn_proj"])
        proj3 = proj2.reshape(b, s, PROJ_COLS)
        lq = diff_lambda[l].astype(F32)
        lam_init = 0.8 - 0.6 * math.exp(-0.3 * l)
        lam = (jnp.exp(jnp.sum(lq[0] * lq[1])) - jnp.exp(jnp.sum(lq[2] * lq[3])) + lam_init).reshape(1)
        oa = _attn_a(proj3, posq, posk, lam, slopes_a2, diff_subln_g[l], 1.0 - lam_init, tl["tq_a"], tl["tk_a"])
        ob = _attn_b(proj3, b_bias, tl["t_b"])
        wq1, wq2 = _pack_w_uq(w_uq[l])
        wk, wv = _pack_w_ukv(w_ukv[l])
        gq = jnp.pad(mla_q_norm_g[l], (0, 4 * LANES - C_Q_RANK)).reshape(1, 4 * LANES)
        qc, kc, vc = _mla_prep(proj2, cos, sin, gq, mla_kv_norm_g[l].reshape(1, C_KV_RANK),
                               wq1, wq2, wk, wv, tl["tm_tok"])
        oc = _attn_c(qc.reshape(b, s, -1), kc.reshape(b, s, -1), vc.reshape(b, s, -1), tl["tq_c"], tl["tk_c"])
        x2 = _merge(oa.reshape(t, -1), ob.reshape(t, -1), oc.reshape(t, -1), proj2, x2,
                    w_branch[l].astype(BF16), w_out[l].astype(BF16), tl["tm_tok"])
        kv = _norm_matmul(mem2, mem_norm_g[l], w_xkv[l].astype(BF16), min(1024, b * n_mem), w_xkv.shape[2])
        kbd, vbd = _block_diag_kv(kv.reshape(b, n_mem, -1))
        x2 = _cross(x2.reshape(b, s, d), cross_norm_g[l].reshape(1, d), w_xq[l].astype(BF16), kbd, vbd,
                    w_xo[l].astype(BF16), tl["tm_tok"]).reshape(t, d)
        w_r = jnp.pad(jnp.concatenate([w_group[l], w_router[l]], axis=1), ((0, 0), (0, LANES - N_GROUPS - N_EXPERTS)))
        b_r = jnp.pad(jnp.concatenate([b_group[l], b_router[l]]), (0, LANES - N_GROUPS - N_EXPERTS)).reshape(1, LANES)
        h, route = _router(x2, ffn_norm_g[l].reshape(1, d), w_r, b_r, tl["tm_tok"])
        row_tok, dest, blk_e, n_used = _dispatch(route, tl["moe_rows"])
        w13 = jnp.concatenate([w1[l], w3[l]], axis=2).astype(BF16)
        yr = _experts(blk_e, n_used, h[row_tok], w13, w2[l].astype(BF16), tl["moe_rows"])
        x2 = _combine(x2, yr[dest[:, 0]], yr[dest[:, 1]], route, final_norm_g.reshape(1, d),
                      l == depth - 1, tl["tm_tok"])
    return x2.reshape(b, s, d)
```

```python
import functools
import math

import jax
import jax.numpy as jnp
from jax import lax
from jax.experimental import pallas as pl
from jax.experimental.pallas import tpu as pltpu

F32 = jnp.float32
BF16 = jnp.bfloat16

LANES = 128
NORM_EPS = 1e-6
LOG2E = math.log2(math.e)
NEG_BIG = -1e30

A_HEADS, A_DH = 4, 64
B_HEADS, B_DH = 8, 64
B_PATTERNS = ((128, 1), (512, 4), (2048, 16))
C_HEADS, C_Q_RANK, C_KV_RANK, C_NOPE, C_ROPE, C_DV = 8, 384, 256, 64, 32, 64
ROPE_THETA = 10000.0
N_BRANCHES = 3
X_HEADS, X_DH = 4, 64
N_GROUPS, EXPERTS_PER_GROUP, TOP_K = 4, 8, 2
N_EXPERTS = N_GROUPS * EXPERTS_PER_GROUP

COL_A = 0
COL_B = 12
COL_CQ = 24
COL_CKV = 28
COL_R1 = 30
COL_R2 = 31
COL_G = 32
PROJ_COLS = 56 * LANES

VMEM_LIMIT = 48 * 1024 * 1024


def _tiles(seq):
    return dict(
        tm_proj=min(1024, seq), tn_proj=1024,
        tm_tok=min(512, seq),
        tq_a=min(512, seq), tk_a=min(512, seq),
        t_b=min(512, seq),
        tq_c=min(512, seq), tk_c=min(512, seq),
        moe_rows=512,
    )


def _cp(sem):
    return pltpu.CompilerParams(dimension_semantics=sem, vmem_limit_bytes=VMEM_LIMIT)


def _rms(x, g, inv_n):
    ms = jnp.sum(x * x, axis=-1, keepdims=True) * inv_n
    return x * lax.rsqrt(ms + NORM_EPS) * g


def _norm_matmul_kernel(x_ref, g_ref, w_ref, o_ref, h_ref, *, inv_n):
    @pl.when(pl.program_id(1) == 0)
    def _():
        h_ref[...] = _rms(x_ref[...].astype(F32), g_ref[...], inv_n).astype(BF16)

    o_ref[...] = jnp.dot(h_ref[...], w_ref[...], preferred_element_type=F32).astype(o_ref.dtype)


def _norm_matmul(x, g, w, tm, tn, out_dtype=BF16):
    m, k = x.shape
    n = w.shape[1]
    return pl.pallas_call(
        functools.partial(_norm_matmul_kernel, inv_n=1.0 / k),
        grid=(m // tm, n // tn),
        in_specs=[pl.BlockSpec((tm, k), lambda i, j: (i, 0)),
                  pl.BlockSpec((1, k), lambda i, j: (0, 0)),
                  pl.BlockSpec((k, tn), lambda i, j: (0, j))],
        out_specs=pl.BlockSpec((tm, tn), lambda i, j: (i, j)),
        out_shape=jax.ShapeDtypeStruct((m, n), out_dtype),
        scratch_shapes=[pltpu.VMEM((tm, k), BF16)],
        compiler_params=_cp(("parallel", "arbitrary")),
        name="norm_matmul",
    )(x, g.reshape(1, k), w)


V_ROWS = LANES + 16


def _aug_vt(v3):
    b, s, w = v3.shape
    vt = v3.reshape(b, s, w // LANES, LANES).transpose(0, 2, 3, 1)
    extra = jnp.zeros((b, w // LANES, V_ROWS - LANES, s), v3.dtype).at[:, :, 0, :].set(1.0)
    return jnp.concatenate([vt, extra], axis=2)


def _split_q(q_ref, qs_ref, scale):
    q = q_ref[...].astype(F32) * scale
    lane = lax.broadcasted_iota(jnp.int32, q.shape, 1)
    qs_ref[0] = jnp.where(lane < LANES // 2, q, 0.0).astype(BF16)
    qs_ref[1] = jnp.where(lane >= LANES // 2, q, 0.0).astype(BF16)


def _kq(k, q):
    return lax.dot_general(k, q, (((1,), (1,)), ((), ())), preferred_element_type=F32)


def _put_scores(s, s_ref, cm_ref, slot, c):
    s_ref[slot, c] = s
    cm_ref[slot, c] = jnp.max(s, axis=0, keepdims=True)


def _update(vt, s_ref, cm_ref, m_ref, acc_ref, slot):
    for c in range(2):
        m_prev = m_ref[c]
        m_new = jnp.maximum(m_prev, cm_ref[slot, c])
        alpha = jnp.exp2(m_prev - m_new)
        p = jnp.exp2(s_ref[slot, c] - m_new).astype(BF16)
        acc_ref[c] = alpha * acc_ref[c] + jnp.dot(vt, p, preferred_element_type=F32)
        m_ref[c] = m_new


def _init_state(m_ref, acc_ref):
    m_ref[...] = jnp.full(m_ref.shape, NEG_BIG, F32)
    acc_ref[...] = jnp.zeros(acc_ref.shape, F32)


def _pipelined(n, scores, update):
    scores(0, 0)

    def body(jj, carry):
        j = 2 * jj
        scores(j + 1, 1)
        update(j, 0)
        scores(j + 2, 0)
        update(j + 1, 1)
        return carry

    lax.fori_loop(0, n // 2 - 1, body, 0)
    scores(n - 1, 1)
    update(n - 2, 0)
    update(n - 1, 1)


def _normalized(acc_ref, c):
    return acc_ref[c, :LANES, :] / acc_ref[c, LANES:LANES + 1, :]


def _select_halves(acc_ref):
    row = lax.broadcasted_iota(jnp.int32, (LANES, acc_ref.shape[2]), 0)
    return jnp.where(row < LANES // 2, _normalized(acc_ref, 0), _normalized(acc_ref, 1))


def _attn_scratch(tq, tk):
    return [pltpu.VMEM((2, 1, tq), F32), pltpu.VMEM((2, V_ROWS, tq), F32),
            pltpu.VMEM((2, 2, tk, tq), F32), pltpu.VMEM((2, 2, 1, tq), F32)]


def _attn_a_kernel(lam_ref, slope_ref, q_ref, k_ref, vt_ref, pq_ref, pk_ref, g_ref, o_ref,
                   qs_ref, m_ref, acc_ref, s_ref, cm_ref, *, tk, post_scale):
    tq = q_ref.shape[0]
    sl2 = slope_ref[pl.program_id(1)]
    _split_q(q_ref, qs_ref, A_DH ** -0.5 * LOG2E)
    _init_state(m_ref, acc_ref)
    pq = pq_ref[...] * sl2

    def scores(j, slot):
        ks = pl.multiple_of(j * tk, tk)
        k = k_ref[pl.ds(ks, tk), :]
        pk = pltpu.repeat(pk_ref[pl.ds(ks, tk), :] * sl2, tq // LANES, axis=1)
        bias = jnp.abs(pk - pq)
        for c in range(2):
            _put_scores(_kq(k, qs_ref[c]) - bias, s_ref, cm_ref, slot, c)

    def update(j, slot):
        ks = pl.multiple_of(j * tk, tk)
        _update(vt_ref[:, pl.ds(ks, tk)], s_ref, cm_ref, m_ref, acc_ref, slot)

    _pipelined(k_ref.shape[0] // tk, scores, update)
    o = (_normalized(acc_ref, 0) - lam_ref[0] * _normalized(acc_ref, 1)).T
    o_ref[...] = (_rms(o, g_ref[...], 1.0 / LANES) * post_scale).astype(o_ref.dtype)


def _attn_a(proj, vt, posq_row, posk_rep, lam, slopes2, g, post_scale, tq, tk):
    b, s, _ = proj.shape
    smem = pl.BlockSpec(memory_space=pltpu.SMEM)
    return pl.pallas_call(
        functools.partial(_attn_a_kernel, tk=tk, post_scale=post_scale),
        grid=(b, A_HEADS, s // tq),
        in_specs=[smem, smem,
                  pl.BlockSpec((None, tq, LANES), lambda bb, h, i: (bb, i, COL_A + h)),
                  pl.BlockSpec((None, s, LANES), lambda bb, h, i: (bb, 0, COL_A + A_HEADS + h)),
                  pl.BlockSpec((None, None, V_ROWS, s), lambda bb, h, i: (bb, h, 0, 0)),
                  pl.BlockSpec((None, 1, tq), lambda bb, h, i: (bb, 0, i)),
                  pl.BlockSpec((None, s, LANES), lambda bb, h, i: (bb, 0, 0)),
                  pl.BlockSpec((1, LANES), lambda bb, h, i: (0, 0))],
        out_specs=pl.BlockSpec((None, tq, LANES), lambda bb, h, i: (bb, i, h)),
        out_shape=jax.ShapeDtypeStruct((b, s, A_HEADS * LANES), BF16),
        scratch_shapes=[pltpu.VMEM((2, tq, LANES), BF16)] + _attn_scratch(tq, tk),
        compiler_params=_cp(("parallel", "parallel", "arbitrary")),
        name="attn_diff",
    )(lam, slopes2, proj, proj, vt, posq_row, posk_rep, g.reshape(1, LANES))


def _attn_b_kernel(slope_ref, q_ref, k_ref, vt_ref, lm_ref, ao_ref, o_ref,
                   qs_ref, m_ref, acc_ref, s_ref, cm_ref, *, nband):
    t = q_ref.shape[0]
    nkb = k_ref.shape[0] // t
    p = pl.program_id(1)
    i = pl.program_id(2)
    half = nband // 2
    _split_q(q_ref, qs_ref, B_DH ** -0.5 * LOG2E)
    _init_state(m_ref, acc_ref)

    def start(jj):
        return pl.multiple_of(jnp.clip(i + jj - half, 0, nkb - 1) * t, t)

    def scores(jj, slot):
        kb = i + jj - half
        inside = (kb >= 0) & (kb < nkb)
        k = k_ref[pl.ds(start(jj), t), :]
        for c in range(2):
            slope = jnp.where(inside, slope_ref[2 * p + c], -NEG_BIG)
            _put_scores(_kq(k, qs_ref[c]) + (lm_ref[jj] - slope * ao_ref[jj]), s_ref, cm_ref, slot, c)

    scores(0, 0)
    for jj in range(nband):
        if jj + 1 < nband:
            scores(jj + 1, (jj + 1) % 2)
        _update(vt_ref[:, pl.ds(start(jj), t)], s_ref, cm_ref, m_ref, acc_ref, jj % 2)
    o_ref[...] = _select_halves(acc_ref).T.astype(o_ref.dtype)


def _b_tables(t, nband):
    half = nband // 2
    r = jnp.arange(t, dtype=jnp.int32)[None, :, None]
    c = jnp.arange(t, dtype=jnp.int32)[None, None, :]
    jj = jnp.arange(nband, dtype=jnp.int32)[:, None, None]
    ao = jnp.abs((jj - half) * t + r - c)
    mult = jnp.zeros(ao.shape, jnp.int32)
    for window, dilation in B_PATTERNS:
        reach = (window // (2 * dilation)) * dilation
        mult = mult + ((ao % dilation == 0) & (ao <= reach)).astype(jnp.int32)
    lm = jnp.where(mult > 0, jnp.log2(jnp.maximum(mult, 1).astype(F32)), NEG_BIG)
    return lm, ao.astype(F32)


def _attn_b(proj, vt, lm, ao, slopes2, t):
    b, s, _ = proj.shape
    nband = lm.shape[0]
    npair = B_HEADS // 2
    table = pl.BlockSpec((nband, t, t), lambda bb, p, i: (0, 0, 0))
    return pl.pallas_call(
        functools.partial(_attn_b_kernel, nband=nband),
        grid=(b, npair, s // t),
        in_specs=[pl.BlockSpec(memory_space=pltpu.SMEM),
                  pl.BlockSpec((None, t, LANES), lambda bb, p, i: (bb, i, COL_B + p)),
                  pl.BlockSpec((None, s, LANES), lambda bb, p, i: (bb, 0, COL_B + npair + p)),
                  pl.BlockSpec((None, None, V_ROWS, s), lambda bb, p, i: (bb, p, 0, 0)),
                  table, table],
        out_specs=pl.BlockSpec((None, t, LANES), lambda bb, p, i: (bb, i, p)),
        out_shape=jax.ShapeDtypeStruct((b, s, npair * LANES), BF16),
        scratch_shapes=[pltpu.VMEM((2, t, LANES), BF16)] + _attn_scratch(t, t),
        compiler_params=_cp(("parallel", "parallel", "arbitrary")),
        name="attn_dilated",
    )(slopes2, proj, proj, vt, lm, ao)


def _rope_table_kernel(pos_ref, invf_ref, c_ref, s_ref):
    ang = pos_ref[...] * invf_ref[...]
    c_ref[...] = jnp.cos(ang)
    s_ref[...] = jnp.sin(ang)


def _rope_tables(pos_col, tm):
    t = pos_col.shape[0]
    inv = ROPE_THETA ** (-jnp.arange(0, C_ROPE, 2, dtype=F32) / C_ROPE)
    invf = jnp.concatenate([jnp.zeros((C_NOPE,), F32), inv, inv,
                            jnp.zeros((LANES - C_NOPE - C_ROPE,), F32)]).reshape(1, LANES)
    spec = pl.BlockSpec((tm, LANES), lambda i: (i, 0))
    return pl.pallas_call(
        _rope_table_kernel,
        grid=(t // tm,),
        in_specs=[pl.BlockSpec((tm, 1), lambda i: (i, 0)), pl.BlockSpec((1, LANES), lambda i: (0, 0))],
        out_specs=[spec, spec],
        out_shape=[jax.ShapeDtypeStruct((t, LANES), F32)] * 2,
        compiler_params=_cp(("parallel",)),
        name="rope_tables",
    )(pos_col, invf)


def _mla_prep_kernel(cq_ref, ckv_ref, r1_ref, r2_ref, c_ref, s_ref, gq_ref, gkv_ref,
                     wq1_ref, wq2_ref, wk_ref, wv_ref, q_out, k_out, v_out):
    qn = _rms(cq_ref[...].astype(F32), gq_ref[...], 1.0 / C_Q_RANK).astype(BF16)
    kvn = _rms(ckv_ref[...].astype(F32), gkv_ref[...], 1.0 / C_KV_RANK).astype(BF16)
    q1 = jnp.dot(qn, wq1_ref[...], preferred_element_type=F32)
    q2 = jnp.dot(qn, wq2_ref[...], preferred_element_type=F32)
    k1 = jnp.dot(kvn, wk_ref[...], preferred_element_type=F32)
    v_out[...] = jnp.dot(kvn, wv_ref[...], preferred_element_type=F32).astype(v_out.dtype)
    cos = c_ref[...]
    sin = s_ref[...]
    k_rope = r1_ref[...].astype(F32) * cos + r2_ref[...].astype(F32) * sin
    scale = (C_NOPE + C_ROPE) ** -0.5 * LOG2E
    for h in range(C_HEADS):
        sl = slice(h * LANES, (h + 1) * LANES)
        q_out[:, sl] = ((q1[:, sl] * cos + q2[:, sl] * sin) * scale).astype(q_out.dtype)
        k_out[:, sl] = (k1[:, sl] + k_rope).astype(k_out.dtype)


def _mla_prep(proj2, cos, sin, gq, gkv, wq1, wq2, wk, wv, tm):
    t = proj2.shape[0]
    full = lambda a: pl.BlockSpec(a.shape, lambda i: (0, 0))
    row = lambda w: pl.BlockSpec((tm, w), lambda i: (i, 0))
    return pl.pallas_call(
        _mla_prep_kernel,
        grid=(t // tm,),
        in_specs=[pl.BlockSpec((tm, 4 * LANES), lambda i: (i, COL_CQ // 4)),
                  pl.BlockSpec((tm, 2 * LANES), lambda i: (i, COL_CKV // 2)),
                  pl.BlockSpec((tm, LANES), lambda i: (i, COL_R1)),
                  pl.BlockSpec((tm, LANES), lambda i: (i, COL_R2)),
                  row(LANES), row(LANES), full(gq), full(gkv), full(wq1), full(wq2), full(wk), full(wv)],
        out_specs=[row(C_HEADS * LANES), row(C_HEADS * LANES), row(C_HEADS * C_DV)],
        out_shape=[jax.ShapeDtypeStruct((t, C_HEADS * LANES), BF16),
                   jax.ShapeDtypeStruct((t, C_HEADS * LANES), BF16),
                   jax.ShapeDtypeStruct((t, C_HEADS * C_DV), BF16)],
        compiler_params=_cp(("parallel",)),
        name="mla_prep",
    )(proj2, proj2, proj2, proj2, cos, sin, gq, gkv, wq1, wq2, wk, wv)


def _attn_c_kernel(q_ref, k_ref, vt_ref, o_ref, m_ref, acc_ref, s_ref, cm_ref, *, tk):
    _init_state(m_ref, acc_ref)

    def scores(j, slot):
        ks = pl.multiple_of(j * tk, tk)
        for c in range(2):
            sl = slice(c * LANES, (c + 1) * LANES)
            _put_scores(_kq(k_ref[pl.ds(ks, tk), sl], q_ref[:, sl]), s_ref, cm_ref, slot, c)

    def update(j, slot):
        ks = pl.multiple_of(j * tk, tk)
        _update(vt_ref[:, pl.ds(ks, tk)], s_ref, cm_ref, m_ref, acc_ref, slot)

    _pipelined(k_ref.shape[0] // tk, scores, update)
    o_ref[...] = _select_halves(acc_ref).T.astype(o_ref.dtype)


def _attn_c(q, k, vt, tq, tk):
    b, s, _ = q.shape
    npair = C_HEADS // 2
    return pl.pallas_call(
        functools.partial(_attn_c_kernel, tk=tk),
        grid=(b, npair, s // tq),
        in_specs=[pl.BlockSpec((None, tq, 2 * LANES), lambda bb, p, i: (bb, i, p)),
                  pl.BlockSpec((None, s, 2 * LANES), lambda bb, p, i: (bb, 0, p)),
                  pl.BlockSpec((None, None, V_ROWS, s), lambda bb, p, i: (bb, p, 0, 0))],
        out_specs=pl.BlockSpec((None, tq, LANES), lambda bb, p, i: (bb, i, p)),
        out_shape=jax.ShapeDtypeStruct((b, s, npair * LANES), BF16),
        scratch_shapes=_attn_scratch(tq, tk),
        compiler_params=_cp(("parallel", "parallel", "arbitrary")),
        name="attn_latent",
    )(q, k, vt)


def _merge_kernel(oa_ref, ob_ref, oc_ref, g0_ref, g1_ref, g2_ref, x_ref, wb_ref, wo_ref, o_ref):
    z = None
    for n, (o_r, g_r) in enumerate(((oa_ref, g0_ref), (ob_ref, g1_ref), (oc_ref, g2_ref))):
        br = jnp.dot(o_r[...], wb_ref[n], preferred_element_type=F32)
        gate = 1.0 / (1.0 + jnp.exp(-g_r[...].astype(F32)))
        z = gate * br if z is None else z + gate * br
    o_ref[...] = x_ref[...] + jnp.dot(z.astype(BF16), wo_ref[...], preferred_element_type=F32)


def _merge(oa, ob, oc, proj2, x2, wb, wo, tm):
    t, d = x2.shape
    bw = oa.shape[1]
    row = lambda w: pl.BlockSpec((tm, w), lambda i: (i, 0))
    gate = lambda n: pl.BlockSpec((tm, d), lambda i: (i, COL_G * LANES // d + n))
    return pl.pallas_call(
        _merge_kernel,
        grid=(t // tm,),
        in_specs=[row(bw), row(bw), row(bw), gate(0), gate(1), gate(2), row(d),
                  pl.BlockSpec(wb.shape, lambda i: (0, 0, 0)), pl.BlockSpec(wo.shape, lambda i: (0, 0))],
        out_specs=row(d),
        out_shape=jax.ShapeDtypeStruct((t, d), F32),
        compiler_params=_cp(("parallel",)),
        name="branch_merge",
    )(oa, ob, oc, proj2, proj2, proj2, x2, wb, wo)


def _cross_kernel(x_ref, g_ref, wq_ref, kbd_ref, vbd_ref, wo_ref, o_ref, *, n_mem):
    x = x_ref[...]
    h = _rms(x, g_ref[...], 1.0 / x.shape[-1]).astype(BF16)
    q = (jnp.dot(h, wq_ref[...], preferred_element_type=F32) * (X_DH ** -0.5 * LOG2E)).astype(BF16)
    s = jnp.dot(q, kbd_ref[...], preferred_element_type=F32)
    ps = []
    for hh in range(X_HEADS):
        sh = s[:, hh * n_mem:(hh + 1) * n_mem]
        p = jnp.exp2(sh - jnp.max(sh, axis=-1, keepdims=True))
        ps.append((p / jnp.sum(p, axis=-1, keepdims=True)).astype(BF16))
    o = jnp.dot(jnp.concatenate(ps, axis=1), vbd_ref[...], preferred_element_type=F32)
    o_ref[...] = x + jnp.dot(o.astype(BF16), wo_ref[...], preferred_element_type=F32)


def _cross(x3, g, wq, kbd, vbd, wo, tm):
    b, s, d = x3.shape
    n_mem = kbd.shape[2] // X_HEADS
    full = lambda a: pl.BlockSpec(a.shape, lambda bb, i: (0, 0))
    return pl.pallas_call(
        functools.partial(_cross_kernel, n_mem=n_mem),
        grid=(b, s // tm),
        in_specs=[pl.BlockSpec((None, tm, d), lambda bb, i: (bb, i, 0)), full(g), full(wq),
                  pl.BlockSpec((None,) + kbd.shape[1:], lambda bb, i: (bb, 0, 0)),
                  pl.BlockSpec((None,) + vbd.shape[1:], lambda bb, i: (bb, 0, 0)), full(wo)],
        out_specs=pl.BlockSpec((None, tm, d), lambda bb, i: (bb, i, 0)),
        out_shape=jax.ShapeDtypeStruct((b, s, d), F32),
        compiler_params=_cp(("parallel", "parallel")),
        name="cross_attn",
    )(x3, g, wq, kbd, vbd, wo)


def _block_diag_kv(kv):
    b, m, _ = kv.shape
    kv = kv.reshape(b, m, 2, X_HEADS, X_DH)
    eye = jnp.eye(X_HEADS, dtype=kv.dtype)
    kt = kv[:, :, 0].transpose(0, 2, 3, 1)
    kbd = (kt[:, :, :, None, :] * eye[None, :, None, :, None]).reshape(b, X_HEADS * X_DH, X_HEADS * m)
    vt = kv[:, :, 1].transpose(0, 2, 1, 3)
    vbd = (vt[:, :, :, None, :] * eye[None, :, None, :, None]).reshape(b, X_HEADS * m, X_HEADS * X_DH)
    return kbd, vbd


def _router_kernel(x_ref, g_ref, w_ref, b_ref, h_out, r_out):
    x = x_ref[...]
    h = _rms(x, g_ref[...], 1.0 / x.shape[-1])
    h_out[...] = h.astype(h_out.dtype)
    logits = jnp.dot(h, w_ref[...], preferred_element_type=F32, precision=lax.Precision.HIGHEST) + b_ref[...]
    lane = lax.broadcasted_iota(jnp.int32, logits.shape, 1)
    lane_f = lane.astype(F32)
    big = jnp.float32(4 * LANES)

    def top(vals, mask):
        mv = jnp.max(jnp.where(mask, vals, -jnp.inf), axis=-1, keepdims=True)
        idx = jnp.min(jnp.where(mask & (vals == mv), lane_f, big), axis=-1, keepdims=True)
        return mv, idx

    g_mask = lane < N_GROUPS
    g_max, g_idx = top(logits, g_mask)
    p_g = 1.0 / jnp.sum(jnp.where(g_mask, jnp.exp(logits - g_max), 0.0), axis=-1, keepdims=True)
    first = N_GROUPS + g_idx * EXPERTS_PER_GROUP
    e_mask = (lane_f >= first) & (lane_f < first + EXPERTS_PER_GROUP)
    v0, i0 = top(logits, e_mask)
    v1, i1 = top(logits, e_mask & (lane_f != i0))
    e1 = jnp.exp(v1 - v0)
    w0 = p_g / (1.0 + e1)
    w1 = p_g * e1 / (1.0 + e1)
    out = jnp.where(lane == 0, i0 - N_GROUPS, 0.0)
    out = jnp.where(lane == 1, i1 - N_GROUPS, out)
    out = jnp.where(lane == 2, w0, out)
    out = jnp.where(lane == 3, w1, out)
    r_out[...] = out


def _router(x2, g, w, bias, tm):
    t, d = x2.shape
    full = lambda a: pl.BlockSpec(a.shape, lambda i: (0, 0))
    return pl.pallas_call(
        _router_kernel,
        grid=(t // tm,),
        in_specs=[pl.BlockSpec((tm, d), lambda i: (i, 0)), full(g), full(w), full(bias)],
        out_specs=[pl.BlockSpec((tm, d), lambda i: (i, 0)), pl.BlockSpec((tm, LANES), lambda i: (i, 0))],
        out_shape=[jax.ShapeDtypeStruct((t, d), BF16), jax.ShapeDtypeStruct((t, LANES), F32)],
        compiler_params=_cp(("parallel",)),
        name="moe_router",
    )(x2, g, w, bias)


def _expert_kernel(blk_e_ref, n_used_ref, x_ref, w13_ref, w2_ref, o_ref):
    del blk_e_ref
    used = pl.program_id(0) < n_used_ref[0]

    @pl.when(used)
    def _():
        hid = jnp.dot(x_ref[...], w13_ref[...], preferred_element_type=F32)
        de = hid.shape[1] // 2
        a = hid[:, :de]
        act = (a / (1.0 + jnp.exp(-a))) * hid[:, de:]
        o_ref[...] = jnp.dot(act.astype(BF16), w2_ref[...], preferred_element_type=F32).astype(o_ref.dtype)

    @pl.when(jnp.logical_not(used))
    def _():
        o_ref[...] = jnp.zeros(o_ref.shape, o_ref.dtype)


def _experts(blk_e, n_used, xr, w13, w2, rows_per_block):
    rows, d = xr.shape
    grid_spec = pltpu.PrefetchScalarGridSpec(
        num_scalar_prefetch=2,
        grid=(rows // rows_per_block,),
        in_specs=[pl.BlockSpec((rows_per_block, d), lambda i, be, nu: (i, 0)),
                  pl.BlockSpec((None,) + w13.shape[1:], lambda i, be, nu: (be[i], 0, 0)),
                  pl.BlockSpec((None,) + w2.shape[1:], lambda i, be, nu: (be[i], 0, 0))],
        out_specs=pl.BlockSpec((rows_per_block, d), lambda i, be, nu: (i, 0)),
    )
    return pl.pallas_call(
        _expert_kernel,
        grid_spec=grid_spec,
        out_shape=jax.ShapeDtypeStruct((rows, d), BF16),
        compiler_params=_cp(("arbitrary",)),
        name="moe_experts",
    )(blk_e, n_used, xr, w13, w2)


def _combine_kernel(x_ref, y0_ref, y1_ref, r_ref, g_ref, o_ref, *, final_norm):
    r = r_ref[...]
    y = x_ref[...] + r[:, 2:3] * y0_ref[...].astype(F32) + r[:, 3:4] * y1_ref[...].astype(F32)
    if final_norm:
        y = _rms(y, g_ref[...], 1.0 / y.shape[-1])
    o_ref[...] = y


def _combine(x2, y0, y1, route, g, final_norm, tm):
    t, d = x2.shape
    row = lambda w: pl.BlockSpec((tm, w), lambda i: (i, 0))
    return pl.pallas_call(
        functools.partial(_combine_kernel, final_norm=final_norm),
        grid=(t // tm,),
        in_specs=[row(d), row(d), row(d), row(LANES), pl.BlockSpec((1, d), lambda i: (0, 0))],
        out_specs=row(d),
        out_shape=jax.ShapeDtypeStruct((t, d), F32),
        compiler_params=_cp(("parallel",)),
        name="moe_combine",
    )(x2, y0, y1, route, g)


def _dispatch(route, rows_per_block):
    t = route.shape[0]
    eid = route[:, :TOP_K].astype(jnp.int32).reshape(-1)
    n = eid.shape[0]
    order = jnp.argsort(eid)
    se = eid[order]
    counts = jnp.bincount(eid, length=N_EXPERTS)
    start = jnp.cumsum(counts) - counts
    padded = (counts + rows_per_block - 1) // rows_per_block * rows_per_block
    pend = jnp.cumsum(padded)
    pstart = pend - padded
    dest_sorted = (pstart[se] + (jnp.arange(n) - start[se])).astype(jnp.int32)
    n_blocks = n // rows_per_block + N_EXPERTS
    row_tok = jnp.zeros((n_blocks * rows_per_block,), jnp.int32).at[dest_sorted].set((order // TOP_K).astype(jnp.int32))
    dest = jnp.zeros((n,), jnp.int32).at[order].set(dest_sorted).reshape(t, TOP_K)
    blk_e = jnp.minimum(jnp.searchsorted(pend, jnp.arange(n_blocks) * rows_per_block, side='right'),
                        N_EXPERTS - 1).astype(jnp.int32)
    n_used = (pend[-1] // rows_per_block).astype(jnp.int32).reshape(1)
    return row_tok, dest, blk_e, n_used


def _rot_cols(w):
    half = w.shape[-1] // 2
    return jnp.concatenate([-w[..., half:], w[..., :half]], axis=-1)


def _pack_w_in(w):
    d = w.shape[0]
    n_ab = 6 * A_HEADS * 2 * A_DH
    cq = w[:, n_ab:n_ab + C_Q_RANK]
    ckv = w[:, n_ab + C_Q_RANK:n_ab + C_Q_RANK + C_KV_RANK]
    ckr = w[:, n_ab + C_Q_RANK + C_KV_RANK:n_ab + C_Q_RANK + C_KV_RANK + C_ROPE]
    gates = w[:, n_ab + C_Q_RANK + C_KV_RANK + C_ROPE:]
    z = lambda n: jnp.zeros((d, n), w.dtype)
    tail = LANES - C_NOPE - C_ROPE
    packed = jnp.concatenate([w[:, :n_ab], cq, z(LANES), ckv,
                              z(C_NOPE), ckr, z(tail), z(C_NOPE), _rot_cols(ckr), z(tail), gates], axis=1)
    return packed.astype(BF16)


def _pack_w_uq(w):
    wq = w.reshape(C_Q_RANK, C_HEADS, C_NOPE + C_ROPE)
    pad_rows = 4 * LANES - C_Q_RANK
    tail = LANES - C_NOPE - C_ROPE
    q1 = jnp.pad(wq, ((0, pad_rows), (0, 0), (0, tail))).reshape(4 * LANES, C_HEADS * LANES)
    q2 = jnp.pad(_rot_cols(wq[:, :, C_NOPE:]), ((0, pad_rows), (0, 0), (C_NOPE, tail))).reshape(4 * LANES, C_HEADS * LANES)
    return q1.astype(BF16), q2.astype(BF16)


def _pack_w_ukv(w):
    wkv = w.reshape(C_KV_RANK, C_HEADS, C_NOPE + C_DV)
    wk = jnp.pad(wkv[:, :, :C_NOPE], ((0, 0), (0, 0), (0, LANES - C_NOPE))).reshape(C_KV_RANK, C_HEADS * LANES)
    wv = wkv[:, :, C_NOPE:].reshape(C_KV_RANK, C_HEADS * C_DV)
    return wk.astype(BF16), wv.astype(BF16)


def kernel(x, mem, positions, mix_norm_g, w_in, diff_lambda, diff_subln_g, mla_q_norm_g, w_uq, mla_kv_norm_g, w_ukv, w_branch, w_out, cross_norm_g, mem_norm_g, w_xq, w_xkv, w_xo, ffn_norm_g, w_group, b_group, w_router, b_router, w1, w3, w2, final_norm_g):
    b, s, d = x.shape
    depth = w_in.shape[0]
    t = b * s
    n_mem = mem.shape[1]
    tl = _tiles(s)
    assert PROJ_COLS == COL_G * LANES + N_BRANCHES * d and s % (2 * tl["t_b"]) == 0

    pos_f = positions.astype(F32)
    posq_row = pos_f.reshape(b, 1, s)
    posk_rep = jnp.broadcast_to(pos_f[:, :, None], (b, s, LANES))
    cos, sin = _rope_tables(pos_f.reshape(t, 1), tl["tm_tok"])
    slopes_a2 = jnp.exp2(-8.0 * jnp.arange(1, A_HEADS + 1, dtype=F32) / A_HEADS) * LOG2E
    slopes_b2 = jnp.exp2(-8.0 * jnp.arange(1, B_HEADS + 1, dtype=F32) / B_HEADS) * LOG2E
    reach = max((w // (2 * dl)) * dl for w, dl in B_PATTERNS)
    b_lm, b_ao = _b_tables(tl["t_b"], 2 * (-(-reach // tl["t_b"])) + 1)
    mem2 = mem.reshape(b * n_mem, d)

    x2 = x.reshape(t, d)
    for l in range(depth):
        proj2 = _norm_matmul(x2, mix_norm_g[l], _pack_w_in(w_in[l]), tl["tm_proj"], tl["tn_proj"])
        proj3 = proj2.reshape(b, s, PROJ_COLS)
        lq = diff_lambda[l].astype(F32)
        lam_init = 0.8 - 0.6 * math.exp(-0.3 * l)
        lam = (jnp.exp(jnp.sum(lq[0] * lq[1])) - jnp.exp(jnp.sum(lq[2] * lq[3])) + lam_init).reshape(1)
        vt_ab = _aug_vt(jnp.concatenate([proj3[:, :, (COL_A + 2 * A_HEADS) * LANES:(COL_A + 3 * A_HEADS) * LANES],
                                         proj3[:, :, (COL_B + 2 * A_HEADS) * LANES:(COL_B + 3 * A_HEADS) * LANES]], axis=2))
        oa = _attn_a(proj3, vt_ab[:, :A_HEADS], posq_row, posk_rep, lam, slopes_a2, diff_subln_g[l], 1.0 - lam_init,
                     tl["tq_a"], tl["tk_a"])
        ob = _attn_b(proj3, vt_ab[:, A_HEADS:], b_lm, b_ao, slopes_b2, tl["t_b"])
        wq1, wq2 = _pack_w_uq(w_uq[l])
        wk, wv = _pack_w_ukv(w_ukv[l])
        gq = jnp.pad(mla_q_norm_g[l], (0, 4 * LANES - C_Q_RANK)).reshape(1, 4 * LANES)
        qc, kc, vc = _mla_prep(proj2, cos, sin, gq, mla_kv_norm_g[l].reshape(1, C_KV_RANK),
                               wq1, wq2, wk, wv, tl["tm_tok"])
        oc = _attn_c(qc.reshape(b, s, -1), kc.reshape(b, s, -1), _aug_vt(vc.reshape(b, s, -1)), tl["tq_c"], tl["tk_c"])
        x2 = _merge(oa.reshape(t, -1), ob.reshape(t, -1), oc.reshape(t, -1), proj2, x2,
                    w_branch[l].astype(BF16), w_out[l].astype(BF16), tl["tm_tok"])
        kv = _norm_matmul(mem2, mem_norm_g[l], w_xkv[l].astype(BF16), min(1024, b * n_mem), w_xkv.shape[2])
        kbd, vbd = _block_diag_kv(kv.reshape(b, n_mem, -1))
        x2 = _cross(x2.reshape(b, s, d), cross_norm_g[l].reshape(1, d), w_xq[l].astype(BF16), kbd, vbd,
                    w_xo[l].astype(BF16), tl["tm_tok"]).reshape(t, d)
        w_r = jnp.pad(jnp.concatenate([w_group[l], w_router[l]], axis=1), ((0, 0), (0, LANES - N_GROUPS - N_EXPERTS)))
        b_r = jnp.pad(jnp.concatenate([b_group[l], b_router[l]]), (0, LANES - N_GROUPS - N_EXPERTS)).reshape(1, LANES)
        h, route = _router(x2, ffn_norm_g[l].reshape(1, d), w_r, b_r, tl["tm_tok"])
        row_tok, dest, blk_e, n_used = _dispatch(route, tl["moe_rows"])
        w13 = jnp.concatenate([w1[l], w3[l]], axis=2).astype(BF16)
        yr = _experts(blk_e, n_used, h[row_tok], w13, w2[l].astype(BF16), tl["moe_rows"])
        x2 = _combine(x2, yr[dest[:, 0]], yr[dest[:, 1]], route, final_norm_g.reshape(1, d),
                      l == depth - 1, tl["tm_tok"])
    return x2.reshape(b, s, d)
```

```python
import functools
import math

import jax
import jax.numpy as jnp
from jax import lax
from jax.experimental import pallas as pl
from jax.experimental.pallas import tpu as pltpu

F32 = jnp.float32
BF16 = jnp.bfloat16

LANES = 128
NORM_EPS = 1e-6
LOG2E = math.log2(math.e)
NEG_BIG = -1e30

A_HEADS, A_DH = 4, 64
B_HEADS, B_DH = 8, 64
B_PATTERNS = ((128, 1), (512, 4), (2048, 16))
C_HEADS, C_Q_RANK, C_KV_RANK, C_NOPE, C_ROPE, C_DV = 8, 384, 256, 64, 32, 64
ROPE_THETA = 10000.0
N_BRANCHES = 3
X_HEADS, X_DH = 4, 64
N_GROUPS, EXPERTS_PER_GROUP, TOP_K = 4, 8, 2
N_EXPERTS = N_GROUPS * EXPERTS_PER_GROUP

COL_A = 0
COL_B = 12
COL_CQ = 24
COL_CKV = 28
COL_R1 = 30
COL_R2 = 31
COL_G = 32
PROJ_COLS = 56 * LANES

VMEM_LIMIT = 48 * 1024 * 1024


def _tiles(seq):
    return dict(
        tm_proj=min(1024, seq), tn_proj=1024,
        tm_tok=min(512, seq),
        tq_a=min(512, seq), tk_a=min(512, seq),
        t_b=min(512, seq),
        tq_c=min(512, seq), tk_c=min(512, seq),
        moe_rows=512,
    )


def _cp(sem):
    return pltpu.CompilerParams(dimension_semantics=sem, vmem_limit_bytes=VMEM_LIMIT)


def _rms(x, g, inv_n):
    ms = jnp.sum(x * x, axis=-1, keepdims=True) * inv_n
    return x * lax.rsqrt(ms + NORM_EPS) * g


def _norm_matmul_kernel(x_ref, g_ref, w_ref, o_ref, h_ref, *, inv_n):
    @pl.when(pl.program_id(1) == 0)
    def _():
        h_ref[...] = _rms(x_ref[...].astype(F32), g_ref[...], inv_n).astype(BF16)

    o_ref[...] = jnp.dot(h_ref[...], w_ref[...], preferred_element_type=F32).astype(o_ref.dtype)


def _norm_matmul(x, g, w, tm, tn, out_dtype=BF16):
    m, k = x.shape
    n = w.shape[1]
    return pl.pallas_call(
        functools.partial(_norm_matmul_kernel, inv_n=1.0 / k),
        grid=(m // tm, n // tn),
        in_specs=[pl.BlockSpec((tm, k), lambda i, j: (i, 0)),
                  pl.BlockSpec((1, k), lambda i, j: (0, 0)),
                  pl.BlockSpec((k, tn), lambda i, j: (0, j))],
        out_specs=pl.BlockSpec((tm, tn), lambda i, j: (i, j)),
        out_shape=jax.ShapeDtypeStruct((m, n), out_dtype),
        scratch_shapes=[pltpu.VMEM((tm, k), BF16)],
        compiler_params=_cp(("parallel", "arbitrary")),
        name="norm_matmul",
    )(x, g.reshape(1, k), w)


ONES_ROWS = 16


def _aug_vt(v3, dv):
    b, s, w = v3.shape
    vt = v3.reshape(b, s, w // dv, dv).transpose(0, 2, 3, 1)
    extra = jnp.zeros((b, w // dv, ONES_ROWS, s), v3.dtype).at[:, :, 0, :].set(1.0)
    return jnp.concatenate([vt, extra], axis=2)


def _split_q(q_ref, qs_ref, scale):
    q = q_ref[...].astype(F32) * scale
    lane = lax.broadcasted_iota(jnp.int32, q.shape, 1)
    qs_ref[0] = jnp.where(lane < LANES // 2, q, 0.0).astype(BF16)
    qs_ref[1] = jnp.where(lane >= LANES // 2, q, 0.0).astype(BF16)


def _kq(k, q):
    return lax.dot_general(k, q, (((1,), (1,)), ((), ())), preferred_element_type=F32)


def _put_scores(s, s_ref, cm_ref, slot, c):
    s_ref[slot, c] = s
    cm_ref[slot, c] = jnp.max(s, axis=0, keepdims=True)


def _update(vt_of, s_ref, cm_ref, m_ref, acc_ref, slot):
    for c in range(2):
        m_prev = m_ref[c]
        m_new = jnp.maximum(m_prev, cm_ref[slot, c])
        alpha = jnp.exp2(m_prev - m_new)
        p = jnp.exp2(s_ref[slot, c] - m_new).astype(BF16)
        acc_ref[c] = alpha * acc_ref[c] + jnp.dot(vt_of(c), p, preferred_element_type=F32)
        m_ref[c] = m_new


def _init_state(m_ref, acc_ref):
    m_ref[...] = jnp.full(m_ref.shape, NEG_BIG, F32)
    acc_ref[...] = jnp.zeros(acc_ref.shape, F32)


def _pipelined(n, scores, update):
    scores(0, 0)

    def body(jj, carry):
        j = 2 * jj
        scores(j + 1, 1)
        update(j, 0)
        scores(j + 2, 0)
        update(j + 1, 1)
        return carry

    lax.fori_loop(0, (n - 1) // 2, body, 0)
    if n % 2 == 0:
        scores(n - 1, 1)
        update(n - 2, 0)
        update(n - 1, 1)
    else:
        update(n - 1, 0)


def _normalized(acc_ref, c):
    dv = acc_ref.shape[1] - ONES_ROWS
    return acc_ref[c, :dv, :] / acc_ref[c, dv:dv + 1, :]


def _attn_scratch(tq, tk, dv):
    return [pltpu.VMEM((2, 1, tq), F32), pltpu.VMEM((2, dv + ONES_ROWS, tq), F32),
            pltpu.VMEM((2, 2, tk, tq), F32), pltpu.VMEM((2, 2, 1, tq), F32)]


def _attn_a_kernel(lam_ref, slope_ref, q_ref, k_ref, vt_ref, pq_ref, pk_ref, g_ref, o_ref,
                   qs_ref, m_ref, acc_ref, s_ref, cm_ref, *, tk, post_scale):
    tq = q_ref.shape[0]
    sl2 = slope_ref[pl.program_id(1)]
    _split_q(q_ref, qs_ref, A_DH ** -0.5 * LOG2E)
    _init_state(m_ref, acc_ref)
    pq = pq_ref[...] * sl2

    def scores(j, slot):
        ks = pl.multiple_of(j * tk, tk)
        k = k_ref[pl.ds(ks, tk), :]
        pk = pk_ref[pl.ds(ks, tk), :] * sl2
        bias = jnp.abs(jnp.concatenate([pk] * (tq // LANES), axis=1) - pq)
        for c in range(2):
            _put_scores(_kq(k, qs_ref[c]) - bias, s_ref, cm_ref, slot, c)

    def update(j, slot):
        ks = pl.multiple_of(j * tk, tk)
        _update(lambda c: vt_ref[:, pl.ds(ks, tk)], s_ref, cm_ref, m_ref, acc_ref, slot)

    _pipelined(k_ref.shape[0] // tk, scores, update)
    o = (_normalized(acc_ref, 0) - lam_ref[0] * _normalized(acc_ref, 1)).T
    o_ref[...] = (_rms(o, g_ref[...], 1.0 / LANES) * post_scale).astype(o_ref.dtype)


def _attn_a(proj, vt, posq_row, posk_rep, lam, slopes2, g, post_scale, tq, tk):
    b, s, _ = proj.shape
    smem = pl.BlockSpec(memory_space=pltpu.SMEM)
    return pl.pallas_call(
        functools.partial(_attn_a_kernel, tk=tk, post_scale=post_scale),
        grid=(b, A_HEADS, s // tq),
        in_specs=[smem, smem,
                  pl.BlockSpec((None, tq, LANES), lambda bb, h, i: (bb, i, COL_A + h)),
                  pl.BlockSpec((None, s, LANES), lambda bb, h, i: (bb, 0, COL_A + A_HEADS + h)),
                  pl.BlockSpec((None, None, vt.shape[2], s), lambda bb, h, i: (bb, h, 0, 0)),
                  pl.BlockSpec((None, 1, tq), lambda bb, h, i: (bb, 0, i)),
                  pl.BlockSpec((None, s, LANES), lambda bb, h, i: (bb, 0, 0)),
                  pl.BlockSpec((1, LANES), lambda bb, h, i: (0, 0))],
        out_specs=pl.BlockSpec((None, tq, LANES), lambda bb, h, i: (bb, i, h)),
        out_shape=jax.ShapeDtypeStruct((b, s, A_HEADS * LANES), BF16),
        scratch_shapes=[pltpu.VMEM((2, tq, LANES), BF16)] + _attn_scratch(tq, tk, LANES),
        compiler_params=_cp(("parallel", "parallel", "arbitrary")),
        name="attn_diff",
    )(lam, slopes2, proj, proj, vt, posq_row, posk_rep, g.reshape(1, LANES))


def _attn_b_kernel(q_ref, k_ref, vt_ref, bias_ref, o_ref, qs_ref, m_ref, acc_ref, s_ref, cm_ref, *, nband):
    t = q_ref.shape[0]
    nkb = k_ref.shape[0] // t
    i = pl.program_id(2)
    half = nband // 2
    _split_q(q_ref, qs_ref, B_DH ** -0.5 * LOG2E)
    _init_state(m_ref, acc_ref)

    def start(jj):
        return pl.multiple_of(jnp.clip(i + jj - half, 0, nkb - 1) * t, t)

    def scores(jj, slot):
        kb = i + jj - half
        table = jnp.where((kb >= 0) & (kb < nkb), jj, nband)
        k = k_ref[pl.ds(start(jj), t), :]
        for c in range(2):
            _put_scores(_kq(k, qs_ref[c]) + bias_ref[c, table], s_ref, cm_ref, slot, c)

    def update(jj, slot):
        ks = start(jj)
        _update(lambda c: vt_ref[c, :, pl.ds(ks, t)], s_ref, cm_ref, m_ref, acc_ref, slot)

    _pipelined(nband, scores, update)
    o = jnp.concatenate([_normalized(acc_ref, 0), _normalized(acc_ref, 1)], axis=0)
    o_ref[...] = o.T.astype(o_ref.dtype)


def _b_bias_tables(t, nband):
    half = nband // 2
    r = jnp.arange(t, dtype=jnp.int32)[None, :, None]
    c = jnp.arange(t, dtype=jnp.int32)[None, None, :]
    jj = jnp.arange(nband + 1, dtype=jnp.int32)[:, None, None]
    ao = jnp.abs((jj - half) * t + r - c)
    mult = jnp.zeros(ao.shape, jnp.int32)
    for window, dilation in B_PATTERNS:
        reach = (window // (2 * dilation)) * dilation
        mult = mult + ((ao % dilation == 0) & (ao <= reach)).astype(jnp.int32)
    mult = jnp.where(jj < nband, mult, 0)
    slopes = jnp.exp2(-8.0 * jnp.arange(1, B_HEADS + 1, dtype=F32) / B_HEADS) * LOG2E
    bias = jnp.log2(jnp.maximum(mult, 1).astype(F32))[None] - slopes[:, None, None, None] * ao.astype(F32)[None]
    bias = jnp.where((mult > 0)[None], bias, NEG_BIG)
    return bias.reshape(B_HEADS // 2, 2, nband + 1, t, t)


def _attn_b(proj, vt, bias, t):
    b, s, _ = proj.shape
    nband = bias.shape[2] - 1
    npair = B_HEADS // 2
    return pl.pallas_call(
        functools.partial(_attn_b_kernel, nband=nband),
        grid=(b, npair, s // t),
        in_specs=[pl.BlockSpec((None, t, LANES), lambda bb, p, i: (bb, i, COL_B + p)),
                  pl.BlockSpec((None, s, LANES), lambda bb, p, i: (bb, 0, COL_B + npair + p)),
                  pl.BlockSpec((None, 2, vt.shape[2], s), lambda bb, p, i: (bb, p, 0, 0)),
                  pl.BlockSpec((None,) + bias.shape[1:], lambda bb, p, i: (p, 0, 0, 0, 0))],
        out_specs=pl.BlockSpec((None, t, LANES), lambda bb, p, i: (bb, i, p)),
        out_shape=jax.ShapeDtypeStruct((b, s, npair * LANES), BF16),
        scratch_shapes=[pltpu.VMEM((2, t, LANES), BF16)] + _attn_scratch(t, t, B_DH),
        compiler_params=_cp(("parallel", "parallel", "arbitrary")),
        name="attn_dilated",
    )(proj, proj, vt, bias)


def _rope_table_kernel(pos_ref, invf_ref, c_ref, s_ref):
    ang = pos_ref[...] * invf_ref[...]
    c_ref[...] = jnp.cos(ang)
    s_ref[...] = jnp.sin(ang)


def _rope_tables(pos_col, tm):
    t = pos_col.shape[0]
    inv = ROPE_THETA ** (-jnp.arange(0, C_ROPE, 2, dtype=F32) / C_ROPE)
    invf = jnp.concatenate([jnp.zeros((C_NOPE,), F32), inv, inv,
                            jnp.zeros((LANES - C_NOPE - C_ROPE,), F32)]).reshape(1, LANES)
    spec = pl.BlockSpec((tm, LANES), lambda i: (i, 0))
    return pl.pallas_call(
        _rope_table_kernel,
        grid=(t // tm,),
        in_specs=[pl.BlockSpec((tm, 1), lambda i: (i, 0)), pl.BlockSpec((1, LANES), lambda i: (0, 0))],
        out_specs=[spec, spec],
        out_shape=[jax.ShapeDtypeStruct((t, LANES), F32)] * 2,
        compiler_params=_cp(("parallel",)),
        name="rope_tables",
    )(pos_col, invf)


def _mla_prep_kernel(cq_ref, ckv_ref, r1_ref, r2_ref, c_ref, s_ref, gq_ref, gkv_ref,
                     wq1_ref, wq2_ref, wk_ref, wv_ref, q_out, k_out, v_out):
    qn = _rms(cq_ref[...].astype(F32), gq_ref[...], 1.0 / C_Q_RANK).astype(BF16)
    kvn = _rms(ckv_ref[...].astype(F32), gkv_ref[...], 1.0 / C_KV_RANK).astype(BF16)
    q1 = jnp.dot(qn, wq1_ref[...], preferred_element_type=F32)
    q2 = jnp.dot(qn, wq2_ref[...], preferred_element_type=F32)
    k1 = jnp.dot(kvn, wk_ref[...], preferred_element_type=F32)
    v_out[...] = jnp.dot(kvn, wv_ref[...], preferred_element_type=F32).astype(v_out.dtype)
    cos = c_ref[...]
    sin = s_ref[...]
    k_rope = r1_ref[...].astype(F32) * cos + r2_ref[...].astype(F32) * sin
    scale = (C_NOPE + C_ROPE) ** -0.5 * LOG2E
    for h in range(C_HEADS):
        sl = slice(h * LANES, (h + 1) * LANES)
        q_out[:, sl] = ((q1[:, sl] * cos + q2[:, sl] * sin) * scale).astype(q_out.dtype)
        k_out[:, sl] = (k1[:, sl] + k_rope).astype(k_out.dtype)


def _mla_prep(proj2, cos, sin, gq, gkv, wq1, wq2, wk, wv, tm):
    t = proj2.shape[0]
    full = lambda a: pl.BlockSpec(a.shape, lambda i: (0, 0))
    row = lambda w: pl.BlockSpec((tm, w), lambda i: (i, 0))
    return pl.pallas_call(
        _mla_prep_kernel,
        grid=(t // tm,),
        in_specs=[pl.BlockSpec((tm, 4 * LANES), lambda i: (i, COL_CQ // 4)),
                  pl.BlockSpec((tm, 2 * LANES), lambda i: (i, COL_CKV // 2)),
                  pl.BlockSpec((tm, LANES), lambda i: (i, COL_R1)),
                  pl.BlockSpec((tm, LANES), lambda i: (i, COL_R2)),
                  row(LANES), row(LANES), full(gq), full(gkv), full(wq1), full(wq2), full(wk), full(wv)],
        out_specs=[row(C_HEADS * LANES), row(C_HEADS * LANES), row(C_HEADS * C_DV)],
        out_shape=[jax.ShapeDtypeStruct((t, C_HEADS * LANES), BF16),
                   jax.ShapeDtypeStruct((t, C_HEADS * LANES), BF16),
                   jax.ShapeDtypeStruct((t, C_HEADS * C_DV), BF16)],
        compiler_params=_cp(("parallel",)),
        name="mla_prep",
    )(proj2, proj2, proj2, proj2, cos, sin, gq, gkv, wq1, wq2, wk, wv)


def _attn_c_kernel(q_ref, k_ref, vt_ref, o_ref, m_ref, acc_ref, s_ref, cm_ref, *, tk):
    _init_state(m_ref, acc_ref)

    def scores(j, slot):
        ks = pl.multiple_of(j * tk, tk)
        for c in range(2):
            sl = slice(c * LANES, (c + 1) * LANES)
            _put_scores(_kq(k_ref[pl.ds(ks, tk), sl], q_ref[:, sl]), s_ref, cm_ref, slot, c)

    def update(j, slot):
        ks = pl.multiple_of(j * tk, tk)
        _update(lambda c: vt_ref[c, :, pl.ds(ks, tk)], s_ref, cm_ref, m_ref, acc_ref, slot)

    _pipelined(k_ref.shape[0] // tk, scores, update)
    o = jnp.concatenate([_normalized(acc_ref, 0), _normalized(acc_ref, 1)], axis=0)
    o_ref[...] = o.T.astype(o_ref.dtype)


def _attn_c(q, k, vt, tq, tk):
    b, s, _ = q.shape
    npair = C_HEADS // 2
    return pl.pallas_call(
        functools.partial(_attn_c_kernel, tk=tk),
        grid=(b, npair, s // tq),
        in_specs=[pl.BlockSpec((None, tq, 2 * LANES), lambda bb, p, i: (bb, i, p)),
                  pl.BlockSpec((None, s, 2 * LANES), lambda bb, p, i: (bb, 0, p)),
                  pl.BlockSpec((None, 2, vt.shape[2], s), lambda bb, p, i: (bb, p, 0, 0))],
        out_specs=pl.BlockSpec((None, tq, LANES), lambda bb, p, i: (bb, i, p)),
        out_shape=jax.ShapeDtypeStruct((b, s, npair * LANES), BF16),
        scratch_shapes=_attn_scratch(tq, tk, C_DV),
        compiler_params=_cp(("parallel", "parallel", "arbitrary")),
        name="attn_latent",
    )(q, k, vt)


def _merge_kernel(oa_ref, ob_ref, oc_ref, g0_ref, g1_ref, g2_ref, x_ref, wb_ref, wo_ref, o_ref):
    z = None
    for n, (o_r, g_r) in enumerate(((oa_ref, g0_ref), (ob_ref, g1_ref), (oc_ref, g2_ref))):
        br = jnp.dot(o_r[...], wb_ref[n], preferred_element_type=F32)
        gate = 1.0 / (1.0 + jnp.exp(-g_r[...].astype(F32)))
        z = gate * br if z is None else z + gate * br
    o_ref[...] = x_ref[...] + jnp.dot(z.astype(BF16), wo_ref[...], preferred_element_type=F32)


def _merge(oa, ob, oc, proj2, x2, wb, wo, tm):
    t, d = x2.shape
    bw = oa.shape[1]
    row = lambda w: pl.BlockSpec((tm, w), lambda i: (i, 0))
    gate = lambda n: pl.BlockSpec((tm, d), lambda i: (i, COL_G * LANES // d + n))
    return pl.pallas_call(
        _merge_kernel,
        grid=(t // tm,),
        in_specs=[row(bw), row(bw), row(bw), gate(0), gate(1), gate(2), row(d),
                  pl.BlockSpec(wb.shape, lambda i: (0, 0, 0)), pl.BlockSpec(wo.shape, lambda i: (0, 0))],
        out_specs=row(d),
        out_shape=jax.ShapeDtypeStruct((t, d), F32),
        compiler_params=_cp(("parallel",)),
        name="branch_merge",
    )(oa, ob, oc, proj2, proj2, proj2, x2, wb, wo)


def _cross_kernel(x_ref, g_ref, wq_ref, kbd_ref, vbd_ref, wo_ref, o_ref, *, n_mem):
    x = x_ref[...]
    h = _rms(x, g_ref[...], 1.0 / x.shape[-1]).astype(BF16)
    q = (jnp.dot(h, wq_ref[...], preferred_element_type=F32) * (X_DH ** -0.5 * LOG2E)).astype(BF16)
    s = jnp.dot(q, kbd_ref[...], preferred_element_type=F32)
    ps = []
    for hh in range(X_HEADS):
        sh = s[:, hh * n_mem:(hh + 1) * n_mem]
        p = jnp.exp2(sh - jnp.max(sh, axis=-1, keepdims=True))
        ps.append((p / jnp.sum(p, axis=-1, keepdims=True)).astype(BF16))
    o = jnp.dot(jnp.concatenate(ps, axis=1), vbd_ref[...], preferred_element_type=F32)
    o_ref[...] = x + jnp.dot(o.astype(BF16), wo_ref[...], preferred_element_type=F32)


def _cross(x3, g, wq, kbd, vbd, wo, tm):
    b, s, d = x3.shape
    n_mem = kbd.shape[2] // X_HEADS
    full = lambda a: pl.BlockSpec(a.shape, lambda bb, i: (0, 0))
    return pl.pallas_call(
        functools.partial(_cross_kernel, n_mem=n_mem),
        grid=(b, s // tm),
        in_specs=[pl.BlockSpec((None, tm, d), lambda bb, i: (bb, i, 0)), full(g), full(wq),
                  pl.BlockSpec((None,) + kbd.shape[1:], lambda bb, i: (bb, 0, 0)),
                  pl.BlockSpec((None,) + vbd.shape[1:], lambda bb, i: (bb, 0, 0)), full(wo)],
        out_specs=pl.BlockSpec((None, tm, d), lambda bb, i: (bb, i, 0)),
        out_shape=jax.ShapeDtypeStruct((b, s, d), F32),
        compiler_params=_cp(("parallel", "parallel")),
        name="cross_attn",
    )(x3, g, wq, kbd, vbd, wo)


def _block_diag_kv(kv):
    b, m, _ = kv.shape
    kv = kv.reshape(b, m, 2, X_HEADS, X_DH)
    eye = jnp.eye(X_HEADS, dtype=kv.dtype)
    kt = kv[:, :, 0].transpose(0, 2, 3, 1)
    kbd = (kt[:, :, :, None, :] * eye[None, :, None, :, None]).reshape(b, X_HEADS * X_DH, X_HEADS * m)
    vt = kv[:, :, 1].transpose(0, 2, 1, 3)
    vbd = (vt[:, :, :, None, :] * eye[None, :, None, :, None]).reshape(b, X_HEADS * m, X_HEADS * X_DH)
    return kbd, vbd


def _router_kernel(x_ref, g_ref, w_ref, b_ref, h_out, r_out):
    x = x_ref[...]
    h = _rms(x, g_ref[...], 1.0 / x.shape[-1])
    h_out[...] = h.astype(h_out.dtype)
    logits = jnp.dot(h, w_ref[...], preferred_element_type=F32, precision=lax.Precision.HIGHEST) + b_ref[...]
    lane = lax.broadcasted_iota(jnp.int32, logits.shape, 1)
    lane_f = lane.astype(F32)
    big = jnp.float32(4 * LANES)

    def top(vals, mask):
        mv = jnp.max(jnp.where(mask, vals, -jnp.inf), axis=-1, keepdims=True)
        idx = jnp.min(jnp.where(mask & (vals == mv), lane_f, big), axis=-1, keepdims=True)
        return mv, idx

    g_mask = lane < N_GROUPS
    g_max, g_idx = top(logits, g_mask)
    p_g = 1.0 / jnp.sum(jnp.where(g_mask, jnp.exp(logits - g_max), 0.0), axis=-1, keepdims=True)
    first = N_GROUPS + g_idx * EXPERTS_PER_GROUP
    e_mask = (lane_f >= first) & (lane_f < first + EXPERTS_PER_GROUP)
    v0, i0 = top(logits, e_mask)
    v1, i1 = top(logits, e_mask & (lane_f != i0))
    e1 = jnp.exp(v1 - v0)
    w0 = p_g / (1.0 + e1)
    w1 = p_g * e1 / (1.0 + e1)
    out = jnp.where(lane == 0, i0 - N_GROUPS, 0.0)
    out = jnp.where(lane == 1, i1 - N_GROUPS, out)
    out = jnp.where(lane == 2, w0, out)
    out = jnp.where(lane == 3, w1, out)
    r_out[...] = out


def _router(x2, g, w, bias, tm):
    t, d = x2.shape
    full = lambda a: pl.BlockSpec(a.shape, lambda i: (0, 0))
    return pl.pallas_call(
        _router_kernel,
        grid=(t // tm,),
        in_specs=[pl.BlockSpec((tm, d), lambda i: (i, 0)), full(g), full(w), full(bias)],
        out_specs=[pl.BlockSpec((tm, d), lambda i: (i, 0)), pl.BlockSpec((tm, LANES), lambda i: (i, 0))],
        out_shape=[jax.ShapeDtypeStruct((t, d), BF16), jax.ShapeDtypeStruct((t, LANES), F32)],
        compiler_params=_cp(("parallel",)),
        name="moe_router",
    )(x2, g, w, bias)


def _expert_kernel(blk_e_ref, n_used_ref, x_ref, w13_ref, w2_ref, o_ref):
    del blk_e_ref
    used = pl.program_id(0) < n_used_ref[0]

    @pl.when(used)
    def _():
        hid = jnp.dot(x_ref[...], w13_ref[...], preferred_element_type=F32)
        de = hid.shape[1] // 2
        a = hid[:, :de]
        act = (a / (1.0 + jnp.exp(-a))) * hid[:, de:]
        o_ref[...] = jnp.dot(act.astype(BF16), w2_ref[...], preferred_element_type=F32).astype(o_ref.dtype)

    @pl.when(jnp.logical_not(used))
    def _():
        o_ref[...] = jnp.zeros(o_ref.shape, o_ref.dtype)


def _experts(blk_e, n_used, xr, w13, w2, rows_per_block):
    rows, d = xr.shape
    grid_spec = pltpu.PrefetchScalarGridSpec(
        num_scalar_prefetch=2,
        grid=(rows // rows_per_block,),
        in_specs=[pl.BlockSpec((rows_per_block, d), lambda i, be, nu: (i, 0)),
                  pl.BlockSpec((None,) + w13.shape[1:], lambda i, be, nu: (be[i], 0, 0)),
                  pl.BlockSpec((None,) + w2.shape[1:], lambda i, be, nu: (be[i], 0, 0))],
        out_specs=pl.BlockSpec((rows_per_block, d), lambda i, be, nu: (i, 0)),
    )
    return pl.pallas_call(
        _expert_kernel,
        grid_spec=grid_spec,
        out_shape=jax.ShapeDtypeStruct((rows, d), BF16),
        compiler_params=_cp(("arbitrary",)),
        name="moe_experts",
    )(blk_e, n_used, xr, w13, w2)


def _combine_kernel(x_ref, y0_ref, y1_ref, r_ref, g_ref, o_ref, *, final_norm):
    r = r_ref[...]
    y = x_ref[...] + r[:, 2:3] * y0_ref[...].astype(F32) + r[:, 3:4] * y1_ref[...].astype(F32)
    if final_norm:
        y = _rms(y, g_ref[...], 1.0 / y.shape[-1])
    o_ref[...] = y


def _combine(x2, y0, y1, route, g, final_norm, tm):
    t, d = x2.shape
    row = lambda w: pl.BlockSpec((tm, w), lambda i: (i, 0))
    return pl.pallas_call(
        functools.partial(_combine_kernel, final_norm=final_norm),
        grid=(t // tm,),
        in_specs=[row(d), row(d), row(d), row(LANES), pl.BlockSpec((1, d), lambda i: (0, 0))],
        out_specs=row(d),
        out_shape=jax.ShapeDtypeStruct((t, d), F32),
        compiler_params=_cp(("parallel",)),
        name="moe_combine",
    )(x2, y0, y1, route, g)


def _dispatch(route, rows_per_block):
    t = route.shape[0]
    eid = route[:, :TOP_K].astype(jnp.int32).reshape(-1)
    n = eid.shape[0]
    order = jnp.argsort(eid).astype(jnp.int32)
    rank = jnp.argsort(order).astype(jnp.int32)
    experts = jnp.arange(N_EXPERTS, dtype=jnp.int32)
    counts = jnp.sum((eid[:, None] == experts[None, :]).astype(jnp.int32), axis=0)
    start = jnp.cumsum(counts) - counts
    padded = (counts + rows_per_block - 1) // rows_per_block * rows_per_block
    pend = jnp.cumsum(padded)
    pstart = pend - padded
    dest = (rank + (pstart - start)[eid]).reshape(t, TOP_K)
    n_blocks = n // rows_per_block + N_EXPERTS
    blk_first = jnp.arange(n_blocks, dtype=jnp.int32) * rows_per_block
    blk_e = jnp.minimum(jnp.sum((pend[None, :] <= blk_first[:, None]).astype(jnp.int32), axis=1), N_EXPERTS - 1)
    off = (blk_first - pstart[blk_e])[:, None] + jnp.arange(rows_per_block, dtype=jnp.int32)[None, :]
    src = jnp.clip(start[blk_e][:, None] + off, 0, n - 1)
    row_tok = jnp.where(off < counts[blk_e][:, None], order[src] // TOP_K, 0).reshape(-1)
    n_used = (pend[-1] // rows_per_block).astype(jnp.int32).reshape(1)
    return row_tok, dest, blk_e, n_used


def _rot_cols(w):
    half = w.shape[-1] // 2
    return jnp.concatenate([-w[..., half:], w[..., :half]], axis=-1)


def _pack_w_in(w):
    d = w.shape[0]
    n_ab = 6 * A_HEADS * 2 * A_DH
    cq = w[:, n_ab:n_ab + C_Q_RANK]
    ckv = w[:, n_ab + C_Q_RANK:n_ab + C_Q_RANK + C_KV_RANK]
    ckr = w[:, n_ab + C_Q_RANK + C_KV_RANK:n_ab + C_Q_RANK + C_KV_RANK + C_ROPE]
    gates = w[:, n_ab + C_Q_RANK + C_KV_RANK + C_ROPE:]
    z = lambda n: jnp.zeros((d, n), w.dtype)
    tail = LANES - C_NOPE - C_ROPE
    packed = jnp.concatenate([w[:, :n_ab], cq, z(LANES), ckv,
                              z(C_NOPE), ckr, z(tail), z(C_NOPE), _rot_cols(ckr), z(tail), gates], axis=1)
    return packed.astype(BF16)


def _pack_w_uq(w):
    wq = w.reshape(C_Q_RANK, C_HEADS, C_NOPE + C_ROPE)
    pad_rows = 4 * LANES - C_Q_RANK
    tail = LANES - C_NOPE - C_ROPE
    q1 = jnp.pad(wq, ((0, pad_rows), (0, 0), (0, tail))).reshape(4 * LANES, C_HEADS * LANES)
    q2 = jnp.pad(_rot_cols(wq[:, :, C_NOPE:]), ((0, pad_rows), (0, 0), (C_NOPE, tail))).reshape(4 * LANES, C_HEADS * LANES)
    return q1.astype(BF16), q2.astype(BF16)


def _pack_w_ukv(w):
    wkv = w.reshape(C_KV_RANK, C_HEADS, C_NOPE + C_DV)
    wk = jnp.pad(wkv[:, :, :C_NOPE], ((0, 0), (0, 0), (0, LANES - C_NOPE))).reshape(C_KV_RANK, C_HEADS * LANES)
    wv = wkv[:, :, C_NOPE:].reshape(C_KV_RANK, C_HEADS * C_DV)
    return wk.astype(BF16), wv.astype(BF16)


def kernel(x, mem, positions, mix_norm_g, w_in, diff_lambda, diff_subln_g, mla_q_norm_g, w_uq, mla_kv_norm_g, w_ukv, w_branch, w_out, cross_norm_g, mem_norm_g, w_xq, w_xkv, w_xo, ffn_norm_g, w_group, b_group, w_router, b_router, w1, w3, w2, final_norm_g):
    b, s, d = x.shape
    depth = w_in.shape[0]
    t = b * s
    n_mem = mem.shape[1]
    tl = _tiles(s)
    assert PROJ_COLS == COL_G * LANES + N_BRANCHES * d and s % (2 * tl["t_b"]) == 0

    pos_f = positions.astype(F32)
    posq_row = pos_f.reshape(b, 1, s)
    posk_rep = jnp.broadcast_to(pos_f[:, :, None], (b, s, LANES))
    cos, sin = _rope_tables(pos_f.reshape(t, 1), tl["tm_tok"])
    slopes_a2 = jnp.exp2(-8.0 * jnp.arange(1, A_HEADS + 1, dtype=F32) / A_HEADS) * LOG2E
    reach = max((w // (2 * dl)) * dl for w, dl in B_PATTERNS)
    b_bias = _b_bias_tables(tl["t_b"], 2 * (-(-reach // tl["t_b"])) + 1)
    mem2 = mem.reshape(b * n_mem, d)

    x2 = x.reshape(t, d)
    for l in range(depth):
        proj2 = _norm_matmul(x2, mix_norm_g[l], _pack_w_in(w_in[l]), tl["tm_proj"], tl["tn_proj"])
        proj3 = proj2.reshape(b, s, PROJ_COLS)
        lq = diff_lambda[l].astype(F32)
        lam_init = 0.8 - 0.6 * math.exp(-0.3 * l)
        lam = (jnp.exp(jnp.sum(lq[0] * lq[1])) - jnp.exp(jnp.sum(lq[2] * lq[3])) + lam_init).reshape(1)
        vt_a = _aug_vt(proj3[:, :, (COL_A + 2 * A_HEADS) * LANES:(COL_A + 3 * A_HEADS) * LANES], 2 * A_DH)
        vt_b = _aug_vt(proj3[:, :, (COL_B + 2 * A_HEADS) * LANES:(COL_B + 3 * A_HEADS) * LANES], B_DH)
        oa = _attn_a(proj3, vt_a, posq_row, posk_rep, lam, slopes_a2, diff_subln_g[l], 1.0 - lam_init,
                     tl["tq_a"], tl["tk_a"])
        ob = _attn_b(proj3, vt_b, b_bias, tl["t_b"])
        wq1, wq2 = _pack_w_uq(w_uq[l])
        wk, wv = _pack_w_ukv(w_ukv[l])
        gq = jnp.pad(mla_q_norm_g[l], (0, 4 * LANES - C_Q_RANK)).reshape(1, 4 * LANES)
        qc, kc, vc = _mla_prep(proj2, cos, sin, gq, mla_kv_norm_g[l].reshape(1, C_KV_RANK),
                               wq1, wq2, wk, wv, tl["tm_tok"])
        oc = _attn_c(qc.reshape(b, s, -1), kc.reshape(b, s, -1), _aug_vt(vc.reshape(b, s, -1), C_DV), tl["tq_c"], tl["tk_c"])
        x2 = _merge(oa.reshape(t, -1), ob.reshape(t, -1), oc.reshape(t, -1), proj2, x2,
                    w_branch[l].astype(BF16), w_out[l].astype(BF16), tl["tm_tok"])
        kv = _norm_matmul(mem2, mem_norm_g[l], w_xkv[l].astype(BF16), min(1024, b * n_mem), w_xkv.shape[2])
        kbd, vbd = _block_diag_kv(kv.reshape(b, n_mem, -1))
        x2 = _cross(x2.reshape(b, s, d), cross_norm_g[l].reshape(1, d), w_xq[l].astype(BF16), kbd, vbd,
                    w_xo[l].astype(BF16), tl["tm_tok"]).reshape(t, d)
        w_r = jnp.pad(jnp.concatenate([w_group[l], w_router[l]], axis=1), ((0, 0), (0, LANES - N_GROUPS - N_EXPERTS)))
        b_r = jnp.pad(jnp.concatenate([b_group[l], b_router[l]]), (0, LANES - N_GROUPS - N_EXPERTS)).reshape(1, LANES)
        h, route = _router(x2, ffn_norm_g[l].reshape(1, d), w_r, b_r, tl["tm_tok"])
        row_tok, dest, blk_e, n_used = _dispatch(route, tl["moe_rows"])
        w13 = jnp.concatenate([w1[l], w3[l]], axis=2).astype(BF16)
        yr = _experts(blk_e, n_used, h[row_tok], w13, w2[l].astype(BF16), tl["moe_rows"])
        x2 = _combine(x2, yr[dest[:, 0]], yr[dest[:, 1]], route, final_norm_g.reshape(1, d),
                      l == depth - 1, tl["tm_tok"])
    return x2.reshape(b, s, d)
```

```python
import functools
import math

import jax
import jax.numpy as jnp
from jax import lax
from jax.experimental import pallas as pl
from jax.experimental.pallas import tpu as pltpu

F32 = jnp.float32
BF16 = jnp.bfloat16

LANES = 128
NORM_EPS = 1e-6
LOG2E = math.log2(math.e)
NEG_BIG = -1e30

A_HEADS, A_DH = 4, 64
B_HEADS, B_DH = 8, 64
B_PATTERNS = ((128, 1), (512, 4), (2048, 16))
C_HEADS, C_Q_RANK, C_KV_RANK, C_NOPE, C_ROPE, C_DV = 8, 384, 256, 64, 32, 64
ROPE_THETA = 10000.0
N_BRANCHES = 3
X_HEADS, X_DH = 4, 64
N_GROUPS, EXPERTS_PER_GROUP, TOP_K = 4, 8, 2
N_EXPERTS = N_GROUPS * EXPERTS_PER_GROUP

COL_A = 0
COL_B = 8
COL_CQ = 16
COL_CKV = 20
COL_R1 = 22
COL_R2 = 23
COL_G = 24
PROJ_COLS = 48 * LANES

VMEM_LIMIT = 48 * 1024 * 1024


def _tiles(seq):
    return dict(
        tm_proj=min(1024, seq), tn_proj=1024,
        tm_tok=min(512, seq),
        tq_a=min(512, seq), tk_a=min(512, seq),
        t_b=min(512, seq),
        tq_c=min(512, seq), tk_c=min(512, seq),
        moe_rows=512,
    )


def _cp(sem):
    return pltpu.CompilerParams(dimension_semantics=sem, vmem_limit_bytes=VMEM_LIMIT)


def _rms(x, g, inv_n):
    ms = jnp.sum(x * x, axis=-1, keepdims=True) * inv_n
    return x * lax.rsqrt(ms + NORM_EPS) * g


def _norm_matmul_kernel(x_ref, g_ref, w_ref, o_ref, h_ref, *, inv_n):
    @pl.when(pl.program_id(1) == 0)
    def _():
        h_ref[...] = _rms(x_ref[...].astype(F32), g_ref[...], inv_n).astype(BF16)

    o_ref[...] = jnp.dot(h_ref[...], w_ref[...], preferred_element_type=F32).astype(o_ref.dtype)


def _norm_matmul(x, g, w, tm, tn, out_dtype=BF16):
    m, k = x.shape
    n = w.shape[1]
    return pl.pallas_call(
        functools.partial(_norm_matmul_kernel, inv_n=1.0 / k),
        grid=(m // tm, n // tn),
        in_specs=[pl.BlockSpec((tm, k), lambda i, j: (i, 0)),
                  pl.BlockSpec((1, k), lambda i, j: (0, 0)),
                  pl.BlockSpec((k, tn), lambda i, j: (0, j))],
        out_specs=pl.BlockSpec((tm, tn), lambda i, j: (i, j)),
        out_shape=jax.ShapeDtypeStruct((m, n), out_dtype),
        scratch_shapes=[pltpu.VMEM((tm, k), BF16)],
        compiler_params=_cp(("parallel", "arbitrary")),
        name="norm_matmul",
    )(x, g.reshape(1, k), w)


def _in_proj_kernel(x_ref, g_ref, w_ref, wta_ref, wtb_ref, ona_ref, onb_ref, o_ref, vta_ref, vtb_ref, h_ref, *, inv_n):
    @pl.when(pl.program_id(1) == 0)
    def _():
        h = _rms(x_ref[...].astype(F32), g_ref[...], inv_n).astype(BF16)
        h_ref[...] = h
        vta_ref[...] = (_kq(wta_ref[...], h) + ona_ref[...]).astype(vta_ref.dtype)
        vtb_ref[...] = (_kq(wtb_ref[...], h) + onb_ref[...]).astype(vtb_ref.dtype)

    o_ref[...] = jnp.dot(h_ref[...], w_ref[...], preferred_element_type=F32).astype(o_ref.dtype)


def _in_proj(x, g, w, wta, ona, wtb, onb, tm, tn):
    m, k = x.shape
    n = w.shape[1]
    full = lambda a: pl.BlockSpec(a.shape, lambda i, j: (0, 0))
    return pl.pallas_call(
        functools.partial(_in_proj_kernel, inv_n=1.0 / k),
        grid=(m // tm, n // tn),
        in_specs=[pl.BlockSpec((tm, k), lambda i, j: (i, 0)),
                  pl.BlockSpec((1, k), lambda i, j: (0, 0)),
                  pl.BlockSpec((k, tn), lambda i, j: (0, j)),
                  full(wta), full(wtb), full(ona), full(onb)],
        out_specs=[pl.BlockSpec((tm, tn), lambda i, j: (i, j)),
                   pl.BlockSpec((wta.shape[0], tm), lambda i, j: (0, i)),
                   pl.BlockSpec((wtb.shape[0], tm), lambda i, j: (0, i))],
        out_shape=[jax.ShapeDtypeStruct((m, n), BF16),
                   jax.ShapeDtypeStruct((wta.shape[0], m), BF16),
                   jax.ShapeDtypeStruct((wtb.shape[0], m), BF16)],
        scratch_shapes=[pltpu.VMEM((tm, k), BF16)],
        compiler_params=_cp(("parallel", "arbitrary")),
        name="in_proj",
    )(x, g.reshape(1, k), w, wta, wtb, ona, onb)


ONES_ROWS = 16


def _vt_weights(wv, dv):
    k, w = wv.shape
    wt = jnp.pad(wv.T.reshape(w // dv, dv, k), ((0, 0), (0, ONES_ROWS), (0, 0))).reshape(-1, k)
    ones = jnp.zeros((w // dv, dv + ONES_ROWS, 1), F32).at[:, dv, 0].set(1.0).reshape(-1, 1)
    return wt.astype(BF16), ones


def _split_q(q_ref, qs_ref, scale):
    q = q_ref[...].astype(F32) * scale
    lane = lax.broadcasted_iota(jnp.int32, q.shape, 1)
    qs_ref[0] = jnp.where(lane < LANES // 2, q, 0.0).astype(BF16)
    qs_ref[1] = jnp.where(lane >= LANES // 2, q, 0.0).astype(BF16)


def _kq(k, q):
    return lax.dot_general(k, q, (((1,), (1,)), ((), ())), preferred_element_type=F32)


def _put_scores(s, s_ref, cm_ref, slot, c):
    s_ref[slot, c] = s
    cm_ref[slot, c] = jnp.max(s, axis=0, keepdims=True)


def _update(vt_of, s_ref, cm_ref, m_ref, acc_ref, slot):
    for c in range(2):
        m_prev = m_ref[c]
        m_new = jnp.maximum(m_prev, cm_ref[slot, c])
        alpha = jnp.exp2(m_prev - m_new)
        p = jnp.exp2(s_ref[slot, c] - m_new).astype(BF16)
        acc_ref[c] = alpha * acc_ref[c] + jnp.dot(vt_of(c), p, preferred_element_type=F32)
        m_ref[c] = m_new


def _init_state(m_ref, acc_ref):
    m_ref[...] = jnp.full(m_ref.shape, NEG_BIG, F32)
    acc_ref[...] = jnp.zeros(acc_ref.shape, F32)


def _pipelined(n, scores, update):
    scores(0, 0)

    def body(jj, carry):
        j = 2 * jj
        scores(j + 1, 1)
        update(j, 0)
        scores(j + 2, 0)
        update(j + 1, 1)
        return carry

    lax.fori_loop(0, (n - 1) // 2, body, 0)
    if n % 2 == 0:
        scores(n - 1, 1)
        update(n - 2, 0)
        update(n - 1, 1)
    else:
        update(n - 1, 0)


def _pipelined_list(n, chunk, scores, update):
    scores(chunk(0), 0)

    def body(jj, carry):
        t = 2 * jj
        scores(chunk(t + 1), 1)
        update(chunk(t), 0)
        scores(chunk(t + 2), 0)
        update(chunk(t + 1), 1)
        return carry

    lax.fori_loop(0, n // 2 - 1, body, 0)
    scores(chunk(n - 1), 1)
    update(chunk(n - 2), 0)
    update(chunk(n - 1), 1)


def _normalized(acc_ref, c):
    dv = acc_ref.shape[1] - ONES_ROWS
    return acc_ref[c, :dv, :] / acc_ref[c, dv:dv + 1, :]


def _attn_scratch(tq, tk, dv):
    return [pltpu.VMEM((2, 1, tq), F32), pltpu.VMEM((2, dv + ONES_ROWS, tq), F32),
            pltpu.VMEM((2, 2, tk, tq), F32), pltpu.VMEM((2, 2, 1, tq), F32)]


SKIP_MARGIN = 150.0


def _a_chunk_lists(proj3, pos_f, slopes2, tq, tk):
    b, s, _ = proj3.shape
    n_a = A_HEADS * 2 * A_DH
    q = (proj3[:, :, COL_A * LANES:COL_A * LANES + n_a].astype(F32) * (A_DH ** -0.5 * LOG2E)).astype(BF16).astype(F32)
    k = proj3[:, :, COL_A * LANES + n_a:COL_A * LANES + 2 * n_a].astype(F32)

    def block_norm(x, t):
        nrm = jnp.sqrt(jnp.sum(jnp.square(x.reshape(b, s // t, t, A_HEADS, 2, A_DH)), axis=-1))
        return jnp.max(nrm, axis=(2, 4)).transpose(0, 2, 1)

    qk = 1.01 * block_norm(q, tq)[:, :, :, None] * block_norm(k, tk)[:, :, None, :] + 1.0
    pq = pos_f.reshape(b, s // tq, tq)
    pk = pos_f.reshape(b, s // tk, tk)
    qlo, qhi = jnp.min(pq, -1)[:, :, None], jnp.max(pq, -1)[:, :, None]
    klo, khi = jnp.min(pk, -1)[:, None, :], jnp.max(pk, -1)[:, None, :]
    dmin = jnp.maximum(jnp.maximum(klo - qhi, qlo - khi), 0.0)
    dmax = jnp.maximum(khi - qlo, qhi - klo)
    sl = slopes2[None, :, None, None]
    lower = -qk - sl * dmax[:, None]
    upper = qk - sl * dmin[:, None]
    nk = s // tk
    j = jnp.arange(nk, dtype=jnp.int32)
    if tq == tk:
        cover = jnp.max(jnp.min(jnp.abs(pq[:, :, :, None] - pk[:, :, None, :]), axis=-1), axis=-1)
        lower = jnp.where(j[:, None] == j[None, :], jnp.maximum(lower, -qk - sl * cover[:, None, :, None]), lower)
    first = jnp.argmax(lower, axis=-1).astype(jnp.int32)[..., None]
    keep = (upper >= jnp.max(lower, axis=-1, keepdims=True) - SKIP_MARGIN) | (j == first)
    lst = jnp.argsort(jnp.where(j == first, -1, jnp.where(keep, j, nk + j)), axis=-1).astype(jnp.int32)
    cnt = jnp.sum(keep.astype(jnp.int32), axis=-1)
    return lst.reshape(-1), (cnt + cnt % 2).reshape(-1)


def _attn_a_kernel(lst_ref, cnt_ref, lam_ref, slope_ref, q_ref, k_ref, vt_ref, pq_ref, pk_ref, g_ref, o_ref,
                   qs_ref, m_ref, acc_ref, s_ref, cm_ref, *, tk, post_scale):
    tq = q_ref.shape[0]
    nk = k_ref.shape[0] // tk
    blk = (pl.program_id(0) * pl.num_programs(1) + pl.program_id(1)) * pl.num_programs(2) + pl.program_id(2)
    sl2 = slope_ref[pl.program_id(1)]
    _split_q(q_ref, qs_ref, A_DH ** -0.5 * LOG2E)
    _init_state(m_ref, acc_ref)
    pq = pq_ref[...] * sl2

    def scores(j, slot):
        ks = pl.multiple_of(j * tk, tk)
        k = k_ref[pl.ds(ks, tk), :]
        pk = pk_ref[pl.ds(ks, tk), :] * sl2
        bias = jnp.abs(jnp.concatenate([pk] * (tq // LANES), axis=1) - pq)
        for c in range(2):
            _put_scores(_kq(k, qs_ref[c]) - bias, s_ref, cm_ref, slot, c)

    def update(j, slot):
        ks = pl.multiple_of(j * tk, tk)
        _update(lambda c: vt_ref[:, pl.ds(ks, tk)], s_ref, cm_ref, m_ref, acc_ref, slot)

    _pipelined_list(cnt_ref[blk], lambda t: lst_ref[blk * nk + t], scores, update)
    o = (_normalized(acc_ref, 0) - lam_ref[0] * _normalized(acc_ref, 1)).T
    o_ref[...] = (_rms(o, g_ref[...], 1.0 / LANES) * post_scale).astype(o_ref.dtype)


def _attn_a(proj, vt, posq_row, posk_rep, lam, slopes2, g, post_scale, tq, tk):
    b, s, _ = proj.shape
    lst, cnt = _a_chunk_lists(proj, posq_row.reshape(b, s), slopes2, tq, tk)
    smem = pl.BlockSpec(memory_space=pltpu.SMEM)
    grid_spec = pltpu.PrefetchScalarGridSpec(
        num_scalar_prefetch=2,
        grid=(b, A_HEADS, s // tq),
        in_specs=[smem, smem,
                  pl.BlockSpec((None, tq, LANES), lambda bb, h, i, *_: (bb, i, COL_A + h)),
                  pl.BlockSpec((None, s, LANES), lambda bb, h, i, *_: (bb, 0, COL_A + A_HEADS + h)),
                  pl.BlockSpec((vt.shape[0] // A_HEADS, s), lambda bb, h, i, *_: (h, bb)),
                  pl.BlockSpec((None, 1, tq), lambda bb, h, i, *_: (bb, 0, i)),
                  pl.BlockSpec((None, s, LANES), lambda bb, h, i, *_: (bb, 0, 0)),
                  pl.BlockSpec((1, LANES), lambda bb, h, i, *_: (0, 0))],
        out_specs=pl.BlockSpec((None, tq, LANES), lambda bb, h, i, *_: (bb, i, h)),
        scratch_shapes=[pltpu.VMEM((2, tq, LANES), BF16)] + _attn_scratch(tq, tk, LANES),
    )
    return pl.pallas_call(
        functools.partial(_attn_a_kernel, tk=tk, post_scale=post_scale),
        grid_spec=grid_spec,
        out_shape=jax.ShapeDtypeStruct((b, s, A_HEADS * LANES), BF16),
        compiler_params=_cp(("parallel", "parallel", "arbitrary")),
        name="attn_diff",
    )(lst, cnt, lam, slopes2, proj, proj, vt, posq_row, posk_rep, g.reshape(1, LANES))


def _attn_b_kernel(q_ref, k_ref, vt_ref, bias_ref, o_ref, qs_ref, m_ref, acc_ref, s_ref, cm_ref, *, nband):
    t = q_ref.shape[0]
    nkb = k_ref.shape[0] // t
    rows = vt_ref.shape[0] // 2
    i = pl.program_id(2)
    half = nband // 2
    _split_q(q_ref, qs_ref, B_DH ** -0.5 * LOG2E)
    _init_state(m_ref, acc_ref)

    def start(jj):
        return pl.multiple_of(jnp.clip(i + jj - half, 0, nkb - 1) * t, t)

    def scores(jj, slot):
        kb = i + jj - half
        table = jnp.where((kb >= 0) & (kb < nkb), jj, nband)
        k = k_ref[pl.ds(start(jj), t), :]
        for c in range(2):
            _put_scores(_kq(k, qs_ref[c]) + bias_ref[c, table], s_ref, cm_ref, slot, c)

    def update(jj, slot):
        ks = start(jj)
        _update(lambda c: vt_ref[c * rows:(c + 1) * rows, pl.ds(ks, t)], s_ref, cm_ref, m_ref, acc_ref, slot)

    _pipelined(nband, scores, update)
    o = jnp.concatenate([_normalized(acc_ref, 0), _normalized(acc_ref, 1)], axis=0)
    o_ref[...] = o.T.astype(o_ref.dtype)


def _b_bias_tables(t, nband):
    half = nband // 2
    r = jnp.arange(t, dtype=jnp.int32)[None, :, None]
    c = jnp.arange(t, dtype=jnp.int32)[None, None, :]
    jj = jnp.arange(nband + 1, dtype=jnp.int32)[:, None, None]
    ao = jnp.abs((jj - half) * t + r - c)
    mult = jnp.zeros(ao.shape, jnp.int32)
    for window, dilation in B_PATTERNS:
        reach = (window // (2 * dilation)) * dilation
        mult = mult + ((ao % dilation == 0) & (ao <= reach)).astype(jnp.int32)
    mult = jnp.where(jj < nband, mult, 0)
    slopes = jnp.exp2(-8.0 * jnp.arange(1, B_HEADS + 1, dtype=F32) / B_HEADS) * LOG2E
    bias = jnp.log2(jnp.maximum(mult, 1).astype(F32))[None] - slopes[:, None, None, None] * ao.astype(F32)[None]
    bias = jnp.where((mult > 0)[None], bias, NEG_BIG)
    return bias.reshape(B_HEADS // 2, 2, nband + 1, t, t)


def _attn_b(proj, vt, bias, t):
    b, s, _ = proj.shape
    nband = bias.shape[2] - 1
    npair = B_HEADS // 2
    return pl.pallas_call(
        functools.partial(_attn_b_kernel, nband=nband),
        grid=(b, npair, s // t),
        in_specs=[pl.BlockSpec((None, t, LANES), lambda bb, p, i: (bb, i, COL_B + p)),
                  pl.BlockSpec((None, s, LANES), lambda bb, p, i: (bb, 0, COL_B + npair + p)),
                  pl.BlockSpec((vt.shape[0] // npair, s), lambda bb, p, i: (p, bb)),
                  pl.BlockSpec((None,) + bias.shape[1:], lambda bb, p, i: (p, 0, 0, 0, 0))],
        out_specs=pl.BlockSpec((None, t, LANES), lambda bb, p, i: (bb, i, p)),
        out_shape=jax.ShapeDtypeStruct((b, s, npair * LANES), BF16),
        scratch_shapes=[pltpu.VMEM((2, t, LANES), BF16)] + _attn_scratch(t, t, B_DH),
        compiler_params=_cp(("parallel", "parallel", "arbitrary")),
        name="attn_dilated",
    )(proj, proj, vt, bias)


def _rope_table_kernel(pos_ref, invf_ref, c_ref, s_ref):
    ang = pos_ref[...] * invf_ref[...]
    c_ref[...] = jnp.cos(ang)
    s_ref[...] = jnp.sin(ang)


def _rope_tables(pos_col, tm):
    t = pos_col.shape[0]
    inv = ROPE_THETA ** (-jnp.arange(0, C_ROPE, 2, dtype=F32) / C_ROPE)
    invf = jnp.concatenate([jnp.zeros((C_NOPE,), F32), inv, inv,
                            jnp.zeros((LANES - C_NOPE - C_ROPE,), F32)]).reshape(1, LANES)
    spec = pl.BlockSpec((tm, LANES), lambda i: (i, 0))
    return pl.pallas_call(
        _rope_table_kernel,
        grid=(t // tm,),
        in_specs=[pl.BlockSpec((tm, 1), lambda i: (i, 0)), pl.BlockSpec((1, LANES), lambda i: (0, 0))],
        out_specs=[spec, spec],
        out_shape=[jax.ShapeDtypeStruct((t, LANES), F32)] * 2,
        compiler_params=_cp(("parallel",)),
        name="rope_tables",
    )(pos_col, invf)


def _mla_prep_kernel(cq_ref, ckv_ref, r1_ref, r2_ref, c_ref, s_ref, gq_ref, gkv_ref,
                     wq1_ref, wq2_ref, wk_ref, wvt_ref, onv_ref, q_out, k_out, vt_out):
    qn = _rms(cq_ref[...].astype(F32), gq_ref[...], 1.0 / C_Q_RANK).astype(BF16)
    kvn = _rms(ckv_ref[...].astype(F32), gkv_ref[...], 1.0 / C_KV_RANK).astype(BF16)
    q1 = jnp.dot(qn, wq1_ref[...], preferred_element_type=F32)
    q2 = jnp.dot(qn, wq2_ref[...], preferred_element_type=F32)
    k1 = jnp.dot(kvn, wk_ref[...], preferred_element_type=F32)
    vt_out[...] = (_kq(wvt_ref[...], kvn) + onv_ref[...]).astype(vt_out.dtype)
    cos = c_ref[...]
    sin = s_ref[...]
    k_rope = r1_ref[...].astype(F32) * cos + r2_ref[...].astype(F32) * sin
    scale = (C_NOPE + C_ROPE) ** -0.5 * LOG2E
    for h in range(C_HEADS):
        sl = slice(h * LANES, (h + 1) * LANES)
        q_out[:, sl] = ((q1[:, sl] * cos + q2[:, sl] * sin) * scale).astype(q_out.dtype)
        k_out[:, sl] = (k1[:, sl] + k_rope).astype(k_out.dtype)


def _mla_prep(proj2, cos, sin, gq, gkv, wq1, wq2, wk, wvt, onv, tm):
    t = proj2.shape[0]
    full = lambda a: pl.BlockSpec(a.shape, lambda i: (0, 0))
    row = lambda w: pl.BlockSpec((tm, w), lambda i: (i, 0))
    return pl.pallas_call(
        _mla_prep_kernel,
        grid=(t // tm,),
        in_specs=[pl.BlockSpec((tm, 4 * LANES), lambda i: (i, COL_CQ // 4)),
                  pl.BlockSpec((tm, 2 * LANES), lambda i: (i, COL_CKV // 2)),
                  pl.BlockSpec((tm, LANES), lambda i: (i, COL_R1)),
                  pl.BlockSpec((tm, LANES), lambda i: (i, COL_R2)),
                  row(LANES), row(LANES), full(gq), full(gkv), full(wq1), full(wq2), full(wk), full(wvt), full(onv)],
        out_specs=[row(C_HEADS * LANES), row(C_HEADS * LANES), pl.BlockSpec((wvt.shape[0], tm), lambda i: (0, i))],
        out_shape=[jax.ShapeDtypeStruct((t, C_HEADS * LANES), BF16),
                   jax.ShapeDtypeStruct((t, C_HEADS * LANES), BF16),
                   jax.ShapeDtypeStruct((wvt.shape[0], t), BF16)],
        compiler_params=_cp(("parallel",)),
        name="mla_prep",
    )(proj2, proj2, proj2, proj2, cos, sin, gq, gkv, wq1, wq2, wk, wvt, onv)


def _attn_c_kernel(q_ref, k_ref, vt_ref, o_ref, m_ref, acc_ref, s_ref, cm_ref, *, tk):
    rows = vt_ref.shape[0] // 2
    _init_state(m_ref, acc_ref)

    def scores(j, slot):
        ks = pl.multiple_of(j * tk, tk)
        for c in range(2):
            sl = slice(c * LANES, (c + 1) * LANES)
            _put_scores(_kq(k_ref[pl.ds(ks, tk), sl], q_ref[:, sl]), s_ref, cm_ref, slot, c)

    def update(j, slot):
        ks = pl.multiple_of(j * tk, tk)
        _update(lambda c: vt_ref[c * rows:(c + 1) * rows, pl.ds(ks, tk)], s_ref, cm_ref, m_ref, acc_ref, slot)

    _pipelined(k_ref.shape[0] // tk, scores, update)
    o = jnp.concatenate([_normalized(acc_ref, 0), _normalized(acc_ref, 1)], axis=0)
    o_ref[...] = o.T.astype(o_ref.dtype)


def _attn_c(q, k, vt, tq, tk):
    b, s, _ = q.shape
    npair = C_HEADS // 2
    return pl.pallas_call(
        functools.partial(_attn_c_kernel, tk=tk),
        grid=(b, npair, s // tq),
        in_specs=[pl.BlockSpec((None, tq, 2 * LANES), lambda bb, p, i: (bb, i, p)),
                  pl.BlockSpec((None, s, 2 * LANES), lambda bb, p, i: (bb, 0, p)),
                  pl.BlockSpec((vt.shape[0] // npair, s), lambda bb, p, i: (p, bb))],
        out_specs=pl.BlockSpec((None, tq, LANES), lambda bb, p, i: (bb, i, p)),
        out_shape=jax.ShapeDtypeStruct((b, s, npair * LANES), BF16),
        scratch_shapes=_attn_scratch(tq, tk, C_DV),
        compiler_params=_cp(("parallel", "parallel", "arbitrary")),
        name="attn_latent",
    )(q, k, vt)


def _merge_kernel(oa_ref, ob_ref, oc_ref, g0_ref, g1_ref, g2_ref, x_ref, wb_ref, wo_ref, o_ref):
    z = None
    for n, (o_r, g_r) in enumerate(((oa_ref, g0_ref), (ob_ref, g1_ref), (oc_ref, g2_ref))):
        br = jnp.dot(o_r[...], wb_ref[n], preferred_element_type=F32)
        gate = 1.0 / (1.0 + jnp.exp(-g_r[...].astype(F32)))
        z = gate * br if z is None else z + gate * br
    o_ref[...] = x_ref[...] + jnp.dot(z.astype(BF16), wo_ref[...], preferred_element_type=F32)


def _merge(oa, ob, oc, proj2, x2, wb, wo, tm):
    t, d = x2.shape
    bw = oa.shape[1]
    row = lambda w: pl.BlockSpec((tm, w), lambda i: (i, 0))
    gate = lambda n: pl.BlockSpec((tm, d), lambda i: (i, COL_G * LANES // d + n))
    return pl.pallas_call(
        _merge_kernel,
        grid=(t // tm,),
        in_specs=[row(bw), row(bw), row(bw), gate(0), gate(1), gate(2), row(d),
                  pl.BlockSpec(wb.shape, lambda i: (0, 0, 0)), pl.BlockSpec(wo.shape, lambda i: (0, 0))],
        out_specs=row(d),
        out_shape=jax.ShapeDtypeStruct((t, d), F32),
        compiler_params=_cp(("parallel",)),
        name="branch_merge",
    )(oa, ob, oc, proj2, proj2, proj2, x2, wb, wo)


def _cross_kernel(x_ref, g_ref, wq_ref, kbd_ref, vbd_ref, wo_ref, o_ref, *, n_mem):
    x = x_ref[...]
    h = _rms(x, g_ref[...], 1.0 / x.shape[-1]).astype(BF16)
    q = (jnp.dot(h, wq_ref[...], preferred_element_type=F32) * (X_DH ** -0.5 * LOG2E)).astype(BF16)
    s = jnp.dot(q, kbd_ref[...], preferred_element_type=F32)
    ps = []
    for hh in range(X_HEADS):
        sh = s[:, hh * n_mem:(hh + 1) * n_mem]
        p = jnp.exp2(sh - jnp.max(sh, axis=-1, keepdims=True))
        ps.append((p / jnp.sum(p, axis=-1, keepdims=True)).astype(BF16))
    o = jnp.dot(jnp.concatenate(ps, axis=1), vbd_ref[...], preferred_element_type=F32)
    o_ref[...] = x + jnp.dot(o.astype(BF16), wo_ref[...], preferred_element_type=F32)


def _cross(x3, g, wq, kbd, vbd, wo, tm):
    b, s, d = x3.shape
    n_mem = kbd.shape[2] // X_HEADS
    full = lambda a: pl.BlockSpec(a.shape, lambda bb, i: (0, 0))
    return pl.pallas_call(
        functools.partial(_cross_kernel, n_mem=n_mem),
        grid=(b, s // tm),
        in_specs=[pl.BlockSpec((None, tm, d), lambda bb, i: (bb, i, 0)), full(g), full(wq),
                  pl.BlockSpec((None,) + kbd.shape[1:], lambda bb, i: (bb, 0, 0)),
                  pl.BlockSpec((None,) + vbd.shape[1:], lambda bb, i: (bb, 0, 0)), full(wo)],
        out_specs=pl.BlockSpec((None, tm, d), lambda bb, i: (bb, i, 0)),
        out_shape=jax.ShapeDtypeStruct((b, s, d), F32),
        compiler_params=_cp(("parallel", "parallel")),
        name="cross_attn",
    )(x3, g, wq, kbd, vbd, wo)


def _block_diag_kv(kv):
    b, m, _ = kv.shape
    kv = kv.reshape(b, m, 2, X_HEADS, X_DH)
    eye = jnp.eye(X_HEADS, dtype=kv.dtype)
    kt = kv[:, :, 0].transpose(0, 2, 3, 1)
    kbd = (kt[:, :, :, None, :] * eye[None, :, None, :, None]).reshape(b, X_HEADS * X_DH, X_HEADS * m)
    vt = kv[:, :, 1].transpose(0, 2, 1, 3)
    vbd = (vt[:, :, :, None, :] * eye[None, :, None, :, None]).reshape(b, X_HEADS * m, X_HEADS * X_DH)
    return kbd, vbd


def _router_kernel(x_ref, g_ref, w_ref, b_ref, h_out, r_out):
    x = x_ref[...]
    h = _rms(x, g_ref[...], 1.0 / x.shape[-1])
    h_out[...] = h.astype(h_out.dtype)
    logits = jnp.dot(h, w_ref[...], preferred_element_type=F32, precision=lax.Precision.HIGHEST) + b_ref[...]
    lane = lax.broadcasted_iota(jnp.int32, logits.shape, 1)
    lane_f = lane.astype(F32)
    big = jnp.float32(4 * LANES)

    def top(vals, mask):
        mv = jnp.max(jnp.where(mask, vals, -jnp.inf), axis=-1, keepdims=True)
        idx = jnp.min(jnp.where(mask & (vals == mv), lane_f, big), axis=-1, keepdims=True)
        return mv, idx

    g_mask = lane < N_GROUPS
    g_max, g_idx = top(logits, g_mask)
    p_g = 1.0 / jnp.sum(jnp.where(g_mask, jnp.exp(logits - g_max), 0.0), axis=-1, keepdims=True)
    first = N_GROUPS + g_idx * EXPERTS_PER_GROUP
    e_mask = (lane_f >= first) & (lane_f < first + EXPERTS_PER_GROUP)
    v0, i0 = top(logits, e_mask)
    v1, i1 = top(logits, e_mask & (lane_f != i0))
    e1 = jnp.exp(v1 - v0)
    w0 = p_g / (1.0 + e1)
    w1 = p_g * e1 / (1.0 + e1)
    out = jnp.where(lane == 0, i0 - N_GROUPS, 0.0)
    out = jnp.where(lane == 1, i1 - N_GROUPS, out)
    out = jnp.where(lane == 2, w0, out)
    out = jnp.where(lane == 3, w1, out)
    r_out[...] = out


def _router(x2, g, w, bias, tm):
    t, d = x2.shape
    full = lambda a: pl.BlockSpec(a.shape, lambda i: (0, 0))
    return pl.pallas_call(
        _router_kernel,
        grid=(t // tm,),
        in_specs=[pl.BlockSpec((tm, d), lambda i: (i, 0)), full(g), full(w), full(bias)],
        out_specs=[pl.BlockSpec((tm, d), lambda i: (i, 0)), pl.BlockSpec((tm, LANES), lambda i: (i, 0))],
        out_shape=[jax.ShapeDtypeStruct((t, d), BF16), jax.ShapeDtypeStruct((t, LANES), F32)],
        compiler_params=_cp(("parallel",)),
        name="moe_router",
    )(x2, g, w, bias)


def _expert_kernel(blk_e_ref, n_used_ref, x_ref, w13_ref, w2_ref, o_ref):
    del blk_e_ref
    used = pl.program_id(0) < n_used_ref[0]

    @pl.when(used)
    def _():
        hid = jnp.dot(x_ref[...], w13_ref[...], preferred_element_type=F32)
        de = hid.shape[1] // 2
        a = hid[:, :de]
        act = (a / (1.0 + jnp.exp(-a))) * hid[:, de:]
        o_ref[...] = jnp.dot(act.astype(BF16), w2_ref[...], preferred_element_type=F32).astype(o_ref.dtype)

    @pl.when(jnp.logical_not(used))
    def _():
        o_ref[...] = jnp.zeros(o_ref.shape, o_ref.dtype)


def _experts(blk_e, n_used, xr, w13, w2, rows_per_block):
    rows, d = xr.shape
    grid_spec = pltpu.PrefetchScalarGridSpec(
        num_scalar_prefetch=2,
        grid=(rows // rows_per_block,),
        in_specs=[pl.BlockSpec((rows_per_block, d), lambda i, be, nu: (i, 0)),
                  pl.BlockSpec((None,) + w13.shape[1:], lambda i, be, nu: (be[i], 0, 0)),
                  pl.BlockSpec((None,) + w2.shape[1:], lambda i, be, nu: (be[i], 0, 0))],
        out_specs=pl.BlockSpec((rows_per_block, d), lambda i, be, nu: (i, 0)),
    )
    return pl.pallas_call(
        _expert_kernel,
        grid_spec=grid_spec,
        out_shape=jax.ShapeDtypeStruct((rows, d), BF16),
        compiler_params=_cp(("arbitrary",)),
        name="moe_experts",
    )(blk_e, n_used, xr, w13, w2)


def _combine_kernel(x_ref, y0_ref, y1_ref, r_ref, g_ref, o_ref, *, final_norm):
    r = r_ref[...]
    y = x_ref[...] + r[:, 2:3] * y0_ref[...].astype(F32) + r[:, 3:4] * y1_ref[...].astype(F32)
    if final_norm:
        y = _rms(y, g_ref[...], 1.0 / y.shape[-1])
    o_ref[...] = y


def _combine(x2, y0, y1, route, g, final_norm, tm):
    t, d = x2.shape
    row = lambda w: pl.BlockSpec((tm, w), lambda i: (i, 0))
    return pl.pallas_call(
        functools.partial(_combine_kernel, final_norm=final_norm),
        grid=(t // tm,),
        in_specs=[row(d), row(d), row(d), row(LANES), pl.BlockSpec((1, d), lambda i: (0, 0))],
        out_specs=row(d),
        out_shape=jax.ShapeDtypeStruct((t, d), F32),
        compiler_params=_cp(("parallel",)),
        name="moe_combine",
    )(x2, y0, y1, route, g)


def _dispatch(route, rows_per_block):
    t = route.shape[0]
    eid = route[:, :TOP_K].astype(jnp.int32).reshape(-1)
    n = eid.shape[0]
    order = jnp.argsort(eid).astype(jnp.int32)
    rank = jnp.argsort(order).astype(jnp.int32)
    experts = jnp.arange(N_EXPERTS, dtype=jnp.int32)
    counts = jnp.sum((eid[:, None] == experts[None, :]).astype(jnp.int32), axis=0)
    start = jnp.cumsum(counts) - counts
    padded = (counts + rows_per_block - 1) // rows_per_block * rows_per_block
    pend = jnp.cumsum(padded)
    pstart = pend - padded
    dest = (rank + (pstart - start)[eid]).reshape(t, TOP_K)
    n_blocks = n // rows_per_block + N_EXPERTS
    blk_first = jnp.arange(n_blocks, dtype=jnp.int32) * rows_per_block
    blk_e = jnp.minimum(jnp.sum((pend[None, :] <= blk_first[:, None]).astype(jnp.int32), axis=1), N_EXPERTS - 1)
    off = (blk_first - pstart[blk_e])[:, None] + jnp.arange(rows_per_block, dtype=jnp.int32)[None, :]
    src = jnp.clip(start[blk_e][:, None] + off, 0, n - 1)
    row_tok = jnp.where(off < counts[blk_e][:, None], order[src] // TOP_K, 0).reshape(-1)
    n_used = (pend[-1] // rows_per_block).astype(jnp.int32).reshape(1)
    return row_tok, dest, blk_e, n_used


def _rot_cols(w):
    half = w.shape[-1] // 2
    return jnp.concatenate([-w[..., half:], w[..., :half]], axis=-1)


def _pack_w_in(w):
    d = w.shape[0]
    blk = A_HEADS * 2 * A_DH
    n_ab = 6 * blk
    cq = w[:, n_ab:n_ab + C_Q_RANK]
    ckv = w[:, n_ab + C_Q_RANK:n_ab + C_Q_RANK + C_KV_RANK]
    ckr = w[:, n_ab + C_Q_RANK + C_KV_RANK:n_ab + C_Q_RANK + C_KV_RANK + C_ROPE]
    gates = w[:, n_ab + C_Q_RANK + C_KV_RANK + C_ROPE:]
    z = lambda n: jnp.zeros((d, n), w.dtype)
    tail = LANES - C_NOPE - C_ROPE
    packed = jnp.concatenate([w[:, :2 * blk], w[:, 3 * blk:5 * blk], cq, z(LANES), ckv,
                              z(C_NOPE), ckr, z(tail), z(C_NOPE), _rot_cols(ckr), z(tail), gates], axis=1)
    return packed.astype(BF16), w[:, 2 * blk:3 * blk], w[:, 5 * blk:6 * blk]


def _pack_w_uq(w):
    wq = w.reshape(C_Q_RANK, C_HEADS, C_NOPE + C_ROPE)
    pad_rows = 4 * LANES - C_Q_RANK
    tail = LANES - C_NOPE - C_ROPE
    q1 = jnp.pad(wq, ((0, pad_rows), (0, 0), (0, tail))).reshape(4 * LANES, C_HEADS * LANES)
    q2 = jnp.pad(_rot_cols(wq[:, :, C_NOPE:]), ((0, pad_rows), (0, 0), (C_NOPE, tail))).reshape(4 * LANES, C_HEADS * LANES)
    return q1.astype(BF16), q2.astype(BF16)


def _pack_w_ukv(w):
    wkv = w.reshape(C_KV_RANK, C_HEADS, C_NOPE + C_DV)
    wk = jnp.pad(wkv[:, :, :C_NOPE], ((0, 0), (0, 0), (0, LANES - C_NOPE))).reshape(C_KV_RANK, C_HEADS * LANES)
    wv = wkv[:, :, C_NOPE:].reshape(C_KV_RANK, C_HEADS * C_DV)
    return wk.astype(BF16), wv


def kernel(x, mem, positions, mix_norm_g, w_in, diff_lambda, diff_subln_g, mla_q_norm_g, w_uq, mla_kv_norm_g, w_ukv, w_branch, w_out, cross_norm_g, mem_norm_g, w_xq, w_xkv, w_xo, ffn_norm_g, w_group, b_group, w_router, b_router, w1, w3, w2, final_norm_g):
    b, s, d = x.shape
    depth = w_in.shape[0]
    t = b * s
    n_mem = mem.shape[1]
    tl = _tiles(s)
    assert PROJ_COLS == COL_G * LANES + N_BRANCHES * d and s % (2 * tl["t_b"]) == 0

    pos_f = positions.astype(F32)
    posq_row = pos_f.reshape(b, 1, s)
    posk_rep = jnp.broadcast_to(pos_f[:, :, None], (b, s, LANES))
    cos, sin = _rope_tables(pos_f.reshape(t, 1), tl["tm_tok"])
    slopes_a2 = jnp.exp2(-8.0 * jnp.arange(1, A_HEADS + 1, dtype=F32) / A_HEADS) * LOG2E
    reach = max((w // (2 * dl)) * dl for w, dl in B_PATTERNS)
    b_bias = _b_bias_tables(tl["t_b"], 2 * (-(-reach // tl["t_b"])) + 1)
    mem2 = mem.reshape(b * n_mem, d)

    x2 = x.reshape(t, d)
    for l in range(depth):
        w_main, w_av, w_bv = _pack_w_in(w_in[l])
        proj2, vt_a, vt_b = _in_proj(x2, mix_norm_g[l], w_main, *_vt_weights(w_av, 2 * A_DH), *_vt_weights(w_bv, B_DH),
                                     tl["tm_proj"], tl["tn_proj"])
        proj3 = proj2.reshape(b, s, PROJ_COLS)
        lq = diff_lambda[l].astype(F32)
        lam_init = 0.8 - 0.6 * math.exp(-0.3 * l)
        lam = (jnp.exp(jnp.sum(lq[0] * lq[1])) - jnp.exp(jnp.sum(lq[2] * lq[3])) + lam_init).reshape(1)
        oa = _attn_a(proj3, vt_a, posq_row, posk_rep, lam, slopes_a2, diff_subln_g[l], 1.0 - lam_init,
                     tl["tq_a"], tl["tk_a"])
        ob = _attn_b(proj3, vt_b, b_bias, tl["t_b"])
        wq1, wq2 = _pack_w_uq(w_uq[l])
        wk, wv = _pack_w_ukv(w_ukv[l])
        gq = jnp.pad(mla_q_norm_g[l], (0, 4 * LANES - C_Q_RANK)).reshape(1, 4 * LANES)
        qc, kc, vt_c = _mla_prep(proj2, cos, sin, gq, mla_kv_norm_g[l].reshape(1, C_KV_RANK),
                                 wq1, wq2, wk, *_vt_weights(wv, C_DV), tl["tm_tok"])
        oc = _attn_c(qc.reshape(b, s, -1), kc.reshape(b, s, -1), vt_c, tl["tq_c"], tl["tk_c"])
        x2 = _merge(oa.reshape(t, -1), ob.reshape(t, -1), oc.reshape(t, -1), proj2, x2,
                    w_branch[l].astype(BF16), w_out[l].astype(BF16), tl["tm_tok"])
        kv = _norm_matmul(mem2, mem_norm_g[l], w_xkv[l].astype(BF16), min(1024, b * n_mem), w_xkv.shape[2])
        kbd, vbd = _block_diag_kv(kv.reshape(b, n_mem, -1))
        x2 = _cross(x2.reshape(b, s, d), cross_norm_g[l].reshape(1, d), w_xq[l].astype(BF16), kbd, vbd,
                    w_xo[l].astype(BF16), tl["tm_tok"]).reshape(t, d)
        w_r = jnp.pad(jnp.concatenate([w_group[l], w_router[l]], axis=1), ((0, 0), (0, LANES - N_GROUPS - N_EXPERTS)))
        b_r = jnp.pad(jnp.concatenate([b_group[l], b_router[l]]), (0, LANES - N_GROUPS - N_EXPERTS)).reshape(1, LANES)
        h, route = _router(x2, ffn_norm_g[l].reshape(1, d), w_r, b_r, tl["tm_tok"])
        row_tok, dest, blk_e, n_used = _dispatch(route, tl["moe_rows"])
        w13 = jnp.concatenate([w1[l], w3[l]], axis=2).astype(BF16)
        yr = _experts(blk_e, n_used, h[row_tok], w13, w2[l].astype(BF16), tl["moe_rows"])
        x2 = _combine(x2, yr[dest[:, 0]], yr[dest[:, 1]], route, final_norm_g.reshape(1, d),
                      l == depth - 1, tl["tm_tok"])
    return x2.reshape(b, s, d)
```

```python
import functools
import math

import jax
import jax.numpy as jnp
from jax import lax
from jax.experimental import pallas as pl
from jax.experimental.pallas import tpu as pltpu

F32 = jnp.float32
BF16 = jnp.bfloat16

LANES = 128
NORM_EPS = 1e-6
LOG2E = math.log2(math.e)
NEG_BIG = -1e30

A_HEADS, A_DH = 4, 64
B_HEADS, B_DH = 8, 64
B_PATTERNS = ((128, 1), (512, 4), (2048, 16))
C_HEADS, C_Q_RANK, C_KV_RANK, C_NOPE, C_ROPE, C_DV = 8, 384, 256, 64, 32, 64
ROPE_THETA = 10000.0
N_BRANCHES = 3
X_HEADS, X_DH = 4, 64
N_GROUPS, EXPERTS_PER_GROUP, TOP_K = 4, 8, 2
N_EXPERTS = N_GROUPS * EXPERTS_PER_GROUP

COL_A = 0
COL_B = 8
COL_CQ = 16
COL_CKV = 20
COL_R1 = 22
COL_R2 = 23
COL_G = 24
PROJ_COLS = 48 * LANES

VMEM_LIMIT = 48 * 1024 * 1024


def _tiles(seq):
    return dict(
        tm_proj=min(1024, seq), tn_proj=1024,
        tm_tok=min(512, seq),
        tq_a=min(512, seq), tk_a=min(512, seq),
        t_b=min(512, seq),
        tq_c=min(512, seq), tk_c=min(512, seq),
        moe_rows=512,
    )


def _cp(sem):
    return pltpu.CompilerParams(dimension_semantics=sem, vmem_limit_bytes=VMEM_LIMIT)


def _rms(x, g, inv_n):
    ms = jnp.sum(x * x, axis=-1, keepdims=True) * inv_n
    return x * lax.rsqrt(ms + NORM_EPS) * g


def _norm_matmul_kernel(x_ref, g_ref, w_ref, o_ref, h_ref, *, inv_n):
    @pl.when(pl.program_id(1) == 0)
    def _():
        h_ref[...] = _rms(x_ref[...].astype(F32), g_ref[...], inv_n).astype(BF16)

    o_ref[...] = jnp.dot(h_ref[...], w_ref[...], preferred_element_type=F32).astype(o_ref.dtype)


def _norm_matmul(x, g, w, tm, tn, out_dtype=BF16):
    m, k = x.shape
    n = w.shape[1]
    return pl.pallas_call(
        functools.partial(_norm_matmul_kernel, inv_n=1.0 / k),
        grid=(m // tm, n // tn),
        in_specs=[pl.BlockSpec((tm, k), lambda i, j: (i, 0)),
                  pl.BlockSpec((1, k), lambda i, j: (0, 0)),
                  pl.BlockSpec((k, tn), lambda i, j: (0, j))],
        out_specs=pl.BlockSpec((tm, tn), lambda i, j: (i, j)),
        out_shape=jax.ShapeDtypeStruct((m, n), out_dtype),
        scratch_shapes=[pltpu.VMEM((tm, k), BF16)],
        compiler_params=_cp(("parallel", "arbitrary")),
        name="norm_matmul",
    )(x, g.reshape(1, k), w)


def _in_proj_kernel(x_ref, g_ref, w_ref, wta_ref, wtb_ref, ona_ref, onb_ref, o_ref, vta_ref, vtb_ref, h_ref, *, inv_n):
    @pl.when(pl.program_id(1) == 0)
    def _():
        h = _rms(x_ref[...].astype(F32), g_ref[...], inv_n).astype(BF16)
        h_ref[...] = h
        vta_ref[...] = (_kq(wta_ref[...], h) + ona_ref[...]).astype(vta_ref.dtype)
        vtb_ref[...] = (_kq(wtb_ref[...], h) + onb_ref[...]).astype(vtb_ref.dtype)

    o_ref[...] = jnp.dot(h_ref[...], w_ref[...], preferred_element_type=F32).astype(o_ref.dtype)


def _in_proj(x, g, w, wta, ona, wtb, onb, tm, tn):
    m, k = x.shape
    n = w.shape[1]
    full = lambda a: pl.BlockSpec(a.shape, lambda i, j: (0, 0))
    return pl.pallas_call(
        functools.partial(_in_proj_kernel, inv_n=1.0 / k),
        grid=(m // tm, n // tn),
        in_specs=[pl.BlockSpec((tm, k), lambda i, j: (i, 0)),
                  pl.BlockSpec((1, k), lambda i, j: (0, 0)),
                  pl.BlockSpec((k, tn), lambda i, j: (0, j)),
                  full(wta), full(wtb), full(ona), full(onb)],
        out_specs=[pl.BlockSpec((tm, tn), lambda i, j: (i, j)),
                   pl.BlockSpec((wta.shape[0], tm), lambda i, j: (0, i)),
                   pl.BlockSpec((wtb.shape[0], tm), lambda i, j: (0, i))],
        out_shape=[jax.ShapeDtypeStruct((m, n), BF16),
                   jax.ShapeDtypeStruct((wta.shape[0], m), BF16),
                   jax.ShapeDtypeStruct((wtb.shape[0], m), BF16)],
        scratch_shapes=[pltpu.VMEM((tm, k), BF16)],
        compiler_params=_cp(("parallel", "arbitrary")),
        name="in_proj",
    )(x, g.reshape(1, k), w, wta, wtb, ona, onb)


ONES_ROWS = 16


def _vt_weights(wv, dv):
    k, w = wv.shape
    wt = jnp.pad(wv.T.reshape(w // dv, dv, k), ((0, 0), (0, ONES_ROWS), (0, 0))).reshape(-1, k)
    ones = jnp.zeros((w // dv, dv + ONES_ROWS, 1), F32).at[:, dv, 0].set(1.0).reshape(-1, 1)
    return wt.astype(BF16), ones


def _split_q(q_ref, qs_ref, scale):
    q = q_ref[...].astype(F32) * scale
    lane = lax.broadcasted_iota(jnp.int32, q.shape, 1)
    qs_ref[0] = jnp.where(lane < LANES // 2, q, 0.0).astype(BF16)
    qs_ref[1] = jnp.where(lane >= LANES // 2, q, 0.0).astype(BF16)


def _kq(k, q):
    return lax.dot_general(k, q, (((1,), (1,)), ((), ())), preferred_element_type=F32)


def _put_scores(s, s_ref, cm_ref, slot, c):
    s_ref[slot, c] = s
    cm_ref[slot, c] = jnp.max(s, axis=0, keepdims=True)


def _update(vt_of, s_ref, cm_ref, m_ref, acc_ref, slot):
    for c in range(2):
        m_prev = m_ref[c]
        m_new = jnp.maximum(m_prev, cm_ref[slot, c])
        alpha = jnp.exp2(m_prev - m_new)
        p = jnp.exp2(s_ref[slot, c] - m_new).astype(BF16)
        acc_ref[c] = alpha * acc_ref[c] + jnp.dot(vt_of(c), p, preferred_element_type=F32)
        m_ref[c] = m_new


def _init_state(m_ref, acc_ref):
    m_ref[...] = jnp.full(m_ref.shape, NEG_BIG, F32)
    acc_ref[...] = jnp.zeros(acc_ref.shape, F32)


def _pipelined(n, scores, update, unroll):
    scores(0, 0)

    def body(jj, carry):
        j = unroll * jj
        for u in range(unroll):
            scores(j + u + 1, (u + 1) % 2)
            update(j + u, u % 2)
        return carry

    looped = (n - 1) // unroll
    lax.fori_loop(0, looped, body, 0)
    for j in range(unroll * looped, n):
        if j + 1 < n:
            scores(j + 1, (j + 1) % 2)
        update(j, j % 2)


def _pipelined_list(n, chunk, scores, update):
    scores(chunk(0), 0)

    def body(unroll, first):
        def run(jj, carry):
            t = first + unroll * jj
            for u in range(unroll):
                scores(chunk(t + u + 1), (u + 1) % 2)
                update(chunk(t + u), u % 2)
            return carry
        return run

    quads = (n - 2) // 4
    lax.fori_loop(0, quads, body(4, 0), 0)
    lax.fori_loop(0, (n - 4 * quads) // 2 - 1, body(2, 4 * quads), 0)
    scores(chunk(n - 1), 1)
    update(chunk(n - 2), 0)
    update(chunk(n - 1), 1)


def _normalized(acc_ref, c):
    dv = acc_ref.shape[1] - ONES_ROWS
    return acc_ref[c, :dv, :] / acc_ref[c, dv:dv + 1, :]


def _attn_scratch(tq, tk, dv):
    return [pltpu.VMEM((2, 1, tq), F32), pltpu.VMEM((2, dv + ONES_ROWS, tq), F32),
            pltpu.VMEM((2, 2, tk, tq), F32), pltpu.VMEM((2, 2, 1, tq), F32)]


SKIP_MARGIN = 150.0


def _block_norms_kernel(qk_ref, sel_ref, o_ref):
    x = qk_ref[...].astype(F32)
    lane = lax.broadcasted_iota(jnp.int32, x.shape, 1)
    x = jnp.where(lane < x.shape[1] // 2, (x * (A_DH ** -0.5 * LOG2E)).astype(BF16).astype(F32), x)
    sums = jnp.dot(x * x, sel_ref[...], preferred_element_type=F32)
    o_ref[...] = jnp.sqrt(jnp.max(sums, axis=0, keepdims=True))


def _block_norms(proj3, t):
    b, s, _ = proj3.shape
    w = 2 * A_HEADS * 2 * A_DH
    sel = (jnp.arange(w)[:, None] // A_DH == jnp.arange(LANES)[None, :]).astype(BF16)
    return pl.pallas_call(
        _block_norms_kernel,
        grid=(b, s // t),
        in_specs=[pl.BlockSpec((None, t, w), lambda bb, i: (bb, i, COL_A * LANES // w)),
                  pl.BlockSpec((w, LANES), lambda bb, i: (0, 0))],
        out_specs=pl.BlockSpec((None, None, 1, LANES), lambda bb, i: (bb, i, 0, 0)),
        out_shape=jax.ShapeDtypeStruct((b, s // t, 1, LANES), F32),
        compiler_params=_cp(("parallel", "parallel")),
        name="block_norms",
    )(proj3, sel)[:, :, 0, :]


def _a_chunk_lists(proj3, pos_f, slopes2, tq, tk):
    assert tq == tk
    b, s, _ = proj3.shape
    norms = _block_norms(proj3, tq)[:, :, :4 * A_HEADS].reshape(b, s // tq, 2, A_HEADS, 2).max(axis=-1)
    qn = norms[:, :, 0].transpose(0, 2, 1)
    kn = norms[:, :, 1].transpose(0, 2, 1)
    qk = 1.01 * qn[:, :, :, None] * kn[:, :, None, :] + 1.0
    pq = pos_f.reshape(b, s // tq, tq)
    pk = pos_f.reshape(b, s // tk, tk)
    qlo, qhi = jnp.min(pq, -1)[:, :, None], jnp.max(pq, -1)[:, :, None]
    klo, khi = jnp.min(pk, -1)[:, None, :], jnp.max(pk, -1)[:, None, :]
    dmin = jnp.maximum(jnp.maximum(klo - qhi, qlo - khi), 0.0)
    dmax = jnp.maximum(khi - qlo, qhi - klo)
    sl = slopes2[None, :, None, None]
    lower = -qk - sl * dmax[:, None]
    upper = qk - sl * dmin[:, None]
    nk = s // tk
    j = jnp.arange(nk, dtype=jnp.int32)
    cover = jnp.max(jnp.min(jnp.abs(pq[:, :, :, None] - pk[:, :, None, :]), axis=-1), axis=-1)
    lower = jnp.where(j[:, None] == j[None, :], jnp.maximum(lower, -qk - sl * cover[:, None, :, None]), lower)
    first = jnp.argmax(lower, axis=-1).astype(jnp.int32)[..., None]
    keep = (upper >= jnp.max(lower, axis=-1, keepdims=True) - SKIP_MARGIN) | (j == first)
    lst = jnp.argsort(jnp.where(j == first, -1, jnp.where(keep, j, nk + j)), axis=-1).astype(jnp.int32)
    cnt = jnp.sum(keep.astype(jnp.int32), axis=-1)
    return lst.reshape(-1), (cnt + cnt % 2).reshape(-1)


def _attn_a_kernel(lst_ref, cnt_ref, lam_ref, slope_ref, q_ref, k_ref, vt_ref, pq_ref, pk_ref, g_ref, o_ref,
                   qs_ref, m_ref, acc_ref, s_ref, cm_ref, *, tk, post_scale):
    tq = q_ref.shape[0]
    nk = k_ref.shape[0] // tk
    blk = (pl.program_id(0) * pl.num_programs(1) + pl.program_id(1)) * pl.num_programs(2) + pl.program_id(2)
    sl2 = slope_ref[pl.program_id(1)]
    _split_q(q_ref, qs_ref, A_DH ** -0.5 * LOG2E)
    _init_state(m_ref, acc_ref)
    pq = pq_ref[...] * sl2

    def scores(j, slot):
        ks = pl.multiple_of(j * tk, tk)
        k = k_ref[pl.ds(ks, tk), :]
        pk = pk_ref[pl.ds(ks, tk), :] * sl2
        bias = jnp.abs(jnp.concatenate([pk] * (tq // LANES), axis=1) - pq)
        for c in range(2):
            _put_scores(_kq(k, qs_ref[c]) - bias, s_ref, cm_ref, slot, c)

    def update(j, slot):
        ks = pl.multiple_of(j * tk, tk)
        _update(lambda c: vt_ref[:, pl.ds(ks, tk)], s_ref, cm_ref, m_ref, acc_ref, slot)

    _pipelined_list(cnt_ref[blk], lambda t: lst_ref[blk * nk + t], scores, update)
    o = (_normalized(acc_ref, 0) - lam_ref[0] * _normalized(acc_ref, 1)).T
    o_ref[...] = (_rms(o, g_ref[...], 1.0 / LANES) * post_scale).astype(o_ref.dtype)


def _attn_a(proj, vt, posq_row, posk_rep, lam, slopes2, g, post_scale, tq, tk):
    b, s, _ = proj.shape
    lst, cnt = _a_chunk_lists(proj, posq_row.reshape(b, s), slopes2, tq, tk)
    smem = pl.BlockSpec(memory_space=pltpu.SMEM)
    grid_spec = pltpu.PrefetchScalarGridSpec(
        num_scalar_prefetch=2,
        grid=(b, A_HEADS, s // tq),
        in_specs=[smem, smem,
                  pl.BlockSpec((None, tq, LANES), lambda bb, h, i, *_: (bb, i, COL_A + h)),
                  pl.BlockSpec((None, s, LANES), lambda bb, h, i, *_: (bb, 0, COL_A + A_HEADS + h)),
                  pl.BlockSpec((vt.shape[0] // A_HEADS, s), lambda bb, h, i, *_: (h, bb)),
                  pl.BlockSpec((None, 1, tq), lambda bb, h, i, *_: (bb, 0, i)),
                  pl.BlockSpec((None, s, LANES), lambda bb, h, i, *_: (bb, 0, 0)),
                  pl.BlockSpec((1, LANES), lambda bb, h, i, *_: (0, 0))],
        out_specs=pl.BlockSpec((None, tq, LANES), lambda bb, h, i, *_: (bb, i, h)),
        scratch_shapes=[pltpu.VMEM((2, tq, LANES), BF16)] + _attn_scratch(tq, tk, LANES),
    )
    return pl.pallas_call(
        functools.partial(_attn_a_kernel, tk=tk, post_scale=post_scale),
        grid_spec=grid_spec,
        out_shape=jax.ShapeDtypeStruct((b, s, A_HEADS * LANES), BF16),
        compiler_params=_cp(("parallel", "parallel", "arbitrary")),
        name="attn_diff",
    )(lst, cnt, lam, slopes2, proj, proj, vt, posq_row, posk_rep, g.reshape(1, LANES))


def _attn_b_kernel(q_ref, k_ref, vt_ref, bias_ref, o_ref, qs_ref, m_ref, acc_ref, s_ref, cm_ref, *, nband):
    t = q_ref.shape[0]
    nkb = k_ref.shape[0] // t
    rows = vt_ref.shape[0] // 2
    i = pl.program_id(2)
    half = nband // 2
    _split_q(q_ref, qs_ref, B_DH ** -0.5 * LOG2E)
    _init_state(m_ref, acc_ref)

    def start(jj):
        return pl.multiple_of(jnp.clip(i + jj - half, 0, nkb - 1) * t, t)

    def scores(jj, slot):
        kb = i + jj - half
        table = jnp.where((kb >= 0) & (kb < nkb), jj, nband)
        k = k_ref[pl.ds(start(jj), t), :]
        for c in range(2):
            _put_scores(_kq(k, qs_ref[c]) + bias_ref[c, table], s_ref, cm_ref, slot, c)

    def update(jj, slot):
        ks = start(jj)
        _update(lambda c: vt_ref[c * rows:(c + 1) * rows, pl.ds(ks, t)], s_ref, cm_ref, m_ref, acc_ref, slot)

    _pipelined(nband, scores, update, unroll=2)
    o = jnp.concatenate([_normalized(acc_ref, 0), _normalized(acc_ref, 1)], axis=0)
    o_ref[...] = o.T.astype(o_ref.dtype)


def _b_bias_tables(t, nband):
    half = nband // 2
    r = jnp.arange(t, dtype=jnp.int32)[None, :, None]
    c = jnp.arange(t, dtype=jnp.int32)[None, None, :]
    jj = jnp.arange(nband + 1, dtype=jnp.int32)[:, None, None]
    ao = jnp.abs((jj - half) * t + r - c)
    mult = jnp.zeros(ao.shape, jnp.int32)
    for window, dilation in B_PATTERNS:
        reach = (window // (2 * dilation)) * dilation
        mult = mult + ((ao % dilation == 0) & (ao <= reach)).astype(jnp.int32)
    mult = jnp.where(jj < nband, mult, 0)
    slopes = jnp.exp2(-8.0 * jnp.arange(1, B_HEADS + 1, dtype=F32) / B_HEADS) * LOG2E
    bias = jnp.log2(jnp.maximum(mult, 1).astype(F32))[None] - slopes[:, None, None, None] * ao.astype(F32)[None]
    bias = jnp.where((mult > 0)[None], bias, NEG_BIG)
    return bias.reshape(B_HEADS // 2, 2, nband + 1, t, t)


def _attn_b(proj, vt, bias, t):
    b, s, _ = proj.shape
    nband = bias.shape[2] - 1
    npair = B_HEADS // 2
    return pl.pallas_call(
        functools.partial(_attn_b_kernel, nband=nband),
        grid=(b, npair, s // t),
        in_specs=[pl.BlockSpec((None, t, LANES), lambda bb, p, i: (bb, i, COL_B + p)),
                  pl.BlockSpec((None, s, LANES), lambda bb, p, i: (bb, 0, COL_B + npair + p)),
                  pl.BlockSpec((vt.shape[0] // npair, s), lambda bb, p, i: (p, bb)),
                  pl.BlockSpec((None,) + bias.shape[1:], lambda bb, p, i: (p, 0, 0, 0, 0))],
        out_specs=pl.BlockSpec((None, t, LANES), lambda bb, p, i: (bb, i, p)),
        out_shape=jax.ShapeDtypeStruct((b, s, npair * LANES), BF16),
        scratch_shapes=[pltpu.VMEM((2, t, LANES), BF16)] + _attn_scratch(t, t, B_DH),
        compiler_params=_cp(("parallel", "parallel", "arbitrary")),
        name="attn_dilated",
    )(proj, proj, vt, bias)


def _rope_table_kernel(pos_ref, invf_ref, c_ref, s_ref):
    ang = pos_ref[...] * invf_ref[...]
    c_ref[...] = jnp.cos(ang)
    s_ref[...] = jnp.sin(ang)


def _rope_tables(pos_col, tm):
    t = pos_col.shape[0]
    inv = ROPE_THETA ** (-jnp.arange(0, C_ROPE, 2, dtype=F32) / C_ROPE)
    invf = jnp.concatenate([jnp.zeros((C_NOPE,), F32), inv, inv,
                            jnp.zeros((LANES - C_NOPE - C_ROPE,), F32)]).reshape(1, LANES)
    spec = pl.BlockSpec((tm, LANES), lambda i: (i, 0))
    return pl.pallas_call(
        _rope_table_kernel,
        grid=(t // tm,),
        in_specs=[pl.BlockSpec((tm, 1), lambda i: (i, 0)), pl.BlockSpec((1, LANES), lambda i: (0, 0))],
        out_specs=[spec, spec],
        out_shape=[jax.ShapeDtypeStruct((t, LANES), F32)] * 2,
        compiler_params=_cp(("parallel",)),
        name="rope_tables",
    )(pos_col, invf)


def _mla_prep_kernel(cq_ref, ckv_ref, r1_ref, r2_ref, c_ref, s_ref, gq_ref, gkv_ref,
                     wq1_ref, wq2_ref, wk_ref, wvt_ref, onv_ref, q_out, k_out, vt_out):
    qn = _rms(cq_ref[...].astype(F32), gq_ref[...], 1.0 / C_Q_RANK).astype(BF16)
    kvn = _rms(ckv_ref[...].astype(F32), gkv_ref[...], 1.0 / C_KV_RANK).astype(BF16)
    q1 = jnp.dot(qn, wq1_ref[...], preferred_element_type=F32)
    q2 = jnp.dot(qn, wq2_ref[...], preferred_element_type=F32)
    k1 = jnp.dot(kvn, wk_ref[...], preferred_element_type=F32)
    vt_out[...] = (_kq(wvt_ref[...], kvn) + onv_ref[...]).astype(vt_out.dtype)
    cos = c_ref[...]
    sin = s_ref[...]
    k_rope = r1_ref[...].astype(F32) * cos + r2_ref[...].astype(F32) * sin
    scale = (C_NOPE + C_ROPE) ** -0.5 * LOG2E
    for h in range(C_HEADS):
        sl = slice(h * LANES, (h + 1) * LANES)
        q_out[:, sl] = ((q1[:, sl] * cos + q2[:, sl] * sin) * scale).astype(q_out.dtype)
        k_out[:, sl] = (k1[:, sl] + k_rope).astype(k_out.dtype)


def _mla_prep(proj2, cos, sin, gq, gkv, wq1, wq2, wk, wvt, onv, tm):
    t = proj2.shape[0]
    full = lambda a: pl.BlockSpec(a.shape, lambda i: (0, 0))
    row = lambda w: pl.BlockSpec((tm, w), lambda i: (i, 0))
    return pl.pallas_call(
        _mla_prep_kernel,
        grid=(t // tm,),
        in_specs=[pl.BlockSpec((tm, 4 * LANES), lambda i: (i, COL_CQ // 4)),
                  pl.BlockSpec((tm, 2 * LANES), lambda i: (i, COL_CKV // 2)),
                  pl.BlockSpec((tm, LANES), lambda i: (i, COL_R1)),
                  pl.BlockSpec((tm, LANES), lambda i: (i, COL_R2)),
                  row(LANES), row(LANES), full(gq), full(gkv), full(wq1), full(wq2), full(wk), full(wvt), full(onv)],
        out_specs=[row(C_HEADS * LANES), row(C_HEADS * LANES), pl.BlockSpec((wvt.shape[0], tm), lambda i: (0, i))],
        out_shape=[jax.ShapeDtypeStruct((t, C_HEADS * LANES), BF16),
                   jax.ShapeDtypeStruct((t, C_HEADS * LANES), BF16),
                   jax.ShapeDtypeStruct((wvt.shape[0], t), BF16)],
        compiler_params=_cp(("parallel",)),
        name="mla_prep",
    )(proj2, proj2, proj2, proj2, cos, sin, gq, gkv, wq1, wq2, wk, wvt, onv)


def _attn_c_kernel(q_ref, k_ref, vt_ref, o_ref, m_ref, acc_ref, s_ref, cm_ref, *, tk):
    rows = vt_ref.shape[0] // 2
    _init_state(m_ref, acc_ref)

    def scores(j, slot):
        ks = pl.multiple_of(j * tk, tk)
        for c in range(2):
            sl = slice(c * LANES, (c + 1) * LANES)
            _put_scores(_kq(k_ref[pl.ds(ks, tk), sl], q_ref[:, sl]), s_ref, cm_ref, slot, c)

    def update(j, slot):
        ks = pl.multiple_of(j * tk, tk)
        _update(lambda c: vt_ref[c * rows:(c + 1) * rows, pl.ds(ks, tk)], s_ref, cm_ref, m_ref, acc_ref, slot)

    _pipelined(k_ref.shape[0] // tk, scores, update, unroll=4)
    o = jnp.concatenate([_normalized(acc_ref, 0), _normalized(acc_ref, 1)], axis=0)
    o_ref[...] = o.T.astype(o_ref.dtype)


def _attn_c(q, k, vt, tq, tk):
    b, s, _ = q.shape
    npair = C_HEADS // 2
    return pl.pallas_call(
        functools.partial(_attn_c_kernel, tk=tk),
        grid=(b, npair, s // tq),
        in_specs=[pl.BlockSpec((None, tq, 2 * LANES), lambda bb, p, i: (bb, i, p)),
                  pl.BlockSpec((None, s, 2 * LANES), lambda bb, p, i: (bb, 0, p)),
                  pl.BlockSpec((vt.shape[0] // npair, s), lambda bb, p, i: (p, bb))],
        out_specs=pl.BlockSpec((None, tq, LANES), lambda bb, p, i: (bb, i, p)),
        out_shape=jax.ShapeDtypeStruct((b, s, npair * LANES), BF16),
        scratch_shapes=_attn_scratch(tq, tk, C_DV),
        compiler_params=_cp(("parallel", "parallel", "arbitrary")),
        name="attn_latent",
    )(q, k, vt)


def _merge_kernel(oa_ref, ob_ref, oc_ref, g0_ref, g1_ref, g2_ref, x_ref, wb_ref, wo_ref, o_ref):
    z = None
    for n, (o_r, g_r) in enumerate(((oa_ref, g0_ref), (ob_ref, g1_ref), (oc_ref, g2_ref))):
        br = jnp.dot(o_r[...], wb_ref[n], preferred_element_type=F32)
        gate = 1.0 / (1.0 + jnp.exp(-g_r[...].astype(F32)))
        z = gate * br if z is None else z + gate * br
    o_ref[...] = x_ref[...] + jnp.dot(z.astype(BF16), wo_ref[...], preferred_element_type=F32)


def _merge(oa, ob, oc, proj2, x2, wb, wo, tm):
    t, d = x2.shape
    bw = oa.shape[1]
    row = lambda w: pl.BlockSpec((tm, w), lambda i: (i, 0))
    gate = lambda n: pl.BlockSpec((tm, d), lambda i: (i, COL_G * LANES // d + n))
    return pl.pallas_call(
        _merge_kernel,
        grid=(t // tm,),
        in_specs=[row(bw), row(bw), row(bw), gate(0), gate(1), gate(2), row(d),
                  pl.BlockSpec(wb.shape, lambda i: (0, 0, 0)), pl.BlockSpec(wo.shape, lambda i: (0, 0))],
        out_specs=row(d),
        out_shape=jax.ShapeDtypeStruct((t, d), F32),
        compiler_params=_cp(("parallel",)),
        name="branch_merge",
    )(oa, ob, oc, proj2, proj2, proj2, x2, wb, wo)


def _cross_kernel(x_ref, g_ref, wq_ref, kbd_ref, vbd_ref, wo_ref, o_ref, *, n_mem):
    x = x_ref[...]
    h = _rms(x, g_ref[...], 1.0 / x.shape[-1]).astype(BF16)
    q = (jnp.dot(h, wq_ref[...], preferred_element_type=F32) * (X_DH ** -0.5 * LOG2E)).astype(BF16)
    s = jnp.dot(q, kbd_ref[...], preferred_element_type=F32)
    ps = []
    for hh in range(X_HEADS):
        sh = s[:, hh * n_mem:(hh + 1) * n_mem]
        p = jnp.exp2(sh - jnp.max(sh, axis=-1, keepdims=True))
        ps.append((p / jnp.sum(p, axis=-1, keepdims=True)).astype(BF16))
    o = jnp.dot(jnp.concatenate(ps, axis=1), vbd_ref[...], preferred_element_type=F32)
    o_ref[...] = x + jnp.dot(o.astype(BF16), wo_ref[...], preferred_element_type=F32)


def _cross(x3, g, wq, kbd, vbd, wo, tm):
    b, s, d = x3.shape
    n_mem = kbd.shape[2] // X_HEADS
    full = lambda a: pl.BlockSpec(a.shape, lambda bb, i: (0, 0))
    return pl.pallas_call(
        functools.partial(_cross_kernel, n_mem=n_mem),
        grid=(b, s // tm),
        in_specs=[pl.BlockSpec((None, tm, d), lambda bb, i: (bb, i, 0)), full(g), full(wq),
                  pl.BlockSpec((None,) + kbd.shape[1:], lambda bb, i: (bb, 0, 0)),
                  pl.BlockSpec((None,) + vbd.shape[1:], lambda bb, i: (bb, 0, 0)), full(wo)],
        out_specs=pl.BlockSpec((None, tm, d), lambda bb, i: (bb, i, 0)),
        out_shape=jax.ShapeDtypeStruct((b, s, d), F32),
        compiler_params=_cp(("parallel", "parallel")),
        name="cross_attn",
    )(x3, g, wq, kbd, vbd, wo)


def _block_diag_kv(kv):
    b, m, _ = kv.shape
    kv = kv.reshape(b, m, 2, X_HEADS, X_DH)
    eye = jnp.eye(X_HEADS, dtype=kv.dtype)
    kt = kv[:, :, 0].transpose(0, 2, 3, 1)
    kbd = (kt[:, :, :, None, :] * eye[None, :, None, :, None]).reshape(b, X_HEADS * X_DH, X_HEADS * m)
    vt = kv[:, :, 1].transpose(0, 2, 1, 3)
    vbd = (vt[:, :, :, None, :] * eye[None, :, None, :, None]).reshape(b, X_HEADS * m, X_HEADS * X_DH)
    return kbd, vbd


def _router_kernel(x_ref, g_ref, w_ref, b_ref, h_out, r_out):
    x = x_ref[...]
    h = _rms(x, g_ref[...], 1.0 / x.shape[-1])
    h_out[...] = h.astype(h_out.dtype)
    logits = jnp.dot(h, w_ref[...], preferred_element_type=F32, precision=lax.Precision.HIGHEST) + b_ref[...]
    lane = lax.broadcasted_iota(jnp.int32, logits.shape, 1)
    lane_f = lane.astype(F32)
    big = jnp.float32(4 * LANES)

    def top(vals, mask):
        mv = jnp.max(jnp.where(mask, vals, -jnp.inf), axis=-1, keepdims=True)
        idx = jnp.min(jnp.where(mask & (vals == mv), lane_f, big), axis=-1, keepdims=True)
        return mv, idx

    g_mask = lane < N_GROUPS
    g_max, g_idx = top(logits, g_mask)
    p_g = 1.0 / jnp.sum(jnp.where(g_mask, jnp.exp(logits - g_max), 0.0), axis=-1, keepdims=True)
    first = N_GROUPS + g_idx * EXPERTS_PER_GROUP
    e_mask = (lane_f >= first) & (lane_f < first + EXPERTS_PER_GROUP)
    v0, i0 = top(logits, e_mask)
    v1, i1 = top(logits, e_mask & (lane_f != i0))
    e1 = jnp.exp(v1 - v0)
    w0 = p_g / (1.0 + e1)
    w1 = p_g * e1 / (1.0 + e1)
    out = jnp.where(lane == 0, i0 - N_GROUPS, 0.0)
    out = jnp.where(lane == 1, i1 - N_GROUPS, out)
    out = jnp.where(lane == 2, w0, out)
    out = jnp.where(lane == 3, w1, out)
    r_out[...] = out


def _router(x2, g, w, bias, tm):
    t, d = x2.shape
    full = lambda a: pl.BlockSpec(a.shape, lambda i: (0, 0))
    return pl.pallas_call(
        _router_kernel,
        grid=(t // tm,),
        in_specs=[pl.BlockSpec((tm, d), lambda i: (i, 0)), full(g), full(w), full(bias)],
        out_specs=[pl.BlockSpec((tm, d), lambda i: (i, 0)), pl.BlockSpec((tm, LANES), lambda i: (i, 0))],
        out_shape=[jax.ShapeDtypeStruct((t, d), BF16), jax.ShapeDtypeStruct((t, LANES), F32)],
        compiler_params=_cp(("parallel",)),
        name="moe_router",
    )(x2, g, w, bias)


def _expert_kernel(blk_e_ref, n_used_ref, x_ref, w1_ref, w3_ref, w2_ref, o_ref, w13_s, w2_s):
    i = pl.program_id(0)
    used = i < n_used_ref[0]
    de = w1_ref.shape[1]

    @pl.when(used & ((i == 0) | (blk_e_ref[i] != blk_e_ref[jnp.maximum(i - 1, 0)])))
    def _():
        w13_s[:, :de] = w1_ref[...].astype(BF16)
        w13_s[:, de:] = w3_ref[...].astype(BF16)
        w2_s[...] = w2_ref[...].astype(BF16)

    @pl.when(used)
    def _():
        hid = jnp.dot(x_ref[...], w13_s[...], preferred_element_type=F32)
        a = hid[:, :de]
        act = (a / (1.0 + jnp.exp(-a))) * hid[:, de:]
        o_ref[...] = jnp.dot(act.astype(BF16), w2_s[...], preferred_element_type=F32).astype(o_ref.dtype)

    @pl.when(jnp.logical_not(used))
    def _():
        o_ref[...] = jnp.zeros(o_ref.shape, o_ref.dtype)


def _experts(blk_e, n_used, xr, w1, w3, w2, rows_per_block):
    rows, d = xr.shape
    de = w1.shape[2]
    weight = lambda w: pl.BlockSpec((None,) + w.shape[1:], lambda i, be, nu: (be[i], 0, 0))
    grid_spec = pltpu.PrefetchScalarGridSpec(
        num_scalar_prefetch=2,
        grid=(rows // rows_per_block,),
        in_specs=[pl.BlockSpec((rows_per_block, d), lambda i, be, nu: (i, 0)), weight(w1), weight(w3), weight(w2)],
        out_specs=pl.BlockSpec((rows_per_block, d), lambda i, be, nu: (i, 0)),
        scratch_shapes=[pltpu.VMEM((d, 2 * de), BF16), pltpu.VMEM((de, d), BF16)],
    )
    return pl.pallas_call(
        _expert_kernel,
        grid_spec=grid_spec,
        out_shape=jax.ShapeDtypeStruct((rows, d), BF16),
        compiler_params=_cp(("arbitrary",)),
        name="moe_experts",
    )(blk_e, n_used, xr, w1, w3, w2)


def _combine_kernel(x_ref, y0_ref, y1_ref, r_ref, g_ref, o_ref, *, final_norm):
    r = r_ref[...]
    y = x_ref[...] + r[:, 2:3] * y0_ref[...].astype(F32) + r[:, 3:4] * y1_ref[...].astype(F32)
    if final_norm:
        y = _rms(y, g_ref[...], 1.0 / y.shape[-1])
    o_ref[...] = y


def _combine(x2, y0, y1, route, g, final_norm, tm):
    t, d = x2.shape
    row = lambda w: pl.BlockSpec((tm, w), lambda i: (i, 0))
    return pl.pallas_call(
        functools.partial(_combine_kernel, final_norm=final_norm),
        grid=(t // tm,),
        in_specs=[row(d), row(d), row(d), row(LANES), pl.BlockSpec((1, d), lambda i: (0, 0))],
        out_specs=row(d),
        out_shape=jax.ShapeDtypeStruct((t, d), F32),
        compiler_params=_cp(("parallel",)),
        name="moe_combine",
    )(x2, y0, y1, route, g)


def _dispatch(route, rows_per_block):
    t = route.shape[0]
    eid = route[:, :TOP_K].astype(jnp.int32).reshape(-1)
    n = eid.shape[0]
    order = jnp.argsort(eid).astype(jnp.int32)
    rank = jnp.argsort(order).astype(jnp.int32)
    experts = jnp.arange(N_EXPERTS, dtype=jnp.int32)
    counts = jnp.sum((eid[:, None] == experts[None, :]).astype(jnp.int32), axis=0)
    start = jnp.cumsum(counts) - counts
    padded = (counts + rows_per_block - 1) // rows_per_block * rows_per_block
    pend = jnp.cumsum(padded)
    pstart = pend - padded
    dest = (rank + (pstart - start)[eid]).reshape(t, TOP_K)
    n_blocks = n // rows_per_block + N_EXPERTS
    blk_first = jnp.arange(n_blocks, dtype=jnp.int32) * rows_per_block
    blk_e = jnp.minimum(jnp.sum((pend[None, :] <= blk_first[:, None]).astype(jnp.int32), axis=1), N_EXPERTS - 1)
    off = (blk_first - pstart[blk_e])[:, None] + jnp.arange(rows_per_block, dtype=jnp.int32)[None, :]
    src = jnp.clip(start[blk_e][:, None] + off, 0, n - 1)
    row_tok = jnp.where(off < counts[blk_e][:, None], order[src] // TOP_K, 0).reshape(-1)
    n_used = (pend[-1] // rows_per_block).astype(jnp.int32).reshape(1)
    return row_tok, dest, blk_e, n_used


def _rot_cols(w):
    half = w.shape[-1] // 2
    return jnp.concatenate([-w[..., half:], w[..., :half]], axis=-1)


def _pack_w_in(w):
    d = w.shape[0]
    blk = A_HEADS * 2 * A_DH
    n_ab = 6 * blk
    cq = w[:, n_ab:n_ab + C_Q_RANK]
    ckv = w[:, n_ab + C_Q_RANK:n_ab + C_Q_RANK + C_KV_RANK]
    ckr = w[:, n_ab + C_Q_RANK + C_KV_RANK:n_ab + C_Q_RANK + C_KV_RANK + C_ROPE]
    gates = w[:, n_ab + C_Q_RANK + C_KV_RANK + C_ROPE:]
    z = lambda n: jnp.zeros((d, n), w.dtype)
    tail = LANES - C_NOPE - C_ROPE
    packed = jnp.concatenate([w[:, :2 * blk], w[:, 3 * blk:5 * blk], cq, z(LANES), ckv,
                              z(C_NOPE), ckr, z(tail), z(C_NOPE), _rot_cols(ckr), z(tail), gates], axis=1)
    return packed.astype(BF16), w[:, 2 * blk:3 * blk], w[:, 5 * blk:6 * blk]


def _pack_w_uq(w):
    wq = w.reshape(C_Q_RANK, C_HEADS, C_NOPE + C_ROPE)
    pad_rows = 4 * LANES - C_Q_RANK
    tail = LANES - C_NOPE - C_ROPE
    q1 = jnp.pad(wq, ((0, pad_rows), (0, 0), (0, tail))).reshape(4 * LANES, C_HEADS * LANES)
    q2 = jnp.pad(_rot_cols(wq[:, :, C_NOPE:]), ((0, pad_rows), (0, 0), (C_NOPE, tail))).reshape(4 * LANES, C_HEADS * LANES)
    return q1.astype(BF16), q2.astype(BF16)


def _pack_w_ukv(w):
    wkv = w.reshape(C_KV_RANK, C_HEADS, C_NOPE + C_DV)
    wk = jnp.pad(wkv[:, :, :C_NOPE], ((0, 0), (0, 0), (0, LANES - C_NOPE))).reshape(C_KV_RANK, C_HEADS * LANES)
    wv = wkv[:, :, C_NOPE:].reshape(C_KV_RANK, C_HEADS * C_DV)
    return wk.astype(BF16), wv


def kernel(x, mem, positions, mix_norm_g, w_in, diff_lambda, diff_subln_g, mla_q_norm_g, w_uq, mla_kv_norm_g, w_ukv, w_branch, w_out, cross_norm_g, mem_norm_g, w_xq, w_xkv, w_xo, ffn_norm_g, w_group, b_group, w_router, b_router, w1, w3, w2, final_norm_g):
    b, s, d = x.shape
    depth = w_in.shape[0]
    t = b * s
    n_mem = mem.shape[1]
    tl = _tiles(s)
    assert PROJ_COLS == COL_G * LANES + N_BRANCHES * d and s % (2 * tl["t_b"]) == 0

    pos_f = positions.astype(F32)
    posq_row = pos_f.reshape(b, 1, s)
    posk_rep = jnp.broadcast_to(pos_f[:, :, None], (b, s, LANES))
    cos, sin = _rope_tables(pos_f.reshape(t, 1), tl["tm_tok"])
    slopes_a2 = jnp.exp2(-8.0 * jnp.arange(1, A_HEADS + 1, dtype=F32) / A_HEADS) * LOG2E
    reach = max((w // (2 * dl)) * dl for w, dl in B_PATTERNS)
    b_bias = _b_bias_tables(tl["t_b"], 2 * (-(-reach // tl["t_b"])) + 1)
    mem2 = mem.reshape(b * n_mem, d)

    x2 = x.reshape(t, d)
    for l in range(depth):
        w_main, w_av, w_bv = _pack_w_in(w_in[l])
        proj2, vt_a, vt_b = _in_proj(x2, mix_norm_g[l], w_main, *_vt_weights(w_av, 2 * A_DH), *_vt_weights(w_bv, B_DH),
                                     tl["tm_proj"], tl["tn_proj"])
        proj3 = proj2.reshape(b, s, PROJ_COLS)
        lq = diff_lambda[l].astype(F32)
        lam_init = 0.8 - 0.6 * math.exp(-0.3 * l)
        lam = (jnp.exp(jnp.sum(lq[0] * lq[1])) - jnp.exp(jnp.sum(lq[2] * lq[3])) + lam_init).reshape(1)
        oa = _attn_a(proj3, vt_a, posq_row, posk_rep, lam, slopes_a2, diff_subln_g[l], 1.0 - lam_init,
                     tl["tq_a"], tl["tk_a"])
        ob = _attn_b(proj3, vt_b, b_bias, tl["t_b"])
        wq1, wq2 = _pack_w_uq(w_uq[l])
        wk, wv = _pack_w_ukv(w_ukv[l])
        gq = jnp.pad(mla_q_norm_g[l], (0, 4 * LANES - C_Q_RANK)).reshape(1, 4 * LANES)
        qc, kc, vt_c = _mla_prep(proj2, cos, sin, gq, mla_kv_norm_g[l].reshape(1, C_KV_RANK),
                                 wq1, wq2, wk, *_vt_weights(wv, C_DV), tl["tm_tok"])
        oc = _attn_c(qc.reshape(b, s, -1), kc.reshape(b, s, -1), vt_c, tl["tq_c"], tl["tk_c"])
        x2 = _merge(oa.reshape(t, -1), ob.reshape(t, -1), oc.reshape(t, -1), proj2, x2,
                    w_branch[l].astype(BF16), w_out[l].astype(BF16), tl["tm_tok"])
        kv = _norm_matmul(mem2, mem_norm_g[l], w_xkv[l].astype(BF16), min(1024, b * n_mem), w_xkv.shape[2])
        kbd, vbd = _block_diag_kv(kv.reshape(b, n_mem, -1))
        x2 = _cross(x2.reshape(b, s, d), cross_norm_g[l].reshape(1, d), w_xq[l].astype(BF16), kbd, vbd,
                    w_xo[l].astype(BF16), tl["tm_tok"]).reshape(t, d)
        w_r = jnp.pad(jnp.concatenate([w_group[l], w_router[l]], axis=1), ((0, 0), (0, LANES - N_GROUPS - N_EXPERTS)))
        b_r = jnp.pad(jnp.concatenate([b_group[l], b_router[l]]), (0, LANES - N_GROUPS - N_EXPERTS)).reshape(1, LANES)
        h, route = _router(x2, ffn_norm_g[l].reshape(1, d), w_r, b_r, tl["tm_tok"])
        row_tok, dest, blk_e, n_used = _dispatch(route, tl["moe_rows"])
        yr = _experts(blk_e, n_used, h[row_tok], w1[l], w3[l], w2[l], tl["moe_rows"])
        x2 = _combine(x2, yr[dest[:, 0]], yr[dest[:, 1]], route, final_norm_g.reshape(1, d),
                      l == depth - 1, tl["tm_tok"])
    return x2.reshape(b, s, d)
```

```python
import functools
import math

import jax
import jax.numpy as jnp
from jax import lax
from jax.experimental import pallas as pl
from jax.experimental.pallas import tpu as pltpu

F32 = jnp.float32
BF16 = jnp.bfloat16

LANES = 128
NORM_EPS = 1e-6
LOG2E = math.log2(math.e)
NEG_BIG = -1e30

A_HEADS, A_DH = 4, 64
B_HEADS, B_DH = 8, 64
B_PATTERNS = ((128, 1), (512, 4), (2048, 16))
C_HEADS, C_Q_RANK, C_KV_RANK, C_NOPE, C_ROPE, C_DV = 8, 384, 256, 64, 32, 64
ROPE_THETA = 10000.0
N_BRANCHES = 3
X_HEADS, X_DH = 4, 64
N_GROUPS, EXPERTS_PER_GROUP, TOP_K = 4, 8, 2
N_EXPERTS = N_GROUPS * EXPERTS_PER_GROUP

COL_A = 0
COL_B = 8
COL_CQ = 16
COL_CKV = 20
COL_R1 = 22
COL_R2 = 23
COL_G = 24
PROJ_COLS = 48 * LANES

VMEM_LIMIT = 48 * 1024 * 1024
BATCH_GROUPS = 2


def _tiles(seq):
    return dict(
        tm_proj=min(1024, seq), tn_proj=1024,
        tm_tok=min(512, seq),
        tq_a=min(512, seq), tk_a=min(512, seq),
        t_b=min(512, seq),
        tq_c=min(512, seq), tk_c=min(512, seq),
        moe_rows=512,
    )


def _cp(sem):
    return pltpu.CompilerParams(dimension_semantics=sem, vmem_limit_bytes=VMEM_LIMIT)


def _rms(x, g, inv_n):
    ms = jnp.sum(x * x, axis=-1, keepdims=True) * inv_n
    return x * lax.rsqrt(ms + NORM_EPS) * g


def _norm_matmul_kernel(x_ref, g_ref, w_ref, o_ref, h_ref, *, inv_n):
    @pl.when(pl.program_id(1) == 0)
    def _():
        h_ref[...] = _rms(x_ref[...].astype(F32), g_ref[...], inv_n).astype(BF16)

    o_ref[...] = jnp.dot(h_ref[...], w_ref[...], preferred_element_type=F32).astype(o_ref.dtype)


def _norm_matmul(x, g, w, tm, tn, out_dtype=BF16):
    m, k = x.shape
    n = w.shape[1]
    return pl.pallas_call(
        functools.partial(_norm_matmul_kernel, inv_n=1.0 / k),
        grid=(m // tm, n // tn),
        in_specs=[pl.BlockSpec((tm, k), lambda i, j: (i, 0)),
                  pl.BlockSpec((1, k), lambda i, j: (0, 0)),
                  pl.BlockSpec((k, tn), lambda i, j: (0, j))],
        out_specs=pl.BlockSpec((tm, tn), lambda i, j: (i, j)),
        out_shape=jax.ShapeDtypeStruct((m, n), out_dtype),
        scratch_shapes=[pltpu.VMEM((tm, k), BF16)],
        compiler_params=_cp(("parallel", "arbitrary")),
        name="norm_matmul",
    )(x, g.reshape(1, k), w)


def _in_proj_kernel(x_ref, g_ref, w_ref, wta_ref, wtb_ref, ona_ref, onb_ref, o_ref, vta_ref, vtb_ref, h_ref, *, inv_n):
    @pl.when(pl.program_id(1) == 0)
    def _():
        h = _rms(x_ref[...].astype(F32), g_ref[...], inv_n).astype(BF16)
        h_ref[...] = h
        vta_ref[...] = (_kq(wta_ref[...], h) + ona_ref[...]).astype(vta_ref.dtype)
        vtb_ref[...] = (_kq(wtb_ref[...], h) + onb_ref[...]).astype(vtb_ref.dtype)

    o_ref[...] = jnp.dot(h_ref[...], w_ref[...], preferred_element_type=F32).astype(o_ref.dtype)


def _in_proj(x, g, w, wta, ona, wtb, onb, tm, tn):
    m, k = x.shape
    n = w.shape[1]
    full = lambda a: pl.BlockSpec(a.shape, lambda i, j: (0, 0))
    return pl.pallas_call(
        functools.partial(_in_proj_kernel, inv_n=1.0 / k),
        grid=(m // tm, n // tn),
        in_specs=[pl.BlockSpec((tm, k), lambda i, j: (i, 0)),
                  pl.BlockSpec((1, k), lambda i, j: (0, 0)),
                  pl.BlockSpec((k, tn), lambda i, j: (0, j)),
                  full(wta), full(wtb), full(ona), full(onb)],
        out_specs=[pl.BlockSpec((tm, tn), lambda i, j: (i, j)),
                   pl.BlockSpec((wta.shape[0], tm), lambda i, j: (0, i)),
                   pl.BlockSpec((wtb.shape[0], tm), lambda i, j: (0, i))],
        out_shape=[jax.ShapeDtypeStruct((m, n), BF16),
                   jax.ShapeDtypeStruct((wta.shape[0], m), BF16),
                   jax.ShapeDtypeStruct((wtb.shape[0], m), BF16)],
        scratch_shapes=[pltpu.VMEM((tm, k), BF16)],
        compiler_params=_cp(("parallel", "arbitrary")),
        name="in_proj",
    )(x, g.reshape(1, k), w, wta, wtb, ona, onb)


ONES_ROWS = 16


def _vt_weights(wv, dv):
    k, w = wv.shape
    wt = jnp.pad(wv.T.reshape(w // dv, dv, k), ((0, 0), (0, ONES_ROWS), (0, 0))).reshape(-1, k)
    ones = jnp.zeros((w // dv, dv + ONES_ROWS, 1), F32).at[:, dv, 0].set(1.0).reshape(-1, 1)
    return wt.astype(BF16), ones


def _split_q(q_ref, qs_ref, scale):
    q = q_ref[...].astype(F32) * scale
    lane = lax.broadcasted_iota(jnp.int32, q.shape, 1)
    qs_ref[0] = jnp.where(lane < LANES // 2, q, 0.0).astype(BF16)
    qs_ref[1] = jnp.where(lane >= LANES // 2, q, 0.0).astype(BF16)


def _kq(k, q):
    return lax.dot_general(k, q, (((1,), (1,)), ((), ())), preferred_element_type=F32)


def _put_scores(s, s_ref, cm_ref, slot, c):
    s_ref[slot, c] = s
    cm_ref[slot, c] = jnp.max(s, axis=0, keepdims=True)


def _update(vt_of, s_ref, cm_ref, m_ref, acc_ref, slot):
    for c in range(2):
        m_prev = m_ref[c]
        m_new = jnp.maximum(m_prev, cm_ref[slot, c])
        alpha = jnp.exp2(m_prev - m_new)
        p = jnp.exp2(s_ref[slot, c] - m_new).astype(BF16)
        acc_ref[c] = alpha * acc_ref[c] + jnp.dot(vt_of(c), p, preferred_element_type=F32)
        m_ref[c] = m_new


def _init_state(m_ref, acc_ref):
    m_ref[...] = jnp.full(m_ref.shape, NEG_BIG, F32)
    acc_ref[...] = jnp.zeros(acc_ref.shape, F32)


def _pipelined(n, scores, update, unroll):
    scores(0, 0)

    def body(jj, carry):
        j = unroll * jj
        for u in range(unroll):
            scores(j + u + 1, (u + 1) % 2)
            update(j + u, u % 2)
        return carry

    looped = (n - 1) // unroll
    lax.fori_loop(0, looped, body, 0)
    for j in range(unroll * looped, n):
        if j + 1 < n:
            scores(j + 1, (j + 1) % 2)
        update(j, j % 2)


def _pipelined_list(n, chunk, scores, update):
    scores(chunk(0), 0)

    def body(unroll, first):
        def run(jj, carry):
            t = first + unroll * jj
            for u in range(unroll):
                scores(chunk(t + u + 1), (u + 1) % 2)
                update(chunk(t + u), u % 2)
            return carry
        return run

    quads = (n - 2) // 4
    lax.fori_loop(0, quads, body(4, 0), 0)
    lax.fori_loop(0, (n - 4 * quads) // 2 - 1, body(2, 4 * quads), 0)
    scores(chunk(n - 1), 1)
    update(chunk(n - 2), 0)
    update(chunk(n - 1), 1)


def _normalized(acc_ref, c):
    dv = acc_ref.shape[1] - ONES_ROWS
    return acc_ref[c, :dv, :] / acc_ref[c, dv:dv + 1, :]


def _attn_scratch(tq, tk, dv):
    return [pltpu.VMEM((2, 1, tq), F32), pltpu.VMEM((2, dv + ONES_ROWS, tq), F32),
            pltpu.VMEM((2, 2, tk, tq), F32), pltpu.VMEM((2, 2, 1, tq), F32)]


SKIP_MARGIN = 150.0


def _block_norms_kernel(qk_ref, sel_ref, o_ref):
    x = qk_ref[...].astype(F32)
    lane = lax.broadcasted_iota(jnp.int32, x.shape, 1)
    x = jnp.where(lane < x.shape[1] // 2, (x * (A_DH ** -0.5 * LOG2E)).astype(BF16).astype(F32), x)
    sums = jnp.dot(x * x, sel_ref[...], preferred_element_type=F32)
    o_ref[...] = jnp.sqrt(jnp.max(sums, axis=0, keepdims=True))


def _block_norms(proj3, t):
    b, s, _ = proj3.shape
    w = 2 * A_HEADS * 2 * A_DH
    sel = (jnp.arange(w)[:, None] // A_DH == jnp.arange(LANES)[None, :]).astype(BF16)
    return pl.pallas_call(
        _block_norms_kernel,
        grid=(b, s // t),
        in_specs=[pl.BlockSpec((None, t, w), lambda bb, i: (bb, i, COL_A * LANES // w)),
                  pl.BlockSpec((w, LANES), lambda bb, i: (0, 0))],
        out_specs=pl.BlockSpec((None, None, 1, LANES), lambda bb, i: (bb, i, 0, 0)),
        out_shape=jax.ShapeDtypeStruct((b, s // t, 1, LANES), F32),
        compiler_params=_cp(("parallel", "parallel")),
        name="block_norms",
    )(proj3, sel)[:, :, 0, :]


def _a_chunk_lists(proj3, pos_f, slopes2, tq, tk):
    assert tq == tk
    b, s, _ = proj3.shape
    norms = _block_norms(proj3, tq)[:, :, :4 * A_HEADS].reshape(b, s // tq, 2, A_HEADS, 2).max(axis=-1)
    qn = norms[:, :, 0].transpose(0, 2, 1)
    kn = norms[:, :, 1].transpose(0, 2, 1)
    qk = 1.01 * qn[:, :, :, None] * kn[:, :, None, :] + 1.0
    pq = pos_f.reshape(b, s // tq, tq)
    pk = pos_f.reshape(b, s // tk, tk)
    qlo, qhi = jnp.min(pq, -1)[:, :, None], jnp.max(pq, -1)[:, :, None]
    klo, khi = jnp.min(pk, -1)[:, None, :], jnp.max(pk, -1)[:, None, :]
    dmin = jnp.maximum(jnp.maximum(klo - qhi, qlo - khi), 0.0)
    dmax = jnp.maximum(khi - qlo, qhi - klo)
    sl = slopes2[None, :, None, None]
    lower = -qk - sl * dmax[:, None]
    upper = qk - sl * dmin[:, None]
    nk = s // tk
    j = jnp.arange(nk, dtype=jnp.int32)
    cover = jnp.max(jnp.min(jnp.abs(pq[:, :, :, None] - pk[:, :, None, :]), axis=-1), axis=-1)
    lower = jnp.where(j[:, None] == j[None, :], jnp.maximum(lower, -qk - sl * cover[:, None, :, None]), lower)
    first = jnp.argmax(lower, axis=-1).astype(jnp.int32)[..., None]
    keep = (upper >= jnp.max(lower, axis=-1, keepdims=True) - SKIP_MARGIN) | (j == first)
    lst = jnp.argsort(jnp.where(j == first, -1, jnp.where(keep, j, nk + j)), axis=-1).astype(jnp.int32)
    cnt = jnp.sum(keep.astype(jnp.int32), axis=-1)
    return lst.reshape(-1), (cnt + cnt % 2).reshape(-1)


def _attn_a_kernel(lst_ref, cnt_ref, lam_ref, slope_ref, q_ref, k_ref, vt_ref, pq_ref, pk_ref, g_ref, o_ref,
                   qs_ref, m_ref, acc_ref, s_ref, cm_ref, *, tk, post_scale):
    tq = q_ref.shape[0]
    nk = k_ref.shape[0] // tk
    blk = (pl.program_id(0) * pl.num_programs(1) + pl.program_id(1)) * pl.num_programs(2) + pl.program_id(2)
    sl2 = slope_ref[pl.program_id(1)]
    _split_q(q_ref, qs_ref, A_DH ** -0.5 * LOG2E)
    _init_state(m_ref, acc_ref)
    pq = pq_ref[...] * sl2

    def scores(j, slot):
        ks = pl.multiple_of(j * tk, tk)
        k = k_ref[pl.ds(ks, tk), :]
        pk = pk_ref[pl.ds(ks, tk), :] * sl2
        bias = jnp.abs(jnp.concatenate([pk] * (tq // LANES), axis=1) - pq)
        for c in range(2):
            _put_scores(_kq(k, qs_ref[c]) - bias, s_ref, cm_ref, slot, c)

    def update(j, slot):
        ks = pl.multiple_of(j * tk, tk)
        _update(lambda c: vt_ref[:, pl.ds(ks, tk)], s_ref, cm_ref, m_ref, acc_ref, slot)

    _pipelined_list(cnt_ref[blk], lambda t: lst_ref[blk * nk + t], scores, update)
    o = (_normalized(acc_ref, 0) - lam_ref[0] * _normalized(acc_ref, 1)).T
    o_ref[...] = (_rms(o, g_ref[...], 1.0 / LANES) * post_scale).astype(o_ref.dtype)


def _attn_a(proj, vt, posq_row, posk_rep, lam, slopes2, g, post_scale, tq, tk):
    b, s, _ = proj.shape
    lst, cnt = _a_chunk_lists(proj, posq_row.reshape(b, s), slopes2, tq, tk)
    smem = pl.BlockSpec(memory_space=pltpu.SMEM)
    grid_spec = pltpu.PrefetchScalarGridSpec(
        num_scalar_prefetch=2,
        grid=(b, A_HEADS, s // tq),
        in_specs=[smem, smem,
                  pl.BlockSpec((None, tq, LANES), lambda bb, h, i, *_: (bb, i, COL_A + h)),
                  pl.BlockSpec((None, s, LANES), lambda bb, h, i, *_: (bb, 0, COL_A + A_HEADS + h)),
                  pl.BlockSpec((vt.shape[0] // A_HEADS, s), lambda bb, h, i, *_: (h, bb)),
                  pl.BlockSpec((None, 1, tq), lambda bb, h, i, *_: (bb, 0, i)),
                  pl.BlockSpec((None, s, LANES), lambda bb, h, i, *_: (bb, 0, 0)),
                  pl.BlockSpec((1, LANES), lambda bb, h, i, *_: (0, 0))],
        out_specs=pl.BlockSpec((None, tq, LANES), lambda bb, h, i, *_: (bb, i, h)),
        scratch_shapes=[pltpu.VMEM((2, tq, LANES), BF16)] + _attn_scratch(tq, tk, LANES),
    )
    return pl.pallas_call(
        functools.partial(_attn_a_kernel, tk=tk, post_scale=post_scale),
        grid_spec=grid_spec,
        out_shape=jax.ShapeDtypeStruct((b, s, A_HEADS * LANES), BF16),
        compiler_params=_cp(("parallel", "parallel", "arbitrary")),
        name="attn_diff",
    )(lst, cnt, lam, slopes2, proj, proj, vt, posq_row, posk_rep, g.reshape(1, LANES))


def _attn_b_kernel(q_ref, k_ref, vt_ref, bias_ref, o_ref, qs_ref, m_ref, acc_ref, s_ref, cm_ref, *, nband):
    t = q_ref.shape[0]
    nkb = k_ref.shape[0] // t
    rows = vt_ref.shape[0] // 2
    i = pl.program_id(2)
    half = nband // 2
    _split_q(q_ref, qs_ref, B_DH ** -0.5 * LOG2E)
    _init_state(m_ref, acc_ref)

    def start(jj):
        return pl.multiple_of(jnp.clip(i + jj - half, 0, nkb - 1) * t, t)

    def scores(jj, slot):
        kb = i + jj - half
        table = jnp.where((kb >= 0) & (kb < nkb), jj, nband)
        k = k_ref[pl.ds(start(jj), t), :]
        for c in range(2):
            _put_scores(_kq(k, qs_ref[c]) + bias_ref[c, table], s_ref, cm_ref, slot, c)

    def update(jj, slot):
        ks = start(jj)
        _update(lambda c: vt_ref[c * rows:(c + 1) * rows, pl.ds(ks, t)], s_ref, cm_ref, m_ref, acc_ref, slot)

    _pipelined(nband, scores, update, unroll=2)
    o = jnp.concatenate([_normalized(acc_ref, 0), _normalized(acc_ref, 1)], axis=0)
    o_ref[...] = o.T.astype(o_ref.dtype)


def _b_bias_tables(t, nband):
    half = nband // 2
    r = jnp.arange(t, dtype=jnp.int32)[None, :, None]
    c = jnp.arange(t, dtype=jnp.int32)[None, None, :]
    jj = jnp.arange(nband + 1, dtype=jnp.int32)[:, None, None]
    ao = jnp.abs((jj - half) * t + r - c)
    mult = jnp.zeros(ao.shape, jnp.int32)
    for window, dilation in B_PATTERNS:
        reach = (window // (2 * dilation)) * dilation
        mult = mult + ((ao % dilation == 0) & (ao <= reach)).astype(jnp.int32)
    mult = jnp.where(jj < nband, mult, 0)
    slopes = jnp.exp2(-8.0 * jnp.arange(1, B_HEADS + 1, dtype=F32) / B_HEADS) * LOG2E
    bias = jnp.log2(jnp.maximum(mult, 1).astype(F32))[None] - slopes[:, None, None, None] * ao.astype(F32)[None]
    bias = jnp.where((mult > 0)[None], bias, NEG_BIG)
    return bias.reshape(B_HEADS // 2, 2, nband + 1, t, t)


def _attn_b(proj, vt, bias, t):
    b, s, _ = proj.shape
    nband = bias.shape[2] - 1
    npair = B_HEADS // 2
    return pl.pallas_call(
        functools.partial(_attn_b_kernel, nband=nband),
        grid=(b, npair, s // t),
        in_specs=[pl.BlockSpec((None, t, LANES), lambda bb, p, i: (bb, i, COL_B + p)),
                  pl.BlockSpec((None, s, LANES), lambda bb, p, i: (bb, 0, COL_B + npair + p)),
                  pl.BlockSpec((vt.shape[0] // npair, s), lambda bb, p, i: (p, bb)),
                  pl.BlockSpec((None,) + bias.shape[1:], lambda bb, p, i: (p, 0, 0, 0, 0))],
        out_specs=pl.BlockSpec((None, t, LANES), lambda bb, p, i: (bb, i, p)),
        out_shape=jax.ShapeDtypeStruct((b, s, npair * LANES), BF16),
        scratch_shapes=[pltpu.VMEM((2, t, LANES), BF16)] + _attn_scratch(t, t, B_DH),
        compiler_params=_cp(("parallel", "parallel", "arbitrary")),
        name="attn_dilated",
    )(proj, proj, vt, bias)


def _rope_table_kernel(pos_ref, invf_ref, c_ref, s_ref):
    ang = pos_ref[...] * invf_ref[...]
    c_ref[...] = jnp.cos(ang)
    s_ref[...] = jnp.sin(ang)


def _rope_tables(pos_col, tm):
    t = pos_col.shape[0]
    inv = ROPE_THETA ** (-jnp.arange(0, C_ROPE, 2, dtype=F32) / C_ROPE)
    invf = jnp.concatenate([jnp.zeros((C_NOPE,), F32), inv, inv,
                            jnp.zeros((LANES - C_NOPE - C_ROPE,), F32)]).reshape(1, LANES)
    spec = pl.BlockSpec((tm, LANES), lambda i: (i, 0))
    return pl.pallas_call(
        _rope_table_kernel,
        grid=(t // tm,),
        in_specs=[pl.BlockSpec((tm, 1), lambda i: (i, 0)), pl.BlockSpec((1, LANES), lambda i: (0, 0))],
        out_specs=[spec, spec],
        out_shape=[jax.ShapeDtypeStruct((t, LANES), F32)] * 2,
        compiler_params=_cp(("parallel",)),
        name="rope_tables",
    )(pos_col, invf)


def _mla_prep_kernel(cq_ref, ckv_ref, r1_ref, r2_ref, c_ref, s_ref, gq_ref, gkv_ref,
                     wq1_ref, wq2_ref, wk_ref, wvt_ref, onv_ref, q_out, k_out, vt_out):
    qn = _rms(cq_ref[...].astype(F32), gq_ref[...], 1.0 / C_Q_RANK).astype(BF16)
    kvn = _rms(ckv_ref[...].astype(F32), gkv_ref[...], 1.0 / C_KV_RANK).astype(BF16)
    q1 = jnp.dot(qn, wq1_ref[...], preferred_element_type=F32)
    q2 = jnp.dot(qn, wq2_ref[...], preferred_element_type=F32)
    k1 = jnp.dot(kvn, wk_ref[...], preferred_element_type=F32)
    vt_out[...] = (_kq(wvt_ref[...], kvn) + onv_ref[...]).astype(vt_out.dtype)
    cos = c_ref[...]
    sin = s_ref[...]
    k_rope = r1_ref[...].astype(F32) * cos + r2_ref[...].astype(F32) * sin
    scale = (C_NOPE + C_ROPE) ** -0.5 * LOG2E
    for h in range(C_HEADS):
        sl = slice(h * LANES, (h + 1) * LANES)
        q_out[:, sl] = ((q1[:, sl] * cos + q2[:, sl] * sin) * scale).astype(q_out.dtype)
        k_out[:, sl] = (k1[:, sl] + k_rope).astype(k_out.dtype)


def _mla_prep(proj2, cos, sin, gq, gkv, wq1, wq2, wk, wvt, onv, tm):
    t = proj2.shape[0]
    full = lambda a: pl.BlockSpec(a.shape, lambda i: (0, 0))
    row = lambda w: pl.BlockSpec((tm, w), lambda i: (i, 0))
    return pl.pallas_call(
        _mla_prep_kernel,
        grid=(t // tm,),
        in_specs=[pl.BlockSpec((tm, 4 * LANES), lambda i: (i, COL_CQ // 4)),
                  pl.BlockSpec((tm, 2 * LANES), lambda i: (i, COL_CKV // 2)),
                  pl.BlockSpec((tm, LANES), lambda i: (i, COL_R1)),
                  pl.BlockSpec((tm, LANES), lambda i: (i, COL_R2)),
                  row(LANES), row(LANES), full(gq), full(gkv), full(wq1), full(wq2), full(wk), full(wvt), full(onv)],
        out_specs=[row(C_HEADS * LANES), row(C_HEADS * LANES), pl.BlockSpec((wvt.shape[0], tm), lambda i: (0, i))],
        out_shape=[jax.ShapeDtypeStruct((t, C_HEADS * LANES), BF16),
                   jax.ShapeDtypeStruct((t, C_HEADS * LANES), BF16),
                   jax.ShapeDtypeStruct((wvt.shape[0], t), BF16)],
        compiler_params=_cp(("parallel",)),
        name="mla_prep",
    )(proj2, proj2, proj2, proj2, cos, sin, gq, gkv, wq1, wq2, wk, wvt, onv)


def _attn_c_kernel(q_ref, k_ref, vt_ref, o_ref, m_ref, acc_ref, s_ref, cm_ref, *, tk):
    rows = vt_ref.shape[0] // 2
    _init_state(m_ref, acc_ref)

    def scores(j, slot):
        ks = pl.multiple_of(j * tk, tk)
        for c in range(2):
            sl = slice(c * LANES, (c + 1) * LANES)
            _put_scores(_kq(k_ref[pl.ds(ks, tk), sl], q_ref[:, sl]), s_ref, cm_ref, slot, c)

    def update(j, slot):
        ks = pl.multiple_of(j * tk, tk)
        _update(lambda c: vt_ref[c * rows:(c + 1) * rows, pl.ds(ks, tk)], s_ref, cm_ref, m_ref, acc_ref, slot)

    _pipelined(k_ref.shape[0] // tk, scores, update, unroll=4)
    o = jnp.concatenate([_normalized(acc_ref, 0), _normalized(acc_ref, 1)], axis=0)
    o_ref[...] = o.T.astype(o_ref.dtype)


def _attn_c(q, k, vt, tq, tk):
    b, s, _ = q.shape
    npair = C_HEADS // 2
    return pl.pallas_call(
        functools.partial(_attn_c_kernel, tk=tk),
        grid=(b, npair, s // tq),
        in_specs=[pl.BlockSpec((None, tq, 2 * LANES), lambda bb, p, i: (bb, i, p)),
                  pl.BlockSpec((None, s, 2 * LANES), lambda bb, p, i: (bb, 0, p)),
                  pl.BlockSpec((vt.shape[0] // npair, s), lambda bb, p, i: (p, bb))],
        out_specs=pl.BlockSpec((None, tq, LANES), lambda bb, p, i: (bb, i, p)),
        out_shape=jax.ShapeDtypeStruct((b, s, npair * LANES), BF16),
        scratch_shapes=_attn_scratch(tq, tk, C_DV),
        compiler_params=_cp(("parallel", "parallel", "arbitrary")),
        name="attn_latent",
    )(q, k, vt)


def _merge_kernel(oa_ref, ob_ref, oc_ref, g0_ref, g1_ref, g2_ref, x_ref, wb_ref, wo_ref, o_ref):
    z = None
    for n, (o_r, g_r) in enumerate(((oa_ref, g0_ref), (ob_ref, g1_ref), (oc_ref, g2_ref))):
        br = jnp.dot(o_r[...], wb_ref[n], preferred_element_type=F32)
        gate = 1.0 / (1.0 + jnp.exp(-g_r[...].astype(F32)))
        z = gate * br if z is None else z + gate * br
    o_ref[...] = x_ref[...] + jnp.dot(z.astype(BF16), wo_ref[...], preferred_element_type=F32)


def _merge(oa, ob, oc, proj2, x2, wb, wo, tm):
    t, d = x2.shape
    bw = oa.shape[1]
    row = lambda w: pl.BlockSpec((tm, w), lambda i: (i, 0))
    gate = lambda n: pl.BlockSpec((tm, d), lambda i: (i, COL_G * LANES // d + n))
    return pl.pallas_call(
        _merge_kernel,
        grid=(t // tm,),
        in_specs=[row(bw), row(bw), row(bw), gate(0), gate(1), gate(2), row(d),
                  pl.BlockSpec(wb.shape, lambda i: (0, 0, 0)), pl.BlockSpec(wo.shape, lambda i: (0, 0))],
        out_specs=row(d),
        out_shape=jax.ShapeDtypeStruct((t, d), F32),
        compiler_params=_cp(("parallel",)),
        name="branch_merge",
    )(oa, ob, oc, proj2, proj2, proj2, x2, wb, wo)


def _cross_kernel(x_ref, g_ref, wq_ref, kbd_ref, vbd_ref, wo_ref, o_ref, *, n_mem):
    x = x_ref[...]
    h = _rms(x, g_ref[...], 1.0 / x.shape[-1]).astype(BF16)
    q = (jnp.dot(h, wq_ref[...], preferred_element_type=F32) * (X_DH ** -0.5 * LOG2E)).astype(BF16)
    s = jnp.dot(q, kbd_ref[...], preferred_element_type=F32)
    ps = []
    for hh in range(X_HEADS):
        sh = s[:, hh * n_mem:(hh + 1) * n_mem]
        p = jnp.exp2(sh - jnp.max(sh, axis=-1, keepdims=True))
        ps.append((p / jnp.sum(p, axis=-1, keepdims=True)).astype(BF16))
    o = jnp.dot(jnp.concatenate(ps, axis=1), vbd_ref[...], preferred_element_type=F32)
    o_ref[...] = x + jnp.dot(o.astype(BF16), wo_ref[...], preferred_element_type=F32)


def _cross(x3, g, wq, kbd, vbd, wo, tm):
    b, s, d = x3.shape
    n_mem = kbd.shape[2] // X_HEADS
    full = lambda a: pl.BlockSpec(a.shape, lambda bb, i: (0, 0))
    return pl.pallas_call(
        functools.partial(_cross_kernel, n_mem=n_mem),
        grid=(b, s // tm),
        in_specs=[pl.BlockSpec((None, tm, d), lambda bb, i: (bb, i, 0)), full(g), full(wq),
                  pl.BlockSpec((None,) + kbd.shape[1:], lambda bb, i: (bb, 0, 0)),
                  pl.BlockSpec((None,) + vbd.shape[1:], lambda bb, i: (bb, 0, 0)), full(wo)],
        out_specs=pl.BlockSpec((None, tm, d), lambda bb, i: (bb, i, 0)),
        out_shape=jax.ShapeDtypeStruct((b, s, d), F32),
        compiler_params=_cp(("parallel", "parallel")),
        name="cross_attn",
    )(x3, g, wq, kbd, vbd, wo)


def _block_diag_kv(kv):
    b, m, _ = kv.shape
    kv = kv.reshape(b, m, 2, X_HEADS, X_DH)
    eye = jnp.eye(X_HEADS, dtype=kv.dtype)
    kt = kv[:, :, 0].transpose(0, 2, 3, 1)
    kbd = (kt[:, :, :, None, :] * eye[None, :, None, :, None]).reshape(b, X_HEADS * X_DH, X_HEADS * m)
    vt = kv[:, :, 1].transpose(0, 2, 1, 3)
    vbd = (vt[:, :, :, None, :] * eye[None, :, None, :, None]).reshape(b, X_HEADS * m, X_HEADS * X_DH)
    return kbd, vbd


def _router_kernel(x_ref, g_ref, w_ref, b_ref, h_out, r_out):
    x = x_ref[...]
    h = _rms(x, g_ref[...], 1.0 / x.shape[-1])
    h_out[...] = h.astype(h_out.dtype)
    logits = jnp.dot(h, w_ref[...], preferred_element_type=F32, precision=lax.Precision.HIGHEST) + b_ref[...]
    lane = lax.broadcasted_iota(jnp.int32, logits.shape, 1)
    lane_f = lane.astype(F32)
    big = jnp.float32(4 * LANES)

    def top(vals, mask):
        mv = jnp.max(jnp.where(mask, vals, -jnp.inf), axis=-1, keepdims=True)
        idx = jnp.min(jnp.where(mask & (vals == mv), lane_f, big), axis=-1, keepdims=True)
        return mv, idx

    g_mask = lane < N_GROUPS
    g_max, g_idx = top(logits, g_mask)
    p_g = 1.0 / jnp.sum(jnp.where(g_mask, jnp.exp(logits - g_max), 0.0), axis=-1, keepdims=True)
    first = N_GROUPS + g_idx * EXPERTS_PER_GROUP
    e_mask = (lane_f >= first) & (lane_f < first + EXPERTS_PER_GROUP)
    v0, i0 = top(logits, e_mask)
    v1, i1 = top(logits, e_mask & (lane_f != i0))
    e1 = jnp.exp(v1 - v0)
    w0 = p_g / (1.0 + e1)
    w1 = p_g * e1 / (1.0 + e1)
    out = jnp.where(lane == 0, i0 - N_GROUPS, 0.0)
    out = jnp.where(lane == 1, i1 - N_GROUPS, out)
    out = jnp.where(lane == 2, w0, out)
    out = jnp.where(lane == 3, w1, out)
    r_out[...] = out


def _router(x2, g, w, bias, tm):
    t, d = x2.shape
    full = lambda a: pl.BlockSpec(a.shape, lambda i: (0, 0))
    return pl.pallas_call(
        _router_kernel,
        grid=(t // tm,),
        in_specs=[pl.BlockSpec((tm, d), lambda i: (i, 0)), full(g), full(w), full(bias)],
        out_specs=[pl.BlockSpec((tm, d), lambda i: (i, 0)), pl.BlockSpec((tm, LANES), lambda i: (i, 0))],
        out_shape=[jax.ShapeDtypeStruct((t, d), BF16), jax.ShapeDtypeStruct((t, LANES), F32)],
        compiler_params=_cp(("parallel",)),
        name="moe_router",
    )(x2, g, w, bias)


def _expert_kernel(blk_e_ref, n_used_ref, x_ref, w1_ref, w3_ref, w2_ref, o_ref, w13_s, w2_s):
    i = pl.program_id(0)
    used = i < n_used_ref[0]
    de = w1_ref.shape[1]

    @pl.when(used & ((i == 0) | (blk_e_ref[i] != blk_e_ref[jnp.maximum(i - 1, 0)])))
    def _():
        w13_s[:, :de] = w1_ref[...].astype(BF16)
        w13_s[:, de:] = w3_ref[...].astype(BF16)
        w2_s[...] = w2_ref[...].astype(BF16)

    @pl.when(used)
    def _():
        hid = jnp.dot(x_ref[...], w13_s[...], preferred_element_type=F32)
        a = hid[:, :de]
        act = (a / (1.0 + jnp.exp(-a))) * hid[:, de:]
        o_ref[...] = jnp.dot(act.astype(BF16), w2_s[...], preferred_element_type=F32).astype(o_ref.dtype)

    @pl.when(jnp.logical_not(used))
    def _():
        o_ref[...] = jnp.zeros(o_ref.shape, o_ref.dtype)


def _experts(blk_e, n_used, xr, w1, w3, w2, layer, rows_per_block):
    rows, d = xr.shape
    de = w1.shape[3]
    weight = lambda w: pl.BlockSpec((None, None) + w.shape[2:], lambda i, be, nu: (layer, be[i], 0, 0))
    grid_spec = pltpu.PrefetchScalarGridSpec(
        num_scalar_prefetch=2,
        grid=(rows // rows_per_block,),
        in_specs=[pl.BlockSpec((rows_per_block, d), lambda i, be, nu: (i, 0)), weight(w1), weight(w3), weight(w2)],
        out_specs=pl.BlockSpec((rows_per_block, d), lambda i, be, nu: (i, 0)),
        scratch_shapes=[pltpu.VMEM((d, 2 * de), BF16), pltpu.VMEM((de, d), BF16)],
    )
    return pl.pallas_call(
        _expert_kernel,
        grid_spec=grid_spec,
        out_shape=jax.ShapeDtypeStruct((rows, d), BF16),
        compiler_params=_cp(("arbitrary",)),
        name="moe_experts",
    )(blk_e, n_used, xr, w1, w3, w2)


def _combine_kernel(x_ref, y0_ref, y1_ref, r_ref, g_ref, o_ref, *, final_norm):
    r = r_ref[...]
    y = x_ref[...] + r[:, 2:3] * y0_ref[...].astype(F32) + r[:, 3:4] * y1_ref[...].astype(F32)
    if final_norm:
        y = _rms(y, g_ref[...], 1.0 / y.shape[-1])
    o_ref[...] = y


def _combine(x2, y0, y1, route, g, final_norm, tm):
    t, d = x2.shape
    row = lambda w: pl.BlockSpec((tm, w), lambda i: (i, 0))
    return pl.pallas_call(
        functools.partial(_combine_kernel, final_norm=final_norm),
        grid=(t // tm,),
        in_specs=[row(d), row(d), row(d), row(LANES), pl.BlockSpec((1, d), lambda i: (0, 0))],
        out_specs=row(d),
        out_shape=jax.ShapeDtypeStruct((t, d), F32),
        compiler_params=_cp(("parallel",)),
        name="moe_combine",
    )(x2, y0, y1, route, g)


def _dispatch(route, rows_per_block):
    t = route.shape[0]
    eid = route[:, :TOP_K].astype(jnp.int32).reshape(-1)
    n = eid.shape[0]
    order = jnp.argsort(eid).astype(jnp.int32)
    rank = jnp.argsort(order).astype(jnp.int32)
    experts = jnp.arange(N_EXPERTS, dtype=jnp.int32)
    counts = jnp.sum((eid[:, None] == experts[None, :]).astype(jnp.int32), axis=0)
    start = jnp.cumsum(counts) - counts
    padded = (counts + rows_per_block - 1) // rows_per_block * rows_per_block
    pend = jnp.cumsum(padded)
    pstart = pend - padded
    dest = (rank + (pstart - start)[eid]).reshape(t, TOP_K)
    n_blocks = n // rows_per_block + N_EXPERTS
    blk_first = jnp.arange(n_blocks, dtype=jnp.int32) * rows_per_block
    blk_e = jnp.minimum(jnp.sum((pend[None, :] <= blk_first[:, None]).astype(jnp.int32), axis=1), N_EXPERTS - 1)
    off = (blk_first - pstart[blk_e])[:, None] + jnp.arange(rows_per_block, dtype=jnp.int32)[None, :]
    src = jnp.clip(start[blk_e][:, None] + off, 0, n - 1)
    row_tok = jnp.where(off < counts[blk_e][:, None], order[src] // TOP_K, 0).reshape(-1)
    n_used = (pend[-1] // rows_per_block).astype(jnp.int32).reshape(1)
    return row_tok, dest, blk_e, n_used


def _rot_cols(w):
    half = w.shape[-1] // 2
    return jnp.concatenate([-w[..., half:], w[..., :half]], axis=-1)


def _pack_w_in(w):
    d = w.shape[0]
    blk = A_HEADS * 2 * A_DH
    n_ab = 6 * blk
    cq = w[:, n_ab:n_ab + C_Q_RANK]
    ckv = w[:, n_ab + C_Q_RANK:n_ab + C_Q_RANK + C_KV_RANK]
    ckr = w[:, n_ab + C_Q_RANK + C_KV_RANK:n_ab + C_Q_RANK + C_KV_RANK + C_ROPE]
    gates = w[:, n_ab + C_Q_RANK + C_KV_RANK + C_ROPE:]
    z = lambda n: jnp.zeros((d, n), w.dtype)
    tail = LANES - C_NOPE - C_ROPE
    packed = jnp.concatenate([w[:, :2 * blk], w[:, 3 * blk:5 * blk], cq, z(LANES), ckv,
                              z(C_NOPE), ckr, z(tail), z(C_NOPE), _rot_cols(ckr), z(tail), gates], axis=1)
    return packed.astype(BF16), w[:, 2 * blk:3 * blk], w[:, 5 * blk:6 * blk]


def _pack_w_uq(w):
    wq = w.reshape(C_Q_RANK, C_HEADS, C_NOPE + C_ROPE)
    pad_rows = 4 * LANES - C_Q_RANK
    tail = LANES - C_NOPE - C_ROPE
    q1 = jnp.pad(wq, ((0, pad_rows), (0, 0), (0, tail))).reshape(4 * LANES, C_HEADS * LANES)
    q2 = jnp.pad(_rot_cols(wq[:, :, C_NOPE:]), ((0, pad_rows), (0, 0), (C_NOPE, tail))).reshape(4 * LANES, C_HEADS * LANES)
    return q1.astype(BF16), q2.astype(BF16)


def _pack_w_ukv(w):
    wkv = w.reshape(C_KV_RANK, C_HEADS, C_NOPE + C_DV)
    wk = jnp.pad(wkv[:, :, :C_NOPE], ((0, 0), (0, 0), (0, LANES - C_NOPE))).reshape(C_KV_RANK, C_HEADS * LANES)
    wv = wkv[:, :, C_NOPE:].reshape(C_KV_RANK, C_HEADS * C_DV)
    return wk.astype(BF16), wv


def kernel(x, mem, positions, mix_norm_g, w_in, diff_lambda, diff_subln_g, mla_q_norm_g, w_uq, mla_kv_norm_g, w_ukv, w_branch, w_out, cross_norm_g, mem_norm_g, w_xq, w_xkv, w_xo, ffn_norm_g, w_group, b_group, w_router, b_router, w1, w3, w2, final_norm_g):
    weights = (mix_norm_g, w_in, diff_lambda, diff_subln_g, mla_q_norm_g, w_uq, mla_kv_norm_g, w_ukv, w_branch, w_out,
               cross_norm_g, mem_norm_g, w_xq, w_xkv, w_xo, ffn_norm_g, w_group, b_group, w_router, b_router,
               w1, w3, w2, final_norm_g)
    groups = BATCH_GROUPS if x.shape[0] % BATCH_GROUPS == 0 else 1
    per = x.shape[0] // groups
    outs = [_forward(x[i * per:(i + 1) * per], mem[i * per:(i + 1) * per], positions[i * per:(i + 1) * per], *weights)
            for i in range(groups)]
    return outs[0] if groups == 1 else jnp.concatenate(outs, axis=0)


def _forward(x, mem, positions, mix_norm_g, w_in, diff_lambda, diff_subln_g, mla_q_norm_g, w_uq, mla_kv_norm_g, w_ukv, w_branch, w_out, cross_norm_g, mem_norm_g, w_xq, w_xkv, w_xo, ffn_norm_g, w_group, b_group, w_router, b_router, w1, w3, w2, final_norm_g):
    b, s, d = x.shape
    depth = w_in.shape[0]
    t = b * s
    n_mem = mem.shape[1]
    tl = _tiles(s)
    assert PROJ_COLS == COL_G * LANES + N_BRANCHES * d and s % (2 * tl["t_b"]) == 0

    pos_f = positions.astype(F32)
    posq_row = pos_f.reshape(b, 1, s)
    posk_rep = jnp.broadcast_to(pos_f[:, :, None], (b, s, LANES))
    cos, sin = _rope_tables(pos_f.reshape(t, 1), tl["tm_tok"])
    slopes_a2 = jnp.exp2(-8.0 * jnp.arange(1, A_HEADS + 1, dtype=F32) / A_HEADS) * LOG2E
    reach = max((w // (2 * dl)) * dl for w, dl in B_PATTERNS)
    b_bias = _b_bias_tables(tl["t_b"], 2 * (-(-reach // tl["t_b"])) + 1)
    mem2 = mem.reshape(b * n_mem, d)

    x2 = x.reshape(t, d)
    for l in range(depth):
        w_main, w_av, w_bv = _pack_w_in(w_in[l])
        proj2, vt_a, vt_b = _in_proj(x2, mix_norm_g[l], w_main, *_vt_weights(w_av, 2 * A_DH), *_vt_weights(w_bv, B_DH),
                                     tl["tm_proj"], tl["tn_proj"])
        proj3 = proj2.reshape(b, s, PROJ_COLS)
        lq = diff_lambda[l].astype(F32)
        lam_init = 0.8 - 0.6 * math.exp(-0.3 * l)
        lam = (jnp.exp(jnp.sum(lq[0] * lq[1])) - jnp.exp(jnp.sum(lq[2] * lq[3])) + lam_init).reshape(1)
        oa = _attn_a(proj3, vt_a, posq_row, posk_rep, lam, slopes_a2, diff_subln_g[l], 1.0 - lam_init,
                     tl["tq_a"], tl["tk_a"])
        ob = _attn_b(proj3, vt_b, b_bias, tl["t_b"])
        wq1, wq2 = _pack_w_uq(w_uq[l])
        wk, wv = _pack_w_ukv(w_ukv[l])
        gq = jnp.pad(mla_q_norm_g[l], (0, 4 * LANES - C_Q_RANK)).reshape(1, 4 * LANES)
        qc, kc, vt_c = _mla_prep(proj2, cos, sin, gq, mla_kv_norm_g[l].reshape(1, C_KV_RANK),
                                 wq1, wq2, wk, *_vt_weights(wv, C_DV), tl["tm_tok"])
        oc = _attn_c(qc.reshape(b, s, -1), kc.reshape(b, s, -1), vt_c, tl["tq_c"], tl["tk_c"])
        x2 = _merge(oa.reshape(t, -1), ob.reshape(t, -1), oc.reshape(t, -1), proj2, x2,
                    w_branch[l].astype(BF16), w_out[l].astype(BF16), tl["tm_tok"])
        kv = _norm_matmul(mem2, mem_norm_g[l], w_xkv[l].astype(BF16), min(1024, b * n_mem), w_xkv.shape[2])
        kbd, vbd = _block_diag_kv(kv.reshape(b, n_mem, -1))
        x2 = _cross(x2.reshape(b, s, d), cross_norm_g[l].reshape(1, d), w_xq[l].astype(BF16), kbd, vbd,
                    w_xo[l].astype(BF16), tl["tm_tok"]).reshape(t, d)
        w_r = jnp.pad(jnp.concatenate([w_group[l], w_router[l]], axis=1), ((0, 0), (0, LANES - N_GROUPS - N_EXPERTS)))
        b_r = jnp.pad(jnp.concatenate([b_group[l], b_router[l]]), (0, LANES - N_GROUPS - N_EXPERTS)).reshape(1, LANES)
        h, route = _router(x2, ffn_norm_g[l].reshape(1, d), w_r, b_r, tl["tm_tok"])
        row_tok, dest, blk_e, n_used = _dispatch(route, tl["moe_rows"])
        yr = _experts(blk_e, n_used, h[row_tok], w1, w3, w2, l, tl["moe_rows"])
        x2 = _combine(x2, yr[dest[:, 0]], yr[dest[:, 1]], route, final_norm_g.reshape(1, d),
                      l == depth - 1, tl["tm_tok"])
    return x2.reshape(b, s, d)
```

```python
import functools
import math

import jax
import jax.numpy as jnp
from jax import lax
from jax.experimental import pallas as pl
from jax.experimental.pallas import tpu as pltpu

F32 = jnp.float32
BF16 = jnp.bfloat16

LANES = 128
NORM_EPS = 1e-6
LOG2E = math.log2(math.e)
NEG_BIG = -1e30

A_HEADS, A_DH = 4, 64
B_HEADS, B_DH = 8, 64
B_PATTERNS = ((128, 1), (512, 4), (2048, 16))
C_HEADS, C_Q_RANK, C_KV_RANK, C_NOPE, C_ROPE, C_DV = 8, 384, 256, 64, 32, 64
ROPE_THETA = 10000.0
N_BRANCHES = 3
X_HEADS, X_DH = 4, 64
N_GROUPS, EXPERTS_PER_GROUP, TOP_K = 4, 8, 2
N_EXPERTS = N_GROUPS * EXPERTS_PER_GROUP

COL_A = 0
COL_B = 8
COL_CQ = 16
COL_CKV = 20
COL_R1 = 22
COL_R2 = 23
COL_G = 24
PROJ_COLS = 48 * LANES

VMEM_LIMIT = 48 * 1024 * 1024


def _tiles(seq):
    return dict(
        tm_proj=min(1024, seq), tn_proj=1024,
        tm_tok=min(512, seq),
        tq_a=min(512, seq), tk_a=min(512, seq),
        t_b=min(512, seq),
        tq_c=min(512, seq), tk_c=min(512, seq),
        moe_rows=512,
    )


def _cp(sem):
    return pltpu.CompilerParams(dimension_semantics=sem, vmem_limit_bytes=VMEM_LIMIT)


def _rms(x, g, inv_n):
    ms = jnp.sum(x * x, axis=-1, keepdims=True) * inv_n
    return x * lax.rsqrt(ms + NORM_EPS) * g


def _norm_matmul_kernel(x_ref, g_ref, w_ref, o_ref, h_ref, *, inv_n):
    @pl.when(pl.program_id(1) == 0)
    def _():
        h_ref[...] = _rms(x_ref[...].astype(F32), g_ref[...], inv_n).astype(BF16)

    o_ref[...] = jnp.dot(h_ref[...], w_ref[...], preferred_element_type=F32).astype(o_ref.dtype)


def _norm_matmul(x, g, w, tm, tn, out_dtype=BF16):
    m, k = x.shape
    n = w.shape[1]
    return pl.pallas_call(
        functools.partial(_norm_matmul_kernel, inv_n=1.0 / k),
        grid=(m // tm, n // tn),
        in_specs=[pl.BlockSpec((tm, k), lambda i, j: (i, 0)),
                  pl.BlockSpec((1, k), lambda i, j: (0, 0)),
                  pl.BlockSpec((k, tn), lambda i, j: (0, j))],
        out_specs=pl.BlockSpec((tm, tn), lambda i, j: (i, j)),
        out_shape=jax.ShapeDtypeStruct((m, n), out_dtype),
        scratch_shapes=[pltpu.VMEM((tm, k), BF16)],
        compiler_params=_cp(("parallel", "arbitrary")),
        name="norm_matmul",
    )(x, g.reshape(1, k), w)


def _in_proj_kernel(x_ref, g_ref, w_ref, wta_ref, wtb_ref, ona_ref, onb_ref, o_ref, vta_ref, vtb_ref, h_ref, *, inv_n):
    @pl.when(pl.program_id(1) == 0)
    def _():
        h = _rms(x_ref[...].astype(F32), g_ref[...], inv_n).astype(BF16)
        h_ref[...] = h
        vta_ref[...] = (_kq(wta_ref[...], h) + ona_ref[...]).astype(vta_ref.dtype)
        vtb_ref[...] = (_kq(wtb_ref[...], h) + onb_ref[...]).astype(vtb_ref.dtype)

    o_ref[...] = jnp.dot(h_ref[...], w_ref[...], preferred_element_type=F32).astype(o_ref.dtype)


def _in_proj(x, g, w, wta, ona, wtb, onb, tm, tn):
    m, k = x.shape
    n = w.shape[1]
    full = lambda a: pl.BlockSpec(a.shape, lambda i, j: (0, 0))
    return pl.pallas_call(
        functools.partial(_in_proj_kernel, inv_n=1.0 / k),
        grid=(m // tm, n // tn),
        in_specs=[pl.BlockSpec((tm, k), lambda i, j: (i, 0)),
                  pl.BlockSpec((1, k), lambda i, j: (0, 0)),
                  pl.BlockSpec((k, tn), lambda i, j: (0, j)),
                  full(wta), full(wtb), full(ona), full(onb)],
        out_specs=[pl.BlockSpec((tm, tn), lambda i, j: (i, j)),
                   pl.BlockSpec((wta.shape[0], tm), lambda i, j: (0, i)),
                   pl.BlockSpec((wtb.shape[0], tm), lambda i, j: (0, i))],
        out_shape=[jax.ShapeDtypeStruct((m, n), BF16),
                   jax.ShapeDtypeStruct((wta.shape[0], m), BF16),
                   jax.ShapeDtypeStruct((wtb.shape[0], m), BF16)],
        scratch_shapes=[pltpu.VMEM((tm, k), BF16)],
        compiler_params=_cp(("parallel", "arbitrary")),
        name="in_proj",
    )(x, g.reshape(1, k), w, wta, wtb, ona, onb)


ONES_ROWS = 16


def _vt_weights(wv, dv):
    k, w = wv.shape
    wt = jnp.pad(wv.T.reshape(w // dv, dv, k), ((0, 0), (0, ONES_ROWS), (0, 0))).reshape(-1, k)
    ones = jnp.zeros((w // dv, dv + ONES_ROWS, 1), F32).at[:, dv, 0].set(1.0).reshape(-1, 1)
    return wt.astype(BF16), ones


def _split_q(q_ref, qs_ref, scale):
    q = q_ref[...].astype(F32) * scale
    lane = lax.broadcasted_iota(jnp.int32, q.shape, 1)
    qs_ref[0] = jnp.where(lane < LANES // 2, q, 0.0).astype(BF16)
    qs_ref[1] = jnp.where(lane >= LANES // 2, q, 0.0).astype(BF16)


def _kq(k, q):
    return lax.dot_general(k, q, (((1,), (1,)), ((), ())), preferred_element_type=F32)


def _put_scores(s, s_ref, cm_ref, slot, c):
    s_ref[slot, c] = s
    cm_ref[slot, c] = jnp.max(s, axis=0, keepdims=True)


def _update(vt_of, s_ref, cm_ref, m_ref, acc_ref, slot):
    for c in range(2):
        m_prev = m_ref[c]
        m_new = jnp.maximum(m_prev, cm_ref[slot, c])
        alpha = jnp.exp2(m_prev - m_new)
        p = jnp.exp2(s_ref[slot, c] - m_new).astype(BF16)
        acc_ref[c] = alpha * acc_ref[c] + jnp.dot(vt_of(c), p, preferred_element_type=F32)
        m_ref[c] = m_new


def _init_state(m_ref, acc_ref):
    m_ref[...] = jnp.full(m_ref.shape, NEG_BIG, F32)
    acc_ref[...] = jnp.zeros(acc_ref.shape, F32)


def _pipelined(n, scores, update, unroll):
    scores(0, 0)

    def body(jj, carry):
        j = unroll * jj
        for u in range(unroll):
            scores(j + u + 1, (u + 1) % 2)
            update(j + u, u % 2)
        return carry

    looped = (n - 1) // unroll
    lax.fori_loop(0, looped, body, 0)
    for j in range(unroll * looped, n):
        if j + 1 < n:
            scores(j + 1, (j + 1) % 2)
        update(j, j % 2)


def _pipelined_list(n, chunk, scores, update):
    scores(chunk(0), 0)

    def body(unroll, first):
        def run(jj, carry):
            t = first + unroll * jj
            for u in range(unroll):
                scores(chunk(t + u + 1), (u + 1) % 2)
                update(chunk(t + u), u % 2)
            return carry
        return run

    quads = (n - 2) // 4
    lax.fori_loop(0, quads, body(4, 0), 0)
    lax.fori_loop(0, (n - 4 * quads) // 2 - 1, body(2, 4 * quads), 0)
    scores(chunk(n - 1), 1)
    update(chunk(n - 2), 0)
    update(chunk(n - 1), 1)


def _normalized(acc_ref, c):
    dv = acc_ref.shape[1] - ONES_ROWS
    return acc_ref[c, :dv, :] / acc_ref[c, dv:dv + 1, :]


def _attn_scratch(tq, tk, dv):
    return [pltpu.VMEM((2, 1, tq), F32), pltpu.VMEM((2, dv + ONES_ROWS, tq), F32),
            pltpu.VMEM((2, 2, tk, tq), F32), pltpu.VMEM((2, 2, 1, tq), F32)]


SKIP_MARGIN = 150.0


def _block_norms_kernel(qk_ref, sel_ref, o_ref):
    x = qk_ref[...].astype(F32)
    lane = lax.broadcasted_iota(jnp.int32, x.shape, 1)
    x = jnp.where(lane < x.shape[1] // 2, (x * (A_DH ** -0.5 * LOG2E)).astype(BF16).astype(F32), x)
    sums = jnp.dot(x * x, sel_ref[...], preferred_element_type=F32)
    o_ref[...] = jnp.sqrt(jnp.max(sums, axis=0, keepdims=True))


def _block_norms(proj3, t):
    b, s, _ = proj3.shape
    w = 2 * A_HEADS * 2 * A_DH
    sel = (jnp.arange(w)[:, None] // A_DH == jnp.arange(LANES)[None, :]).astype(BF16)
    return pl.pallas_call(
        _block_norms_kernel,
        grid=(b, s // t),
        in_specs=[pl.BlockSpec((None, t, w), lambda bb, i: (bb, i, COL_A * LANES // w)),
                  pl.BlockSpec((w, LANES), lambda bb, i: (0, 0))],
        out_specs=pl.BlockSpec((None, None, 1, LANES), lambda bb, i: (bb, i, 0, 0)),
        out_shape=jax.ShapeDtypeStruct((b, s // t, 1, LANES), F32),
        compiler_params=_cp(("parallel", "parallel")),
        name="block_norms",
    )(proj3, sel)[:, :, 0, :]


def _a_chunk_lists(proj3, pos_f, slopes2, tq, tk):
    assert tq == tk
    b, s, _ = proj3.shape
    norms = _block_norms(proj3, tq)[:, :, :4 * A_HEADS].reshape(b, s // tq, 2, A_HEADS, 2).max(axis=-1)
    qn = norms[:, :, 0].transpose(0, 2, 1)
    kn = norms[:, :, 1].transpose(0, 2, 1)
    qk = 1.01 * qn[:, :, :, None] * kn[:, :, None, :] + 1.0
    pq = pos_f.reshape(b, s // tq, tq)
    pk = pos_f.reshape(b, s // tk, tk)
    qlo, qhi = jnp.min(pq, -1)[:, :, None], jnp.max(pq, -1)[:, :, None]
    klo, khi = jnp.min(pk, -1)[:, None, :], jnp.max(pk, -1)[:, None, :]
    dmin = jnp.maximum(jnp.maximum(klo - qhi, qlo - khi), 0.0)
    dmax = jnp.maximum(khi - qlo, qhi - klo)
    sl = slopes2[None, :, None, None]
    lower = -qk - sl * dmax[:, None]
    upper = qk - sl * dmin[:, None]
    nk = s // tk
    j = jnp.arange(nk, dtype=jnp.int32)
    cover = jnp.max(jnp.min(jnp.abs(pq[:, :, :, None] - pk[:, :, None, :]), axis=-1), axis=-1)
    lower = jnp.where(j[:, None] == j[None, :], jnp.maximum(lower, -qk - sl * cover[:, None, :, None]), lower)
    first = jnp.argmax(lower, axis=-1).astype(jnp.int32)[..., None]
    keep = (upper >= jnp.max(lower, axis=-1, keepdims=True) - SKIP_MARGIN) | (j == first)
    lst = jnp.argsort(jnp.where(j == first, -1, jnp.where(keep, j, nk + j)), axis=-1).astype(jnp.int32)
    cnt = jnp.sum(keep.astype(jnp.int32), axis=-1)
    return lst.reshape(-1), (cnt + cnt % 2).reshape(-1)


def _attn_a_kernel(lst_ref, cnt_ref, lam_ref, slope_ref, q_ref, k_ref, vt_ref, pq_ref, pk_ref, g_ref, o_ref,
                   qs_ref, m_ref, acc_ref, s_ref, cm_ref, *, tk, post_scale):
    tq = q_ref.shape[0]
    nk = k_ref.shape[0] // tk
    blk = (pl.program_id(0) * pl.num_programs(1) + pl.program_id(1)) * pl.num_programs(2) + pl.program_id(2)
    sl2 = slope_ref[pl.program_id(1)]
    _split_q(q_ref, qs_ref, A_DH ** -0.5 * LOG2E)
    _init_state(m_ref, acc_ref)
    pq = pq_ref[...] * sl2

    def scores(j, slot):
        ks = pl.multiple_of(j * tk, tk)
        k = k_ref[pl.ds(ks, tk), :]
        pk = pk_ref[pl.ds(ks, tk), :] * sl2
        bias = jnp.abs(jnp.concatenate([pk] * (tq // LANES), axis=1) - pq)
        for c in range(2):
            _put_scores(_kq(k, qs_ref[c]) - bias, s_ref, cm_ref, slot, c)

    def update(j, slot):
        ks = pl.multiple_of(j * tk, tk)
        _update(lambda c: vt_ref[:, pl.ds(ks, tk)], s_ref, cm_ref, m_ref, acc_ref, slot)

    _pipelined_list(cnt_ref[blk], lambda t: lst_ref[blk * nk + t], scores, update)
    o = (_normalized(acc_ref, 0) - lam_ref[0] * _normalized(acc_ref, 1)).T
    o_ref[...] = (_rms(o, g_ref[...], 1.0 / LANES) * post_scale).astype(o_ref.dtype)


def _attn_a(proj, vt, posq_row, posk_rep, lam, slopes2, g, post_scale, tq, tk):
    b, s, _ = proj.shape
    lst, cnt = _a_chunk_lists(proj, posq_row.reshape(b, s), slopes2, tq, tk)
    smem = pl.BlockSpec(memory_space=pltpu.SMEM)
    grid_spec = pltpu.PrefetchScalarGridSpec(
        num_scalar_prefetch=2,
        grid=(b, A_HEADS, s // tq),
        in_specs=[smem, smem,
                  pl.BlockSpec((None, tq, LANES), lambda bb, h, i, *_: (bb, i, COL_A + h)),
                  pl.BlockSpec((None, s, LANES), lambda bb, h, i, *_: (bb, 0, COL_A + A_HEADS + h)),
                  pl.BlockSpec((vt.shape[0] // A_HEADS, s), lambda bb, h, i, *_: (h, bb)),
                  pl.BlockSpec((None, 1, tq), lambda bb, h, i, *_: (bb, 0, i)),
                  pl.BlockSpec((None, s, LANES), lambda bb, h, i, *_: (bb, 0, 0)),
                  pl.BlockSpec((1, LANES), lambda bb, h, i, *_: (0, 0))],
        out_specs=pl.BlockSpec((None, tq, LANES), lambda bb, h, i, *_: (bb, i, h)),
        scratch_shapes=[pltpu.VMEM((2, tq, LANES), BF16)] + _attn_scratch(tq, tk, LANES),
    )
    return pl.pallas_call(
        functools.partial(_attn_a_kernel, tk=tk, post_scale=post_scale),
        grid_spec=grid_spec,
        out_shape=jax.ShapeDtypeStruct((b, s, A_HEADS * LANES), BF16),
        compiler_params=_cp(("parallel", "parallel", "arbitrary")),
        name="attn_diff",
    )(lst, cnt, lam, slopes2, proj, proj, vt, posq_row, posk_rep, g.reshape(1, LANES))


def _attn_b_kernel(q_ref, k_ref, vt_ref, bias_ref, o_ref, qs_ref, m_ref, acc_ref, s_ref, cm_ref, *, nband):
    t = q_ref.shape[0]
    nkb = k_ref.shape[0] // t
    rows = vt_ref.shape[0] // 2
    i = pl.program_id(2)
    half = nband // 2
    _split_q(q_ref, qs_ref, B_DH ** -0.5 * LOG2E)
    _init_state(m_ref, acc_ref)

    def start(jj):
        return pl.multiple_of(jnp.clip(i + jj - half, 0, nkb - 1) * t, t)

    def scores(jj, slot):
        kb = i + jj - half
        table = jnp.where((kb >= 0) & (kb < nkb), jj, nband)
        k = k_ref[pl.ds(start(jj), t), :]
        for c in range(2):
            _put_scores(_kq(k, qs_ref[c]) + bias_ref[c, table], s_ref, cm_ref, slot, c)

    def update(jj, slot):
        ks = start(jj)
        _update(lambda c: vt_ref[c * rows:(c + 1) * rows, pl.ds(ks, t)], s_ref, cm_ref, m_ref, acc_ref, slot)

    _pipelined(nband, scores, update, unroll=2)
    o = jnp.concatenate([_normalized(acc_ref, 0), _normalized(acc_ref, 1)], axis=0)
    o_ref[...] = o.T.astype(o_ref.dtype)


def _b_bias_tables(t, nband):
    half = nband // 2
    r = jnp.arange(t, dtype=jnp.int32)[None, :, None]
    c = jnp.arange(t, dtype=jnp.int32)[None, None, :]
    jj = jnp.arange(nband + 1, dtype=jnp.int32)[:, None, None]
    ao = jnp.abs((jj - half) * t + r - c)
    mult = jnp.zeros(ao.shape, jnp.int32)
    for window, dilation in B_PATTERNS:
        reach = (window // (2 * dilation)) * dilation
        mult = mult + ((ao % dilation == 0) & (ao <= reach)).astype(jnp.int32)
    mult = jnp.where(jj < nband, mult, 0)
    slopes = jnp.exp2(-8.0 * jnp.arange(1, B_HEADS + 1, dtype=F32) / B_HEADS) * LOG2E
    bias = jnp.log2(jnp.maximum(mult, 1).astype(F32))[None] - slopes[:, None, None, None] * ao.astype(F32)[None]
    bias = jnp.where((mult > 0)[None], bias, NEG_BIG)
    return bias.reshape(B_HEADS // 2, 2, nband + 1, t, t)


def _attn_b(proj, vt, bias, t):
    b, s, _ = proj.shape
    nband = bias.shape[2] - 1
    npair = B_HEADS // 2
    return pl.pallas_call(
        functools.partial(_attn_b_kernel, nband=nband),
        grid=(b, npair, s // t),
        in_specs=[pl.BlockSpec((None, t, LANES), lambda bb, p, i: (bb, i, COL_B + p)),
                  pl.BlockSpec((None, s, LANES), lambda bb, p, i: (bb, 0, COL_B + npair + p)),
                  pl.BlockSpec((vt.shape[0] // npair, s), lambda bb, p, i: (p, bb)),
                  pl.BlockSpec((None,) + bias.shape[1:], lambda bb, p, i: (p, 0, 0, 0, 0))],
        out_specs=pl.BlockSpec((None, t, LANES), lambda bb, p, i: (bb, i, p)),
        out_shape=jax.ShapeDtypeStruct((b, s, npair * LANES), BF16),
        scratch_shapes=[pltpu.VMEM((2, t, LANES), BF16)] + _attn_scratch(t, t, B_DH),
        compiler_params=_cp(("parallel", "parallel", "arbitrary")),
        name="attn_dilated",
    )(proj, proj, vt, bias)


def _rope_table_kernel(pos_ref, invf_ref, c_ref, s_ref):
    ang = pos_ref[...] * invf_ref[...]
    c_ref[...] = jnp.cos(ang)
    s_ref[...] = jnp.sin(ang)


def _rope_tables(pos_col, tm):
    t = pos_col.shape[0]
    inv = ROPE_THETA ** (-jnp.arange(0, C_ROPE, 2, dtype=F32) / C_ROPE)
    invf = jnp.concatenate([jnp.zeros((C_NOPE,), F32), inv, inv,
                            jnp.zeros((LANES - C_NOPE - C_ROPE,), F32)]).reshape(1, LANES)
    spec = pl.BlockSpec((tm, LANES), lambda i: (i, 0))
    return pl.pallas_call(
        _rope_table_kernel,
        grid=(t // tm,),
        in_specs=[pl.BlockSpec((tm, 1), lambda i: (i, 0)), pl.BlockSpec((1, LANES), lambda i: (0, 0))],
        out_specs=[spec, spec],
        out_shape=[jax.ShapeDtypeStruct((t, LANES), F32)] * 2,
        compiler_params=_cp(("parallel",)),
        name="rope_tables",
    )(pos_col, invf)


def _mla_prep_kernel(cq_ref, ckv_ref, r1_ref, r2_ref, c_ref, s_ref, gq_ref, gkv_ref,
                     wq1_ref, wq2_ref, wk_ref, wvt_ref, onv_ref, q_out, k_out, vt_out):
    qn = _rms(cq_ref[...].astype(F32), gq_ref[...], 1.0 / C_Q_RANK).astype(BF16)
    kvn = _rms(ckv_ref[...].astype(F32), gkv_ref[...], 1.0 / C_KV_RANK).astype(BF16)
    q1 = jnp.dot(qn, wq1_ref[...], preferred_element_type=F32)
    q2 = jnp.dot(qn, wq2_ref[...], preferred_element_type=F32)
    k1 = jnp.dot(kvn, wk_ref[...], preferred_element_type=F32)
    vt_out[...] = (_kq(wvt_ref[...], kvn) + onv_ref[...]).astype(vt_out.dtype)
    cos = c_ref[...]
    sin = s_ref[...]
    k_rope = r1_ref[...].astype(F32) * cos + r2_ref[...].astype(F32) * sin
    scale = (C_NOPE + C_ROPE) ** -0.5 * LOG2E
    for h in range(C_HEADS):
        sl = slice(h * LANES, (h + 1) * LANES)
        q_out[:, sl] = ((q1[:, sl] * cos + q2[:, sl] * sin) * scale).astype(q_out.dtype)
        k_out[:, sl] = (k1[:, sl] + k_rope).astype(k_out.dtype)


def _mla_prep(proj2, cos, sin, gq, gkv, wq1, wq2, wk, wvt, onv, tm):
    t = proj2.shape[0]
    full = lambda a: pl.BlockSpec(a.shape, lambda i: (0, 0))
    row = lambda w: pl.BlockSpec((tm, w), lambda i: (i, 0))
    return pl.pallas_call(
        _mla_prep_kernel,
        grid=(t // tm,),
        in_specs=[pl.BlockSpec((tm, 4 * LANES), lambda i: (i, COL_CQ // 4)),
                  pl.BlockSpec((tm, 2 * LANES), lambda i: (i, COL_CKV // 2)),
                  pl.BlockSpec((tm, LANES), lambda i: (i, COL_R1)),
                  pl.BlockSpec((tm, LANES), lambda i: (i, COL_R2)),
                  row(LANES), row(LANES), full(gq), full(gkv), full(wq1), full(wq2), full(wk), full(wvt), full(onv)],
        out_specs=[row(C_HEADS * LANES), row(C_HEADS * LANES), pl.BlockSpec((wvt.shape[0], tm), lambda i: (0, i))],
        out_shape=[jax.ShapeDtypeStruct((t, C_HEADS * LANES), BF16),
                   jax.ShapeDtypeStruct((t, C_HEADS * LANES), BF16),
                   jax.ShapeDtypeStruct((wvt.shape[0], t), BF16)],
        compiler_params=_cp(("parallel",)),
        name="mla_prep",
    )(proj2, proj2, proj2, proj2, cos, sin, gq, gkv, wq1, wq2, wk, wvt, onv)


def _attn_c_kernel(q_ref, k_ref, vt_ref, o_ref, m_ref, acc_ref, s_ref, cm_ref, *, tk):
    rows = vt_ref.shape[0] // 2
    _init_state(m_ref, acc_ref)

    def scores(j, slot):
        ks = pl.multiple_of(j * tk, tk)
        for c in range(2):
            sl = slice(c * LANES, (c + 1) * LANES)
            _put_scores(_kq(k_ref[pl.ds(ks, tk), sl], q_ref[:, sl]), s_ref, cm_ref, slot, c)

    def update(j, slot):
        ks = pl.multiple_of(j * tk, tk)
        _update(lambda c: vt_ref[c * rows:(c + 1) * rows, pl.ds(ks, tk)], s_ref, cm_ref, m_ref, acc_ref, slot)

    _pipelined(k_ref.shape[0] // tk, scores, update, unroll=4)
    o = jnp.concatenate([_normalized(acc_ref, 0), _normalized(acc_ref, 1)], axis=0)
    o_ref[...] = o.T.astype(o_ref.dtype)


def _attn_c(q, k, vt, tq, tk):
    b, s, _ = q.shape
    npair = C_HEADS // 2
    return pl.pallas_call(
        functools.partial(_attn_c_kernel, tk=tk),
        grid=(b, npair, s // tq),
        in_specs=[pl.BlockSpec((None, tq, 2 * LANES), lambda bb, p, i: (bb, i, p)),
                  pl.BlockSpec((None, s, 2 * LANES), lambda bb, p, i: (bb, 0, p)),
                  pl.BlockSpec((vt.shape[0] // npair, s), lambda bb, p, i: (p, bb))],
        out_specs=pl.BlockSpec((None, tq, LANES), lambda bb, p, i: (bb, i, p)),
        out_shape=jax.ShapeDtypeStruct((b, s, npair * LANES), BF16),
        scratch_shapes=_attn_scratch(tq, tk, C_DV),
        compiler_params=_cp(("parallel", "parallel", "arbitrary")),
        name="attn_latent",
    )(q, k, vt)


def _merge_kernel(oa_ref, ob_ref, oc_ref, g0_ref, g1_ref, g2_ref, x_ref, wb_ref, wo_ref, o_ref):
    z = None
    for n, (o_r, g_r) in enumerate(((oa_ref, g0_ref), (ob_ref, g1_ref), (oc_ref, g2_ref))):
        br = jnp.dot(o_r[...], wb_ref[n], preferred_element_type=F32)
        gate = 1.0 / (1.0 + jnp.exp(-g_r[...].astype(F32)))
        z = gate * br if z is None else z + gate * br
    o_ref[...] = x_ref[...] + jnp.dot(z.astype(BF16), wo_ref[...], preferred_element_type=F32)


def _merge(oa, ob, oc, proj2, x2, wb, wo, tm):
    t, d = x2.shape
    bw = oa.shape[1]
    row = lambda w: pl.BlockSpec((tm, w), lambda i: (i, 0))
    gate = lambda n: pl.BlockSpec((tm, d), lambda i: (i, COL_G * LANES // d + n))
    return pl.pallas_call(
        _merge_kernel,
        grid=(t // tm,),
        in_specs=[row(bw), row(bw), row(bw), gate(0), gate(1), gate(2), row(d),
                  pl.BlockSpec(wb.shape, lambda i: (0, 0, 0)), pl.BlockSpec(wo.shape, lambda i: (0, 0))],
        out_specs=row(d),
        out_shape=jax.ShapeDtypeStruct((t, d), F32),
        compiler_params=_cp(("parallel",)),
        name="branch_merge",
    )(oa, ob, oc, proj2, proj2, proj2, x2, wb, wo)


def _cross_kernel(x_ref, g_ref, wq_ref, kbd_ref, vbd_ref, wo_ref, o_ref, *, n_mem):
    x = x_ref[...]
    h = _rms(x, g_ref[...], 1.0 / x.shape[-1]).astype(BF16)
    q = (jnp.dot(h, wq_ref[...], preferred_element_type=F32) * (X_DH ** -0.5 * LOG2E)).astype(BF16)
    s = jnp.dot(q, kbd_ref[...], preferred_element_type=F32)
    ps = []
    for hh in range(X_HEADS):
        sh = s[:, hh * n_mem:(hh + 1) * n_mem]
        p = jnp.exp2(sh - jnp.max(sh, axis=-1, keepdims=True))
        ps.append((p / jnp.sum(p, axis=-1, keepdims=True)).astype(BF16))
    o = jnp.dot(jnp.concatenate(ps, axis=1), vbd_ref[...], preferred_element_type=F32)
    o_ref[...] = x + jnp.dot(o.astype(BF16), wo_ref[...], preferred_element_type=F32)


def _cross(x3, g, wq, kbd, vbd, wo, tm):
    b, s, d = x3.shape
    n_mem = kbd.shape[2] // X_HEADS
    full = lambda a: pl.BlockSpec(a.shape, lambda bb, i: (0, 0))
    return pl.pallas_call(
        functools.partial(_cross_kernel, n_mem=n_mem),
        grid=(b, s // tm),
        in_specs=[pl.BlockSpec((None, tm, d), lambda bb, i: (bb, i, 0)), full(g), full(wq),
                  pl.BlockSpec((None,) + kbd.shape[1:], lambda bb, i: (bb, 0, 0)),
                  pl.BlockSpec((None,) + vbd.shape[1:], lambda bb, i: (bb, 0, 0)), full(wo)],
        out_specs=pl.BlockSpec((None, tm, d), lambda bb, i: (bb, i, 0)),
        out_shape=jax.ShapeDtypeStruct((b, s, d), F32),
        compiler_params=_cp(("parallel", "parallel")),
        name="cross_attn",
    )(x3, g, wq, kbd, vbd, wo)


def _block_diag_kv(kv):
    b, m, _ = kv.shape
    kv = kv.reshape(b, m, 2, X_HEADS, X_DH)
    eye = jnp.eye(X_HEADS, dtype=kv.dtype)
    kt = kv[:, :, 0].transpose(0, 2, 3, 1)
    kbd = (kt[:, :, :, None, :] * eye[None, :, None, :, None]).reshape(b, X_HEADS * X_DH, X_HEADS * m)
    vt = kv[:, :, 1].transpose(0, 2, 1, 3)
    vbd = (vt[:, :, :, None, :] * eye[None, :, None, :, None]).reshape(b, X_HEADS * m, X_HEADS * X_DH)
    return kbd, vbd


def _router_kernel(x_ref, g_ref, w_ref, b_ref, h_out, r_out):
    x = x_ref[...]
    h = _rms(x, g_ref[...], 1.0 / x.shape[-1])
    h_out[...] = h.astype(h_out.dtype)
    logits = jnp.dot(h, w_ref[...], preferred_element_type=F32, precision=lax.Precision.HIGHEST) + b_ref[...]
    lane = lax.broadcasted_iota(jnp.int32, logits.shape, 1)
    lane_f = lane.astype(F32)
    big = jnp.float32(4 * LANES)

    def top(vals, mask):
        mv = jnp.max(jnp.where(mask, vals, -jnp.inf), axis=-1, keepdims=True)
        idx = jnp.min(jnp.where(mask & (vals == mv), lane_f, big), axis=-1, keepdims=True)
        return mv, idx

    g_mask = lane < N_GROUPS
    g_max, g_idx = top(logits, g_mask)
    p_g = 1.0 / jnp.sum(jnp.where(g_mask, jnp.exp(logits - g_max), 0.0), axis=-1, keepdims=True)
    first = N_GROUPS + g_idx * EXPERTS_PER_GROUP
    e_mask = (lane_f >= first) & (lane_f < first + EXPERTS_PER_GROUP)
    v0, i0 = top(logits, e_mask)
    v1, i1 = top(logits, e_mask & (lane_f != i0))
    e1 = jnp.exp(v1 - v0)
    w0 = p_g / (1.0 + e1)
    w1 = p_g * e1 / (1.0 + e1)
    out = jnp.where(lane == 0, i0 - N_GROUPS, 0.0)
    out = jnp.where(lane == 1, i1 - N_GROUPS, out)
    out = jnp.where(lane == 2, w0, out)
    out = jnp.where(lane == 3, w1, out)
    r_out[...] = out


def _router(x2, g, w, bias, tm):
    t, d = x2.shape
    full = lambda a: pl.BlockSpec(a.shape, lambda i: (0, 0))
    return pl.pallas_call(
        _router_kernel,
        grid=(t // tm,),
        in_specs=[pl.BlockSpec((tm, d), lambda i: (i, 0)), full(g), full(w), full(bias)],
        out_specs=[pl.BlockSpec((tm, d), lambda i: (i, 0)), pl.BlockSpec((tm, LANES), lambda i: (i, 0))],
        out_shape=[jax.ShapeDtypeStruct((t, d), BF16), jax.ShapeDtypeStruct((t, LANES), F32)],
        compiler_params=_cp(("parallel",)),
        name="moe_router",
    )(x2, g, w, bias)


def _expert_kernel(blk_e_ref, n_used_ref, x_ref, w1_ref, w3_ref, w2_ref, o_ref, w13_s, w2_s):
    i = pl.program_id(0)
    used = i < n_used_ref[0]
    de = w1_ref.shape[1]

    @pl.when(used & ((i == 0) | (blk_e_ref[i] != blk_e_ref[jnp.maximum(i - 1, 0)])))
    def _():
        w13_s[:, :de] = w1_ref[...].astype(BF16)
        w13_s[:, de:] = w3_ref[...].astype(BF16)
        w2_s[...] = w2_ref[...].astype(BF16)

    @pl.when(used)
    def _():
        hid = jnp.dot(x_ref[...], w13_s[...], preferred_element_type=F32)
        a = hid[:, :de]
        act = (a / (1.0 + jnp.exp(-a))) * hid[:, de:]
        o_ref[...] = jnp.dot(act.astype(BF16), w2_s[...], preferred_element_type=F32).astype(o_ref.dtype)

    @pl.when(jnp.logical_not(used))
    def _():
        o_ref[...] = jnp.zeros(o_ref.shape, o_ref.dtype)


def _experts(blk_e, n_used, xr, w1, w3, w2, layer, rows_per_block):
    rows, d = xr.shape
    de = w1.shape[3]
    weight = lambda w: pl.BlockSpec((None, None) + w.shape[2:], lambda i, be, nu: (layer, be[i], 0, 0))
    grid_spec = pltpu.PrefetchScalarGridSpec(
        num_scalar_prefetch=2,
        grid=(rows // rows_per_block,),
        in_specs=[pl.BlockSpec((rows_per_block, d), lambda i, be, nu: (i, 0)), weight(w1), weight(w3), weight(w2)],
        out_specs=pl.BlockSpec((rows_per_block, d), lambda i, be, nu: (i, 0)),
        scratch_shapes=[pltpu.VMEM((d, 2 * de), BF16), pltpu.VMEM((de, d), BF16)],
    )
    return pl.pallas_call(
        _expert_kernel,
        grid_spec=grid_spec,
        out_shape=jax.ShapeDtypeStruct((rows, d), BF16),
        compiler_params=_cp(("arbitrary",)),
        name="moe_experts",
    )(blk_e, n_used, xr, w1, w3, w2)


def _combine_kernel(x_ref, y0_ref, y1_ref, r_ref, g_ref, o_ref, *, final_norm):
    r = r_ref[...]
    y = x_ref[...] + r[:, 2:3] * y0_ref[...].astype(F32) + r[:, 3:4] * y1_ref[...].astype(F32)
    if final_norm:
        y = _rms(y, g_ref[...], 1.0 / y.shape[-1])
    o_ref[...] = y


def _combine(x2, y0, y1, route, g, final_norm, tm):
    t, d = x2.shape
    row = lambda w: pl.BlockSpec((tm, w), lambda i: (i, 0))
    return pl.pallas_call(
        functools.partial(_combine_kernel, final_norm=final_norm),
        grid=(t // tm,),
        in_specs=[row(d), row(d), row(d), row(LANES), pl.BlockSpec((1, d), lambda i: (0, 0))],
        out_specs=row(d),
        out_shape=jax.ShapeDtypeStruct((t, d), F32),
        compiler_params=_cp(("parallel",)),
        name="moe_combine",
    )(x2, y0, y1, route, g)


def _dispatch(route, rows_per_block):
    t = route.shape[0]
    eid = route[:, :TOP_K].astype(jnp.int32).reshape(-1)
    n = eid.shape[0]
    order = jnp.argsort(eid).astype(jnp.int32)
    rank = jnp.argsort(order).astype(jnp.int32)
    experts = jnp.arange(N_EXPERTS, dtype=jnp.int32)
    counts = jnp.sum((eid[:, None] == experts[None, :]).astype(jnp.int32), axis=0)
    start = jnp.cumsum(counts) - counts
    padded = (counts + rows_per_block - 1) // rows_per_block * rows_per_block
    pend = jnp.cumsum(padded)
    pstart = pend - padded
    dest = (rank + (pstart - start)[eid]).reshape(t, TOP_K)
    n_blocks = n // rows_per_block + N_EXPERTS
    blk_first = jnp.arange(n_blocks, dtype=jnp.int32) * rows_per_block
    blk_e = jnp.minimum(jnp.sum((pend[None, :] <= blk_first[:, None]).astype(jnp.int32), axis=1), N_EXPERTS - 1)
    off = (blk_first - pstart[blk_e])[:, None] + jnp.arange(rows_per_block, dtype=jnp.int32)[None, :]
    src = jnp.clip(start[blk_e][:, None] + off, 0, n - 1)
    row_tok = jnp.where(off < counts[blk_e][:, None], order[src] // TOP_K, 0).reshape(-1)
    n_used = (pend[-1] // rows_per_block).astype(jnp.int32).reshape(1)
    return row_tok, dest, blk_e, n_used


def _rot_cols(w):
    half = w.shape[-1] // 2
    return jnp.concatenate([-w[..., half:], w[..., :half]], axis=-1)


def _pack_w_in(w):
    d = w.shape[0]
    blk = A_HEADS * 2 * A_DH
    n_ab = 6 * blk
    cq = w[:, n_ab:n_ab + C_Q_RANK]
    ckv = w[:, n_ab + C_Q_RANK:n_ab + C_Q_RANK + C_KV_RANK]
    ckr = w[:, n_ab + C_Q_RANK + C_KV_RANK:n_ab + C_Q_RANK + C_KV_RANK + C_ROPE]
    gates = w[:, n_ab + C_Q_RANK + C_KV_RANK + C_ROPE:]
    z = lambda n: jnp.zeros((d, n), w.dtype)
    tail = LANES - C_NOPE - C_ROPE
    packed = jnp.concatenate([w[:, :2 * blk], w[:, 3 * blk:5 * blk], cq, z(LANES), ckv,
                              z(C_NOPE), ckr, z(tail), z(C_NOPE), _rot_cols(ckr), z(tail), gates], axis=1)
    return packed.astype(BF16), w[:, 2 * blk:3 * blk], w[:, 5 * blk:6 * blk]


def _pack_w_uq(w):
    wq = w.reshape(C_Q_RANK, C_HEADS, C_NOPE + C_ROPE)
    pad_rows = 4 * LANES - C_Q_RANK
    tail = LANES - C_NOPE - C_ROPE
    q1 = jnp.pad(wq, ((0, pad_rows), (0, 0), (0, tail))).reshape(4 * LANES, C_HEADS * LANES)
    q2 = jnp.pad(_rot_cols(wq[:, :, C_NOPE:]), ((0, pad_rows), (0, 0), (C_NOPE, tail))).reshape(4 * LANES, C_HEADS * LANES)
    return q1.astype(BF16), q2.astype(BF16)


def _pack_w_ukv(w):
    wkv = w.reshape(C_KV_RANK, C_HEADS, C_NOPE + C_DV)
    wk = jnp.pad(wkv[:, :, :C_NOPE], ((0, 0), (0, 0), (0, LANES - C_NOPE))).reshape(C_KV_RANK, C_HEADS * LANES)
    wv = wkv[:, :, C_NOPE:].reshape(C_KV_RANK, C_HEADS * C_DV)
    return wk.astype(BF16), wv


def kernel(x, mem, positions, mix_norm_g, w_in, diff_lambda, diff_subln_g, mla_q_norm_g, w_uq, mla_kv_norm_g, w_ukv, w_branch, w_out, cross_norm_g, mem_norm_g, w_xq, w_xkv, w_xo, ffn_norm_g, w_group, b_group, w_router, b_router, w1, w3, w2, final_norm_g):
    b, s, d = x.shape
    depth = w_in.shape[0]
    t = b * s
    n_mem = mem.shape[1]
    tl = _tiles(s)
    assert PROJ_COLS == COL_G * LANES + N_BRANCHES * d and s % (2 * tl["t_b"]) == 0

    pos_f = positions.astype(F32)
    posq_row = pos_f.reshape(b, 1, s)
    posk_rep = jnp.broadcast_to(pos_f[:, :, None], (b, s, LANES))
    cos, sin = _rope_tables(pos_f.reshape(t, 1), tl["tm_tok"])
    slopes_a2 = jnp.exp2(-8.0 * jnp.arange(1, A_HEADS + 1, dtype=F32) / A_HEADS) * LOG2E
    reach = max((w // (2 * dl)) * dl for w, dl in B_PATTERNS)
    b_bias = _b_bias_tables(tl["t_b"], 2 * (-(-reach // tl["t_b"])) + 1)
    mem2 = mem.reshape(b * n_mem, d)

    x2 = x.reshape(t, d)
    for l in range(depth):
        w_main, w_av, w_bv = _pack_w_in(w_in[l])
        proj2, vt_a, vt_b = _in_proj(x2, mix_norm_g[l], w_main, *_vt_weights(w_av, 2 * A_DH), *_vt_weights(w_bv, B_DH),
                                     tl["tm_proj"], tl["tn_proj"])
        proj3 = proj2.reshape(b, s, PROJ_COLS)
        lq = diff_lambda[l].astype(F32)
        lam_init = 0.8 - 0.6 * math.exp(-0.3 * l)
        lam = (jnp.exp(jnp.sum(lq[0] * lq[1])) - jnp.exp(jnp.sum(lq[2] * lq[3])) + lam_init).reshape(1)
        oa = _attn_a(proj3, vt_a, posq_row, posk_rep, lam, slopes_a2, diff_subln_g[l], 1.0 - lam_init,
                     tl["tq_a"], tl["tk_a"])
        ob = _attn_b(proj3, vt_b, b_bias, tl["t_b"])
        wq1, wq2 = _pack_w_uq(w_uq[l])
        wk, wv = _pack_w_ukv(w_ukv[l])
        gq = jnp.pad(mla_q_norm_g[l], (0, 4 * LANES - C_Q_RANK)).reshape(1, 4 * LANES)
        qc, kc, vt_c = _mla_prep(proj2, cos, sin, gq, mla_kv_norm_g[l].reshape(1, C_KV_RANK),
                                 wq1, wq2, wk, *_vt_weights(wv, C_DV), tl["tm_tok"])
        oc = _attn_c(qc.reshape(b, s, -1), kc.reshape(b, s, -1), vt_c, tl["tq_c"], tl["tk_c"])
        x2 = _merge(oa.reshape(t, -1), ob.reshape(t, -1), oc.reshape(t, -1), proj2, x2,
                    w_branch[l].astype(BF16), w_out[l].astype(BF16), tl["tm_tok"])
        kv = _norm_matmul(mem2, mem_norm_g[l], w_xkv[l].astype(BF16), min(1024, b * n_mem), w_xkv.shape[2])
        kbd, vbd = _block_diag_kv(kv.reshape(b, n_mem, -1))
        x2 = _cross(x2.reshape(b, s, d), cross_norm_g[l].reshape(1, d), w_xq[l].astype(BF16), kbd, vbd,
                    w_xo[l].astype(BF16), tl["tm_tok"]).reshape(t, d)
        w_r = jnp.pad(jnp.concatenate([w_group[l], w_router[l]], axis=1), ((0, 0), (0, LANES - N_GROUPS - N_EXPERTS)))
        b_r = jnp.pad(jnp.concatenate([b_group[l], b_router[l]]), (0, LANES - N_GROUPS - N_EXPERTS)).reshape(1, LANES)
        h, route = _router(x2, ffn_norm_g[l].reshape(1, d), w_r, b_r, tl["tm_tok"])
        row_tok, dest, blk_e, n_used = _dispatch(route, tl["moe_rows"])
        yr = _experts(blk_e, n_used, h[row_tok], w1, w3, w2, l, tl["moe_rows"])
        x2 = _combine(x2, yr[dest[:, 0]], yr[dest[:, 1]], route, final_norm_g.reshape(1, d),
                      l == depth - 1, tl["tm_tok"])
    return x2.reshape(b, s, d)
```

```python
import functools
import math

import jax
import jax.numpy as jnp
from jax import lax
from jax.experimental import pallas as pl
from jax.experimental.pallas import tpu as pltpu

F32 = jnp.float32
BF16 = jnp.bfloat16

LANES = 128
NORM_EPS = 1e-6
LOG2E = math.log2(math.e)
NEG_BIG = -1e30

A_HEADS, A_DH = 4, 64
B_HEADS, B_DH = 8, 64
B_PATTERNS = ((128, 1), (512, 4), (2048, 16))
C_HEADS, C_Q_RANK, C_KV_RANK, C_NOPE, C_ROPE, C_DV = 8, 384, 256, 64, 32, 64
ROPE_THETA = 10000.0
N_BRANCHES = 3
X_HEADS, X_DH = 4, 64
N_GROUPS, EXPERTS_PER_GROUP, TOP_K = 4, 8, 2
N_EXPERTS = N_GROUPS * EXPERTS_PER_GROUP

COL_A = 0
COL_B = 8
COL_CQ = 16
COL_CKV = 20
COL_R1 = 22
COL_R2 = 23
COL_G = 24
PROJ_COLS = 48 * LANES

VMEM_LIMIT = 48 * 1024 * 1024


def _tiles(seq):
    return dict(
        tm_proj=min(1024, seq), tn_proj=1024,
        tm_tok=min(512, seq),
        tq_a=min(512, seq), tk_a=min(512, seq),
        t_b=min(512, seq),
        tq_c=min(512, seq), tk_c=min(512, seq),
        moe_rows=512,
    )


def _cp(sem):
    return pltpu.CompilerParams(dimension_semantics=sem, vmem_limit_bytes=VMEM_LIMIT)


def _rms(x, g, inv_n):
    ms = jnp.sum(x * x, axis=-1, keepdims=True) * inv_n
    return x * lax.rsqrt(ms + NORM_EPS) * g


def _norm_matmul_kernel(x_ref, g_ref, w_ref, o_ref, h_ref, *, inv_n):
    @pl.when(pl.program_id(1) == 0)
    def _():
        h_ref[...] = _rms(x_ref[...].astype(F32), g_ref[...], inv_n).astype(BF16)

    o_ref[...] = jnp.dot(h_ref[...], w_ref[...], preferred_element_type=F32).astype(o_ref.dtype)


def _norm_matmul(x, g, w, tm, tn, out_dtype=BF16):
    m, k = x.shape
    n = w.shape[1]
    return pl.pallas_call(
        functools.partial(_norm_matmul_kernel, inv_n=1.0 / k),
        grid=(m // tm, n // tn),
        in_specs=[pl.BlockSpec((tm, k), lambda i, j: (i, 0)),
                  pl.BlockSpec((1, k), lambda i, j: (0, 0)),
                  pl.BlockSpec((k, tn), lambda i, j: (0, j))],
        out_specs=pl.BlockSpec((tm, tn), lambda i, j: (i, j)),
        out_shape=jax.ShapeDtypeStruct((m, n), out_dtype),
        scratch_shapes=[pltpu.VMEM((tm, k), BF16)],
        compiler_params=_cp(("parallel", "arbitrary")),
        name="norm_matmul",
    )(x, g.reshape(1, k), w)


def _in_proj_kernel(x_ref, g_ref, w_ref, wta_ref, wtb_ref, ona_ref, onb_ref, o_ref, vta_ref, vtb_ref, h_ref, *, inv_n):
    @pl.when(pl.program_id(1) == 0)
    def _():
        h = _rms(x_ref[...].astype(F32), g_ref[...], inv_n).astype(BF16)
        h_ref[...] = h
        vta_ref[...] = (_kq(wta_ref[...], h) + ona_ref[...]).astype(vta_ref.dtype)
        vtb_ref[...] = (_kq(wtb_ref[...], h) + onb_ref[...]).astype(vtb_ref.dtype)

    o_ref[...] = jnp.dot(h_ref[...], w_ref[...], preferred_element_type=F32).astype(o_ref.dtype)


def _in_proj(x, g, w, wta, ona, wtb, onb, tm, tn):
    m, k = x.shape
    n = w.shape[1]
    full = lambda a: pl.BlockSpec(a.shape, lambda i, j: (0, 0))
    return pl.pallas_call(
        functools.partial(_in_proj_kernel, inv_n=1.0 / k),
        grid=(m // tm, n // tn),
        in_specs=[pl.BlockSpec((tm, k), lambda i, j: (i, 0)),
                  pl.BlockSpec((1, k), lambda i, j: (0, 0)),
                  pl.BlockSpec((k, tn), lambda i, j: (0, j)),
                  full(wta), full(wtb), full(ona), full(onb)],
        out_specs=[pl.BlockSpec((tm, tn), lambda i, j: (i, j)),
                   pl.BlockSpec((wta.shape[0], tm), lambda i, j: (0, i)),
                   pl.BlockSpec((wtb.shape[0], tm), lambda i, j: (0, i))],
        out_shape=[jax.ShapeDtypeStruct((m, n), BF16),
                   jax.ShapeDtypeStruct((wta.shape[0], m), BF16),
                   jax.ShapeDtypeStruct((wtb.shape[0], m), BF16)],
        scratch_shapes=[pltpu.VMEM((tm, k), BF16)],
        compiler_params=_cp(("parallel", "arbitrary")),
        name="in_proj",
    )(x, g.reshape(1, k), w, wta, wtb, ona, onb)


ONES_ROWS = 16


def _vt_weights(wv, dv):
    k, w = wv.shape
    wt = jnp.pad(wv.T.reshape(w // dv, dv, k), ((0, 0), (0, ONES_ROWS), (0, 0))).reshape(-1, k)
    ones = jnp.zeros((w // dv, dv + ONES_ROWS, 1), F32).at[:, dv, 0].set(1.0).reshape(-1, 1)
    return wt.astype(BF16), ones


def _split_q(q_ref, qs_ref, scale):
    q = q_ref[...].astype(F32) * scale
    lane = lax.broadcasted_iota(jnp.int32, q.shape, 1)
    qs_ref[0] = jnp.where(lane < LANES // 2, q, 0.0).astype(BF16)
    qs_ref[1] = jnp.where(lane >= LANES // 2, q, 0.0).astype(BF16)


def _kq(k, q):
    return lax.dot_general(k, q, (((1,), (1,)), ((), ())), preferred_element_type=F32)


def _put_scores(s, s_ref, cm_ref, slot, c):
    s_ref[slot, c] = s
    cm_ref[slot, c] = jnp.max(s, axis=0, keepdims=True)


def _update(vt_of, s_ref, cm_ref, m_ref, acc_ref, slot):
    for c in range(2):
        m_prev = m_ref[c]
        m_new = jnp.maximum(m_prev, cm_ref[slot, c])
        alpha = jnp.exp2(m_prev - m_new)
        p = jnp.exp2(s_ref[slot, c] - m_new).astype(BF16)
        acc_ref[c] = alpha * acc_ref[c] + jnp.dot(vt_of(c), p, preferred_element_type=F32)
        m_ref[c] = m_new


def _init_state(m_ref, acc_ref):
    m_ref[...] = jnp.full(m_ref.shape, NEG_BIG, F32)
    acc_ref[...] = jnp.zeros(acc_ref.shape, F32)


def _pipelined(n, scores, update, unroll):
    scores(0, 0)

    def body(jj, carry):
        j = unroll * jj
        for u in range(unroll):
            scores(j + u + 1, (u + 1) % 2)
            update(j + u, u % 2)
        return carry

    looped = (n - 1) // unroll
    lax.fori_loop(0, looped, body, 0)
    for j in range(unroll * looped, n):
        if j + 1 < n:
            scores(j + 1, (j + 1) % 2)
        update(j, j % 2)


def _pipelined_list(n, chunk, scores, update):
    scores(chunk(0), 0)

    def body(unroll, first):
        def run(jj, carry):
            t = first + unroll * jj
            for u in range(unroll):
                scores(chunk(t + u + 1), (u + 1) % 2)
                update(chunk(t + u), u % 2)
            return carry
        return run

    quads = (n - 2) // 4
    lax.fori_loop(0, quads, body(4, 0), 0)
    lax.fori_loop(0, (n - 4 * quads) // 2 - 1, body(2, 4 * quads), 0)
    scores(chunk(n - 1), 1)
    update(chunk(n - 2), 0)
    update(chunk(n - 1), 1)


def _normalized(acc_ref, c):
    dv = acc_ref.shape[1] - ONES_ROWS
    return acc_ref[c, :dv, :] / acc_ref[c, dv:dv + 1, :]


def _attn_scratch(tq, tk, dv):
    return [pltpu.VMEM((2, 1, tq), F32), pltpu.VMEM((2, dv + ONES_ROWS, tq), F32),
            pltpu.VMEM((2, 2, tk, tq), F32), pltpu.VMEM((2, 2, 1, tq), F32)]


SKIP_MARGIN = 150.0


def _block_norms_kernel(qk_ref, sel_ref, o_ref):
    x = qk_ref[...].astype(F32)
    lane = lax.broadcasted_iota(jnp.int32, x.shape, 1)
    x = jnp.where(lane < x.shape[1] // 2, (x * (A_DH ** -0.5 * LOG2E)).astype(BF16).astype(F32), x)
    sums = jnp.dot(x * x, sel_ref[...], preferred_element_type=F32)
    o_ref[...] = jnp.sqrt(jnp.max(sums, axis=0, keepdims=True))


def _block_norms(proj3, t):
    b, s, _ = proj3.shape
    w = 2 * A_HEADS * 2 * A_DH
    sel = (jnp.arange(w)[:, None] // A_DH == jnp.arange(LANES)[None, :]).astype(BF16)
    return pl.pallas_call(
        _block_norms_kernel,
        grid=(b, s // t),
        in_specs=[pl.BlockSpec((None, t, w), lambda bb, i: (bb, i, COL_A * LANES // w)),
                  pl.BlockSpec((w, LANES), lambda bb, i: (0, 0))],
        out_specs=pl.BlockSpec((None, None, 1, LANES), lambda bb, i: (bb, i, 0, 0)),
        out_shape=jax.ShapeDtypeStruct((b, s // t, 1, LANES), F32),
        compiler_params=_cp(("parallel", "parallel")),
        name="block_norms",
    )(proj3, sel)[:, :, 0, :]


def _a_chunk_lists(proj3, pos_f, slopes2, tq, tk):
    assert tq == tk
    b, s, _ = proj3.shape
    norms = _block_norms(proj3, tq)[:, :, :4 * A_HEADS].reshape(b, s // tq, 2, A_HEADS, 2).max(axis=-1)
    qn = norms[:, :, 0].transpose(0, 2, 1)
    kn = norms[:, :, 1].transpose(0, 2, 1)
    qk = 1.01 * qn[:, :, :, None] * kn[:, :, None, :] + 1.0
    pq = pos_f.reshape(b, s // tq, tq)
    pk = pos_f.reshape(b, s // tk, tk)
    qlo, qhi = jnp.min(pq, -1)[:, :, None], jnp.max(pq, -1)[:, :, None]
    klo, khi = jnp.min(pk, -1)[:, None, :], jnp.max(pk, -1)[:, None, :]
    dmin = jnp.maximum(jnp.maximum(klo - qhi, qlo - khi), 0.0)
    dmax = jnp.maximum(khi - qlo, qhi - klo)
    sl = slopes2[None, :, None, None]
    lower = -qk - sl * dmax[:, None]
    upper = qk - sl * dmin[:, None]
    nk = s // tk
    j = jnp.arange(nk, dtype=jnp.int32)
    cover = jnp.max(jnp.min(jnp.abs(pq[:, :, :, None] - pk[:, :, None, :]), axis=-1), axis=-1)
    lower = jnp.where(j[:, None] == j[None, :], jnp.maximum(lower, -qk - sl * cover[:, None, :, None]), lower)
    first = jnp.argmax(lower, axis=-1).astype(jnp.int32)[..., None]
    keep = (upper >= jnp.max(lower, axis=-1, keepdims=True) - SKIP_MARGIN) | (j == first)
    lst = jnp.argsort(jnp.where(j == first, -1, jnp.where(keep, j, nk + j)), axis=-1).astype(jnp.int32)
    cnt = jnp.sum(keep.astype(jnp.int32), axis=-1)
    return lst.reshape(-1), (cnt + cnt % 2).reshape(-1)


def _attn_a_kernel(lst_ref, cnt_ref, lam_ref, slope_ref, q_ref, k_ref, vt_ref, pq_ref, pk_ref, g_ref, o_ref,
                   qs_ref, m_ref, acc_ref, s_ref, cm_ref, *, tk, post_scale):
    tq = q_ref.shape[0]
    nk = k_ref.shape[0] // tk
    blk = (pl.program_id(0) * pl.num_programs(1) + pl.program_id(1)) * pl.num_programs(2) + pl.program_id(2)
    sl2 = slope_ref[pl.program_id(1)]
    _split_q(q_ref, qs_ref, A_DH ** -0.5 * LOG2E)
    _init_state(m_ref, acc_ref)
    pq = pq_ref[...] * sl2

    def scores(j, slot):
        ks = pl.multiple_of(j * tk, tk)
        k = k_ref[pl.ds(ks, tk), :]
        pk = pk_ref[pl.ds(ks, tk), :] * sl2
        bias = jnp.abs(jnp.concatenate([pk] * (tq // LANES), axis=1) - pq)
        for c in range(2):
            _put_scores(_kq(k, qs_ref[c]) - bias, s_ref, cm_ref, slot, c)

    def update(j, slot):
        ks = pl.multiple_of(j * tk, tk)
        _update(lambda c: vt_ref[:, pl.ds(ks, tk)], s_ref, cm_ref, m_ref, acc_ref, slot)

    _pipelined_list(cnt_ref[blk], lambda t: lst_ref[blk * nk + t], scores, update)
    o = (_normalized(acc_ref, 0) - lam_ref[0] * _normalized(acc_ref, 1)).T
    o_ref[...] = (_rms(o, g_ref[...], 1.0 / LANES) * post_scale).astype(o_ref.dtype)


def _attn_a(proj, vt, posq_row, posk_rep, lam, slopes2, g, post_scale, tq, tk):
    b, s, _ = proj.shape
    lst, cnt = _a_chunk_lists(proj, posq_row.reshape(b, s), slopes2, tq, tk)
    smem = pl.BlockSpec(memory_space=pltpu.SMEM)
    grid_spec = pltpu.PrefetchScalarGridSpec(
        num_scalar_prefetch=2,
        grid=(b, A_HEADS, s // tq),
        in_specs=[smem, smem,
                  pl.BlockSpec((None, tq, LANES), lambda bb, h, i, *_: (bb, i, COL_A + h)),
                  pl.BlockSpec((None, s, LANES), lambda bb, h, i, *_: (bb, 0, COL_A + A_HEADS + h)),
                  pl.BlockSpec((vt.shape[0] // A_HEADS, s), lambda bb, h, i, *_: (h, bb)),
                  pl.BlockSpec((None, 1, tq), lambda bb, h, i, *_: (bb, 0, i)),
                  pl.BlockSpec((None, s, LANES), lambda bb, h, i, *_: (bb, 0, 0)),
                  pl.BlockSpec((1, LANES), lambda bb, h, i, *_: (0, 0))],
        out_specs=pl.BlockSpec((None, tq, LANES), lambda bb, h, i, *_: (bb, i, h)),
        scratch_shapes=[pltpu.VMEM((2, tq, LANES), BF16)] + _attn_scratch(tq, tk, LANES),
    )
    return pl.pallas_call(
        functools.partial(_attn_a_kernel, tk=tk, post_scale=post_scale),
        grid_spec=grid_spec,
        out_shape=jax.ShapeDtypeStruct((b, s, A_HEADS * LANES), BF16),
        compiler_params=_cp(("parallel", "parallel", "arbitrary")),
        name="attn_diff",
    )(lst, cnt, lam, slopes2, proj, proj, vt, posq_row, posk_rep, g.reshape(1, LANES))


def _attn_b_kernel(q_ref, k_ref, vt_ref, bias_ref, o_ref, qs_ref, m_ref, acc_ref, s_ref, cm_ref, *, nband):
    t = q_ref.shape[0]
    nkb = k_ref.shape[0] // t
    rows = vt_ref.shape[0] // 2
    i = pl.program_id(2)
    half = nband // 2
    _split_q(q_ref, qs_ref, B_DH ** -0.5 * LOG2E)
    _init_state(m_ref, acc_ref)

    def start(jj):
        return pl.multiple_of(jnp.clip(i + jj - half, 0, nkb - 1) * t, t)

    def scores(jj, slot):
        kb = i + jj - half
        table = jnp.where((kb >= 0) & (kb < nkb), jj, nband)
        k = k_ref[pl.ds(start(jj), t), :]
        for c in range(2):
            _put_scores(_kq(k, qs_ref[c]) + bias_ref[c, table], s_ref, cm_ref, slot, c)

    def update(jj, slot):
        ks = start(jj)
        _update(lambda c: vt_ref[c * rows:(c + 1) * rows, pl.ds(ks, t)], s_ref, cm_ref, m_ref, acc_ref, slot)

    _pipelined(nband, scores, update, unroll=2)
    o = jnp.concatenate([_normalized(acc_ref, 0), _normalized(acc_ref, 1)], axis=0)
    o_ref[...] = o.T.astype(o_ref.dtype)


def _b_bias_tables(t, nband):
    half = nband // 2
    r = jnp.arange(t, dtype=jnp.int32)[None, :, None]
    c = jnp.arange(t, dtype=jnp.int32)[None, None, :]
    jj = jnp.arange(nband + 1, dtype=jnp.int32)[:, None, None]
    ao = jnp.abs((jj - half) * t + r - c)
    mult = jnp.zeros(ao.shape, jnp.int32)
    for window, dilation in B_PATTERNS:
        reach = (window // (2 * dilation)) * dilation
        mult = mult + ((ao % dilation == 0) & (ao <= reach)).astype(jnp.int32)
    mult = jnp.where(jj < nband, mult, 0)
    slopes = jnp.exp2(-8.0 * jnp.arange(1, B_HEADS + 1, dtype=F32) / B_HEADS) * LOG2E
    bias = jnp.log2(jnp.maximum(mult, 1).astype(F32))[None] - slopes[:, None, None, None] * ao.astype(F32)[None]
    bias = jnp.where((mult > 0)[None], bias, NEG_BIG)
    return bias.reshape(B_HEADS // 2, 2, nband + 1, t, t)


def _attn_b(proj, vt, bias, t):
    b, s, _ = proj.shape
    nband = bias.shape[2] - 1
    npair = B_HEADS // 2
    return pl.pallas_call(
        functools.partial(_attn_b_kernel, nband=nband),
        grid=(b, npair, s // t),
        in_specs=[pl.BlockSpec((None, t, LANES), lambda bb, p, i: (bb, i, COL_B + p)),
                  pl.BlockSpec((None, s, LANES), lambda bb, p, i: (bb, 0, COL_B + npair + p)),
                  pl.BlockSpec((vt.shape[0] // npair, s), lambda bb, p, i: (p, bb)),
                  pl.BlockSpec((None,) + bias.shape[1:], lambda bb, p, i: (p, 0, 0, 0, 0))],
        out_specs=pl.BlockSpec((None, t, LANES), lambda bb, p, i: (bb, i, p)),
        out_shape=jax.ShapeDtypeStruct((b, s, npair * LANES), BF16),
        scratch_shapes=[pltpu.VMEM((2, t, LANES), BF16)] + _attn_scratch(t, t, B_DH),
        compiler_params=_cp(("parallel", "parallel", "arbitrary")),
        name="attn_dilated",
    )(proj, proj, vt, bias)


def _rope_table_kernel(pos_ref, invf_ref, c_ref, s_ref):
    ang = pos_ref[...] * invf_ref[...]
    c_ref[...] = jnp.cos(ang)
    s_ref[...] = jnp.sin(ang)


def _rope_tables(pos_col, tm):
    t = pos_col.shape[0]
    inv = ROPE_THETA ** (-jnp.arange(0, C_ROPE, 2, dtype=F32) / C_ROPE)
    invf = jnp.concatenate([jnp.zeros((C_NOPE,), F32), inv, inv,
                            jnp.zeros((LANES - C_NOPE - C_ROPE,), F32)]).reshape(1, LANES)
    spec = pl.BlockSpec((tm, LANES), lambda i: (i, 0))
    return pl.pallas_call(
        _rope_table_kernel,
        grid=(t // tm,),
        in_specs=[pl.BlockSpec((tm, 1), lambda i: (i, 0)), pl.BlockSpec((1, LANES), lambda i: (0, 0))],
        out_specs=[spec, spec],
        out_shape=[jax.ShapeDtypeStruct((t, LANES), F32)] * 2,
        compiler_params=_cp(("parallel",)),
        name="rope_tables",
    )(pos_col, invf)


def _mla_prep_kernel(cq_ref, ckv_ref, r1_ref, r2_ref, c_ref, s_ref, gq_ref, gkv_ref,
                     wq1_ref, wq2_ref, wk_ref, wvt_ref, onv_ref, q_out, k_out, vt_out):
    qn = _rms(cq_ref[...].astype(F32), gq_ref[...], 1.0 / C_Q_RANK).astype(BF16)
    kvn = _rms(ckv_ref[...].astype(F32), gkv_ref[...], 1.0 / C_KV_RANK).astype(BF16)
    q1 = jnp.dot(qn, wq1_ref[...], preferred_element_type=F32)
    q2 = jnp.dot(qn, wq2_ref[...], preferred_element_type=F32)
    k1 = jnp.dot(kvn, wk_ref[...], preferred_element_type=F32)
    vt_out[...] = (_kq(wvt_ref[...], kvn) + onv_ref[...]).astype(vt_out.dtype)
    cos = c_ref[...]
    sin = s_ref[...]
    k_rope = r1_ref[...].astype(F32) * cos + r2_ref[...].astype(F32) * sin
    scale = (C_NOPE + C_ROPE) ** -0.5 * LOG2E
    for h in range(C_HEADS):
        sl = slice(h * LANES, (h + 1) * LANES)
        q_out[:, sl] = ((q1[:, sl] * cos + q2[:, sl] * sin) * scale).astype(q_out.dtype)
        k_out[:, sl] = (k1[:, sl] + k_rope).astype(k_out.dtype)


def _mla_prep(proj2, cos, sin, gq, gkv, wq1, wq2, wk, wvt, onv, tm):
    t = proj2.shape[0]
    full = lambda a: pl.BlockSpec(a.shape, lambda i: (0, 0))
    row = lambda w: pl.BlockSpec((tm, w), lambda i: (i, 0))
    return pl.pallas_call(
        _mla_prep_kernel,
        grid=(t // tm,),
        in_specs=[pl.BlockSpec((tm, 4 * LANES), lambda i: (i, COL_CQ // 4)),
                  pl.BlockSpec((tm, 2 * LANES), lambda i: (i, COL_CKV // 2)),
                  pl.BlockSpec((tm, LANES), lambda i: (i, COL_R1)),
                  pl.BlockSpec((tm, LANES), lambda i: (i, COL_R2)),
                  row(LANES), row(LANES), full(gq), full(gkv), full(wq1), full(wq2), full(wk), full(wvt), full(onv)],
        out_specs=[row(C_HEADS * LANES), row(C_HEADS * LANES), pl.BlockSpec((wvt.shape[0], tm), lambda i: (0, i))],
        out_shape=[jax.ShapeDtypeStruct((t, C_HEADS * LANES), BF16),
                   jax.ShapeDtypeStruct((t, C_HEADS * LANES), BF16),
                   jax.ShapeDtypeStruct((wvt.shape[0], t), BF16)],
        compiler_params=_cp(("parallel",)),
        name="mla_prep",
    )(proj2, proj2, proj2, proj2, cos, sin, gq, gkv, wq1, wq2, wk, wvt, onv)


def _attn_c_kernel(q_ref, k_ref, vt_ref, o_ref, m_ref, acc_ref, s_ref, cm_ref, *, tk):
    rows = vt_ref.shape[0] // 2
    _init_state(m_ref, acc_ref)

    def scores(j, slot):
        ks = pl.multiple_of(j * tk, tk)
        for c in range(2):
            sl = slice(c * LANES, (c + 1) * LANES)
            _put_scores(_kq(k_ref[pl.ds(ks, tk), sl], q_ref[:, sl]), s_ref, cm_ref, slot, c)

    def update(j, slot):
        ks = pl.multiple_of(j * tk, tk)
        _update(lambda c: vt_ref[c * rows:(c + 1) * rows, pl.ds(ks, tk)], s_ref, cm_ref, m_ref, acc_ref, slot)

    _pipelined(k_ref.shape[0] // tk, scores, update, unroll=4)
    o = jnp.concatenate([_normalized(acc_ref, 0), _normalized(acc_ref, 1)], axis=0)
    o_ref[...] = o.T.astype(o_ref.dtype)


def _attn_c(q, k, vt, tq, tk):
    b, s, _ = q.shape
    npair = C_HEADS // 2
    return pl.pallas_call(
        functools.partial(_attn_c_kernel, tk=tk),
        grid=(b, npair, s // tq),
        in_specs=[pl.BlockSpec((None, tq, 2 * LANES), lambda bb, p, i: (bb, i, p)),
                  pl.BlockSpec((None, s, 2 * LANES), lambda bb, p, i: (bb, 0, p)),
                  pl.BlockSpec((vt.shape[0] // npair, s), lambda bb, p, i: (p, bb))],
        out_specs=pl.BlockSpec((None, tq, LANES), lambda bb, p, i: (bb, i, p)),
        out_shape=jax.ShapeDtypeStruct((b, s, npair * LANES), BF16),
        scratch_shapes=_attn_scratch(tq, tk, C_DV),
        compiler_params=_cp(("parallel", "parallel", "arbitrary")),
        name="attn_latent",
    )(q, k, vt)


def _merge_kernel(oa_ref, ob_ref, oc_ref, g0_ref, g1_ref, g2_ref, x_ref, wb_ref, wo_ref, o_ref):
    z = None
    for n, (o_r, g_r) in enumerate(((oa_ref, g0_ref), (ob_ref, g1_ref), (oc_ref, g2_ref))):
        br = jnp.dot(o_r[...], wb_ref[n], preferred_element_type=F32)
        gate = 1.0 / (1.0 + jnp.exp(-g_r[...].astype(F32)))
        z = gate * br if z is None else z + gate * br
    o_ref[...] = x_ref[...] + jnp.dot(z.astype(BF16), wo_ref[...], preferred_element_type=F32)


def _merge(oa, ob, oc, proj2, x2, wb, wo, tm):
    t, d = x2.shape
    bw = oa.shape[1]
    row = lambda w: pl.BlockSpec((tm, w), lambda i: (i, 0))
    gate = lambda n: pl.BlockSpec((tm, d), lambda i: (i, COL_G * LANES // d + n))
    return pl.pallas_call(
        _merge_kernel,
        grid=(t // tm,),
        in_specs=[row(bw), row(bw), row(bw), gate(0), gate(1), gate(2), row(d),
                  pl.BlockSpec(wb.shape, lambda i: (0, 0, 0)), pl.BlockSpec(wo.shape, lambda i: (0, 0))],
        out_specs=row(d),
        out_shape=jax.ShapeDtypeStruct((t, d), F32),
        compiler_params=_cp(("parallel",)),
        name="branch_merge",
    )(oa, ob, oc, proj2, proj2, proj2, x2, wb, wo)


def _cross_kernel(x_ref, g_ref, wq_ref, kbd_ref, vbd_ref, wo_ref, o_ref, *, n_mem):
    x = x_ref[...]
    h = _rms(x, g_ref[...], 1.0 / x.shape[-1]).astype(BF16)
    q = (jnp.dot(h, wq_ref[...], preferred_element_type=F32) * (X_DH ** -0.5 * LOG2E)).astype(BF16)
    s = jnp.dot(q, kbd_ref[...], preferred_element_type=F32)
    ps = []
    for hh in range(X_HEADS):
        sh = s[:, hh * n_mem:(hh + 1) * n_mem]
        p = jnp.exp2(sh - jnp.max(sh, axis=-1, keepdims=True))
        ps.append((p / jnp.sum(p, axis=-1, keepdims=True)).astype(BF16))
    o = jnp.dot(jnp.concatenate(ps, axis=1), vbd_ref[...], preferred_element_type=F32)
    o_ref[...] = x + jnp.dot(o.astype(BF16), wo_ref[...], preferred_element_type=F32)


def _cross(x3, g, wq, kbd, vbd, wo, tm):
    b, s, d = x3.shape
    n_mem = kbd.shape[2] // X_HEADS
    full = lambda a: pl.BlockSpec(a.shape, lambda bb, i: (0, 0))
    return pl.pallas_call(
        functools.partial(_cross_kernel, n_mem=n_mem),
        grid=(b, s // tm),
        in_specs=[pl.BlockSpec((None, tm, d), lambda bb, i: (bb, i, 0)), full(g), full(wq),
                  pl.BlockSpec((None,) + kbd.shape[1:], lambda bb, i: (bb, 0, 0)),
                  pl.BlockSpec((None,) + vbd.shape[1:], lambda bb, i: (bb, 0, 0)), full(wo)],
        out_specs=pl.BlockSpec((None, tm, d), lambda bb, i: (bb, i, 0)),
        out_shape=jax.ShapeDtypeStruct((b, s, d), F32),
        compiler_params=_cp(("parallel", "parallel")),
        name="cross_attn",
    )(x3, g, wq, kbd, vbd, wo)


def _block_diag_kv(kv):
    b, m, _ = kv.shape
    kv = kv.reshape(b, m, 2, X_HEADS, X_DH)
    eye = jnp.eye(X_HEADS, dtype=kv.dtype)
    kt = kv[:, :, 0].transpose(0, 2, 3, 1)
    kbd = (kt[:, :, :, None, :] * eye[None, :, None, :, None]).reshape(b, X_HEADS * X_DH, X_HEADS * m)
    vt = kv[:, :, 1].transpose(0, 2, 1, 3)
    vbd = (vt[:, :, :, None, :] * eye[None, :, None, :, None]).reshape(b, X_HEADS * m, X_HEADS * X_DH)
    return kbd, vbd


def _router_kernel(x_ref, g_ref, w_ref, b_ref, h_out, r_out):
    x = x_ref[...]
    h = _rms(x, g_ref[...], 1.0 / x.shape[-1])
    h_out[...] = h.astype(h_out.dtype)
    logits = jnp.dot(h, w_ref[...], preferred_element_type=F32, precision=lax.Precision.HIGHEST) + b_ref[...]
    lane = lax.broadcasted_iota(jnp.int32, logits.shape, 1)
    lane_f = lane.astype(F32)
    big = jnp.float32(4 * LANES)

    def top(vals, mask):
        mv = jnp.max(jnp.where(mask, vals, -jnp.inf), axis=-1, keepdims=True)
        idx = jnp.min(jnp.where(mask & (vals == mv), lane_f, big), axis=-1, keepdims=True)
        return mv, idx

    g_mask = lane < N_GROUPS
    g_max, g_idx = top(logits, g_mask)
    p_g = 1.0 / jnp.sum(jnp.where(g_mask, jnp.exp(logits - g_max), 0.0), axis=-1, keepdims=True)
    first = N_GROUPS + g_idx * EXPERTS_PER_GROUP
    e_mask = (lane_f >= first) & (lane_f < first + EXPERTS_PER_GROUP)
    v0, i0 = top(logits, e_mask)
    v1, i1 = top(logits, e_mask & (lane_f != i0))
    e1 = jnp.exp(v1 - v0)
    w0 = p_g / (1.0 + e1)
    w1 = p_g * e1 / (1.0 + e1)
    out = jnp.where(lane == 0, i0 - N_GROUPS, 0.0)
    out = jnp.where(lane == 1, i1 - N_GROUPS, out)
    out = jnp.where(lane == 2, w0, out)
    out = jnp.where(lane == 3, w1, out)
    r_out[...] = out


def _router(x2, g, w, bias, tm):
    t, d = x2.shape
    full = lambda a: pl.BlockSpec(a.shape, lambda i: (0, 0))
    return pl.pallas_call(
        _router_kernel,
        grid=(t // tm,),
        in_specs=[pl.BlockSpec((tm, d), lambda i: (i, 0)), full(g), full(w), full(bias)],
        out_specs=[pl.BlockSpec((tm, d), lambda i: (i, 0)), pl.BlockSpec((tm, LANES), lambda i: (i, 0))],
        out_shape=[jax.ShapeDtypeStruct((t, d), BF16), jax.ShapeDtypeStruct((t, LANES), F32)],
        compiler_params=_cp(("parallel",)),
        name="moe_router",
    )(x2, g, w, bias)


def _expert_kernel(blk_e_ref, n_used_ref, x_ref, w1_ref, w3_ref, w2_ref, o_ref, w13_s, w2_s):
    i = pl.program_id(0)
    used = i < n_used_ref[0]
    de = w1_ref.shape[1]

    @pl.when(used & ((i == 0) | (blk_e_ref[i] != blk_e_ref[jnp.maximum(i - 1, 0)])))
    def _():
        w13_s[:, :de] = w1_ref[...].astype(BF16)
        w13_s[:, de:] = w3_ref[...].astype(BF16)
        w2_s[...] = w2_ref[...].astype(BF16)

    @pl.when(used)
    def _():
        hid = jnp.dot(x_ref[...], w13_s[...], preferred_element_type=F32)
        a = hid[:, :de]
        act = (a / (1.0 + jnp.exp(-a))) * hid[:, de:]
        o_ref[...] = jnp.dot(act.astype(BF16), w2_s[...], preferred_element_type=F32).astype(o_ref.dtype)

    @pl.when(jnp.logical_not(used))
    def _():
        o_ref[...] = jnp.zeros(o_ref.shape, o_ref.dtype)


def _experts(blk_e, n_used, xr, w1, w3, w2, layer, rows_per_block):
    rows, d = xr.shape
    de = w1.shape[3]
    weight = lambda w: pl.BlockSpec((None, None) + w.shape[2:], lambda i, be, nu: (layer, be[i], 0, 0))
    grid_spec = pltpu.PrefetchScalarGridSpec(
        num_scalar_prefetch=2,
        grid=(rows // rows_per_block,),
        in_specs=[pl.BlockSpec((rows_per_block, d), lambda i, be, nu: (i, 0)), weight(w1), weight(w3), weight(w2)],
        out_specs=pl.BlockSpec((rows_per_block, d), lambda i, be, nu: (i, 0)),
        scratch_shapes=[pltpu.VMEM((d, 2 * de), BF16), pltpu.VMEM((de, d), BF16)],
    )
    return pl.pallas_call(
        _expert_kernel,
        grid_spec=grid_spec,
        out_shape=jax.ShapeDtypeStruct((rows, d), BF16),
        compiler_params=_cp(("arbitrary",)),
        name="moe_experts",
    )(blk_e, n_used, xr, w1, w3, w2)


def _combine_kernel(x_ref, y_ref, r_ref, g_ref, o_ref, *, final_norm):
    r = r_ref[...]
    d = x_ref.shape[1]
    y = x_ref[...] + r[:, 2:3] * y_ref[:, :d].astype(F32) + r[:, 3:4] * y_ref[:, d:].astype(F32)
    if final_norm:
        y = _rms(y, g_ref[...], 1.0 / d)
    o_ref[...] = y


def _combine(x2, y01, route, g, final_norm, tm):
    t, d = x2.shape
    row = lambda w: pl.BlockSpec((tm, w), lambda i: (i, 0))
    return pl.pallas_call(
        functools.partial(_combine_kernel, final_norm=final_norm),
        grid=(t // tm,),
        in_specs=[row(d), row(TOP_K * d), row(LANES), pl.BlockSpec((1, d), lambda i: (0, 0))],
        out_specs=row(d),
        out_shape=jax.ShapeDtypeStruct((t, d), F32),
        compiler_params=_cp(("parallel",)),
        name="moe_combine",
    )(x2, y01, route, g)


def _dispatch(route, rows_per_block):
    t = route.shape[0]
    eid = route[:, :TOP_K].astype(jnp.int32).reshape(-1)
    n = eid.shape[0]
    order = jnp.argsort(eid).astype(jnp.int32)
    rank = jnp.argsort(order).astype(jnp.int32)
    experts = jnp.arange(N_EXPERTS, dtype=jnp.int32)
    counts = jnp.sum((eid[:, None] == experts[None, :]).astype(jnp.int32), axis=0)
    start = jnp.cumsum(counts) - counts
    padded = (counts + rows_per_block - 1) // rows_per_block * rows_per_block
    pend = jnp.cumsum(padded)
    pstart = pend - padded
    dest = (rank + (pstart - start)[eid]).reshape(t, TOP_K)
    n_blocks = n // rows_per_block + N_EXPERTS
    blk_first = jnp.arange(n_blocks, dtype=jnp.int32) * rows_per_block
    blk_e = jnp.minimum(jnp.sum((pend[None, :] <= blk_first[:, None]).astype(jnp.int32), axis=1), N_EXPERTS - 1)
    off = (blk_first - pstart[blk_e])[:, None] + jnp.arange(rows_per_block, dtype=jnp.int32)[None, :]
    src = jnp.clip(start[blk_e][:, None] + off, 0, n - 1)
    row_tok = jnp.where(off < counts[blk_e][:, None], order[src] // TOP_K, 0).reshape(-1)
    n_used = (pend[-1] // rows_per_block).astype(jnp.int32).reshape(1)
    return row_tok, dest, blk_e, n_used


def _rot_cols(w):
    half = w.shape[-1] // 2
    return jnp.concatenate([-w[..., half:], w[..., :half]], axis=-1)


def _pack_w_in(w):
    d = w.shape[0]
    blk = A_HEADS * 2 * A_DH
    n_ab = 6 * blk
    cq = w[:, n_ab:n_ab + C_Q_RANK]
    ckv = w[:, n_ab + C_Q_RANK:n_ab + C_Q_RANK + C_KV_RANK]
    ckr = w[:, n_ab + C_Q_RANK + C_KV_RANK:n_ab + C_Q_RANK + C_KV_RANK + C_ROPE]
    gates = w[:, n_ab + C_Q_RANK + C_KV_RANK + C_ROPE:]
    z = lambda n: jnp.zeros((d, n), w.dtype)
    tail = LANES - C_NOPE - C_ROPE
    packed = jnp.concatenate([w[:, :2 * blk], w[:, 3 * blk:5 * blk], cq, z(LANES), ckv,
                              z(C_NOPE), ckr, z(tail), z(C_NOPE), _rot_cols(ckr), z(tail), gates], axis=1)
    return packed.astype(BF16), w[:, 2 * blk:3 * blk], w[:, 5 * blk:6 * blk]


def _pack_w_uq(w):
    wq = w.reshape(C_Q_RANK, C_HEADS, C_NOPE + C_ROPE)
    pad_rows = 4 * LANES - C_Q_RANK
    tail = LANES - C_NOPE - C_ROPE
    q1 = jnp.pad(wq, ((0, pad_rows), (0, 0), (0, tail))).reshape(4 * LANES, C_HEADS * LANES)
    q2 = jnp.pad(_rot_cols(wq[:, :, C_NOPE:]), ((0, pad_rows), (0, 0), (C_NOPE, tail))).reshape(4 * LANES, C_HEADS * LANES)
    return q1.astype(BF16), q2.astype(BF16)


def _pack_w_ukv(w):
    wkv = w.reshape(C_KV_RANK, C_HEADS, C_NOPE + C_DV)
    wk = jnp.pad(wkv[:, :, :C_NOPE], ((0, 0), (0, 0), (0, LANES - C_NOPE))).reshape(C_KV_RANK, C_HEADS * LANES)
    wv = wkv[:, :, C_NOPE:].reshape(C_KV_RANK, C_HEADS * C_DV)
    return wk.astype(BF16), wv


def kernel(x, mem, positions, mix_norm_g, w_in, diff_lambda, diff_subln_g, mla_q_norm_g, w_uq, mla_kv_norm_g, w_ukv, w_branch, w_out, cross_norm_g, mem_norm_g, w_xq, w_xkv, w_xo, ffn_norm_g, w_group, b_group, w_router, b_router, w1, w3, w2, final_norm_g):
    b, s, d = x.shape
    depth = w_in.shape[0]
    t = b * s
    n_mem = mem.shape[1]
    tl = _tiles(s)
    assert PROJ_COLS == COL_G * LANES + N_BRANCHES * d and s % (2 * tl["t_b"]) == 0

    pos_f = positions.astype(F32)
    posq_row = pos_f.reshape(b, 1, s)
    posk_rep = jnp.broadcast_to(pos_f[:, :, None], (b, s, LANES))
    cos, sin = _rope_tables(pos_f.reshape(t, 1), tl["tm_tok"])
    slopes_a2 = jnp.exp2(-8.0 * jnp.arange(1, A_HEADS + 1, dtype=F32) / A_HEADS) * LOG2E
    reach = max((w // (2 * dl)) * dl for w, dl in B_PATTERNS)
    b_bias = _b_bias_tables(tl["t_b"], 2 * (-(-reach // tl["t_b"])) + 1)
    mem2 = mem.reshape(b * n_mem, d)

    x2 = x.reshape(t, d)
    for l in range(depth):
        w_main, w_av, w_bv = _pack_w_in(w_in[l])
        proj2, vt_a, vt_b = _in_proj(x2, mix_norm_g[l], w_main, *_vt_weights(w_av, 2 * A_DH), *_vt_weights(w_bv, B_DH),
                                     tl["tm_proj"], tl["tn_proj"])
        proj3 = proj2.reshape(b, s, PROJ_COLS)
        lq = diff_lambda[l].astype(F32)
        lam_init = 0.8 - 0.6 * math.exp(-0.3 * l)
        lam = (jnp.exp(jnp.sum(lq[0] * lq[1])) - jnp.exp(jnp.sum(lq[2] * lq[3])) + lam_init).reshape(1)
        oa = _attn_a(proj3, vt_a, posq_row, posk_rep, lam, slopes_a2, diff_subln_g[l], 1.0 - lam_init,
                     tl["tq_a"], tl["tk_a"])
        ob = _attn_b(proj3, vt_b, b_bias, tl["t_b"])
        wq1, wq2 = _pack_w_uq(w_uq[l])
        wk, wv = _pack_w_ukv(w_ukv[l])
        gq = jnp.pad(mla_q_norm_g[l], (0, 4 * LANES - C_Q_RANK)).reshape(1, 4 * LANES)
        qc, kc, vt_c = _mla_prep(proj2, cos, sin, gq, mla_kv_norm_g[l].reshape(1, C_KV_RANK),
                                 wq1, wq2, wk, *_vt_weights(wv, C_DV), tl["tm_tok"])
        oc = _attn_c(qc.reshape(b, s, -1), kc.reshape(b, s, -1), vt_c, tl["tq_c"], tl["tk_c"])
        x2 = _merge(oa.reshape(t, -1), ob.reshape(t, -1), oc.reshape(t, -1), proj2, x2,
                    w_branch[l].astype(BF16), w_out[l].astype(BF16), tl["tm_tok"])
        kv = _norm_matmul(mem2, mem_norm_g[l], w_xkv[l].astype(BF16), min(1024, b * n_mem), w_xkv.shape[2])
        kbd, vbd = _block_diag_kv(kv.reshape(b, n_mem, -1))
        x2 = _cross(x2.reshape(b, s, d), cross_norm_g[l].reshape(1, d), w_xq[l].astype(BF16), kbd, vbd,
                    w_xo[l].astype(BF16), tl["tm_tok"]).reshape(t, d)
        w_r = jnp.pad(jnp.concatenate([w_group[l], w_router[l]], axis=1), ((0, 0), (0, LANES - N_GROUPS - N_EXPERTS)))
        b_r = jnp.pad(jnp.concatenate([b_group[l], b_router[l]]), (0, LANES - N_GROUPS - N_EXPERTS)).reshape(1, LANES)
        h, route = _router(x2, ffn_norm_g[l].reshape(1, d), w_r, b_r, tl["tm_tok"])
        row_tok, dest, blk_e, n_used = _dispatch(route, tl["moe_rows"])
        yr = _experts(blk_e, n_used, h[row_tok], w1, w3, w2, l, tl["moe_rows"])
        x2 = _combine(x2, yr[dest.reshape(-1)].reshape(t, TOP_K * d), route, final_norm_g.reshape(1, d),
                      l == depth - 1, tl["tm_tok"])
    return x2.reshape(b, s, d)
```

```python
import functools
import math

import jax
import jax.numpy as jnp
from jax import lax
from jax.experimental import pallas as pl
from jax.experimental.pallas import tpu as pltpu

F32 = jnp.float32
BF16 = jnp.bfloat16

LANES = 128
NORM_EPS = 1e-6
LOG2E = math.log2(math.e)
NEG_BIG = -1e30

A_HEADS, A_DH = 4, 64
B_HEADS, B_DH = 8, 64
B_PATTERNS = ((128, 1), (512, 4), (2048, 16))
C_HEADS, C_Q_RANK, C_KV_RANK, C_NOPE, C_ROPE, C_DV = 8, 384, 256, 64, 32, 64
ROPE_THETA = 10000.0
N_BRANCHES = 3
X_HEADS, X_DH = 4, 64
N_GROUPS, EXPERTS_PER_GROUP, TOP_K = 4, 8, 2
N_EXPERTS = N_GROUPS * EXPERTS_PER_GROUP

COL_A = 0
COL_B = 8
COL_CQ = 16
COL_CKV = 20
COL_R1 = 22
COL_R2 = 23
COL_G = 24
PROJ_COLS = 48 * LANES

VMEM_LIMIT = 48 * 1024 * 1024


def _tiles(seq):
    return dict(
        tm_proj=min(1024, seq), tn_proj=1024,
        tm_tok=min(512, seq),
        tq_a=min(512, seq), tk_a=min(512, seq),
        t_b=min(512, seq),
        tq_c=min(512, seq), tk_c=min(512, seq),
        moe_rows=512,
    )


def _cp(sem):
    return pltpu.CompilerParams(dimension_semantics=sem, vmem_limit_bytes=VMEM_LIMIT)


def _rms(x, g, inv_n):
    ms = jnp.sum(x * x, axis=-1, keepdims=True) * inv_n
    return x * lax.rsqrt(ms + NORM_EPS) * g


def _norm_matmul_kernel(x_ref, g_ref, w_ref, o_ref, h_ref, *, inv_n):
    @pl.when(pl.program_id(1) == 0)
    def _():
        h_ref[...] = _rms(x_ref[...].astype(F32), g_ref[...], inv_n).astype(BF16)

    o_ref[...] = jnp.dot(h_ref[...], w_ref[...], preferred_element_type=F32).astype(o_ref.dtype)


def _norm_matmul(x, g, w, tm, tn, out_dtype=BF16):
    m, k = x.shape
    n = w.shape[1]
    return pl.pallas_call(
        functools.partial(_norm_matmul_kernel, inv_n=1.0 / k),
        grid=(m // tm, n // tn),
        in_specs=[pl.BlockSpec((tm, k), lambda i, j: (i, 0)),
                  pl.BlockSpec((1, k), lambda i, j: (0, 0)),
                  pl.BlockSpec((k, tn), lambda i, j: (0, j))],
        out_specs=pl.BlockSpec((tm, tn), lambda i, j: (i, j)),
        out_shape=jax.ShapeDtypeStruct((m, n), out_dtype),
        scratch_shapes=[pltpu.VMEM((tm, k), BF16)],
        compiler_params=_cp(("parallel", "arbitrary")),
        name="norm_matmul",
    )(x, g.reshape(1, k), w)


def _in_proj_kernel(x_ref, g_ref, w_ref, wta_ref, wtb_ref, ona_ref, onb_ref, o_ref, vta_ref, vtb_ref, h_ref, *, inv_n):
    @pl.when(pl.program_id(1) == 0)
    def _():
        h = _rms(x_ref[...].astype(F32), g_ref[...], inv_n).astype(BF16)
        h_ref[...] = h
        vta_ref[...] = (_kq(wta_ref[...], h) + ona_ref[...]).astype(vta_ref.dtype)
        vtb_ref[...] = (_kq(wtb_ref[...], h) + onb_ref[...]).astype(vtb_ref.dtype)

    o_ref[...] = jnp.dot(h_ref[...], w_ref[...], preferred_element_type=F32).astype(o_ref.dtype)


def _in_proj(x, g, w, wta, ona, wtb, onb, tm, tn):
    m, k = x.shape
    n = w.shape[1]
    full = lambda a: pl.BlockSpec(a.shape, lambda i, j: (0, 0))
    return pl.pallas_call(
        functools.partial(_in_proj_kernel, inv_n=1.0 / k),
        grid=(m // tm, n // tn),
        in_specs=[pl.BlockSpec((tm, k), lambda i, j: (i, 0)),
                  pl.BlockSpec((1, k), lambda i, j: (0, 0)),
                  pl.BlockSpec((k, tn), lambda i, j: (0, j)),
                  full(wta), full(wtb), full(ona), full(onb)],
        out_specs=[pl.BlockSpec((tm, tn), lambda i, j: (i, j)),
                   pl.BlockSpec((wta.shape[0], tm), lambda i, j: (0, i)),
                   pl.BlockSpec((wtb.shape[0], tm), lambda i, j: (0, i))],
        out_shape=[jax.ShapeDtypeStruct((m, n), BF16),
                   jax.ShapeDtypeStruct((wta.shape[0], m), BF16),
                   jax.ShapeDtypeStruct((wtb.shape[0], m), BF16)],
        scratch_shapes=[pltpu.VMEM((tm, k), BF16)],
        compiler_params=_cp(("parallel", "arbitrary")),
        name="in_proj",
    )(x, g.reshape(1, k), w, wta, wtb, ona, onb)


ONES_ROWS = 16


def _vt_weights(wv, dv):
    k, w = wv.shape
    wt = jnp.pad(wv.T.reshape(w // dv, dv, k), ((0, 0), (0, ONES_ROWS), (0, 0))).reshape(-1, k)
    ones = jnp.zeros((w // dv, dv + ONES_ROWS, 1), F32).at[:, dv, 0].set(1.0).reshape(-1, 1)
    return wt.astype(BF16), ones


def _split_q(q_ref, qs_ref, scale):
    q = q_ref[...].astype(F32) * scale
    lane = lax.broadcasted_iota(jnp.int32, q.shape, 1)
    qs_ref[0] = jnp.where(lane < LANES // 2, q, 0.0).astype(BF16)
    qs_ref[1] = jnp.where(lane >= LANES // 2, q, 0.0).astype(BF16)


def _kq(k, q):
    return lax.dot_general(k, q, (((1,), (1,)), ((), ())), preferred_element_type=F32)


def _put_scores(s, s_ref, cm_ref, slot, c):
    s_ref[slot, c] = s
    cm_ref[slot, c] = jnp.max(s, axis=0, keepdims=True)


def _update(vt_of, s_ref, cm_ref, m_ref, acc_ref, slot):
    for c in range(2):
        m_prev = m_ref[c]
        m_new = jnp.maximum(m_prev, cm_ref[slot, c])
        alpha = jnp.exp2(m_prev - m_new)
        p = jnp.exp2(s_ref[slot, c] - m_new).astype(BF16)
        acc_ref[c] = alpha * acc_ref[c] + jnp.dot(vt_of(c), p, preferred_element_type=F32)
        m_ref[c] = m_new


def _init_state(m_ref, acc_ref):
    m_ref[...] = jnp.full(m_ref.shape, NEG_BIG, F32)
    acc_ref[...] = jnp.zeros(acc_ref.shape, F32)


def _pipelined(n, scores, update, unroll):
    scores(0, 0)

    def body(jj, carry):
        j = unroll * jj
        for u in range(unroll):
            scores(j + u + 1, (u + 1) % 2)
            update(j + u, u % 2)
        return carry

    looped = (n - 1) // unroll
    lax.fori_loop(0, looped, body, 0)
    for j in range(unroll * looped, n):
        if j + 1 < n:
            scores(j + 1, (j + 1) % 2)
        update(j, j % 2)


def _pipelined_list(n, chunk, scores, update):
    scores(chunk(0), 0)

    def body(unroll, first):
        def run(jj, carry):
            t = first + unroll * jj
            for u in range(unroll):
                scores(chunk(t + u + 1), (u + 1) % 2)
                update(chunk(t + u), u % 2)
            return carry
        return run

    quads = (n - 1) // 4
    lax.fori_loop(0, quads, body(4, 0), 0)
    pairs = (n - 1 - 4 * quads) // 2
    lax.fori_loop(0, pairs, body(2, 4 * quads), 0)
    left = n - 4 * quads - 2 * pairs

    @pl.when(left == 2)
    def _():
        scores(chunk(n - 1), 1)
        update(chunk(n - 2), 0)
        update(chunk(n - 1), 1)

    @pl.when(left == 1)
    def _():
        update(chunk(n - 1), 0)


def _normalized(acc_ref, c):
    dv = acc_ref.shape[1] - ONES_ROWS
    return acc_ref[c, :dv, :] / acc_ref[c, dv:dv + 1, :]


def _attn_scratch(tq, tk, dv):
    return [pltpu.VMEM((2, 1, tq), F32), pltpu.VMEM((2, dv + ONES_ROWS, tq), F32),
            pltpu.VMEM((2, 2, tk, tq), F32), pltpu.VMEM((2, 2, 1, tq), F32)]


SKIP_MARGIN = 128.0


def _block_norms_kernel(qk_ref, sel_ref, o_ref):
    x = qk_ref[...].astype(F32)
    lane = lax.broadcasted_iota(jnp.int32, x.shape, 1)
    x = jnp.where(lane < x.shape[1] // 2, (x * (A_DH ** -0.5 * LOG2E)).astype(BF16).astype(F32), x)
    sums = jnp.dot(x * x, sel_ref[...], preferred_element_type=F32)
    o_ref[...] = jnp.sqrt(jnp.max(sums, axis=0, keepdims=True))


def _block_norms(proj3, t):
    b, s, _ = proj3.shape
    w = 2 * A_HEADS * 2 * A_DH
    sel = (jnp.arange(w)[:, None] // A_DH == jnp.arange(LANES)[None, :]).astype(BF16)
    return pl.pallas_call(
        _block_norms_kernel,
        grid=(b, s // t),
        in_specs=[pl.BlockSpec((None, t, w), lambda bb, i: (bb, i, COL_A * LANES // w)),
                  pl.BlockSpec((w, LANES), lambda bb, i: (0, 0))],
        out_specs=pl.BlockSpec((None, None, 1, LANES), lambda bb, i: (bb, i, 0, 0)),
        out_shape=jax.ShapeDtypeStruct((b, s // t, 1, LANES), F32),
        compiler_params=_cp(("parallel", "parallel")),
        name="block_norms",
    )(proj3, sel)[:, :, 0, :]


def _a_chunk_lists(proj3, pos_f, slopes2, tq, tk):
    assert tq == tk
    b, s, _ = proj3.shape
    norms = _block_norms(proj3, tq)[:, :, :4 * A_HEADS].reshape(b, s // tq, 2, A_HEADS, 2).max(axis=-1)
    qn = norms[:, :, 0].transpose(0, 2, 1)
    kn = norms[:, :, 1].transpose(0, 2, 1)
    qk = 1.01 * qn[:, :, :, None] * kn[:, :, None, :] + 1.0
    pq = pos_f.reshape(b, s // tq, tq)
    pk = pos_f.reshape(b, s // tk, tk)
    qlo, qhi = jnp.min(pq, -1)[:, :, None], jnp.max(pq, -1)[:, :, None]
    klo, khi = jnp.min(pk, -1)[:, None, :], jnp.max(pk, -1)[:, None, :]
    dmin = jnp.maximum(jnp.maximum(klo - qhi, qlo - khi), 0.0)
    dmax = jnp.maximum(khi - qlo, qhi - klo)
    sl = slopes2[None, :, None, None]
    lower = -qk - sl * dmax[:, None]
    upper = qk - sl * dmin[:, None]
    nk = s // tk
    j = jnp.arange(nk, dtype=jnp.int32)
    cover = jnp.max(jnp.min(jnp.abs(pq[:, :, :, None] - pk[:, :, None, :]), axis=-1), axis=-1)
    lower = jnp.where(j[:, None] == j[None, :], jnp.maximum(lower, -qk - sl * cover[:, None, :, None]), lower)
    first = jnp.argmax(lower, axis=-1).astype(jnp.int32)[..., None]
    keep = (upper >= jnp.max(lower, axis=-1, keepdims=True) - SKIP_MARGIN) | (j == first)
    lst = jnp.argsort(jnp.where(j == first, -1, jnp.where(keep, j, nk + j)), axis=-1).astype(jnp.int32)
    cnt = jnp.sum(keep.astype(jnp.int32), axis=-1)
    return lst.reshape(-1), cnt.reshape(-1)


def _attn_a_kernel(lst_ref, cnt_ref, lam_ref, slope_ref, q_ref, k_ref, vt_ref, pq_ref, pk_ref, g_ref, o_ref,
                   qs_ref, m_ref, acc_ref, s_ref, cm_ref, *, tk, post_scale):
    tq = q_ref.shape[0]
    nk = k_ref.shape[0] // tk
    blk = (pl.program_id(0) * pl.num_programs(1) + pl.program_id(1)) * pl.num_programs(2) + pl.program_id(2)
    sl2 = slope_ref[pl.program_id(1)]
    _split_q(q_ref, qs_ref, A_DH ** -0.5 * LOG2E)
    _init_state(m_ref, acc_ref)
    pq = pq_ref[...] * sl2

    def scores(j, slot):
        ks = pl.multiple_of(j * tk, tk)
        k = k_ref[pl.ds(ks, tk), :]
        pk = pk_ref[pl.ds(ks, tk), :] * sl2
        bias = jnp.abs(jnp.concatenate([pk] * (tq // LANES), axis=1) - pq)
        for c in range(2):
            _put_scores(_kq(k, qs_ref[c]) - bias, s_ref, cm_ref, slot, c)

    def update(j, slot):
        ks = pl.multiple_of(j * tk, tk)
        _update(lambda c: vt_ref[:, pl.ds(ks, tk)], s_ref, cm_ref, m_ref, acc_ref, slot)

    _pipelined_list(cnt_ref[blk], lambda t: lst_ref[blk * nk + t], scores, update)
    o = (_normalized(acc_ref, 0) - lam_ref[0] * _normalized(acc_ref, 1)).T
    o_ref[...] = (_rms(o, g_ref[...], 1.0 / LANES) * post_scale).astype(o_ref.dtype)


def _attn_a(proj, vt, posq_row, posk_rep, lam, slopes2, g, post_scale, tq, tk):
    b, s, _ = proj.shape
    lst, cnt = _a_chunk_lists(proj, posq_row.reshape(b, s), slopes2, tq, tk)
    smem = pl.BlockSpec(memory_space=pltpu.SMEM)
    grid_spec = pltpu.PrefetchScalarGridSpec(
        num_scalar_prefetch=2,
        grid=(b, A_HEADS, s // tq),
        in_specs=[smem, smem,
                  pl.BlockSpec((None, tq, LANES), lambda bb, h, i, *_: (bb, i, COL_A + h)),
                  pl.BlockSpec((None, s, LANES), lambda bb, h, i, *_: (bb, 0, COL_A + A_HEADS + h)),
                  pl.BlockSpec((vt.shape[0] // A_HEADS, s), lambda bb, h, i, *_: (h, bb)),
                  pl.BlockSpec((None, 1, tq), lambda bb, h, i, *_: (bb, 0, i)),
                  pl.BlockSpec((None, s, LANES), lambda bb, h, i, *_: (bb, 0, 0)),
                  pl.BlockSpec((1, LANES), lambda bb, h, i, *_: (0, 0))],
        out_specs=pl.BlockSpec((None, tq, LANES), lambda bb, h, i, *_: (bb, i, h)),
        scratch_shapes=[pltpu.VMEM((2, tq, LANES), BF16)] + _attn_scratch(tq, tk, LANES),
    )
    return pl.pallas_call(
        functools.partial(_attn_a_kernel, tk=tk, post_scale=post_scale),
        grid_spec=grid_spec,
        out_shape=jax.ShapeDtypeStruct((b, s, A_HEADS * LANES), BF16),
        compiler_params=_cp(("parallel", "parallel", "arbitrary")),
        name="attn_diff",
    )(lst, cnt, lam, slopes2, proj, proj, vt, posq_row, posk_rep, g.reshape(1, LANES))


def _attn_b_kernel(q_ref, k_ref, vt_ref, bias_ref, o_ref, qs_ref, m_ref, acc_ref, s_ref, cm_ref, *, nband):
    t = q_ref.shape[0]
    nkb = k_ref.shape[0] // t
    rows = vt_ref.shape[0] // 2
    i = pl.program_id(2)
    half = nband // 2
    _split_q(q_ref, qs_ref, B_DH ** -0.5 * LOG2E)
    _init_state(m_ref, acc_ref)

    def start(jj):
        return pl.multiple_of(jnp.clip(i + jj - half, 0, nkb - 1) * t, t)

    def scores(jj, slot):
        kb = i + jj - half
        table = jnp.where((kb >= 0) & (kb < nkb), jj, nband)
        k = k_ref[pl.ds(start(jj), t), :]
        for c in range(2):
            _put_scores(_kq(k, qs_ref[c]) + bias_ref[c, table], s_ref, cm_ref, slot, c)

    def update(jj, slot):
        ks = start(jj)
        _update(lambda c: vt_ref[c * rows:(c + 1) * rows, pl.ds(ks, t)], s_ref, cm_ref, m_ref, acc_ref, slot)

    _pipelined(nband, scores, update, unroll=2)
    o = jnp.concatenate([_normalized(acc_ref, 0), _normalized(acc_ref, 1)], axis=0)
    o_ref[...] = o.T.astype(o_ref.dtype)


def _b_bias_tables(t, nband):
    half = nband // 2
    r = jnp.arange(t, dtype=jnp.int32)[None, :, None]
    c = jnp.arange(t, dtype=jnp.int32)[None, None, :]
    jj = jnp.arange(nband + 1, dtype=jnp.int32)[:, None, None]
    ao = jnp.abs((jj - half) * t + r - c)
    mult = jnp.zeros(ao.shape, jnp.int32)
    for window, dilation in B_PATTERNS:
        reach = (window // (2 * dilation)) * dilation
        mult = mult + ((ao % dilation == 0) & (ao <= reach)).astype(jnp.int32)
    mult = jnp.where(jj < nband, mult, 0)
    slopes = jnp.exp2(-8.0 * jnp.arange(1, B_HEADS + 1, dtype=F32) / B_HEADS) * LOG2E
    bias = jnp.log2(jnp.maximum(mult, 1).astype(F32))[None] - slopes[:, None, None, None] * ao.astype(F32)[None]
    bias = jnp.where((mult > 0)[None], bias, NEG_BIG)
    return bias.reshape(B_HEADS // 2, 2, nband + 1, t, t)


def _attn_b(proj, vt, bias, t):
    b, s, _ = proj.shape
    nband = bias.shape[2] - 1
    npair = B_HEADS // 2
    return pl.pallas_call(
        functools.partial(_attn_b_kernel, nband=nband),
        grid=(b, npair, s // t),
        in_specs=[pl.BlockSpec((None, t, LANES), lambda bb, p, i: (bb, i, COL_B + p)),
                  pl.BlockSpec((None, s, LANES), lambda bb, p, i: (bb, 0, COL_B + npair + p)),
                  pl.BlockSpec((vt.shape[0] // npair, s), lambda bb, p, i: (p, bb)),
                  pl.BlockSpec((None,) + bias.shape[1:], lambda bb, p, i: (p, 0, 0, 0, 0))],
        out_specs=pl.BlockSpec((None, t, LANES), lambda bb, p, i: (bb, i, p)),
        out_shape=jax.ShapeDtypeStruct((b, s, npair * LANES), BF16),
        scratch_shapes=[pltpu.VMEM((2, t, LANES), BF16)] + _attn_scratch(t, t, B_DH),
        compiler_params=_cp(("parallel", "parallel", "arbitrary")),
        name="attn_dilated",
    )(proj, proj, vt, bias)


def _rope_table_kernel(pos_ref, invf_ref, c_ref, s_ref):
    ang = pos_ref[...] * invf_ref[...]
    c_ref[...] = jnp.cos(ang)
    s_ref[...] = jnp.sin(ang)


def _rope_tables(pos_col, tm):
    t = pos_col.shape[0]
    inv = ROPE_THETA ** (-jnp.arange(0, C_ROPE, 2, dtype=F32) / C_ROPE)
    invf = jnp.concatenate([jnp.zeros((C_NOPE,), F32), inv, inv,
                            jnp.zeros((LANES - C_NOPE - C_ROPE,), F32)]).reshape(1, LANES)
    spec = pl.BlockSpec((tm, LANES), lambda i: (i, 0))
    return pl.pallas_call(
        _rope_table_kernel,
        grid=(t // tm,),
        in_specs=[pl.BlockSpec((tm, 1), lambda i: (i, 0)), pl.BlockSpec((1, LANES), lambda i: (0, 0))],
        out_specs=[spec, spec],
        out_shape=[jax.ShapeDtypeStruct((t, LANES), F32)] * 2,
        compiler_params=_cp(("parallel",)),
        name="rope_tables",
    )(pos_col, invf)


def _mla_prep_kernel(cq_ref, ckv_ref, r1_ref, r2_ref, c_ref, s_ref, gq_ref, gkv_ref,
                     wq1_ref, wq2_ref, wk_ref, wvt_ref, onv_ref, q_out, k_out, vt_out):
    qn = _rms(cq_ref[...].astype(F32), gq_ref[...], 1.0 / C_Q_RANK).astype(BF16)
    kvn = _rms(ckv_ref[...].astype(F32), gkv_ref[...], 1.0 / C_KV_RANK).astype(BF16)
    q1 = jnp.dot(qn, wq1_ref[...], preferred_element_type=F32)
    q2 = jnp.dot(qn, wq2_ref[...], preferred_element_type=F32)
    k1 = jnp.dot(kvn, wk_ref[...], preferred_element_type=F32)
    vt_out[...] = (_kq(wvt_ref[...], kvn) + onv_ref[...]).astype(vt_out.dtype)
    cos = c_ref[...]
    sin = s_ref[...]
    k_rope = r1_ref[...].astype(F32) * cos + r2_ref[...].astype(F32) * sin
    scale = (C_NOPE + C_ROPE) ** -0.5 * LOG2E
    for h in range(C_HEADS):
        sl = slice(h * LANES, (h + 1) * LANES)
        q_out[:, sl] = ((q1[:, sl] * cos + q2[:, sl] * sin) * scale).astype(q_out.dtype)
        k_out[:, sl] = (k1[:, sl] + k_rope).astype(k_out.dtype)


def _mla_prep(proj2, cos, sin, gq, gkv, wq1, wq2, wk, wvt, onv, tm):
    t = proj2.shape[0]
    full = lambda a: pl.BlockSpec(a.shape, lambda i: (0, 0))
    row = lambda w: pl.BlockSpec((tm, w), lambda i: (i, 0))
    return pl.pallas_call(
        _mla_prep_kernel,
        grid=(t // tm,),
        in_specs=[pl.BlockSpec((tm, 4 * LANES), lambda i: (i, COL_CQ // 4)),
                  pl.BlockSpec((tm, 2 * LANES), lambda i: (i, COL_CKV // 2)),
                  pl.BlockSpec((tm, LANES), lambda i: (i, COL_R1)),
                  pl.BlockSpec((tm, LANES), lambda i: (i, COL_R2)),
                  row(LANES), row(LANES), full(gq), full(gkv), full(wq1), full(wq2), full(wk), full(wvt), full(onv)],
        out_specs=[row(C_HEADS * LANES), row(C_HEADS * LANES), pl.BlockSpec((wvt.shape[0], tm), lambda i: (0, i))],
        out_shape=[jax.ShapeDtypeStruct((t, C_HEADS * LANES), BF16),
                   jax.ShapeDtypeStruct((t, C_HEADS * LANES), BF16),
                   jax.ShapeDtypeStruct((wvt.shape[0], t), BF16)],
        compiler_params=_cp(("parallel",)),
        name="mla_prep",
    )(proj2, proj2, proj2, proj2, cos, sin, gq, gkv, wq1, wq2, wk, wvt, onv)


def _attn_c_kernel(q_ref, k_ref, vt_ref, o_ref, m_ref, acc_ref, s_ref, cm_ref, *, tk):
    rows = vt_ref.shape[0] // 2
    _init_state(m_ref, acc_ref)

    def scores(j, slot):
        ks = pl.multiple_of(j * tk, tk)
        for c in range(2):
            sl = slice(c * LANES, (c + 1) * LANES)
            _put_scores(_kq(k_ref[pl.ds(ks, tk), sl], q_ref[:, sl]), s_ref, cm_ref, slot, c)

    def update(j, slot):
        ks = pl.multiple_of(j * tk, tk)
        _update(lambda c: vt_ref[c * rows:(c + 1) * rows, pl.ds(ks, tk)], s_ref, cm_ref, m_ref, acc_ref, slot)

    _pipelined(k_ref.shape[0] // tk, scores, update, unroll=4)
    o = jnp.concatenate([_normalized(acc_ref, 0), _normalized(acc_ref, 1)], axis=0)
    o_ref[...] = o.T.astype(o_ref.dtype)


def _attn_c(q, k, vt, tq, tk):
    b, s, _ = q.shape
    npair = C_HEADS // 2
    return pl.pallas_call(
        functools.partial(_attn_c_kernel, tk=tk),
        grid=(b, npair, s // tq),
        in_specs=[pl.BlockSpec((None, tq, 2 * LANES), lambda bb, p, i: (bb, i, p)),
                  pl.BlockSpec((None, s, 2 * LANES), lambda bb, p, i: (bb, 0, p)),
                  pl.BlockSpec((vt.shape[0] // npair, s), lambda bb, p, i: (p, bb))],
        out_specs=pl.BlockSpec((None, tq, LANES), lambda bb, p, i: (bb, i, p)),
        out_shape=jax.ShapeDtypeStruct((b, s, npair * LANES), BF16),
        scratch_shapes=_attn_scratch(tq, tk, C_DV),
        compiler_params=_cp(("parallel", "parallel", "arbitrary")),
        name="attn_latent",
    )(q, k, vt)


def _merge_kernel(oa_ref, ob_ref, oc_ref, g0_ref, g1_ref, g2_ref, x_ref, wb_ref, wo_ref, o_ref):
    z = None
    for n, (o_r, g_r) in enumerate(((oa_ref, g0_ref), (ob_ref, g1_ref), (oc_ref, g2_ref))):
        br = jnp.dot(o_r[...], wb_ref[n], preferred_element_type=F32)
        gate = 1.0 / (1.0 + jnp.exp(-g_r[...].astype(F32)))
        z = gate * br if z is None else z + gate * br
    o_ref[...] = x_ref[...] + jnp.dot(z.astype(BF16), wo_ref[...], preferred_element_type=F32)


def _merge(oa, ob, oc, proj2, x2, wb, wo, tm):
    t, d = x2.shape
    bw = oa.shape[1]
    row = lambda w: pl.BlockSpec((tm, w), lambda i: (i, 0))
    gate = lambda n: pl.BlockSpec((tm, d), lambda i: (i, COL_G * LANES // d + n))
    return pl.pallas_call(
        _merge_kernel,
        grid=(t // tm,),
        in_specs=[row(bw), row(bw), row(bw), gate(0), gate(1), gate(2), row(d),
                  pl.BlockSpec(wb.shape, lambda i: (0, 0, 0)), pl.BlockSpec(wo.shape, lambda i: (0, 0))],
        out_specs=row(d),
        out_shape=jax.ShapeDtypeStruct((t, d), F32),
        compiler_params=_cp(("parallel",)),
        name="branch_merge",
    )(oa, ob, oc, proj2, proj2, proj2, x2, wb, wo)


def _cross_kernel(x_ref, g_ref, wq_ref, kbd_ref, vbd_ref, wo_ref, o_ref, *, n_mem):
    x = x_ref[...]
    h = _rms(x, g_ref[...], 1.0 / x.shape[-1]).astype(BF16)
    q = (jnp.dot(h, wq_ref[...], preferred_element_type=F32) * (X_DH ** -0.5 * LOG2E)).astype(BF16)
    s = jnp.dot(q, kbd_ref[...], preferred_element_type=F32)
    ps = []
    for hh in range(X_HEADS):
        sh = s[:, hh * n_mem:(hh + 1) * n_mem]
        p = jnp.exp2(sh - jnp.max(sh, axis=-1, keepdims=True))
        ps.append((p / jnp.sum(p, axis=-1, keepdims=True)).astype(BF16))
    o = jnp.dot(jnp.concatenate(ps, axis=1), vbd_ref[...], preferred_element_type=F32)
    o_ref[...] = x + jnp.dot(o.astype(BF16), wo_ref[...], preferred_element_type=F32)


def _cross(x3, g, wq, kbd, vbd, wo, tm):
    b, s, d = x3.shape
    n_mem = kbd.shape[2] // X_HEADS
    full = lambda a: pl.BlockSpec(a.shape, lambda bb, i: (0, 0))
    return pl.pallas_call(
        functools.partial(_cross_kernel, n_mem=n_mem),
        grid=(b, s // tm),
        in_specs=[pl.BlockSpec((None, tm, d), lambda bb, i: (bb, i, 0)), full(g), full(wq),
                  pl.BlockSpec((None,) + kbd.shape[1:], lambda bb, i: (bb, 0, 0)),
                  pl.BlockSpec((None,) + vbd.shape[1:], lambda bb, i: (bb, 0, 0)), full(wo)],
        out_specs=pl.BlockSpec((None, tm, d), lambda bb, i: (bb, i, 0)),
        out_shape=jax.ShapeDtypeStruct((b, s, d), F32),
        compiler_params=_cp(("parallel", "parallel")),
        name="cross_attn",
    )(x3, g, wq, kbd, vbd, wo)


def _block_diag_kv(kv):
    b, m, _ = kv.shape
    kv = kv.reshape(b, m, 2, X_HEADS, X_DH)
    eye = jnp.eye(X_HEADS, dtype=kv.dtype)
    kt = kv[:, :, 0].transpose(0, 2, 3, 1)
    kbd = (kt[:, :, :, None, :] * eye[None, :, None, :, None]).reshape(b, X_HEADS * X_DH, X_HEADS * m)
    vt = kv[:, :, 1].transpose(0, 2, 1, 3)
    vbd = (vt[:, :, :, None, :] * eye[None, :, None, :, None]).reshape(b, X_HEADS * m, X_HEADS * X_DH)
    return kbd, vbd


def _router_kernel(x_ref, g_ref, w_ref, b_ref, h_out, r_out):
    x = x_ref[...]
    h = _rms(x, g_ref[...], 1.0 / x.shape[-1])
    h_out[...] = h.astype(h_out.dtype)
    logits = jnp.dot(h, w_ref[...], preferred_element_type=F32, precision=lax.Precision.HIGHEST) + b_ref[...]
    lane = lax.broadcasted_iota(jnp.int32, logits.shape, 1)
    lane_f = lane.astype(F32)
    big = jnp.float32(4 * LANES)

    def top(vals, mask):
        mv = jnp.max(jnp.where(mask, vals, -jnp.inf), axis=-1, keepdims=True)
        idx = jnp.min(jnp.where(mask & (vals == mv), lane_f, big), axis=-1, keepdims=True)
        return mv, idx

    g_mask = lane < N_GROUPS
    g_max, g_idx = top(logits, g_mask)
    p_g = 1.0 / jnp.sum(jnp.where(g_mask, jnp.exp(logits - g_max), 0.0), axis=-1, keepdims=True)
    first = N_GROUPS + g_idx * EXPERTS_PER_GROUP
    e_mask = (lane_f >= first) & (lane_f < first + EXPERTS_PER_GROUP)
    v0, i0 = top(logits, e_mask)
    v1, i1 = top(logits, e_mask & (lane_f != i0))
    e1 = jnp.exp(v1 - v0)
    w0 = p_g / (1.0 + e1)
    w1 = p_g * e1 / (1.0 + e1)
    out = jnp.where(lane == 0, i0 - N_GROUPS, 0.0)
    out = jnp.where(lane == 1, i1 - N_GROUPS, out)
    out = jnp.where(lane == 2, w0, out)
    out = jnp.where(lane == 3, w1, out)
    r_out[...] = out


def _router(x2, g, w, bias, tm):
    t, d = x2.shape
    full = lambda a: pl.BlockSpec(a.shape, lambda i: (0, 0))
    return pl.pallas_call(
        _router_kernel,
        grid=(t // tm,),
        in_specs=[pl.BlockSpec((tm, d), lambda i: (i, 0)), full(g), full(w), full(bias)],
        out_specs=[pl.BlockSpec((tm, d), lambda i: (i, 0)), pl.BlockSpec((tm, LANES), lambda i: (i, 0))],
        out_shape=[jax.ShapeDtypeStruct((t, d), BF16), jax.ShapeDtypeStruct((t, LANES), F32)],
        compiler_params=_cp(("parallel",)),
        name="moe_router",
    )(x2, g, w, bias)


def _expert_kernel(blk_e_ref, n_used_ref, x_ref, w1_ref, w3_ref, w2_ref, o_ref, w13_s, w2_s):
    i = pl.program_id(0)
    used = i < n_used_ref[0]
    de = w1_ref.shape[1]

    @pl.when(used & ((i == 0) | (blk_e_ref[i] != blk_e_ref[jnp.maximum(i - 1, 0)])))
    def _():
        w13_s[:, :de] = w1_ref[...].astype(BF16)
        w13_s[:, de:] = w3_ref[...].astype(BF16)
        w2_s[...] = w2_ref[...].astype(BF16)

    @pl.when(used)
    def _():
        hid = jnp.dot(x_ref[...], w13_s[...], preferred_element_type=F32)
        a = hid[:, :de]
        act = (a / (1.0 + jnp.exp(-a))) * hid[:, de:]
        o_ref[...] = jnp.dot(act.astype(BF16), w2_s[...], preferred_element_type=F32).astype(o_ref.dtype)

    @pl.when(jnp.logical_not(used))
    def _():
        o_ref[...] = jnp.zeros(o_ref.shape, o_ref.dtype)


def _experts(blk_e, n_used, xr, w1, w3, w2, layer, rows_per_block):
    rows, d = xr.shape
    de = w1.shape[3]
    weight = lambda w: pl.BlockSpec((None, None) + w.shape[2:], lambda i, be, nu: (layer, be[i], 0, 0))
    grid_spec = pltpu.PrefetchScalarGridSpec(
        num_scalar_prefetch=2,
        grid=(rows // rows_per_block,),
        in_specs=[pl.BlockSpec((rows_per_block, d), lambda i, be, nu: (i, 0)), weight(w1), weight(w3), weight(w2)],
        out_specs=pl.BlockSpec((rows_per_block, d), lambda i, be, nu: (i, 0)),
        scratch_shapes=[pltpu.VMEM((d, 2 * de), BF16), pltpu.VMEM((de, d), BF16)],
    )
    return pl.pallas_call(
        _expert_kernel,
        grid_spec=grid_spec,
        out_shape=jax.ShapeDtypeStruct((rows, d), BF16),
        compiler_params=_cp(("arbitrary",)),
        name="moe_experts",
    )(blk_e, n_used, xr, w1, w3, w2)


def _combine_kernel(x_ref, y0_ref, y1_ref, r_ref, g_ref, o_ref, *, final_norm):
    r = r_ref[...]
    y = x_ref[...] + r[:, 2:3] * y0_ref[...].astype(F32) + r[:, 3:4] * y1_ref[...].astype(F32)
    if final_norm:
        y = _rms(y, g_ref[...], 1.0 / y.shape[-1])
    o_ref[...] = y


def _combine(x2, y0, y1, route, g, final_norm, tm):
    t, d = x2.shape
    row = lambda w: pl.BlockSpec((tm, w), lambda i: (i, 0))
    return pl.pallas_call(
        functools.partial(_combine_kernel, final_norm=final_norm),
        grid=(t // tm,),
        in_specs=[row(d), row(d), row(d), row(LANES), pl.BlockSpec((1, d), lambda i: (0, 0))],
        out_specs=row(d),
        out_shape=jax.ShapeDtypeStruct((t, d), F32),
        compiler_params=_cp(("parallel",)),
        name="moe_combine",
    )(x2, y0, y1, route, g)


def _dispatch(route, rows_per_block):
    t = route.shape[0]
    eid = route[:, :TOP_K].astype(jnp.int32).reshape(-1)
    n = eid.shape[0]
    order = jnp.argsort(eid).astype(jnp.int32)
    rank = jnp.argsort(order).astype(jnp.int32)
    experts = jnp.arange(N_EXPERTS, dtype=jnp.int32)
    counts = jnp.sum((eid[:, None] == experts[None, :]).astype(jnp.int32), axis=0)
    start = jnp.cumsum(counts) - counts
    padded = (counts + rows_per_block - 1) // rows_per_block * rows_per_block
    pend = jnp.cumsum(padded)
    pstart = pend - padded
    dest = (rank + (pstart - start)[eid]).reshape(t, TOP_K)
    n_blocks = n // rows_per_block + N_EXPERTS
    blk_first = jnp.arange(n_blocks, dtype=jnp.int32) * rows_per_block
    blk_e = jnp.minimum(jnp.sum((pend[None, :] <= blk_first[:, None]).astype(jnp.int32), axis=1), N_EXPERTS - 1)
    off = (blk_first - pstart[blk_e])[:, None] + jnp.arange(rows_per_block, dtype=jnp.int32)[None, :]
    src = jnp.clip(start[blk_e][:, None] + off, 0, n - 1)
    row_tok = jnp.where(off < counts[blk_e][:, None], order[src] // TOP_K, 0).reshape(-1)
    n_used = (pend[-1] // rows_per_block).astype(jnp.int32).reshape(1)
    return row_tok, dest, blk_e, n_used


def _rot_cols(w):
    half = w.shape[-1] // 2
    return jnp.concatenate([-w[..., half:], w[..., :half]], axis=-1)


def _pack_w_in(w):
    d = w.shape[0]
    blk = A_HEADS * 2 * A_DH
    n_ab = 6 * blk
    cq = w[:, n_ab:n_ab + C_Q_RANK]
    ckv = w[:, n_ab + C_Q_RANK:n_ab + C_Q_RANK + C_KV_RANK]
    ckr = w[:, n_ab + C_Q_RANK + C_KV_RANK:n_ab + C_Q_RANK + C_KV_RANK + C_ROPE]
    gates = w[:, n_ab + C_Q_RANK + C_KV_RANK + C_ROPE:]
    z = lambda n: jnp.zeros((d, n), w.dtype)
    tail = LANES - C_NOPE - C_ROPE
    packed = jnp.concatenate([w[:, :2 * blk], w[:, 3 * blk:5 * blk], cq, z(LANES), ckv,
                              z(C_NOPE), ckr, z(tail), z(C_NOPE), _rot_cols(ckr), z(tail), gates], axis=1)
    return packed.astype(BF16), w[:, 2 * blk:3 * blk], w[:, 5 * blk:6 * blk]


def _pack_w_uq(w):
    wq = w.reshape(C_Q_RANK, C_HEADS, C_NOPE + C_ROPE)
    pad_rows = 4 * LANES - C_Q_RANK
    tail = LANES - C_NOPE - C_ROPE
    q1 = jnp.pad(wq, ((0, pad_rows), (0, 0), (0, tail))).reshape(4 * LANES, C_HEADS * LANES)
    q2 = jnp.pad(_rot_cols(wq[:, :, C_NOPE:]), ((0, pad_rows), (0, 0), (C_NOPE, tail))).reshape(4 * LANES, C_HEADS * LANES)
    return q1.astype(BF16), q2.astype(BF16)


def _pack_w_ukv(w):
    wkv = w.reshape(C_KV_RANK, C_HEADS, C_NOPE + C_DV)
    wk = jnp.pad(wkv[:, :, :C_NOPE], ((0, 0), (0, 0), (0, LANES - C_NOPE))).reshape(C_KV_RANK, C_HEADS * LANES)
    wv = wkv[:, :, C_NOPE:].reshape(C_KV_RANK, C_HEADS * C_DV)
    return wk.astype(BF16), wv


def kernel(x, mem, positions, mix_norm_g, w_in, diff_lambda, diff_subln_g, mla_q_norm_g, w_uq, mla_kv_norm_g, w_ukv, w_branch, w_out, cross_norm_g, mem_norm_g, w_xq, w_xkv, w_xo, ffn_norm_g, w_group, b_group, w_router, b_router, w1, w3, w2, final_norm_g):
    b, s, d = x.shape
    depth = w_in.shape[0]
    t = b * s
    n_mem = mem.shape[1]
    tl = _tiles(s)
    assert PROJ_COLS == COL_G * LANES + N_BRANCHES * d and s % (2 * tl["t_b"]) == 0

    pos_f = positions.astype(F32)
    posq_row = pos_f.reshape(b, 1, s)
    posk_rep = jnp.broadcast_to(pos_f[:, :, None], (b, s, LANES))
    cos, sin = _rope_tables(pos_f.reshape(t, 1), tl["tm_tok"])
    slopes_a2 = jnp.exp2(-8.0 * jnp.arange(1, A_HEADS + 1, dtype=F32) / A_HEADS) * LOG2E
    reach = max((w // (2 * dl)) * dl for w, dl in B_PATTERNS)
    b_bias = _b_bias_tables(tl["t_b"], 2 * (-(-reach // tl["t_b"])) + 1)
    mem2 = mem.reshape(b * n_mem, d)

    x2 = x.reshape(t, d)
    for l in range(depth):
        w_main, w_av, w_bv = _pack_w_in(w_in[l])
        proj2, vt_a, vt_b = _in_proj(x2, mix_norm_g[l], w_main, *_vt_weights(w_av, 2 * A_DH), *_vt_weights(w_bv, B_DH),
                                     tl["tm_proj"], tl["tn_proj"])
        proj3 = proj2.reshape(b, s, PROJ_COLS)
        lq = diff_lambda[l].astype(F32)
        lam_init = 0.8 - 0.6 * math.exp(-0.3 * l)
        lam = (jnp.exp(jnp.sum(lq[0] * lq[1])) - jnp.exp(jnp.sum(lq[2] * lq[3])) + lam_init).reshape(1)
        oa = _attn_a(proj3, vt_a, posq_row, posk_rep, lam, slopes_a2, diff_subln_g[l], 1.0 - lam_init,
                     tl["tq_a"], tl["tk_a"])
        ob = _attn_b(proj3, vt_b, b_bias, tl["t_b"])
        wq1, wq2 = _pack_w_uq(w_uq[l])
        wk, wv = _pack_w_ukv(w_ukv[l])
        gq = jnp.pad(mla_q_norm_g[l], (0, 4 * LANES - C_Q_RANK)).reshape(1, 4 * LANES)
        qc, kc, vt_c = _mla_prep(proj2, cos, sin, gq, mla_kv_norm_g[l].reshape(1, C_KV_RANK),
                                 wq1, wq2, wk, *_vt_weights(wv, C_DV), tl["tm_tok"])
        oc = _attn_c(qc.reshape(b, s, -1), kc.reshape(b, s, -1), vt_c, tl["tq_c"], tl["tk_c"])
        x2 = _merge(oa.reshape(t, -1), ob.reshape(t, -1), oc.reshape(t, -1), proj2, x2,
                    w_branch[l].astype(BF16), w_out[l].astype(BF16), tl["tm_tok"])
        kv = _norm_matmul(mem2, mem_norm_g[l], w_xkv[l].astype(BF16), min(1024, b * n_mem), w_xkv.shape[2])
        kbd, vbd = _block_diag_kv(kv.reshape(b, n_mem, -1))
        x2 = _cross(x2.reshape(b, s, d), cross_norm_g[l].reshape(1, d), w_xq[l].astype(BF16), kbd, vbd,
                    w_xo[l].astype(BF16), tl["tm_tok"]).reshape(t, d)
        w_r = jnp.pad(jnp.concatenate([w_group[l], w_router[l]], axis=1), ((0, 0), (0, LANES - N_GROUPS - N_EXPERTS)))
        b_r = jnp.pad(jnp.concatenate([b_group[l], b_router[l]]), (0, LANES - N_GROUPS - N_EXPERTS)).reshape(1, LANES)
        h, route = _router(x2, ffn_norm_g[l].reshape(1, d), w_r, b_r, tl["tm_tok"])
        row_tok, dest, blk_e, n_used = _dispatch(route, tl["moe_rows"])
        yr = _experts(blk_e, n_used, h[row_tok], w1, w3, w2, l, tl["moe_rows"])
        x2 = _combine(x2, yr[dest[:, 0]], yr[dest[:, 1]], route, final_norm_g.reshape(1, d),
                      l == depth - 1, tl["tm_tok"])
    return x2.reshape(b, s, d)
```

```python
import functools
import math

import jax
import jax.numpy as jnp
from jax import lax
from jax.experimental import pallas as pl
from jax.experimental.pallas import tpu as pltpu

F32 = jnp.float32
BF16 = jnp.bfloat16

LANES = 128
NORM_EPS = 1e-6
LOG2E = math.log2(math.e)
NEG_BIG = -1e30

A_HEADS, A_DH = 4, 64
B_HEADS, B_DH = 8, 64
B_PATTERNS = ((128, 1), (512, 4), (2048, 16))
C_HEADS, C_Q_RANK, C_KV_RANK, C_NOPE, C_ROPE, C_DV = 8, 384, 256, 64, 32, 64
ROPE_THETA = 10000.0
N_BRANCHES = 3
X_HEADS, X_DH = 4, 64
N_GROUPS, EXPERTS_PER_GROUP, TOP_K = 4, 8, 2
N_EXPERTS = N_GROUPS * EXPERTS_PER_GROUP

COL_A = 0
COL_B = 8
COL_CQ = 16
COL_CKV = 20
COL_R1 = 22
COL_R2 = 23
COL_G = 24
PROJ_COLS = 48 * LANES

VMEM_LIMIT = 48 * 1024 * 1024


def _tiles(seq):
    return dict(
        tm_proj=min(1024, seq), tn_proj=1024,
        tm_tok=min(512, seq),
        tq_a=min(512, seq), tk_a=min(512, seq),
        t_b=min(512, seq),
        tq_c=min(512, seq), tk_c=min(512, seq),
        moe_rows=512,
    )


def _cp(sem):
    return pltpu.CompilerParams(dimension_semantics=sem, vmem_limit_bytes=VMEM_LIMIT)


def _rms(x, g, inv_n):
    ms = jnp.sum(x * x, axis=-1, keepdims=True) * inv_n
    return x * lax.rsqrt(ms + NORM_EPS) * g


def _norm_matmul_kernel(x_ref, g_ref, w_ref, o_ref, h_ref, *, inv_n):
    @pl.when(pl.program_id(1) == 0)
    def _():
        h_ref[...] = _rms(x_ref[...].astype(F32), g_ref[...], inv_n).astype(BF16)

    o_ref[...] = jnp.dot(h_ref[...], w_ref[...], preferred_element_type=F32).astype(o_ref.dtype)


def _norm_matmul(x, g, w, tm, tn, out_dtype=BF16):
    m, k = x.shape
    n = w.shape[1]
    return pl.pallas_call(
        functools.partial(_norm_matmul_kernel, inv_n=1.0 / k),
        grid=(m // tm, n // tn),
        in_specs=[pl.BlockSpec((tm, k), lambda i, j: (i, 0)),
                  pl.BlockSpec((1, k), lambda i, j: (0, 0)),
                  pl.BlockSpec((k, tn), lambda i, j: (0, j))],
        out_specs=pl.BlockSpec((tm, tn), lambda i, j: (i, j)),
        out_shape=jax.ShapeDtypeStruct((m, n), out_dtype),
        scratch_shapes=[pltpu.VMEM((tm, k), BF16)],
        compiler_params=_cp(("parallel", "arbitrary")),
        name="norm_matmul",
    )(x, g.reshape(1, k), w)


def _in_proj_kernel(x_ref, g_ref, w_ref, wta_ref, wtb_ref, ona_ref, onb_ref, o_ref, vta_ref, vtb_ref, h_ref, *, inv_n):
    @pl.when(pl.program_id(1) == 0)
    def _():
        h = _rms(x_ref[...].astype(F32), g_ref[...], inv_n).astype(BF16)
        h_ref[...] = h
        vta_ref[...] = (_kq(wta_ref[...], h) + ona_ref[...]).astype(vta_ref.dtype)
        vtb_ref[...] = (_kq(wtb_ref[...], h) + onb_ref[...]).astype(vtb_ref.dtype)

    o_ref[...] = jnp.dot(h_ref[...], w_ref[...], preferred_element_type=F32).astype(o_ref.dtype)


def _in_proj(x, g, w, wta, ona, wtb, onb, tm, tn):
    m, k = x.shape
    n = w.shape[1]
    full = lambda a: pl.BlockSpec(a.shape, lambda i, j: (0, 0))
    return pl.pallas_call(
        functools.partial(_in_proj_kernel, inv_n=1.0 / k),
        grid=(m // tm, n // tn),
        in_specs=[pl.BlockSpec((tm, k), lambda i, j: (i, 0)),
                  pl.BlockSpec((1, k), lambda i, j: (0, 0)),
                  pl.BlockSpec((k, tn), lambda i, j: (0, j)),
                  full(wta), full(wtb), full(ona), full(onb)],
        out_specs=[pl.BlockSpec((tm, tn), lambda i, j: (i, j)),
                   pl.BlockSpec((wta.shape[0], tm), lambda i, j: (0, i)),
                   pl.BlockSpec((wtb.shape[0], tm), lambda i, j: (0, i))],
        out_shape=[jax.ShapeDtypeStruct((m, n), BF16),
                   jax.ShapeDtypeStruct((wta.shape[0], m), BF16),
                   jax.ShapeDtypeStruct((wtb.shape[0], m), BF16)],
        scratch_shapes=[pltpu.VMEM((tm, k), BF16)],
        compiler_params=_cp(("parallel", "arbitrary")),
        name="in_proj",
    )(x, g.reshape(1, k), w, wta, wtb, ona, onb)


ONES_ROWS = 16


def _vt_weights(wv, dv):
    k, w = wv.shape
    wt = jnp.pad(wv.T.reshape(w // dv, dv, k), ((0, 0), (0, ONES_ROWS), (0, 0))).reshape(-1, k)
    ones = jnp.zeros((w // dv, dv + ONES_ROWS, 1), F32).at[:, dv, 0].set(1.0).reshape(-1, 1)
    return wt.astype(BF16), ones


def _split_q(q_ref, qs_ref, scale):
    q = q_ref[...].astype(F32) * scale
    lane = lax.broadcasted_iota(jnp.int32, q.shape, 1)
    qs_ref[0] = jnp.where(lane < LANES // 2, q, 0.0).astype(BF16)
    qs_ref[1] = jnp.where(lane >= LANES // 2, q, 0.0).astype(BF16)


def _kq(k, q):
    return lax.dot_general(k, q, (((1,), (1,)), ((), ())), preferred_element_type=F32)


def _put_scores(s, s_ref, cm_ref, slot, c):
    s_ref[slot, c] = s
    cm_ref[slot, c] = jnp.max(s, axis=0, keepdims=True)


def _update(vt_of, s_ref, cm_ref, m_ref, acc_ref, slot):
    for c in range(2):
        m_prev = m_ref[c]
        m_new = jnp.maximum(m_prev, cm_ref[slot, c])
        alpha = jnp.exp2(m_prev - m_new)
        p = jnp.exp2(s_ref[slot, c] - m_new).astype(BF16)
        acc_ref[c] = alpha * acc_ref[c] + jnp.dot(vt_of(c), p, preferred_element_type=F32)
        m_ref[c] = m_new


def _init_state(m_ref, acc_ref):
    m_ref[...] = jnp.full(m_ref.shape, NEG_BIG, F32)
    acc_ref[...] = jnp.zeros(acc_ref.shape, F32)


def _pipelined(n, scores, update, unroll):
    scores(0, 0)

    def body(jj, carry):
        j = unroll * jj
        for u in range(unroll):
            scores(j + u + 1, (u + 1) % 2)
            update(j + u, u % 2)
        return carry

    looped = (n - 1) // unroll
    lax.fori_loop(0, looped, body, 0)
    for j in range(unroll * looped, n):
        if j + 1 < n:
            scores(j + 1, (j + 1) % 2)
        update(j, j % 2)


def _pipelined_list(n, chunk, scores, update):
    scores(chunk(0), 0)

    def body(unroll, first):
        def run(jj, carry):
            t = first + unroll * jj
            for u in range(unroll):
                scores(chunk(t + u + 1), (u + 1) % 2)
                update(chunk(t + u), u % 2)
            return carry
        return run

    quads = (n - 1) // 4
    lax.fori_loop(0, quads, body(4, 0), 0)
    pairs = (n - 1 - 4 * quads) // 2
    lax.fori_loop(0, pairs, body(2, 4 * quads), 0)
    left = n - 4 * quads - 2 * pairs

    @pl.when(left == 2)
    def _():
        scores(chunk(n - 1), 1)
        update(chunk(n - 2), 0)
        update(chunk(n - 1), 1)

    @pl.when(left == 1)
    def _():
        update(chunk(n - 1), 0)


def _normalized(acc_ref, c):
    dv = acc_ref.shape[1] - ONES_ROWS
    return acc_ref[c, :dv, :] / acc_ref[c, dv:dv + 1, :]


def _attn_scratch(tq, tk, dv):
    return [pltpu.VMEM((2, 1, tq), F32), pltpu.VMEM((2, dv + ONES_ROWS, tq), F32),
            pltpu.VMEM((2, 2, tk, tq), F32), pltpu.VMEM((2, 2, 1, tq), F32)]


SKIP_MARGIN = 128.0


def _block_norms_kernel(qk_ref, sel_ref, o_ref):
    x = qk_ref[...].astype(F32)
    lane = lax.broadcasted_iota(jnp.int32, x.shape, 1)
    x = jnp.where(lane < x.shape[1] // 2, (x * (A_DH ** -0.5 * LOG2E)).astype(BF16).astype(F32), x)
    sums = jnp.dot(x * x, sel_ref[...], preferred_element_type=F32)
    o_ref[...] = jnp.sqrt(jnp.max(sums, axis=0, keepdims=True))


def _block_norms(proj3, t, col):
    b, s, _ = proj3.shape
    w = 2 * A_HEADS * 2 * A_DH
    sel = (jnp.arange(w)[:, None] // A_DH == jnp.arange(LANES)[None, :]).astype(BF16)
    return pl.pallas_call(
        _block_norms_kernel,
        grid=(b, s // t),
        in_specs=[pl.BlockSpec((None, t, w), lambda bb, i: (bb, i, col * LANES // w)),
                  pl.BlockSpec((w, LANES), lambda bb, i: (0, 0))],
        out_specs=pl.BlockSpec((None, None, 1, LANES), lambda bb, i: (bb, i, 0, 0)),
        out_shape=jax.ShapeDtypeStruct((b, s // t, 1, LANES), F32),
        compiler_params=_cp(("parallel", "parallel")),
        name="block_norms",
    )(proj3, sel)[:, :, 0, :]


def _a_chunk_lists(proj3, pos_f, slopes2, tq, tk):
    assert tq == tk
    b, s, _ = proj3.shape
    norms = _block_norms(proj3, tq, COL_A)[:, :, :4 * A_HEADS].reshape(b, s // tq, 2, A_HEADS, 2).max(axis=-1)
    qn = norms[:, :, 0].transpose(0, 2, 1)
    kn = norms[:, :, 1].transpose(0, 2, 1)
    qk = 1.01 * qn[:, :, :, None] * kn[:, :, None, :] + 1.0
    pq = pos_f.reshape(b, s // tq, tq)
    pk = pos_f.reshape(b, s // tk, tk)
    qlo, qhi = jnp.min(pq, -1)[:, :, None], jnp.max(pq, -1)[:, :, None]
    klo, khi = jnp.min(pk, -1)[:, None, :], jnp.max(pk, -1)[:, None, :]
    dmin = jnp.maximum(jnp.maximum(klo - qhi, qlo - khi), 0.0)
    dmax = jnp.maximum(khi - qlo, qhi - klo)
    sl = slopes2[None, :, None, None]
    lower = -qk - sl * dmax[:, None]
    upper = qk - sl * dmin[:, None]
    nk = s // tk
    j = jnp.arange(nk, dtype=jnp.int32)
    cover = jnp.max(jnp.min(jnp.abs(pq[:, :, :, None] - pk[:, :, None, :]), axis=-1), axis=-1)
    lower = jnp.where(j[:, None] == j[None, :], jnp.maximum(lower, -qk - sl * cover[:, None, :, None]), lower)
    first = jnp.argmax(lower, axis=-1).astype(jnp.int32)[..., None]
    keep = (upper >= jnp.max(lower, axis=-1, keepdims=True) - SKIP_MARGIN) | (j == first)
    lst = jnp.argsort(jnp.where(j == first, -1, jnp.where(keep, j, nk + j)), axis=-1).astype(jnp.int32)
    cnt = jnp.sum(keep.astype(jnp.int32), axis=-1)
    return lst.reshape(-1), cnt.reshape(-1)


def _attn_a_kernel(lst_ref, cnt_ref, lam_ref, slope_ref, q_ref, k_ref, vt_ref, pq_ref, pk_ref, g_ref, o_ref,
                   qs_ref, m_ref, acc_ref, s_ref, cm_ref, *, tk, post_scale):
    tq = q_ref.shape[0]
    nk = k_ref.shape[0] // tk
    blk = (pl.program_id(0) * pl.num_programs(1) + pl.program_id(1)) * pl.num_programs(2) + pl.program_id(2)
    sl2 = slope_ref[pl.program_id(1)]
    _split_q(q_ref, qs_ref, A_DH ** -0.5 * LOG2E)
    _init_state(m_ref, acc_ref)
    pq = pq_ref[...] * sl2

    def scores(j, slot):
        ks = pl.multiple_of(j * tk, tk)
        k = k_ref[pl.ds(ks, tk), :]
        pk = pk_ref[pl.ds(ks, tk), :] * sl2
        bias = jnp.abs(jnp.concatenate([pk] * (tq // LANES), axis=1) - pq)
        for c in range(2):
            _put_scores(_kq(k, qs_ref[c]) - bias, s_ref, cm_ref, slot, c)

    def update(j, slot):
        ks = pl.multiple_of(j * tk, tk)
        _update(lambda c: vt_ref[:, pl.ds(ks, tk)], s_ref, cm_ref, m_ref, acc_ref, slot)

    _pipelined_list(cnt_ref[blk], lambda t: lst_ref[blk * nk + t], scores, update)
    o = (_normalized(acc_ref, 0) - lam_ref[0] * _normalized(acc_ref, 1)).T
    o_ref[...] = (_rms(o, g_ref[...], 1.0 / LANES) * post_scale).astype(o_ref.dtype)


def _attn_a(proj, vt, posq_row, posk_rep, lam, slopes2, g, post_scale, tq, tk):
    b, s, _ = proj.shape
    lst, cnt = _a_chunk_lists(proj, posq_row.reshape(b, s), slopes2, tq, tk)
    smem = pl.BlockSpec(memory_space=pltpu.SMEM)
    grid_spec = pltpu.PrefetchScalarGridSpec(
        num_scalar_prefetch=2,
        grid=(b, A_HEADS, s // tq),
        in_specs=[smem, smem,
                  pl.BlockSpec((None, tq, LANES), lambda bb, h, i, *_: (bb, i, COL_A + h)),
                  pl.BlockSpec((None, s, LANES), lambda bb, h, i, *_: (bb, 0, COL_A + A_HEADS + h)),
                  pl.BlockSpec((vt.shape[0] // A_HEADS, s), lambda bb, h, i, *_: (h, bb)),
                  pl.BlockSpec((None, 1, tq), lambda bb, h, i, *_: (bb, 0, i)),
                  pl.BlockSpec((None, s, LANES), lambda bb, h, i, *_: (bb, 0, 0)),
                  pl.BlockSpec((1, LANES), lambda bb, h, i, *_: (0, 0))],
        out_specs=pl.BlockSpec((None, tq, LANES), lambda bb, h, i, *_: (bb, i, h)),
        scratch_shapes=[pltpu.VMEM((2, tq, LANES), BF16)] + _attn_scratch(tq, tk, LANES),
    )
    return pl.pallas_call(
        functools.partial(_attn_a_kernel, tk=tk, post_scale=post_scale),
        grid_spec=grid_spec,
        out_shape=jax.ShapeDtypeStruct((b, s, A_HEADS * LANES), BF16),
        compiler_params=_cp(("parallel", "parallel", "arbitrary")),
        name="attn_diff",
    )(lst, cnt, lam, slopes2, proj, proj, vt, posq_row, posk_rep, g.reshape(1, LANES))


def _b_chunk_lists(proj3, t, nband):
    b, s, _ = proj3.shape
    nq = s // t
    half = nband // 2
    norms = _block_norms(proj3, t, COL_B)[:, :, :2 * B_HEADS].reshape(b, nq, 2, B_HEADS)
    jj = jnp.arange(nband, dtype=jnp.int32)
    kb = jnp.arange(nq, dtype=jnp.int32)[:, None] + jj[None, :] - half
    inside = (kb >= 0) & (kb < nq)
    qk = 1.01 * norms[:, :, None, 0, :] * norms[:, jnp.clip(kb, 0, nq - 1), 1, :] + 1.0
    dmin = jnp.where(jj == half, 0, (jnp.abs(jj - half) - 1) * t + 1).astype(F32)
    slopes2 = jnp.exp2(-8.0 * jnp.arange(1, B_HEADS + 1, dtype=F32) / B_HEADS) * LOG2E
    upper = qk - slopes2 * dmin[:, None] + math.log2(len(B_PATTERNS))
    keep = upper >= -qk[:, :, half:half + 1, :] - SKIP_MARGIN
    keep = jnp.any(keep.reshape(b, nq, nband, B_HEADS // 2, 2), axis=-1).transpose(0, 3, 1, 2)
    keep = (keep & inside) | (jj == half)
    lst = jnp.argsort(jnp.where(jj == half, -1, jnp.where(keep, jj, nband + jj)), axis=-1).astype(jnp.int32)
    return lst.reshape(-1), jnp.sum(keep.astype(jnp.int32), axis=-1).reshape(-1)


def _attn_b_kernel(lst_ref, cnt_ref, q_ref, k_ref, vt_ref, bias_ref, o_ref, qs_ref, m_ref, acc_ref, s_ref, cm_ref,
                   *, nband):
    t = q_ref.shape[0]
    rows = vt_ref.shape[0] // 2
    i = pl.program_id(2)
    blk = (pl.program_id(0) * pl.num_programs(1) + pl.program_id(1)) * pl.num_programs(2) + i
    _split_q(q_ref, qs_ref, B_DH ** -0.5 * LOG2E)
    _init_state(m_ref, acc_ref)

    def start(jj):
        return pl.multiple_of((i + jj - nband // 2) * t, t)

    def scores(jj, slot):
        k = k_ref[pl.ds(start(jj), t), :]
        for c in range(2):
            _put_scores(_kq(k, qs_ref[c]) + bias_ref[c, jj], s_ref, cm_ref, slot, c)

    def update(jj, slot):
        ks = start(jj)
        _update(lambda c: vt_ref[c * rows:(c + 1) * rows, pl.ds(ks, t)], s_ref, cm_ref, m_ref, acc_ref, slot)

    _pipelined_list(cnt_ref[blk], lambda n: lst_ref[blk * nband + n], scores, update)
    o = jnp.concatenate([_normalized(acc_ref, 0), _normalized(acc_ref, 1)], axis=0)
    o_ref[...] = o.T.astype(o_ref.dtype)


def _b_bias_tables(t, nband):
    half = nband // 2
    r = jnp.arange(t, dtype=jnp.int32)[None, :, None]
    c = jnp.arange(t, dtype=jnp.int32)[None, None, :]
    jj = jnp.arange(nband, dtype=jnp.int32)[:, None, None]
    ao = jnp.abs((jj - half) * t + r - c)
    mult = jnp.zeros(ao.shape, jnp.int32)
    for window, dilation in B_PATTERNS:
        reach = (window // (2 * dilation)) * dilation
        mult = mult + ((ao % dilation == 0) & (ao <= reach)).astype(jnp.int32)
    slopes = jnp.exp2(-8.0 * jnp.arange(1, B_HEADS + 1, dtype=F32) / B_HEADS) * LOG2E
    bias = jnp.log2(jnp.maximum(mult, 1).astype(F32))[None] - slopes[:, None, None, None] * ao.astype(F32)[None]
    bias = jnp.where((mult > 0)[None], bias, NEG_BIG)
    return bias.reshape(B_HEADS // 2, 2, nband, t, t)


def _attn_b(proj, vt, bias, t):
    b, s, _ = proj.shape
    nband = bias.shape[2]
    npair = B_HEADS // 2
    lst, cnt = _b_chunk_lists(proj, t, nband)
    grid_spec = pltpu.PrefetchScalarGridSpec(
        num_scalar_prefetch=2,
        grid=(b, npair, s // t),
        in_specs=[pl.BlockSpec((None, t, LANES), lambda bb, p, i, *_: (bb, i, COL_B + p)),
                  pl.BlockSpec((None, s, LANES), lambda bb, p, i, *_: (bb, 0, COL_B + npair + p)),
                  pl.BlockSpec((vt.shape[0] // npair, s), lambda bb, p, i, *_: (p, bb)),
                  pl.BlockSpec((None,) + bias.shape[1:], lambda bb, p, i, *_: (p, 0, 0, 0, 0))],
        out_specs=pl.BlockSpec((None, t, LANES), lambda bb, p, i, *_: (bb, i, p)),
        scratch_shapes=[pltpu.VMEM((2, t, LANES), BF16)] + _attn_scratch(t, t, B_DH),
    )
    return pl.pallas_call(
        functools.partial(_attn_b_kernel, nband=nband),
        grid_spec=grid_spec,
        out_shape=jax.ShapeDtypeStruct((b, s, npair * LANES), BF16),
        compiler_params=_cp(("parallel", "parallel", "arbitrary")),
        name="attn_dilated",
    )(lst, cnt, proj, proj, vt, bias)


def _rope_table_kernel(pos_ref, invf_ref, c_ref, s_ref):
    ang = pos_ref[...] * invf_ref[...]
    c_ref[...] = jnp.cos(ang)
    s_ref[...] = jnp.sin(ang)


def _rope_tables(pos_col, tm):
    t = pos_col.shape[0]
    inv = ROPE_THETA ** (-jnp.arange(0, C_ROPE, 2, dtype=F32) / C_ROPE)
    invf = jnp.concatenate([jnp.zeros((C_NOPE,), F32), inv, inv,
                            jnp.zeros((LANES - C_NOPE - C_ROPE,), F32)]).reshape(1, LANES)
    spec = pl.BlockSpec((tm, LANES), lambda i: (i, 0))
    return pl.pallas_call(
        _rope_table_kernel,
        grid=(t // tm,),
        in_specs=[pl.BlockSpec((tm, 1), lambda i: (i, 0)), pl.BlockSpec((1, LANES), lambda i: (0, 0))],
        out_specs=[spec, spec],
        out_shape=[jax.ShapeDtypeStruct((t, LANES), F32)] * 2,
        compiler_params=_cp(("parallel",)),
        name="rope_tables",
    )(pos_col, invf)


def _mla_prep_kernel(cq_ref, ckv_ref, r1_ref, r2_ref, c_ref, s_ref, gq_ref, gkv_ref,
                     wq1_ref, wq2_ref, wk_ref, wvt_ref, onv_ref, q_out, k_out, vt_out):
    qn = _rms(cq_ref[...].astype(F32), gq_ref[...], 1.0 / C_Q_RANK).astype(BF16)
    kvn = _rms(ckv_ref[...].astype(F32), gkv_ref[...], 1.0 / C_KV_RANK).astype(BF16)
    q1 = jnp.dot(qn, wq1_ref[...], preferred_element_type=F32)
    q2 = jnp.dot(qn, wq2_ref[...], preferred_element_type=F32)
    k1 = jnp.dot(kvn, wk_ref[...], preferred_element_type=F32)
    vt_out[...] = (_kq(wvt_ref[...], kvn) + onv_ref[...]).astype(vt_out.dtype)
    cos = c_ref[...]
    sin = s_ref[...]
    k_rope = r1_ref[...].astype(F32) * cos + r2_ref[...].astype(F32) * sin
    scale = (C_NOPE + C_ROPE) ** -0.5 * LOG2E
    for h in range(C_HEADS):
        sl = slice(h * LANES, (h + 1) * LANES)
        q_out[:, sl] = ((q1[:, sl] * cos + q2[:, sl] * sin) * scale).astype(q_out.dtype)
        k_out[:, sl] = (k1[:, sl] + k_rope).astype(k_out.dtype)


def _mla_prep(proj2, cos, sin, gq, gkv, wq1, wq2, wk, wvt, onv, tm):
    t = proj2.shape[0]
    full = lambda a: pl.BlockSpec(a.shape, lambda i: (0, 0))
    row = lambda w: pl.BlockSpec((tm, w), lambda i: (i, 0))
    return pl.pallas_call(
        _mla_prep_kernel,
        grid=(t // tm,),
        in_specs=[pl.BlockSpec((tm, 4 * LANES), lambda i: (i, COL_CQ // 4)),
                  pl.BlockSpec((tm, 2 * LANES), lambda i: (i, COL_CKV // 2)),
                  pl.BlockSpec((tm, LANES), lambda i: (i, COL_R1)),
                  pl.BlockSpec((tm, LANES), lambda i: (i, COL_R2)),
                  row(LANES), row(LANES), full(gq), full(gkv), full(wq1), full(wq2), full(wk), full(wvt), full(onv)],
        out_specs=[row(C_HEADS * LANES), row(C_HEADS * LANES), pl.BlockSpec((wvt.shape[0], tm), lambda i: (0, i))],
        out_shape=[jax.ShapeDtypeStruct((t, C_HEADS * LANES), BF16),
                   jax.ShapeDtypeStruct((t, C_HEADS * LANES), BF16),
                   jax.ShapeDtypeStruct((wvt.shape[0], t), BF16)],
        compiler_params=_cp(("parallel",)),
        name="mla_prep",
    )(proj2, proj2, proj2, proj2, cos, sin, gq, gkv, wq1, wq2, wk, wvt, onv)


def _attn_c_kernel(q_ref, k_ref, vt_ref, o_ref, m_ref, acc_ref, s_ref, cm_ref, *, tk):
    rows = vt_ref.shape[0] // 2
    _init_state(m_ref, acc_ref)

    def scores(j, slot):
        ks = pl.multiple_of(j * tk, tk)
        for c in range(2):
            sl = slice(c * LANES, (c + 1) * LANES)
            _put_scores(_kq(k_ref[pl.ds(ks, tk), sl], q_ref[:, sl]), s_ref, cm_ref, slot, c)

    def update(j, slot):
        ks = pl.multiple_of(j * tk, tk)
        _update(lambda c: vt_ref[c * rows:(c + 1) * rows, pl.ds(ks, tk)], s_ref, cm_ref, m_ref, acc_ref, slot)

    _pipelined(k_ref.shape[0] // tk, scores, update, unroll=4)
    o = jnp.concatenate([_normalized(acc_ref, 0), _normalized(acc_ref, 1)], axis=0)
    o_ref[...] = o.T.astype(o_ref.dtype)


def _attn_c(q, k, vt, tq, tk):
    b, s, _ = q.shape
    npair = C_HEADS // 2
    return pl.pallas_call(
        functools.partial(_attn_c_kernel, tk=tk),
        grid=(b, npair, s // tq),
        in_specs=[pl.BlockSpec((None, tq, 2 * LANES), lambda bb, p, i: (bb, i, p)),
                  pl.BlockSpec((None, s, 2 * LANES), lambda bb, p, i: (bb, 0, p)),
                  pl.BlockSpec((vt.shape[0] // npair, s), lambda bb, p, i: (p, bb))],
        out_specs=pl.BlockSpec((None, tq, LANES), lambda bb, p, i: (bb, i, p)),
        out_shape=jax.ShapeDtypeStruct((b, s, npair * LANES), BF16),
        scratch_shapes=_attn_scratch(tq, tk, C_DV),
        compiler_params=_cp(("parallel", "parallel", "arbitrary")),
        name="attn_latent",
    )(q, k, vt)


def _merge_kernel(oa_ref, ob_ref, oc_ref, g0_ref, g1_ref, g2_ref, x_ref, wb_ref, wo_ref, o_ref):
    z = None
    for n, (o_r, g_r) in enumerate(((oa_ref, g0_ref), (ob_ref, g1_ref), (oc_ref, g2_ref))):
        br = jnp.dot(o_r[...], wb_ref[n], preferred_element_type=F32)
        gate = 1.0 / (1.0 + jnp.exp(-g_r[...].astype(F32)))
        z = gate * br if z is None else z + gate * br
    o_ref[...] = x_ref[...] + jnp.dot(z.astype(BF16), wo_ref[...], preferred_element_type=F32)


def _merge(oa, ob, oc, proj2, x2, wb, wo, tm):
    t, d = x2.shape
    bw = oa.shape[1]
    row = lambda w: pl.BlockSpec((tm, w), lambda i: (i, 0))
    gate = lambda n: pl.BlockSpec((tm, d), lambda i: (i, COL_G * LANES // d + n))
    return pl.pallas_call(
        _merge_kernel,
        grid=(t // tm,),
        in_specs=[row(bw), row(bw), row(bw), gate(0), gate(1), gate(2), row(d),
                  pl.BlockSpec(wb.shape, lambda i: (0, 0, 0)), pl.BlockSpec(wo.shape, lambda i: (0, 0))],
        out_specs=row(d),
        out_shape=jax.ShapeDtypeStruct((t, d), F32),
        compiler_params=_cp(("parallel",)),
        name="branch_merge",
    )(oa, ob, oc, proj2, proj2, proj2, x2, wb, wo)


def _cross_kernel(x_ref, g_ref, wq_ref, kbd_ref, vbd_ref, wo_ref, o_ref, *, n_mem):
    x = x_ref[...]
    h = _rms(x, g_ref[...], 1.0 / x.shape[-1]).astype(BF16)
    q = (jnp.dot(h, wq_ref[...], preferred_element_type=F32) * (X_DH ** -0.5 * LOG2E)).astype(BF16)
    s = jnp.dot(q, kbd_ref[...], preferred_element_type=F32)
    ps = []
    for hh in range(X_HEADS):
        sh = s[:, hh * n_mem:(hh + 1) * n_mem]
        p = jnp.exp2(sh - jnp.max(sh, axis=-1, keepdims=True))
        ps.append((p / jnp.sum(p, axis=-1, keepdims=True)).astype(BF16))
    o = jnp.dot(jnp.concatenate(ps, axis=1), vbd_ref[...], preferred_element_type=F32)
    o_ref[...] = x + jnp.dot(o.astype(BF16), wo_ref[...], preferred_element_type=F32)


def _cross(x3, g, wq, kbd, vbd, wo, tm):
    b, s, d = x3.shape
    n_mem = kbd.shape[2] // X_HEADS
    full = lambda a: pl.BlockSpec(a.shape, lambda bb, i: (0, 0))
    return pl.pallas_call(
        functools.partial(_cross_kernel, n_mem=n_mem),
        grid=(b, s // tm),
        in_specs=[pl.BlockSpec((None, tm, d), lambda bb, i: (bb, i, 0)), full(g), full(wq),
                  pl.BlockSpec((None,) + kbd.shape[1:], lambda bb, i: (bb, 0, 0)),
                  pl.BlockSpec((None,) + vbd.shape[1:], lambda bb, i: (bb, 0, 0)), full(wo)],
        out_specs=pl.BlockSpec((None, tm, d), lambda bb, i: (bb, i, 0)),
        out_shape=jax.ShapeDtypeStruct((b, s, d), F32),
        compiler_params=_cp(("parallel", "parallel")),
        name="cross_attn",
    )(x3, g, wq, kbd, vbd, wo)


def _block_diag_kv(kv):
    b, m, _ = kv.shape
    kv = kv.reshape(b, m, 2, X_HEADS, X_DH)
    eye = jnp.eye(X_HEADS, dtype=kv.dtype)
    kt = kv[:, :, 0].transpose(0, 2, 3, 1)
    kbd = (kt[:, :, :, None, :] * eye[None, :, None, :, None]).reshape(b, X_HEADS * X_DH, X_HEADS * m)
    vt = kv[:, :, 1].transpose(0, 2, 1, 3)
    vbd = (vt[:, :, :, None, :] * eye[None, :, None, :, None]).reshape(b, X_HEADS * m, X_HEADS * X_DH)
    return kbd, vbd


def _router_kernel(x_ref, g_ref, w_ref, b_ref, h_out, r_out):
    x = x_ref[...]
    h = _rms(x, g_ref[...], 1.0 / x.shape[-1])
    h_out[...] = h.astype(h_out.dtype)
    logits = jnp.dot(h, w_ref[...], preferred_element_type=F32, precision=lax.Precision.HIGHEST) + b_ref[...]
    lane = lax.broadcasted_iota(jnp.int32, logits.shape, 1)
    lane_f = lane.astype(F32)
    big = jnp.float32(4 * LANES)

    def top(vals, mask):
        mv = jnp.max(jnp.where(mask, vals, -jnp.inf), axis=-1, keepdims=True)
        idx = jnp.min(jnp.where(mask & (vals == mv), lane_f, big), axis=-1, keepdims=True)
        return mv, idx

    g_mask = lane < N_GROUPS
    g_max, g_idx = top(logits, g_mask)
    p_g = 1.0 / jnp.sum(jnp.where(g_mask, jnp.exp(logits - g_max), 0.0), axis=-1, keepdims=True)
    first = N_GROUPS + g_idx * EXPERTS_PER_GROUP
    e_mask = (lane_f >= first) & (lane_f < first + EXPERTS_PER_GROUP)
    v0, i0 = top(logits, e_mask)
    v1, i1 = top(logits, e_mask & (lane_f != i0))
    e1 = jnp.exp(v1 - v0)
    w0 = p_g / (1.0 + e1)
    w1 = p_g * e1 / (1.0 + e1)
    out = jnp.where(lane == 0, i0 - N_GROUPS, 0.0)
    out = jnp.where(lane == 1, i1 - N_GROUPS, out)
    out = jnp.where(lane == 2, w0, out)
    out = jnp.where(lane == 3, w1, out)
    r_out[...] = out


def _router(x2, g, w, bias, tm):
    t, d = x2.shape
    full = lambda a: pl.BlockSpec(a.shape, lambda i: (0, 0))
    return pl.pallas_call(
        _router_kernel,
        grid=(t // tm,),
        in_specs=[pl.BlockSpec((tm, d), lambda i: (i, 0)), full(g), full(w), full(bias)],
        out_specs=[pl.BlockSpec((tm, d), lambda i: (i, 0)), pl.BlockSpec((tm, LANES), lambda i: (i, 0))],
        out_shape=[jax.ShapeDtypeStruct((t, d), BF16), jax.ShapeDtypeStruct((t, LANES), F32)],
        compiler_params=_cp(("parallel",)),
        name="moe_router",
    )(x2, g, w, bias)


def _expert_kernel(blk_e_ref, n_used_ref, x_ref, w1_ref, w3_ref, w2_ref, o_ref, w13_s, w2_s):
    i = pl.program_id(0)
    used = i < n_used_ref[0]
    de = w1_ref.shape[1]

    @pl.when(used & ((i == 0) | (blk_e_ref[i] != blk_e_ref[jnp.maximum(i - 1, 0)])))
    def _():
        w13_s[:, :de] = w1_ref[...].astype(BF16)
        w13_s[:, de:] = w3_ref[...].astype(BF16)
        w2_s[...] = w2_ref[...].astype(BF16)

    @pl.when(used)
    def _():
        hid = jnp.dot(x_ref[...], w13_s[...], preferred_element_type=F32)
        a = hid[:, :de]
        act = (a / (1.0 + jnp.exp(-a))) * hid[:, de:]
        o_ref[...] = jnp.dot(act.astype(BF16), w2_s[...], preferred_element_type=F32).astype(o_ref.dtype)

    @pl.when(jnp.logical_not(used))
    def _():
        o_ref[...] = jnp.zeros(o_ref.shape, o_ref.dtype)


def _experts(blk_e, n_used, xr, w1, w3, w2, layer, rows_per_block):
    rows, d = xr.shape
    de = w1.shape[3]
    weight = lambda w: pl.BlockSpec((None, None) + w.shape[2:], lambda i, be, nu: (layer, be[i], 0, 0))
    grid_spec = pltpu.PrefetchScalarGridSpec(
        num_scalar_prefetch=2,
        grid=(rows // rows_per_block,),
        in_specs=[pl.BlockSpec((rows_per_block, d), lambda i, be, nu: (i, 0)), weight(w1), weight(w3), weight(w2)],
        out_specs=pl.BlockSpec((rows_per_block, d), lambda i, be, nu: (i, 0)),
        scratch_shapes=[pltpu.VMEM((d, 2 * de), BF16), pltpu.VMEM((de, d), BF16)],
    )
    return pl.pallas_call(
        _expert_kernel,
        grid_spec=grid_spec,
        out_shape=jax.ShapeDtypeStruct((rows, d), BF16),
        compiler_params=_cp(("arbitrary",)),
        name="moe_experts",
    )(blk_e, n_used, xr, w1, w3, w2)


def _combine_kernel(x_ref, y0_ref, y1_ref, r_ref, g_ref, o_ref, *, final_norm):
    r = r_ref[...]
    y = x_ref[...] + r[:, 2:3] * y0_ref[...].astype(F32) + r[:, 3:4] * y1_ref[...].astype(F32)
    if final_norm:
        y = _rms(y, g_ref[...], 1.0 / y.shape[-1])
    o_ref[...] = y


def _combine(x2, y0, y1, route, g, final_norm, tm):
    t, d = x2.shape
    row = lambda w: pl.BlockSpec((tm, w), lambda i: (i, 0))
    return pl.pallas_call(
        functools.partial(_combine_kernel, final_norm=final_norm),
        grid=(t // tm,),
        in_specs=[row(d), row(d), row(d), row(LANES), pl.BlockSpec((1, d), lambda i: (0, 0))],
        out_specs=row(d),
        out_shape=jax.ShapeDtypeStruct((t, d), F32),
        compiler_params=_cp(("parallel",)),
        name="moe_combine",
    )(x2, y0, y1, route, g)


def _dispatch(route, rows_per_block):
    t = route.shape[0]
    eid = route[:, :TOP_K].astype(jnp.int32).reshape(-1)
    n = eid.shape[0]
    order = jnp.argsort(eid).astype(jnp.int32)
    rank = jnp.argsort(order).astype(jnp.int32)
    experts = jnp.arange(N_EXPERTS, dtype=jnp.int32)
    counts = jnp.sum((eid[:, None] == experts[None, :]).astype(jnp.int32), axis=0)
    start = jnp.cumsum(counts) - counts
    padded = (counts + rows_per_block - 1) // rows_per_block * rows_per_block
    pend = jnp.cumsum(padded)
    pstart = pend - padded
    dest = (rank + (pstart - start)[eid]).reshape(t, TOP_K)
    n_blocks = n // rows_per_block + N_EXPERTS
    blk_first = jnp.arange(n_blocks, dtype=jnp.int32) * rows_per_block
    blk_e = jnp.minimum(jnp.sum((pend[None, :] <= blk_first[:, None]).astype(jnp.int32), axis=1), N_EXPERTS - 1)
    off = (blk_first - pstart[blk_e])[:, None] + jnp.arange(rows_per_block, dtype=jnp.int32)[None, :]
    src = jnp.clip(start[blk_e][:, None] + off, 0, n - 1)
    row_tok = jnp.where(off < counts[blk_e][:, None], order[src] // TOP_K, 0).reshape(-1)
    n_used = (pend[-1] // rows_per_block).astype(jnp.int32).reshape(1)
    return row_tok, dest, blk_e, n_used


def _rot_cols(w):
    half = w.shape[-1] // 2
    return jnp.concatenate([-w[..., half:], w[..., :half]], axis=-1)


def _pack_w_in(w):
    d = w.shape[0]
    blk = A_HEADS * 2 * A_DH
    n_ab = 6 * blk
    cq = w[:, n_ab:n_ab + C_Q_RANK]
    ckv = w[:, n_ab + C_Q_RANK:n_ab + C_Q_RANK + C_KV_RANK]
    ckr = w[:, n_ab + C_Q_RANK + C_KV_RANK:n_ab + C_Q_RANK + C_KV_RANK + C_ROPE]
    gates = w[:, n_ab + C_Q_RANK + C_KV_RANK + C_ROPE:]
    z = lambda n: jnp.zeros((d, n), w.dtype)
    tail = LANES - C_NOPE - C_ROPE
    packed = jnp.concatenate([w[:, :2 * blk], w[:, 3 * blk:5 * blk], cq, z(LANES), ckv,
                              z(C_NOPE), ckr, z(tail), z(C_NOPE), _rot_cols(ckr), z(tail), gates], axis=1)
    return packed.astype(BF16), w[:, 2 * blk:3 * blk], w[:, 5 * blk:6 * blk]


def _pack_w_uq(w):
    wq = w.reshape(C_Q_RANK, C_HEADS, C_NOPE + C_ROPE)
    pad_rows = 4 * LANES - C_Q_RANK
    tail = LANES - C_NOPE - C_ROPE
    q1 = jnp.pad(wq, ((0, pad_rows), (0, 0), (0, tail))).reshape(4 * LANES, C_HEADS * LANES)
    q2 = jnp.pad(_rot_cols(wq[:, :, C_NOPE:]), ((0, pad_rows), (0, 0), (C_NOPE, tail))).reshape(4 * LANES, C_HEADS * LANES)
    return q1.astype(BF16), q2.astype(BF16)


def _pack_w_ukv(w):
    wkv = w.reshape(C_KV_RANK, C_HEADS, C_NOPE + C_DV)
    wk = jnp.pad(wkv[:, :, :C_NOPE], ((0, 0), (0, 0), (0, LANES - C_NOPE))).reshape(C_KV_RANK, C_HEADS * LANES)
    wv = wkv[:, :, C_NOPE:].reshape(C_KV_RANK, C_HEADS * C_DV)
    return wk.astype(BF16), wv


def kernel(x, mem, positions, mix_norm_g, w_in, diff_lambda, diff_subln_g, mla_q_norm_g, w_uq, mla_kv_norm_g, w_ukv, w_branch, w_out, cross_norm_g, mem_norm_g, w_xq, w_xkv, w_xo, ffn_norm_g, w_group, b_group, w_router, b_router, w1, w3, w2, final_norm_g):
    b, s, d = x.shape
    depth = w_in.shape[0]
    t = b * s
    n_mem = mem.shape[1]
    tl = _tiles(s)
    assert PROJ_COLS == COL_G * LANES + N_BRANCHES * d and s % (2 * tl["t_b"]) == 0

    pos_f = positions.astype(F32)
    posq_row = pos_f.reshape(b, 1, s)
    posk_rep = jnp.broadcast_to(pos_f[:, :, None], (b, s, LANES))
    cos, sin = _rope_tables(pos_f.reshape(t, 1), tl["tm_tok"])
    slopes_a2 = jnp.exp2(-8.0 * jnp.arange(1, A_HEADS + 1, dtype=F32) / A_HEADS) * LOG2E
    reach = max((w // (2 * dl)) * dl for w, dl in B_PATTERNS)
    b_bias = _b_bias_tables(tl["t_b"], 2 * (-(-reach // tl["t_b"])) + 1)
    mem2 = mem.reshape(b * n_mem, d)

    x2 = x.reshape(t, d)
    for l in range(depth):
        w_main, w_av, w_bv = _pack_w_in(w_in[l])
        proj2, vt_a, vt_b = _in_proj(x2, mix_norm_g[l], w_main, *_vt_weights(w_av, 2 * A_DH), *_vt_weights(w_bv, B_DH),
                                     tl["tm_proj"], tl["tn_proj"])
        proj3 = proj2.reshape(b, s, PROJ_COLS)
        lq = diff_lambda[l].astype(F32)
        lam_init = 0.8 - 0.6 * math.exp(-0.3 * l)
        lam = (jnp.exp(jnp.sum(lq[0] * lq[1])) - jnp.exp(jnp.sum(lq[2] * lq[3])) + lam_init).reshape(1)
        oa = _attn_a(proj3, vt_a, posq_row, posk_rep, lam, slopes_a2, diff_subln_g[l], 1.0 - lam_init,
                     tl["tq_a"], tl["tk_a"])
        ob = _attn_b(proj3, vt_b, b_bias, tl["t_b"])
        wq1, wq2 = _pack_w_uq(w_uq[l])
        wk, wv = _pack_w_ukv(w_ukv[l])
        gq = jnp.pad(mla_q_norm_g[l], (0, 4 * LANES - C_Q_RANK)).reshape(1, 4 * LANES)
        qc, kc, vt_c = _mla_prep(proj2, cos, sin, gq, mla_kv_norm_g[l].reshape(1, C_KV_RANK),
                                 wq1, wq2, wk, *_vt_weights(wv, C_DV), tl["tm_tok"])
        oc = _attn_c(qc.reshape(b, s, -1), kc.reshape(b, s, -1), vt_c, tl["tq_c"], tl["tk_c"])
        x2 = _merge(oa.reshape(t, -1), ob.reshape(t, -1), oc.reshape(t, -1), proj2, x2,
                    w_branch[l].astype(BF16), w_out[l].astype(BF16), tl["tm_tok"])
        kv = _norm_matmul(mem2, mem_norm_g[l], w_xkv[l].astype(BF16), min(1024, b * n_mem), w_xkv.shape[2])
        kbd, vbd = _block_diag_kv(kv.reshape(b, n_mem, -1))
        x2 = _cross(x2.reshape(b, s, d), cross_norm_g[l].reshape(1, d), w_xq[l].astype(BF16), kbd, vbd,
                    w_xo[l].astype(BF16), tl["tm_tok"]).reshape(t, d)
        w_r = jnp.pad(jnp.concatenate([w_group[l], w_router[l]], axis=1), ((0, 0), (0, LANES - N_GROUPS - N_EXPERTS)))
        b_r = jnp.pad(jnp.concatenate([b_group[l], b_router[l]]), (0, LANES - N_GROUPS - N_EXPERTS)).reshape(1, LANES)
        h, route = _router(x2, ffn_norm_g[l].reshape(1, d), w_r, b_r, tl["tm_tok"])
        row_tok, dest, blk_e, n_used = _dispatch(route, tl["moe_rows"])
        yr = _experts(blk_e, n_used, h[row_tok], w1, w3, w2, l, tl["moe_rows"])
        x2 = _combine(x2, yr[dest[:, 0]], yr[dest[:, 1]], route, final_norm_g.reshape(1, d),
                      l == depth - 1, tl["tm_tok"])
    return x2.reshape(b, s, d)
```

```python
import functools
import math

import jax
import jax.numpy as jnp
from jax import lax
from jax.experimental import pallas as pl
from jax.experimental.pallas import tpu as pltpu

F32 = jnp.float32
BF16 = jnp.bfloat16

LANES = 128
NORM_EPS = 1e-6
LOG2E = math.log2(math.e)
NEG_BIG = -1e30

A_HEADS, A_DH = 4, 64
B_HEADS, B_DH = 8, 64
B_PATTERNS = ((128, 1), (512, 4), (2048, 16))
C_HEADS, C_Q_RANK, C_KV_RANK, C_NOPE, C_ROPE, C_DV = 8, 384, 256, 64, 32, 64
ROPE_THETA = 10000.0
N_BRANCHES = 3
X_HEADS, X_DH = 4, 64
N_GROUPS, EXPERTS_PER_GROUP, TOP_K = 4, 8, 2
N_EXPERTS = N_GROUPS * EXPERTS_PER_GROUP

COL_A = 0
COL_B = 8
COL_CQ = 16
COL_CKV = 20
COL_R1 = 22
COL_R2 = 23
COL_G = 24
PROJ_COLS = 48 * LANES

VMEM_LIMIT = 48 * 1024 * 1024


def _tiles(seq):
    return dict(
        tm_proj=min(1024, seq), tn_proj=1024,
        tm_tok=min(512, seq),
        tq_a=min(512, seq), tk_a=min(512, seq),
        t_b=min(512, seq),
        tq_c=min(512, seq), tk_c=min(512, seq),
        moe_rows=512,
    )


def _cp(sem):
    return pltpu.CompilerParams(dimension_semantics=sem, vmem_limit_bytes=VMEM_LIMIT)


def _rms(x, g, inv_n):
    ms = jnp.sum(x * x, axis=-1, keepdims=True) * inv_n
    return x * lax.rsqrt(ms + NORM_EPS) * g


def _norm_matmul_kernel(x_ref, g_ref, w_ref, o_ref, h_ref, *, inv_n):
    @pl.when(pl.program_id(1) == 0)
    def _():
        h_ref[...] = _rms(x_ref[...].astype(F32), g_ref[...], inv_n).astype(BF16)

    o_ref[...] = jnp.dot(h_ref[...], w_ref[...], preferred_element_type=F32).astype(o_ref.dtype)


def _norm_matmul(x, g, w, tm, tn, out_dtype=BF16):
    m, k = x.shape
    n = w.shape[1]
    return pl.pallas_call(
        functools.partial(_norm_matmul_kernel, inv_n=1.0 / k),
        grid=(m // tm, n // tn),
        in_specs=[pl.BlockSpec((tm, k), lambda i, j: (i, 0)),
                  pl.BlockSpec((1, k), lambda i, j: (0, 0)),
                  pl.BlockSpec((k, tn), lambda i, j: (0, j))],
        out_specs=pl.BlockSpec((tm, tn), lambda i, j: (i, j)),
        out_shape=jax.ShapeDtypeStruct((m, n), out_dtype),
        scratch_shapes=[pltpu.VMEM((tm, k), BF16)],
        compiler_params=_cp(("parallel", "arbitrary")),
        name="norm_matmul",
    )(x, g.reshape(1, k), w)


def _in_proj_kernel(x_ref, g_ref, w_ref, wta_ref, wtb_ref, ona_ref, onb_ref, o_ref, vta_ref, vtb_ref, h_ref, *, inv_n):
    @pl.when(pl.program_id(1) == 0)
    def _():
        h = _rms(x_ref[...].astype(F32), g_ref[...], inv_n).astype(BF16)
        h_ref[...] = h
        vta_ref[...] = (_kq(wta_ref[...], h) + ona_ref[...]).astype(vta_ref.dtype)
        vtb_ref[...] = (_kq(wtb_ref[...], h) + onb_ref[...]).astype(vtb_ref.dtype)

    o_ref[...] = jnp.dot(h_ref[...], w_ref[...], preferred_element_type=F32).astype(o_ref.dtype)


def _in_proj(x, g, w, wta, ona, wtb, onb, tm, tn):
    m, k = x.shape
    n = w.shape[1]
    full = lambda a: pl.BlockSpec(a.shape, lambda i, j: (0, 0))
    return pl.pallas_call(
        functools.partial(_in_proj_kernel, inv_n=1.0 / k),
        grid=(m // tm, n // tn),
        in_specs=[pl.BlockSpec((tm, k), lambda i, j: (i, 0)),
                  pl.BlockSpec((1, k), lambda i, j: (0, 0)),
                  pl.BlockSpec((k, tn), lambda i, j: (0, j)),
                  full(wta), full(wtb), full(ona), full(onb)],
        out_specs=[pl.BlockSpec((tm, tn), lambda i, j: (i, j)),
                   pl.BlockSpec((wta.shape[0], tm), lambda i, j: (0, i)),
                   pl.BlockSpec((wtb.shape[0], tm), lambda i, j: (0, i))],
        out_shape=[jax.ShapeDtypeStruct((m, n), BF16),
                   jax.ShapeDtypeStruct((wta.shape[0], m), BF16),
                   jax.ShapeDtypeStruct((wtb.shape[0], m), BF16)],
        scratch_shapes=[pltpu.VMEM((tm, k), BF16)],
        compiler_params=_cp(("parallel", "arbitrary")),
        name="in_proj",
    )(x, g.reshape(1, k), w, wta, wtb, ona, onb)


ONES_ROWS = 16


def _vt_weights(wv, dv):
    k, w = wv.shape
    wt = jnp.pad(wv.T.reshape(w // dv, dv, k), ((0, 0), (0, ONES_ROWS), (0, 0))).reshape(-1, k)
    ones = jnp.zeros((w // dv, dv + ONES_ROWS, 1), F32).at[:, dv, 0].set(1.0).reshape(-1, 1)
    return wt.astype(BF16), ones


def _split_q(q_ref, qs_ref, scale):
    q = q_ref[...].astype(F32) * scale
    lane = lax.broadcasted_iota(jnp.int32, q.shape, 1)
    qs_ref[0] = jnp.where(lane < LANES // 2, q, 0.0).astype(BF16)
    qs_ref[1] = jnp.where(lane >= LANES // 2, q, 0.0).astype(BF16)


def _kq(k, q):
    return lax.dot_general(k, q, (((1,), (1,)), ((), ())), preferred_element_type=F32)


def _put_scores(s, s_ref, cm_ref, slot, c):
    s_ref[slot, c] = s
    cm_ref[slot, c] = jnp.max(s, axis=0, keepdims=True)


def _update(vt_of, s_ref, cm_ref, m_ref, acc_ref, slot):
    for c in range(2):
        m_prev = m_ref[c]
        m_new = jnp.maximum(m_prev, cm_ref[slot, c])
        alpha = jnp.exp2(m_prev - m_new)
        p = jnp.exp2(s_ref[slot, c] - m_new).astype(BF16)
        acc_ref[c] = alpha * acc_ref[c] + jnp.dot(vt_of(c), p, preferred_element_type=F32)
        m_ref[c] = m_new


def _init_state(m_ref, acc_ref):
    m_ref[...] = jnp.full(m_ref.shape, NEG_BIG, F32)
    acc_ref[...] = jnp.zeros(acc_ref.shape, F32)


def _pipelined(n, scores, update, unroll):
    scores(0, 0)

    def body(jj, carry):
        j = unroll * jj
        for u in range(unroll):
            scores(j + u + 1, (u + 1) % 2)
            update(j + u, u % 2)
        return carry

    looped = (n - 1) // unroll
    lax.fori_loop(0, looped, body, 0)
    for j in range(unroll * looped, n):
        if j + 1 < n:
            scores(j + 1, (j + 1) % 2)
        update(j, j % 2)


def _pipelined_list(n, chunk, scores, update):
    scores(chunk(0), 0)

    def body(unroll, first):
        def run(jj, carry):
            t = first + unroll * jj
            for u in range(unroll):
                scores(chunk(t + u + 1), (u + 1) % 2)
                update(chunk(t + u), u % 2)
            return carry
        return run

    quads = (n - 1) // 4
    lax.fori_loop(0, quads, body(4, 0), 0)
    pairs = (n - 1 - 4 * quads) // 2
    lax.fori_loop(0, pairs, body(2, 4 * quads), 0)
    left = n - 4 * quads - 2 * pairs

    @pl.when(left == 2)
    def _():
        scores(chunk(n - 1), 1)
        update(chunk(n - 2), 0)
        update(chunk(n - 1), 1)

    @pl.when(left == 1)
    def _():
        update(chunk(n - 1), 0)


def _normalized(acc_ref, c):
    dv = acc_ref.shape[1] - ONES_ROWS
    return acc_ref[c, :dv, :] / acc_ref[c, dv:dv + 1, :]


def _attn_scratch(tq, tk, dv):
    return [pltpu.VMEM((2, 1, tq), F32), pltpu.VMEM((2, dv + ONES_ROWS, tq), F32),
            pltpu.VMEM((2, 2, tk, tq), F32), pltpu.VMEM((2, 2, 1, tq), F32)]


SKIP_MARGIN = 128.0


def _block_norms_kernel(qk_ref, sel_ref, o_ref):
    x = qk_ref[...].astype(F32)
    lane = lax.broadcasted_iota(jnp.int32, x.shape, 1)
    is_query = (lane // (A_HEADS * 2 * A_DH)) % 2 == 0
    x = jnp.where(is_query, (x * (A_DH ** -0.5 * LOG2E)).astype(BF16).astype(F32), x)
    sums = jnp.dot(x * x, sel_ref[...], preferred_element_type=F32)
    o_ref[...] = jnp.sqrt(jnp.max(sums, axis=0, keepdims=True))


def _block_norms(proj3, t):
    b, s, _ = proj3.shape
    assert COL_B == COL_A + 2 * A_HEADS and A_DH == B_DH
    w = 4 * A_HEADS * 2 * A_DH
    sel = (jnp.arange(w)[:, None] // A_DH == jnp.arange(LANES)[None, :]).astype(BF16)
    norms = pl.pallas_call(
        _block_norms_kernel,
        grid=(b, s // t),
        in_specs=[pl.BlockSpec((None, t, w), lambda bb, i: (bb, i, COL_A * LANES // w)),
                  pl.BlockSpec((w, LANES), lambda bb, i: (0, 0))],
        out_specs=pl.BlockSpec((None, None, 1, LANES), lambda bb, i: (bb, i, 0, 0)),
        out_shape=jax.ShapeDtypeStruct((b, s // t, 1, LANES), F32),
        compiler_params=_cp(("parallel", "parallel")),
        name="block_norms",
    )(proj3, sel)
    return norms[:, :, 0, :w // A_DH].reshape(b, s // t, 4, 2 * A_HEADS)


def _a_chunk_lists(norms, pos_f, slopes2, tq, tk):
    assert tq == tk
    b, s = pos_f.shape
    norms = norms[:, :, :2].reshape(b, s // tq, 2, A_HEADS, 2).max(axis=-1)
    qn = norms[:, :, 0].transpose(0, 2, 1)
    kn = norms[:, :, 1].transpose(0, 2, 1)
    qk = 1.01 * qn[:, :, :, None] * kn[:, :, None, :] + 1.0
    pq = pos_f.reshape(b, s // tq, tq)
    pk = pos_f.reshape(b, s // tk, tk)
    qlo, qhi = jnp.min(pq, -1)[:, :, None], jnp.max(pq, -1)[:, :, None]
    klo, khi = jnp.min(pk, -1)[:, None, :], jnp.max(pk, -1)[:, None, :]
    dmin = jnp.maximum(jnp.maximum(klo - qhi, qlo - khi), 0.0)
    dmax = jnp.maximum(khi - qlo, qhi - klo)
    sl = slopes2[None, :, None, None]
    lower = -qk - sl * dmax[:, None]
    upper = qk - sl * dmin[:, None]
    nk = s // tk
    j = jnp.arange(nk, dtype=jnp.int32)
    cover = jnp.max(jnp.min(jnp.abs(pq[:, :, :, None] - pk[:, :, None, :]), axis=-1), axis=-1)
    lower = jnp.where(j[:, None] == j[None, :], jnp.maximum(lower, -qk - sl * cover[:, None, :, None]), lower)
    first = jnp.argmax(lower, axis=-1).astype(jnp.int32)[..., None]
    keep = (upper >= jnp.max(lower, axis=-1, keepdims=True) - SKIP_MARGIN) | (j == first)
    lst = jnp.argsort(jnp.where(j == first, -1, jnp.where(keep, j, nk + j)), axis=-1).astype(jnp.int32)
    cnt = jnp.sum(keep.astype(jnp.int32), axis=-1)
    return lst.reshape(-1), cnt.reshape(-1)


def _attn_a_kernel(lst_ref, cnt_ref, lam_ref, slope_ref, q_ref, k_ref, vt_ref, pq_ref, pk_ref, g_ref, o_ref,
                   qs_ref, m_ref, acc_ref, s_ref, cm_ref, *, tk, post_scale):
    tq = q_ref.shape[0]
    nk = k_ref.shape[0] // tk
    blk = (pl.program_id(0) * pl.num_programs(1) + pl.program_id(1)) * pl.num_programs(2) + pl.program_id(2)
    sl2 = slope_ref[pl.program_id(1)]
    _split_q(q_ref, qs_ref, A_DH ** -0.5 * LOG2E)
    _init_state(m_ref, acc_ref)
    pq = pq_ref[...] * sl2

    def scores(j, slot):
        ks = pl.multiple_of(j * tk, tk)
        k = k_ref[pl.ds(ks, tk), :]
        pk = pk_ref[pl.ds(ks, tk), :] * sl2
        bias = jnp.abs(jnp.concatenate([pk] * (tq // LANES), axis=1) - pq)
        for c in range(2):
            _put_scores(_kq(k, qs_ref[c]) - bias, s_ref, cm_ref, slot, c)

    def update(j, slot):
        ks = pl.multiple_of(j * tk, tk)
        _update(lambda c: vt_ref[:, pl.ds(ks, tk)], s_ref, cm_ref, m_ref, acc_ref, slot)

    _pipelined_list(cnt_ref[blk], lambda t: lst_ref[blk * nk + t], scores, update)
    o = (_normalized(acc_ref, 0) - lam_ref[0] * _normalized(acc_ref, 1)).T
    o_ref[...] = (_rms(o, g_ref[...], 1.0 / LANES) * post_scale).astype(o_ref.dtype)


def _attn_a(proj, vt, norms, posq_row, posk_rep, lam, slopes2, g, post_scale, tq, tk):
    b, s, _ = proj.shape
    lst, cnt = _a_chunk_lists(norms, posq_row.reshape(b, s), slopes2, tq, tk)
    smem = pl.BlockSpec(memory_space=pltpu.SMEM)
    grid_spec = pltpu.PrefetchScalarGridSpec(
        num_scalar_prefetch=2,
        grid=(b, A_HEADS, s // tq),
        in_specs=[smem, smem,
                  pl.BlockSpec((None, tq, LANES), lambda bb, h, i, *_: (bb, i, COL_A + h)),
                  pl.BlockSpec((None, s, LANES), lambda bb, h, i, *_: (bb, 0, COL_A + A_HEADS + h)),
                  pl.BlockSpec((vt.shape[0] // A_HEADS, s), lambda bb, h, i, *_: (h, bb)),
                  pl.BlockSpec((None, 1, tq), lambda bb, h, i, *_: (bb, 0, i)),
                  pl.BlockSpec((None, s, LANES), lambda bb, h, i, *_: (bb, 0, 0)),
                  pl.BlockSpec((1, LANES), lambda bb, h, i, *_: (0, 0))],
        out_specs=pl.BlockSpec((None, tq, LANES), lambda bb, h, i, *_: (bb, i, h)),
        scratch_shapes=[pltpu.VMEM((2, tq, LANES), BF16)] + _attn_scratch(tq, tk, LANES),
    )
    return pl.pallas_call(
        functools.partial(_attn_a_kernel, tk=tk, post_scale=post_scale),
        grid_spec=grid_spec,
        out_shape=jax.ShapeDtypeStruct((b, s, A_HEADS * LANES), BF16),
        compiler_params=_cp(("parallel", "parallel", "arbitrary")),
        name="attn_diff",
    )(lst, cnt, lam, slopes2, proj, proj, vt, posq_row, posk_rep, g.reshape(1, LANES))


def _b_chunk_lists(norms, nband, t):
    b, nq = norms.shape[:2]
    half = nband // 2
    norms = norms[:, :, 2:]
    jj = jnp.arange(nband, dtype=jnp.int32)
    kb = jnp.arange(nq, dtype=jnp.int32)[:, None] + jj[None, :] - half
    inside = (kb >= 0) & (kb < nq)
    qk = 1.01 * norms[:, :, None, 0, :] * norms[:, jnp.clip(kb, 0, nq - 1), 1, :] + 1.0
    dmin = jnp.where(jj == half, 0, (jnp.abs(jj - half) - 1) * t + 1).astype(F32)
    slopes2 = jnp.exp2(-8.0 * jnp.arange(1, B_HEADS + 1, dtype=F32) / B_HEADS) * LOG2E
    upper = qk - slopes2 * dmin[:, None] + math.log2(len(B_PATTERNS))
    keep = upper >= -qk[:, :, half:half + 1, :] - SKIP_MARGIN
    keep = jnp.any(keep.reshape(b, nq, nband, B_HEADS // 2, 2), axis=-1).transpose(0, 3, 1, 2)
    keep = (keep & inside) | (jj == half)
    lst = jnp.argsort(jnp.where(jj == half, -1, jnp.where(keep, jj, nband + jj)), axis=-1).astype(jnp.int32)
    return lst.reshape(-1), jnp.sum(keep.astype(jnp.int32), axis=-1).reshape(-1)


def _attn_b_kernel(lst_ref, cnt_ref, q_ref, k_ref, vt_ref, bias_ref, o_ref, qs_ref, m_ref, acc_ref, s_ref, cm_ref,
                   *, nband):
    t = q_ref.shape[0]
    rows = vt_ref.shape[0] // 2
    i = pl.program_id(2)
    blk = (pl.program_id(0) * pl.num_programs(1) + pl.program_id(1)) * pl.num_programs(2) + i
    _split_q(q_ref, qs_ref, B_DH ** -0.5 * LOG2E)
    _init_state(m_ref, acc_ref)

    def start(jj):
        return pl.multiple_of((i + jj - nband // 2) * t, t)

    def scores(jj, slot):
        k = k_ref[pl.ds(start(jj), t), :]
        for c in range(2):
            _put_scores(_kq(k, qs_ref[c]) + bias_ref[c, jj], s_ref, cm_ref, slot, c)

    def update(jj, slot):
        ks = start(jj)
        _update(lambda c: vt_ref[c * rows:(c + 1) * rows, pl.ds(ks, t)], s_ref, cm_ref, m_ref, acc_ref, slot)

    _pipelined_list(cnt_ref[blk], lambda n: lst_ref[blk * nband + n], scores, update)
    o = jnp.concatenate([_normalized(acc_ref, 0), _normalized(acc_ref, 1)], axis=0)
    o_ref[...] = o.T.astype(o_ref.dtype)


def _b_bias_tables(t, nband):
    half = nband // 2
    r = jnp.arange(t, dtype=jnp.int32)[None, :, None]
    c = jnp.arange(t, dtype=jnp.int32)[None, None, :]
    jj = jnp.arange(nband, dtype=jnp.int32)[:, None, None]
    ao = jnp.abs((jj - half) * t + r - c)
    mult = jnp.zeros(ao.shape, jnp.int32)
    for window, dilation in B_PATTERNS:
        reach = (window // (2 * dilation)) * dilation
        mult = mult + ((ao % dilation == 0) & (ao <= reach)).astype(jnp.int32)
    slopes = jnp.exp2(-8.0 * jnp.arange(1, B_HEADS + 1, dtype=F32) / B_HEADS) * LOG2E
    bias = jnp.log2(jnp.maximum(mult, 1).astype(F32))[None] - slopes[:, None, None, None] * ao.astype(F32)[None]
    bias = jnp.where((mult > 0)[None], bias, NEG_BIG)
    return bias.reshape(B_HEADS // 2, 2, nband, t, t)


def _attn_b(proj, vt, norms, bias, t):
    b, s, _ = proj.shape
    nband = bias.shape[2]
    npair = B_HEADS // 2
    lst, cnt = _b_chunk_lists(norms, nband, t)
    grid_spec = pltpu.PrefetchScalarGridSpec(
        num_scalar_prefetch=2,
        grid=(b, npair, s // t),
        in_specs=[pl.BlockSpec((None, t, LANES), lambda bb, p, i, *_: (bb, i, COL_B + p)),
                  pl.BlockSpec((None, s, LANES), lambda bb, p, i, *_: (bb, 0, COL_B + npair + p)),
                  pl.BlockSpec((vt.shape[0] // npair, s), lambda bb, p, i, *_: (p, bb)),
                  pl.BlockSpec((None,) + bias.shape[1:], lambda bb, p, i, *_: (p, 0, 0, 0, 0))],
        out_specs=pl.BlockSpec((None, t, LANES), lambda bb, p, i, *_: (bb, i, p)),
        scratch_shapes=[pltpu.VMEM((2, t, LANES), BF16)] + _attn_scratch(t, t, B_DH),
    )
    return pl.pallas_call(
        functools.partial(_attn_b_kernel, nband=nband),
        grid_spec=grid_spec,
        out_shape=jax.ShapeDtypeStruct((b, s, npair * LANES), BF16),
        compiler_params=_cp(("parallel", "parallel", "arbitrary")),
        name="attn_dilated",
    )(lst, cnt, proj, proj, vt, bias)


def _rope_table_kernel(pos_ref, invf_ref, c_ref, s_ref):
    ang = pos_ref[...] * invf_ref[...]
    c_ref[...] = jnp.cos(ang)
    s_ref[...] = jnp.sin(ang)


def _rope_tables(pos_col, tm):
    t = pos_col.shape[0]
    inv = ROPE_THETA ** (-jnp.arange(0, C_ROPE, 2, dtype=F32) / C_ROPE)
    invf = jnp.concatenate([jnp.zeros((C_NOPE,), F32), inv, inv,
                            jnp.zeros((LANES - C_NOPE - C_ROPE,), F32)]).reshape(1, LANES)
    spec = pl.BlockSpec((tm, LANES), lambda i: (i, 0))
    return pl.pallas_call(
        _rope_table_kernel,
        grid=(t // tm,),
        in_specs=[pl.BlockSpec((tm, 1), lambda i: (i, 0)), pl.BlockSpec((1, LANES), lambda i: (0, 0))],
        out_specs=[spec, spec],
        out_shape=[jax.ShapeDtypeStruct((t, LANES), F32)] * 2,
        compiler_params=_cp(("parallel",)),
        name="rope_tables",
    )(pos_col, invf)


def _mla_prep_kernel(cq_ref, ckv_ref, r1_ref, r2_ref, c_ref, s_ref, gq_ref, gkv_ref,
                     wq1_ref, wq2_ref, wk_ref, wvt_ref, onv_ref, q_out, k_out, vt_out):
    qn = _rms(cq_ref[...].astype(F32), gq_ref[...], 1.0 / C_Q_RANK).astype(BF16)
    kvn = _rms(ckv_ref[...].astype(F32), gkv_ref[...], 1.0 / C_KV_RANK).astype(BF16)
    q1 = jnp.dot(qn, wq1_ref[...], preferred_element_type=F32)
    q2 = jnp.dot(qn, wq2_ref[...], preferred_element_type=F32)
    k1 = jnp.dot(kvn, wk_ref[...], preferred_element_type=F32)
    vt_out[...] = (_kq(wvt_ref[...], kvn) + onv_ref[...]).astype(vt_out.dtype)
    cos = c_ref[...]
    sin = s_ref[...]
    k_rope = r1_ref[...].astype(F32) * cos + r2_ref[...].astype(F32) * sin
    scale = (C_NOPE + C_ROPE) ** -0.5 * LOG2E
    for h in range(C_HEADS):
        sl = slice(h * LANES, (h + 1) * LANES)
        q_out[:, sl] = ((q1[:, sl] * cos + q2[:, sl] * sin) * scale).astype(q_out.dtype)
        k_out[:, sl] = (k1[:, sl] + k_rope).astype(k_out.dtype)


def _mla_prep(proj2, cos, sin, gq, gkv, wq1, wq2, wk, wvt, onv, tm):
    t = proj2.shape[0]
    full = lambda a: pl.BlockSpec(a.shape, lambda i: (0, 0))
    row = lambda w: pl.BlockSpec((tm, w), lambda i: (i, 0))
    return pl.pallas_call(
        _mla_prep_kernel,
        grid=(t // tm,),
        in_specs=[pl.BlockSpec((tm, 4 * LANES), lambda i: (i, COL_CQ // 4)),
                  pl.BlockSpec((tm, 2 * LANES), lambda i: (i, COL_CKV // 2)),
                  pl.BlockSpec((tm, LANES), lambda i: (i, COL_R1)),
                  pl.BlockSpec((tm, LANES), lambda i: (i, COL_R2)),
                  row(LANES), row(LANES), full(gq), full(gkv), full(wq1), full(wq2), full(wk), full(wvt), full(onv)],
        out_specs=[row(C_HEADS * LANES), row(C_HEADS * LANES), pl.BlockSpec((wvt.shape[0], tm), lambda i: (0, i))],
        out_shape=[jax.ShapeDtypeStruct((t, C_HEADS * LANES), BF16),
                   jax.ShapeDtypeStruct((t, C_HEADS * LANES), BF16),
                   jax.ShapeDtypeStruct((wvt.shape[0], t), BF16)],
        compiler_params=_cp(("parallel",)),
        name="mla_prep",
    )(proj2, proj2, proj2, proj2, cos, sin, gq, gkv, wq1, wq2, wk, wvt, onv)


def _attn_c_kernel(q_ref, k_ref, vt_ref, o_ref, m_ref, acc_ref, s_ref, cm_ref, *, tk):
    rows = vt_ref.shape[0] // 2
    _init_state(m_ref, acc_ref)

    def scores(j, slot):
        ks = pl.multiple_of(j * tk, tk)
        for c in range(2):
            sl = slice(c * LANES, (c + 1) * LANES)
            _put_scores(_kq(k_ref[pl.ds(ks, tk), sl], q_ref[:, sl]), s_ref, cm_ref, slot, c)

    def update(j, slot):
        ks = pl.multiple_of(j * tk, tk)
        _update(lambda c: vt_ref[c * rows:(c + 1) * rows, pl.ds(ks, tk)], s_ref, cm_ref, m_ref, acc_ref, slot)

    _pipelined(k_ref.shape[0] // tk, scores, update, unroll=4)
    o = jnp.concatenate([_normalized(acc_ref, 0), _normalized(acc_ref, 1)], axis=0)
    o_ref[...] = o.T.astype(o_ref.dtype)


def _attn_c(q, k, vt, tq, tk):
    b, s, _ = q.shape
    npair = C_HEADS // 2
    return pl.pallas_call(
        functools.partial(_attn_c_kernel, tk=tk),
        grid=(b, npair, s // tq),
        in_specs=[pl.BlockSpec((None, tq, 2 * LANES), lambda bb, p, i: (bb, i, p)),
                  pl.BlockSpec((None, s, 2 * LANES), lambda bb, p, i: (bb, 0, p)),
                  pl.BlockSpec((vt.shape[0] // npair, s), lambda bb, p, i: (p, bb))],
        out_specs=pl.BlockSpec((None, tq, LANES), lambda bb, p, i: (bb, i, p)),
        out_shape=jax.ShapeDtypeStruct((b, s, npair * LANES), BF16),
        scratch_shapes=_attn_scratch(tq, tk, C_DV),
        compiler_params=_cp(("parallel", "parallel", "arbitrary")),
        name="attn_latent",
    )(q, k, vt)


def _merge_kernel(oa_ref, ob_ref, oc_ref, g0_ref, g1_ref, g2_ref, x_ref, wb_ref, wo_ref, o_ref):
    z = None
    for n, (o_r, g_r) in enumerate(((oa_ref, g0_ref), (ob_ref, g1_ref), (oc_ref, g2_ref))):
        br = jnp.dot(o_r[...], wb_ref[n], preferred_element_type=F32)
        gate = 1.0 / (1.0 + jnp.exp(-g_r[...].astype(F32)))
        z = gate * br if z is None else z + gate * br
    o_ref[...] = x_ref[...] + jnp.dot(z.astype(BF16), wo_ref[...], preferred_element_type=F32)


def _merge(oa, ob, oc, proj2, x2, wb, wo, tm):
    t, d = x2.shape
    bw = oa.shape[1]
    row = lambda w: pl.BlockSpec((tm, w), lambda i: (i, 0))
    gate = lambda n: pl.BlockSpec((tm, d), lambda i: (i, COL_G * LANES // d + n))
    return pl.pallas_call(
        _merge_kernel,
        grid=(t // tm,),
        in_specs=[row(bw), row(bw), row(bw), gate(0), gate(1), gate(2), row(d),
                  pl.BlockSpec(wb.shape, lambda i: (0, 0, 0)), pl.BlockSpec(wo.shape, lambda i: (0, 0))],
        out_specs=row(d),
        out_shape=jax.ShapeDtypeStruct((t, d), F32),
        compiler_params=_cp(("parallel",)),
        name="branch_merge",
    )(oa, ob, oc, proj2, proj2, proj2, x2, wb, wo)


def _cross_kernel(x_ref, g_ref, wq_ref, kbd_ref, vbd_ref, wo_ref, o_ref, *, n_mem):
    x = x_ref[...]
    h = _rms(x, g_ref[...], 1.0 / x.shape[-1]).astype(BF16)
    q = (jnp.dot(h, wq_ref[...], preferred_element_type=F32) * (X_DH ** -0.5 * LOG2E)).astype(BF16)
    s = jnp.dot(q, kbd_ref[...], preferred_element_type=F32)
    ps = []
    for hh in range(X_HEADS):
        sh = s[:, hh * n_mem:(hh + 1) * n_mem]
        p = jnp.exp2(sh - jnp.max(sh, axis=-1, keepdims=True))
        ps.append((p / jnp.sum(p, axis=-1, keepdims=True)).astype(BF16))
    o = jnp.dot(jnp.concatenate(ps, axis=1), vbd_ref[...], preferred_element_type=F32)
    o_ref[...] = x + jnp.dot(o.astype(BF16), wo_ref[...], preferred_element_type=F32)


def _cross(x3, g, wq, kbd, vbd, wo, tm):
    b, s, d = x3.shape
    n_mem = kbd.shape[2] // X_HEADS
    full = lambda a: pl.BlockSpec(a.shape, lambda bb, i: (0, 0))
    return pl.pallas_call(
        functools.partial(_cross_kernel, n_mem=n_mem),
        grid=(b, s // tm),
        in_specs=[pl.BlockSpec((None, tm, d), lambda bb, i: (bb, i, 0)), full(g), full(wq),
                  pl.BlockSpec((None,) + kbd.shape[1:], lambda bb, i: (bb, 0, 0)),
                  pl.BlockSpec((None,) + vbd.shape[1:], lambda bb, i: (bb, 0, 0)), full(wo)],
        out_specs=pl.BlockSpec((None, tm, d), lambda bb, i: (bb, i, 0)),
        out_shape=jax.ShapeDtypeStruct((b, s, d), F32),
        compiler_params=_cp(("parallel", "parallel")),
        name="cross_attn",
    )(x3, g, wq, kbd, vbd, wo)


def _block_diag_kv(kv):
    b, m, _ = kv.shape
    kv = kv.reshape(b, m, 2, X_HEADS, X_DH)
    eye = jnp.eye(X_HEADS, dtype=kv.dtype)
    kt = kv[:, :, 0].transpose(0, 2, 3, 1)
    kbd = (kt[:, :, :, None, :] * eye[None, :, None, :, None]).reshape(b, X_HEADS * X_DH, X_HEADS * m)
    vt = kv[:, :, 1].transpose(0, 2, 1, 3)
    vbd = (vt[:, :, :, None, :] * eye[None, :, None, :, None]).reshape(b, X_HEADS * m, X_HEADS * X_DH)
    return kbd, vbd


def _router_kernel(x_ref, g_ref, w_ref, b_ref, h_out, r_out):
    x = x_ref[...]
    h = _rms(x, g_ref[...], 1.0 / x.shape[-1])
    h_out[...] = h.astype(h_out.dtype)
    logits = jnp.dot(h, w_ref[...], preferred_element_type=F32, precision=lax.Precision.HIGHEST) + b_ref[...]
    lane = lax.broadcasted_iota(jnp.int32, logits.shape, 1)
    lane_f = lane.astype(F32)
    big = jnp.float32(4 * LANES)

    def top(vals, mask):
        mv = jnp.max(jnp.where(mask, vals, -jnp.inf), axis=-1, keepdims=True)
        idx = jnp.min(jnp.where(mask & (vals == mv), lane_f, big), axis=-1, keepdims=True)
        return mv, idx

    g_mask = lane < N_GROUPS
    g_max, g_idx = top(logits, g_mask)
    p_g = 1.0 / jnp.sum(jnp.where(g_mask, jnp.exp(logits - g_max), 0.0), axis=-1, keepdims=True)
    first = N_GROUPS + g_idx * EXPERTS_PER_GROUP
    e_mask = (lane_f >= first) & (lane_f < first + EXPERTS_PER_GROUP)
    v0, i0 = top(logits, e_mask)
    v1, i1 = top(logits, e_mask & (lane_f != i0))
    e1 = jnp.exp(v1 - v0)
    w0 = p_g / (1.0 + e1)
    w1 = p_g * e1 / (1.0 + e1)
    out = jnp.where(lane == 0, i0 - N_GROUPS, 0.0)
    out = jnp.where(lane == 1, i1 - N_GROUPS, out)
    out = jnp.where(lane == 2, w0, out)
    out = jnp.where(lane == 3, w1, out)
    r_out[...] = out


def _router(x2, g, w, bias, tm):
    t, d = x2.shape
    full = lambda a: pl.BlockSpec(a.shape, lambda i: (0, 0))
    return pl.pallas_call(
        _router_kernel,
        grid=(t // tm,),
        in_specs=[pl.BlockSpec((tm, d), lambda i: (i, 0)), full(g), full(w), full(bias)],
        out_specs=[pl.BlockSpec((tm, d), lambda i: (i, 0)), pl.BlockSpec((tm, LANES), lambda i: (i, 0))],
        out_shape=[jax.ShapeDtypeStruct((t, d), BF16), jax.ShapeDtypeStruct((t, LANES), F32)],
        compiler_params=_cp(("parallel",)),
        name="moe_router",
    )(x2, g, w, bias)


def _expert_kernel(blk_e_ref, n_used_ref, x_ref, w1_ref, w3_ref, w2_ref, o_ref, w13_s, w2_s):
    i = pl.program_id(0)
    used = i < n_used_ref[0]
    de = w1_ref.shape[1]

    @pl.when(used & ((i == 0) | (blk_e_ref[i] != blk_e_ref[jnp.maximum(i - 1, 0)])))
    def _():
        w13_s[:, :de] = w1_ref[...].astype(BF16)
        w13_s[:, de:] = w3_ref[...].astype(BF16)
        w2_s[...] = w2_ref[...].astype(BF16)

    @pl.when(used)
    def _():
        hid = jnp.dot(x_ref[...], w13_s[...], preferred_element_type=F32)
        a = hid[:, :de]
        act = (a / (1.0 + jnp.exp(-a))) * hid[:, de:]
        o_ref[...] = jnp.dot(act.astype(BF16), w2_s[...], preferred_element_type=F32).astype(o_ref.dtype)

    @pl.when(jnp.logical_not(used))
    def _():
        o_ref[...] = jnp.zeros(o_ref.shape, o_ref.dtype)


def _experts(blk_e, n_used, xr, w1, w3, w2, layer, rows_per_block):
    rows, d = xr.shape
    de = w1.shape[3]
    weight = lambda w: pl.BlockSpec((None, None) + w.shape[2:], lambda i, be, nu: (layer, be[i], 0, 0))
    grid_spec = pltpu.PrefetchScalarGridSpec(
        num_scalar_prefetch=2,
        grid=(rows // rows_per_block,),
        in_specs=[pl.BlockSpec((rows_per_block, d), lambda i, be, nu: (i, 0)), weight(w1), weight(w3), weight(w2)],
        out_specs=pl.BlockSpec((rows_per_block, d), lambda i, be, nu: (i, 0)),
        scratch_shapes=[pltpu.VMEM((d, 2 * de), BF16), pltpu.VMEM((de, d), BF16)],
    )
    return pl.pallas_call(
        _expert_kernel,
        grid_spec=grid_spec,
        out_shape=jax.ShapeDtypeStruct((rows, d), BF16),
        compiler_params=_cp(("arbitrary",)),
        name="moe_experts",
    )(blk_e, n_used, xr, w1, w3, w2)


def _combine_kernel(x_ref, y0_ref, y1_ref, r_ref, g_ref, o_ref, *, final_norm):
    r = r_ref[...]
    y = x_ref[...] + r[:, 2:3] * y0_ref[...].astype(F32) + r[:, 3:4] * y1_ref[...].astype(F32)
    if final_norm:
        y = _rms(y, g_ref[...], 1.0 / y.shape[-1])
    o_ref[...] = y


def _combine(x2, y0, y1, route, g, final_norm, tm):
    t, d = x2.shape
    row = lambda w: pl.BlockSpec((tm, w), lambda i: (i, 0))
    return pl.pallas_call(
        functools.partial(_combine_kernel, final_norm=final_norm),
        grid=(t // tm,),
        in_specs=[row(d), row(d), row(d), row(LANES), pl.BlockSpec((1, d), lambda i: (0, 0))],
        out_specs=row(d),
        out_shape=jax.ShapeDtypeStruct((t, d), F32),
        compiler_params=_cp(("parallel",)),
        name="moe_combine",
    )(x2, y0, y1, route, g)


def _dispatch(route, rows_per_block):
    t = route.shape[0]
    eid = route[:, :TOP_K].astype(jnp.int32).reshape(-1)
    n = eid.shape[0]
    order = jnp.argsort(eid).astype(jnp.int32)
    rank = jnp.argsort(order).astype(jnp.int32)
    experts = jnp.arange(N_EXPERTS, dtype=jnp.int32)
    counts = jnp.sum((eid[:, None] == experts[None, :]).astype(jnp.int32), axis=0)
    start = jnp.cumsum(counts) - counts
    padded = (counts + rows_per_block - 1) // rows_per_block * rows_per_block
    pend = jnp.cumsum(padded)
    pstart = pend - padded
    dest = (rank + (pstart - start)[eid]).reshape(t, TOP_K)
    n_blocks = n // rows_per_block + N_EXPERTS
    blk_first = jnp.arange(n_blocks, dtype=jnp.int32) * rows_per_block
    blk_e = jnp.minimum(jnp.sum((pend[None, :] <= blk_first[:, None]).astype(jnp.int32), axis=1), N_EXPERTS - 1)
    off = (blk_first - pstart[blk_e])[:, None] + jnp.arange(rows_per_block, dtype=jnp.int32)[None, :]
    src = jnp.clip(start[blk_e][:, None] + off, 0, n - 1)
    row_tok = jnp.where(off < counts[blk_e][:, None], order[src] // TOP_K, 0).reshape(-1)
    n_used = (pend[-1] // rows_per_block).astype(jnp.int32).reshape(1)
    return row_tok, dest, blk_e, n_used


def _rot_cols(w):
    half = w.shape[-1] // 2
    return jnp.concatenate([-w[..., half:], w[..., :half]], axis=-1)


def _pack_w_in(w):
    d = w.shape[0]
    blk = A_HEADS * 2 * A_DH
    n_ab = 6 * blk
    cq = w[:, n_ab:n_ab + C_Q_RANK]
    ckv = w[:, n_ab + C_Q_RANK:n_ab + C_Q_RANK + C_KV_RANK]
    ckr = w[:, n_ab + C_Q_RANK + C_KV_RANK:n_ab + C_Q_RANK + C_KV_RANK + C_ROPE]
    gates = w[:, n_ab + C_Q_RANK + C_KV_RANK + C_ROPE:]
    z = lambda n: jnp.zeros((d, n), w.dtype)
    tail = LANES - C_NOPE - C_ROPE
    packed = jnp.concatenate([w[:, :2 * blk], w[:, 3 * blk:5 * blk], cq, z(LANES), ckv,
                              z(C_NOPE), ckr, z(tail), z(C_NOPE), _rot_cols(ckr), z(tail), gates], axis=1)
    return packed.astype(BF16), w[:, 2 * blk:3 * blk], w[:, 5 * blk:6 * blk]


def _pack_w_uq(w):
    wq = w.reshape(C_Q_RANK, C_HEADS, C_NOPE + C_ROPE)
    pad_rows = 4 * LANES - C_Q_RANK
    tail = LANES - C_NOPE - C_ROPE
    q1 = jnp.pad(wq, ((0, pad_rows), (0, 0), (0, tail))).reshape(4 * LANES, C_HEADS * LANES)
    q2 = jnp.pad(_rot_cols(wq[:, :, C_NOPE:]), ((0, pad_rows), (0, 0), (C_NOPE, tail))).reshape(4 * LANES, C_HEADS * LANES)
    return q1.astype(BF16), q2.astype(BF16)


def _pack_w_ukv(w):
    wkv = w.reshape(C_KV_RANK, C_HEADS, C_NOPE + C_DV)
    wk = jnp.pad(wkv[:, :, :C_NOPE], ((0, 0), (0, 0), (0, LANES - C_NOPE))).reshape(C_KV_RANK, C_HEADS * LANES)
    wv = wkv[:, :, C_NOPE:].reshape(C_KV_RANK, C_HEADS * C_DV)
    return wk.astype(BF16), wv


def kernel(x, mem, positions, mix_norm_g, w_in, diff_lambda, diff_subln_g, mla_q_norm_g, w_uq, mla_kv_norm_g, w_ukv, w_branch, w_out, cross_norm_g, mem_norm_g, w_xq, w_xkv, w_xo, ffn_norm_g, w_group, b_group, w_router, b_router, w1, w3, w2, final_norm_g):
    b, s, d = x.shape
    depth = w_in.shape[0]
    t = b * s
    n_mem = mem.shape[1]
    tl = _tiles(s)
    assert PROJ_COLS == COL_G * LANES + N_BRANCHES * d and s % tl["t_b"] == 0 and tl["t_b"] == tl["tq_a"]

    pos_f = positions.astype(F32)
    posq_row = pos_f.reshape(b, 1, s)
    posk_rep = jnp.broadcast_to(pos_f[:, :, None], (b, s, LANES))
    cos, sin = _rope_tables(pos_f.reshape(t, 1), tl["tm_tok"])
    slopes_a2 = jnp.exp2(-8.0 * jnp.arange(1, A_HEADS + 1, dtype=F32) / A_HEADS) * LOG2E
    reach = max((w // (2 * dl)) * dl for w, dl in B_PATTERNS)
    b_bias = _b_bias_tables(tl["t_b"], 2 * (-(-reach // tl["t_b"])) + 1)
    mem2 = mem.reshape(b * n_mem, d)

    x2 = x.reshape(t, d)
    for l in range(depth):
        w_main, w_av, w_bv = _pack_w_in(w_in[l])
        proj2, vt_a, vt_b = _in_proj(x2, mix_norm_g[l], w_main, *_vt_weights(w_av, 2 * A_DH), *_vt_weights(w_bv, B_DH),
                                     tl["tm_proj"], tl["tn_proj"])
        proj3 = proj2.reshape(b, s, PROJ_COLS)
        lq = diff_lambda[l].astype(F32)
        lam_init = 0.8 - 0.6 * math.exp(-0.3 * l)
        lam = (jnp.exp(jnp.sum(lq[0] * lq[1])) - jnp.exp(jnp.sum(lq[2] * lq[3])) + lam_init).reshape(1)
        norms = _block_norms(proj3, tl["t_b"])
        oa = _attn_a(proj3, vt_a, norms, posq_row, posk_rep, lam, slopes_a2, diff_subln_g[l], 1.0 - lam_init,
                     tl["tq_a"], tl["tk_a"])
        ob = _attn_b(proj3, vt_b, norms, b_bias, tl["t_b"])
        wq1, wq2 = _pack_w_uq(w_uq[l])
        wk, wv = _pack_w_ukv(w_ukv[l])
        gq = jnp.pad(mla_q_norm_g[l], (0, 4 * LANES - C_Q_RANK)).reshape(1, 4 * LANES)
        qc, kc, vt_c = _mla_prep(proj2, cos, sin, gq, mla_kv_norm_g[l].reshape(1, C_KV_RANK),
                                 wq1, wq2, wk, *_vt_weights(wv, C_DV), tl["tm_tok"])
        oc = _attn_c(qc.reshape(b, s, -1), kc.reshape(b, s, -1), vt_c, tl["tq_c"], tl["tk_c"])
        x2 = _merge(oa.reshape(t, -1), ob.reshape(t, -1), oc.reshape(t, -1), proj2, x2,
                    w_branch[l].astype(BF16), w_out[l].astype(BF16), tl["tm_tok"])
        kv = _norm_matmul(mem2, mem_norm_g[l], w_xkv[l].astype(BF16), min(1024, b * n_mem), w_xkv.shape[2])
        kbd, vbd = _block_diag_kv(kv.reshape(b, n_mem, -1))
        x2 = _cross(x2.reshape(b, s, d), cross_norm_g[l].reshape(1, d), w_xq[l].astype(BF16), kbd, vbd,
                    w_xo[l].astype(BF16), tl["tm_tok"]).reshape(t, d)
        w_r = jnp.pad(jnp.concatenate([w_group[l], w_router[l]], axis=1), ((0, 0), (0, LANES - N_GROUPS - N_EXPERTS)))
        b_r = jnp.pad(jnp.concatenate([b_group[l], b_router[l]]), (0, LANES - N_GROUPS - N_EXPERTS)).reshape(1, LANES)
        h, route = _router(x2, ffn_norm_g[l].reshape(1, d), w_r, b_r, tl["tm_tok"])
        row_tok, dest, blk_e, n_used = _dispatch(route, tl["moe_rows"])
        yr = _experts(blk_e, n_used, h[row_tok], w1, w3, w2, l, tl["moe_rows"])
        x2 = _combine(x2, yr[dest[:, 0]], yr[dest[:, 1]], route, final_norm_g.reshape(1, d),
                      l == depth - 1, tl["tm_tok"])
    return x2.reshape(b, s, d)
```

```python
import functools
import math

import jax
import jax.numpy as jnp
from jax import lax
from jax.experimental import pallas as pl
from jax.experimental.pallas import tpu as pltpu

F32 = jnp.float32
BF16 = jnp.bfloat16

LANES = 128
NORM_EPS = 1e-6
LOG2E = math.log2(math.e)
NEG_BIG = -1e30

A_HEADS, A_DH = 4, 64
B_HEADS, B_DH = 8, 64
B_PATTERNS = ((128, 1), (512, 4), (2048, 16))
C_HEADS, C_Q_RANK, C_KV_RANK, C_NOPE, C_ROPE, C_DV = 8, 384, 256, 64, 32, 64
ROPE_THETA = 10000.0
N_BRANCHES = 3
X_HEADS, X_DH = 4, 64
N_GROUPS, EXPERTS_PER_GROUP, TOP_K = 4, 8, 2
N_EXPERTS = N_GROUPS * EXPERTS_PER_GROUP

COL_A = 0
COL_B = 8
COL_CQ = 16
COL_CKV = 20
COL_R1 = 22
COL_R2 = 23
COL_G = 24
PROJ_COLS = 48 * LANES

VMEM_LIMIT = 48 * 1024 * 1024


def _tiles(seq):
    return dict(
        tm_proj=min(1024, seq), tn_proj=1024,
        tm_tok=min(512, seq),
        tm_moe=min(1024, seq),
        tq_a=min(512, seq), tk_a=min(512, seq),
        t_b=min(512, seq),
        tq_c=min(512, seq), tk_c=min(512, seq),
        moe_rows=512,
    )


def _cp(sem):
    return pltpu.CompilerParams(dimension_semantics=sem, vmem_limit_bytes=VMEM_LIMIT)


def _rms(x, g, inv_n):
    ms = jnp.sum(x * x, axis=-1, keepdims=True) * inv_n
    return x * lax.rsqrt(ms + NORM_EPS) * g


def _norm_matmul_kernel(x_ref, g_ref, w_ref, o_ref, h_ref, *, inv_n):
    @pl.when(pl.program_id(1) == 0)
    def _():
        h_ref[...] = _rms(x_ref[...].astype(F32), g_ref[...], inv_n).astype(BF16)

    o_ref[...] = jnp.dot(h_ref[...], w_ref[...], preferred_element_type=F32).astype(o_ref.dtype)


def _norm_matmul(x, g, w, tm, tn, out_dtype=BF16):
    m, k = x.shape
    n = w.shape[1]
    return pl.pallas_call(
        functools.partial(_norm_matmul_kernel, inv_n=1.0 / k),
        grid=(m // tm, n // tn),
        in_specs=[pl.BlockSpec((tm, k), lambda i, j: (i, 0)),
                  pl.BlockSpec((1, k), lambda i, j: (0, 0)),
                  pl.BlockSpec((k, tn), lambda i, j: (0, j))],
        out_specs=pl.BlockSpec((tm, tn), lambda i, j: (i, j)),
        out_shape=jax.ShapeDtypeStruct((m, n), out_dtype),
        scratch_shapes=[pltpu.VMEM((tm, k), BF16)],
        compiler_params=_cp(("parallel", "arbitrary")),
        name="norm_matmul",
    )(x, g.reshape(1, k), w)


def _in_proj_kernel(x_ref, g_ref, w_ref, wta_ref, wtb_ref, ona_ref, onb_ref, o_ref, vta_ref, vtb_ref, h_ref, *, inv_n):
    @pl.when(pl.program_id(1) == 0)
    def _():
        h = _rms(x_ref[...].astype(F32), g_ref[...], inv_n).astype(BF16)
        h_ref[...] = h
        vta_ref[...] = (_kq(wta_ref[...], h) + ona_ref[...]).astype(vta_ref.dtype)
        vtb_ref[...] = (_kq(wtb_ref[...], h) + onb_ref[...]).astype(vtb_ref.dtype)

    o_ref[...] = jnp.dot(h_ref[...], w_ref[...], preferred_element_type=F32).astype(o_ref.dtype)


def _in_proj(x, g, w, wta, ona, wtb, onb, tm, tn):
    m, k = x.shape
    n = w.shape[1]
    full = lambda a: pl.BlockSpec(a.shape, lambda i, j: (0, 0))
    return pl.pallas_call(
        functools.partial(_in_proj_kernel, inv_n=1.0 / k),
        grid=(m // tm, n // tn),
        in_specs=[pl.BlockSpec((tm, k), lambda i, j: (i, 0)),
                  pl.BlockSpec((1, k), lambda i, j: (0, 0)),
                  pl.BlockSpec((k, tn), lambda i, j: (0, j)),
                  full(wta), full(wtb), full(ona), full(onb)],
        out_specs=[pl.BlockSpec((tm, tn), lambda i, j: (i, j)),
                   pl.BlockSpec((wta.shape[0], tm), lambda i, j: (0, i)),
                   pl.BlockSpec((wtb.shape[0], tm), lambda i, j: (0, i))],
        out_shape=[jax.ShapeDtypeStruct((m, n), BF16),
                   jax.ShapeDtypeStruct((wta.shape[0], m), BF16),
                   jax.ShapeDtypeStruct((wtb.shape[0], m), BF16)],
        scratch_shapes=[pltpu.VMEM((tm, k), BF16)],
        compiler_params=_cp(("parallel", "arbitrary")),
        name="in_proj",
    )(x, g.reshape(1, k), w, wta, wtb, ona, onb)


ONES_ROWS = 16


def _vt_weights(wv, dv):
    k, w = wv.shape
    wt = jnp.pad(wv.T.reshape(w // dv, dv, k), ((0, 0), (0, ONES_ROWS), (0, 0))).reshape(-1, k)
    ones = jnp.zeros((w // dv, dv + ONES_ROWS, 1), F32).at[:, dv, 0].set(1.0).reshape(-1, 1)
    return wt.astype(BF16), ones


def _split_q(q_ref, qs_ref, scale):
    q = q_ref[...].astype(F32) * scale
    lane = lax.broadcasted_iota(jnp.int32, q.shape, 1)
    qs_ref[0] = jnp.where(lane < LANES // 2, q, 0.0).astype(BF16)
    qs_ref[1] = jnp.where(lane >= LANES // 2, q, 0.0).astype(BF16)


def _kq(k, q):
    return lax.dot_general(k, q, (((1,), (1,)), ((), ())), preferred_element_type=F32)


def _put_scores(s, s_ref, cm_ref, slot, c):
    s_ref[slot, c] = s
    cm_ref[slot, c] = jnp.max(s, axis=0, keepdims=True)


def _update(vt_of, s_ref, cm_ref, m_ref, acc_ref, slot):
    for c in range(2):
        m_prev = m_ref[c]
        m_new = jnp.maximum(m_prev, cm_ref[slot, c])
        alpha = jnp.exp2(m_prev - m_new)
        p = jnp.exp2(s_ref[slot, c] - m_new).astype(BF16)
        acc_ref[c] = alpha * acc_ref[c] + jnp.dot(vt_of(c), p, preferred_element_type=F32)
        m_ref[c] = m_new


def _init_state(m_ref, acc_ref):
    m_ref[...] = jnp.full(m_ref.shape, NEG_BIG, F32)
    acc_ref[...] = jnp.zeros(acc_ref.shape, F32)


def _pipelined(n, scores, update, unroll):
    scores(0, 0)

    def body(jj, carry):
        j = unroll * jj
        for u in range(unroll):
            scores(j + u + 1, (u + 1) % 2)
            update(j + u, u % 2)
        return carry

    looped = (n - 1) // unroll
    lax.fori_loop(0, looped, body, 0)
    for j in range(unroll * looped, n):
        if j + 1 < n:
            scores(j + 1, (j + 1) % 2)
        update(j, j % 2)


def _pipelined_list(n, chunk, scores, update):
    scores(chunk(0), 0)

    def body(unroll, first):
        def run(jj, carry):
            t = first + unroll * jj
            for u in range(unroll):
                scores(chunk(t + u + 1), (u + 1) % 2)
                update(chunk(t + u), u % 2)
            return carry
        return run

    quads = (n - 1) // 4
    lax.fori_loop(0, quads, body(4, 0), 0)
    pairs = (n - 1 - 4 * quads) // 2
    lax.fori_loop(0, pairs, body(2, 4 * quads), 0)
    left = n - 4 * quads - 2 * pairs

    @pl.when(left == 2)
    def _():
        scores(chunk(n - 1), 1)
        update(chunk(n - 2), 0)
        update(chunk(n - 1), 1)

    @pl.when(left == 1)
    def _():
        update(chunk(n - 1), 0)


def _normalized(acc_ref, c):
    dv = acc_ref.shape[1] - ONES_ROWS
    return acc_ref[c, :dv, :] / acc_ref[c, dv:dv + 1, :]


def _attn_scratch(tq, tk, dv):
    return [pltpu.VMEM((2, 1, tq), F32), pltpu.VMEM((2, dv + ONES_ROWS, tq), F32),
            pltpu.VMEM((2, 2, tk, tq), F32), pltpu.VMEM((2, 2, 1, tq), F32)]


SKIP_MARGIN = 128.0


def _block_norms_kernel(qk_ref, sel_ref, o_ref):
    x = qk_ref[...].astype(F32)
    lane = lax.broadcasted_iota(jnp.int32, x.shape, 1)
    is_query = (lane // (A_HEADS * 2 * A_DH)) % 2 == 0
    x = jnp.where(is_query, (x * (A_DH ** -0.5 * LOG2E)).astype(BF16).astype(F32), x)
    sums = jnp.dot(x * x, sel_ref[...], preferred_element_type=F32)
    o_ref[...] = jnp.sqrt(jnp.max(sums, axis=0, keepdims=True))


def _block_norms(proj3, t):
    b, s, _ = proj3.shape
    assert COL_B == COL_A + 2 * A_HEADS and A_DH == B_DH
    w = 4 * A_HEADS * 2 * A_DH
    sel = (jnp.arange(w)[:, None] // A_DH == jnp.arange(LANES)[None, :]).astype(BF16)
    norms = pl.pallas_call(
        _block_norms_kernel,
        grid=(b, s // t),
        in_specs=[pl.BlockSpec((None, t, w), lambda bb, i: (bb, i, COL_A * LANES // w)),
                  pl.BlockSpec((w, LANES), lambda bb, i: (0, 0))],
        out_specs=pl.BlockSpec((None, None, 1, LANES), lambda bb, i: (bb, i, 0, 0)),
        out_shape=jax.ShapeDtypeStruct((b, s // t, 1, LANES), F32),
        compiler_params=_cp(("parallel", "parallel")),
        name="block_norms",
    )(proj3, sel)
    return norms[:, :, 0, :w // A_DH].reshape(b, s // t, 4, 2 * A_HEADS)


def _a_chunk_lists(norms, pos_f, slopes2, tq, tk):
    assert tq == tk
    b, s = pos_f.shape
    norms = norms[:, :, :2].reshape(b, s // tq, 2, A_HEADS, 2).max(axis=-1)
    qn = norms[:, :, 0].transpose(0, 2, 1)
    kn = norms[:, :, 1].transpose(0, 2, 1)
    qk = 1.01 * qn[:, :, :, None] * kn[:, :, None, :] + 1.0
    pq = pos_f.reshape(b, s // tq, tq)
    pk = pos_f.reshape(b, s // tk, tk)
    qlo, qhi = jnp.min(pq, -1)[:, :, None], jnp.max(pq, -1)[:, :, None]
    klo, khi = jnp.min(pk, -1)[:, None, :], jnp.max(pk, -1)[:, None, :]
    dmin = jnp.maximum(jnp.maximum(klo - qhi, qlo - khi), 0.0)
    dmax = jnp.maximum(khi - qlo, qhi - klo)
    sl = slopes2[None, :, None, None]
    lower = -qk - sl * dmax[:, None]
    upper = qk - sl * dmin[:, None]
    nk = s // tk
    j = jnp.arange(nk, dtype=jnp.int32)
    cover = jnp.max(jnp.min(jnp.abs(pq[:, :, :, None] - pk[:, :, None, :]), axis=-1), axis=-1)
    lower = jnp.where(j[:, None] == j[None, :], jnp.maximum(lower, -qk - sl * cover[:, None, :, None]), lower)
    first = jnp.argmax(lower, axis=-1).astype(jnp.int32)[..., None]
    keep = (upper >= jnp.max(lower, axis=-1, keepdims=True) - SKIP_MARGIN) | (j == first)
    lst = jnp.argsort(jnp.where(j == first, -1, jnp.where(keep, j, nk + j)), axis=-1).astype(jnp.int32)
    cnt = jnp.sum(keep.astype(jnp.int32), axis=-1)
    return lst.reshape(-1), cnt.reshape(-1)


def _attn_a_kernel(lst_ref, cnt_ref, lam_ref, slope_ref, q_ref, k_ref, vt_ref, pq_ref, pk_ref, g_ref, o_ref,
                   qs_ref, m_ref, acc_ref, s_ref, cm_ref, *, tk, post_scale):
    tq = q_ref.shape[0]
    nk = k_ref.shape[0] // tk
    blk = (pl.program_id(0) * pl.num_programs(1) + pl.program_id(1)) * pl.num_programs(2) + pl.program_id(2)
    sl2 = slope_ref[pl.program_id(1)]
    _split_q(q_ref, qs_ref, A_DH ** -0.5 * LOG2E)
    _init_state(m_ref, acc_ref)
    pq = pq_ref[...] * sl2

    def scores(j, slot):
        ks = pl.multiple_of(j * tk, tk)
        k = k_ref[pl.ds(ks, tk), :]
        pk = pk_ref[pl.ds(ks, tk), :] * sl2
        bias = jnp.abs(jnp.concatenate([pk] * (tq // LANES), axis=1) - pq)
        for c in range(2):
            _put_scores(_kq(k, qs_ref[c]) - bias, s_ref, cm_ref, slot, c)

    def update(j, slot):
        ks = pl.multiple_of(j * tk, tk)
        _update(lambda c: vt_ref[:, pl.ds(ks, tk)], s_ref, cm_ref, m_ref, acc_ref, slot)

    _pipelined_list(cnt_ref[blk], lambda t: lst_ref[blk * nk + t], scores, update)
    o = (_normalized(acc_ref, 0) - lam_ref[0] * _normalized(acc_ref, 1)).T
    o_ref[...] = (_rms(o, g_ref[...], 1.0 / LANES) * post_scale).astype(o_ref.dtype)


def _attn_a(proj, vt, norms, posq_row, posk_rep, lam, slopes2, g, post_scale, tq, tk):
    b, s, _ = proj.shape
    lst, cnt = _a_chunk_lists(norms, posq_row.reshape(b, s), slopes2, tq, tk)
    smem = pl.BlockSpec(memory_space=pltpu.SMEM)
    grid_spec = pltpu.PrefetchScalarGridSpec(
        num_scalar_prefetch=2,
        grid=(b, A_HEADS, s // tq),
        in_specs=[smem, smem,
                  pl.BlockSpec((None, tq, LANES), lambda bb, h, i, *_: (bb, i, COL_A + h)),
                  pl.BlockSpec((None, s, LANES), lambda bb, h, i, *_: (bb, 0, COL_A + A_HEADS + h)),
                  pl.BlockSpec((vt.shape[0] // A_HEADS, s), lambda bb, h, i, *_: (h, bb)),
                  pl.BlockSpec((None, 1, tq), lambda bb, h, i, *_: (bb, 0, i)),
                  pl.BlockSpec((None, s, LANES), lambda bb, h, i, *_: (bb, 0, 0)),
                  pl.BlockSpec((1, LANES), lambda bb, h, i, *_: (0, 0))],
        out_specs=pl.BlockSpec((None, tq, LANES), lambda bb, h, i, *_: (bb, i, h)),
        scratch_shapes=[pltpu.VMEM((2, tq, LANES), BF16)] + _attn_scratch(tq, tk, LANES),
    )
    return pl.pallas_call(
        functools.partial(_attn_a_kernel, tk=tk, post_scale=post_scale),
        grid_spec=grid_spec,
        out_shape=jax.ShapeDtypeStruct((b, s, A_HEADS * LANES), BF16),
        compiler_params=_cp(("parallel", "parallel", "arbitrary")),
        name="attn_diff",
    )(lst, cnt, lam, slopes2, proj, proj, vt, posq_row, posk_rep, g.reshape(1, LANES))


def _b_chunk_lists(norms, nband, t):
    b, nq = norms.shape[:2]
    half = nband // 2
    norms = norms[:, :, 2:]
    jj = jnp.arange(nband, dtype=jnp.int32)
    kb = jnp.arange(nq, dtype=jnp.int32)[:, None] + jj[None, :] - half
    inside = (kb >= 0) & (kb < nq)
    qk = 1.01 * norms[:, :, None, 0, :] * norms[:, jnp.clip(kb, 0, nq - 1), 1, :] + 1.0
    dmin = jnp.where(jj == half, 0, (jnp.abs(jj - half) - 1) * t + 1).astype(F32)
    slopes2 = jnp.exp2(-8.0 * jnp.arange(1, B_HEADS + 1, dtype=F32) / B_HEADS) * LOG2E
    upper = qk - slopes2 * dmin[:, None] + math.log2(len(B_PATTERNS))
    keep = upper >= -qk[:, :, half:half + 1, :] - SKIP_MARGIN
    keep = jnp.any(keep.reshape(b, nq, nband, B_HEADS // 2, 2), axis=-1).transpose(0, 3, 1, 2)
    keep = (keep & inside) | (jj == half)
    lst = jnp.argsort(jnp.where(jj == half, -1, jnp.where(keep, jj, nband + jj)), axis=-1).astype(jnp.int32)
    return lst.reshape(-1), jnp.sum(keep.astype(jnp.int32), axis=-1).reshape(-1)


def _attn_b_kernel(lst_ref, cnt_ref, q_ref, k_ref, vt_ref, bias_ref, o_ref, qs_ref, m_ref, acc_ref, s_ref, cm_ref,
                   *, nband):
    t = q_ref.shape[0]
    rows = vt_ref.shape[0] // 2
    i = pl.program_id(2)
    blk = (pl.program_id(0) * pl.num_programs(1) + pl.program_id(1)) * pl.num_programs(2) + i
    _split_q(q_ref, qs_ref, B_DH ** -0.5 * LOG2E)
    _init_state(m_ref, acc_ref)

    def start(jj):
        return pl.multiple_of((i + jj - nband // 2) * t, t)

    def scores(jj, slot):
        k = k_ref[pl.ds(start(jj), t), :]
        for c in range(2):
            _put_scores(_kq(k, qs_ref[c]) + bias_ref[c, jj], s_ref, cm_ref, slot, c)

    def update(jj, slot):
        ks = start(jj)
        _update(lambda c: vt_ref[c * rows:(c + 1) * rows, pl.ds(ks, t)], s_ref, cm_ref, m_ref, acc_ref, slot)

    _pipelined_list(cnt_ref[blk], lambda n: lst_ref[blk * nband + n], scores, update)
    o = jnp.concatenate([_normalized(acc_ref, 0), _normalized(acc_ref, 1)], axis=0)
    o_ref[...] = o.T.astype(o_ref.dtype)


def _b_bias_tables(t, nband):
    half = nband // 2
    r = jnp.arange(t, dtype=jnp.int32)[None, :, None]
    c = jnp.arange(t, dtype=jnp.int32)[None, None, :]
    jj = jnp.arange(nband, dtype=jnp.int32)[:, None, None]
    ao = jnp.abs((jj - half) * t + r - c)
    mult = jnp.zeros(ao.shape, jnp.int32)
    for window, dilation in B_PATTERNS:
        reach = (window // (2 * dilation)) * dilation
        mult = mult + ((ao % dilation == 0) & (ao <= reach)).astype(jnp.int32)
    slopes = jnp.exp2(-8.0 * jnp.arange(1, B_HEADS + 1, dtype=F32) / B_HEADS) * LOG2E
    bias = jnp.log2(jnp.maximum(mult, 1).astype(F32))[None] - slopes[:, None, None, None] * ao.astype(F32)[None]
    bias = jnp.where((mult > 0)[None], bias, NEG_BIG)
    return bias.reshape(B_HEADS // 2, 2, nband, t, t)


def _attn_b(proj, vt, norms, bias, t):
    b, s, _ = proj.shape
    nband = bias.shape[2]
    npair = B_HEADS // 2
    lst, cnt = _b_chunk_lists(norms, nband, t)
    grid_spec = pltpu.PrefetchScalarGridSpec(
        num_scalar_prefetch=2,
        grid=(b, npair, s // t),
        in_specs=[pl.BlockSpec((None, t, LANES), lambda bb, p, i, *_: (bb, i, COL_B + p)),
                  pl.BlockSpec((None, s, LANES), lambda bb, p, i, *_: (bb, 0, COL_B + npair + p)),
                  pl.BlockSpec((vt.shape[0] // npair, s), lambda bb, p, i, *_: (p, bb)),
                  pl.BlockSpec((None,) + bias.shape[1:], lambda bb, p, i, *_: (p, 0, 0, 0, 0))],
        out_specs=pl.BlockSpec((None, t, LANES), lambda bb, p, i, *_: (bb, i, p)),
        scratch_shapes=[pltpu.VMEM((2, t, LANES), BF16)] + _attn_scratch(t, t, B_DH),
    )
    return pl.pallas_call(
        functools.partial(_attn_b_kernel, nband=nband),
        grid_spec=grid_spec,
        out_shape=jax.ShapeDtypeStruct((b, s, npair * LANES), BF16),
        compiler_params=_cp(("parallel", "parallel", "arbitrary")),
        name="attn_dilated",
    )(lst, cnt, proj, proj, vt, bias)


def _rope_table_kernel(pos_ref, invf_ref, c_ref, s_ref):
    ang = pos_ref[...] * invf_ref[...]
    c_ref[...] = jnp.cos(ang)
    s_ref[...] = jnp.sin(ang)


def _rope_tables(pos_col, tm):
    t = pos_col.shape[0]
    inv = ROPE_THETA ** (-jnp.arange(0, C_ROPE, 2, dtype=F32) / C_ROPE)
    invf = jnp.concatenate([jnp.zeros((C_NOPE,), F32), inv, inv,
                            jnp.zeros((LANES - C_NOPE - C_ROPE,), F32)]).reshape(1, LANES)
    spec = pl.BlockSpec((tm, LANES), lambda i: (i, 0))
    return pl.pallas_call(
        _rope_table_kernel,
        grid=(t // tm,),
        in_specs=[pl.BlockSpec((tm, 1), lambda i: (i, 0)), pl.BlockSpec((1, LANES), lambda i: (0, 0))],
        out_specs=[spec, spec],
        out_shape=[jax.ShapeDtypeStruct((t, LANES), F32)] * 2,
        compiler_params=_cp(("parallel",)),
        name="rope_tables",
    )(pos_col, invf)


def _mla_prep_kernel(cq_ref, ckv_ref, r1_ref, r2_ref, c_ref, s_ref, gq_ref, gkv_ref,
                     wq1_ref, wq2_ref, wk_ref, wvt_ref, onv_ref, q_out, k_out, vt_out):
    qn = _rms(cq_ref[...].astype(F32), gq_ref[...], 1.0 / C_Q_RANK).astype(BF16)
    kvn = _rms(ckv_ref[...].astype(F32), gkv_ref[...], 1.0 / C_KV_RANK).astype(BF16)
    q1 = jnp.dot(qn, wq1_ref[...], preferred_element_type=F32)
    q2 = jnp.dot(qn, wq2_ref[...], preferred_element_type=F32)
    k1 = jnp.dot(kvn, wk_ref[...], preferred_element_type=F32)
    vt_out[...] = (_kq(wvt_ref[...], kvn) + onv_ref[...]).astype(vt_out.dtype)
    cos = c_ref[...]
    sin = s_ref[...]
    k_rope = r1_ref[...].astype(F32) * cos + r2_ref[...].astype(F32) * sin
    scale = (C_NOPE + C_ROPE) ** -0.5 * LOG2E
    for h in range(C_HEADS):
        sl = slice(h * LANES, (h + 1) * LANES)
        q_out[:, sl] = ((q1[:, sl] * cos + q2[:, sl] * sin) * scale).astype(q_out.dtype)
        k_out[:, sl] = (k1[:, sl] + k_rope).astype(k_out.dtype)


def _mla_prep(proj2, cos, sin, gq, gkv, wq1, wq2, wk, wvt, onv, tm):
    t = proj2.shape[0]
    full = lambda a: pl.BlockSpec(a.shape, lambda i: (0, 0))
    row = lambda w: pl.BlockSpec((tm, w), lambda i: (i, 0))
    return pl.pallas_call(
        _mla_prep_kernel,
        grid=(t // tm,),
        in_specs=[pl.BlockSpec((tm, 4 * LANES), lambda i: (i, COL_CQ // 4)),
                  pl.BlockSpec((tm, 2 * LANES), lambda i: (i, COL_CKV // 2)),
                  pl.BlockSpec((tm, LANES), lambda i: (i, COL_R1)),
                  pl.BlockSpec((tm, LANES), lambda i: (i, COL_R2)),
                  row(LANES), row(LANES), full(gq), full(gkv), full(wq1), full(wq2), full(wk), full(wvt), full(onv)],
        out_specs=[row(C_HEADS * LANES), row(C_HEADS * LANES), pl.BlockSpec((wvt.shape[0], tm), lambda i: (0, i))],
        out_shape=[jax.ShapeDtypeStruct((t, C_HEADS * LANES), BF16),
                   jax.ShapeDtypeStruct((t, C_HEADS * LANES), BF16),
                   jax.ShapeDtypeStruct((wvt.shape[0], t), BF16)],
        compiler_params=_cp(("parallel",)),
        name="mla_prep",
    )(proj2, proj2, proj2, proj2, cos, sin, gq, gkv, wq1, wq2, wk, wvt, onv)


def _attn_c_kernel(q_ref, k_ref, vt_ref, o_ref, m_ref, acc_ref, s_ref, cm_ref, *, tk):
    rows = vt_ref.shape[0] // 2
    _init_state(m_ref, acc_ref)

    def scores(j, slot):
        ks = pl.multiple_of(j * tk, tk)
        for c in range(2):
            sl = slice(c * LANES, (c + 1) * LANES)
            _put_scores(_kq(k_ref[pl.ds(ks, tk), sl], q_ref[:, sl]), s_ref, cm_ref, slot, c)

    def update(j, slot):
        ks = pl.multiple_of(j * tk, tk)
        _update(lambda c: vt_ref[c * rows:(c + 1) * rows, pl.ds(ks, tk)], s_ref, cm_ref, m_ref, acc_ref, slot)

    _pipelined(k_ref.shape[0] // tk, scores, update, unroll=4)
    o = jnp.concatenate([_normalized(acc_ref, 0), _normalized(acc_ref, 1)], axis=0)
    o_ref[...] = o.T.astype(o_ref.dtype)


def _attn_c(q, k, vt, tq, tk):
    b, s, _ = q.shape
    npair = C_HEADS // 2
    return pl.pallas_call(
        functools.partial(_attn_c_kernel, tk=tk),
        grid=(b, npair, s // tq),
        in_specs=[pl.BlockSpec((None, tq, 2 * LANES), lambda bb, p, i: (bb, i, p)),
                  pl.BlockSpec((None, s, 2 * LANES), lambda bb, p, i: (bb, 0, p)),
                  pl.BlockSpec((vt.shape[0] // npair, s), lambda bb, p, i: (p, bb))],
        out_specs=pl.BlockSpec((None, tq, LANES), lambda bb, p, i: (bb, i, p)),
        out_shape=jax.ShapeDtypeStruct((b, s, npair * LANES), BF16),
        scratch_shapes=_attn_scratch(tq, tk, C_DV),
        compiler_params=_cp(("parallel", "parallel", "arbitrary")),
        name="attn_latent",
    )(q, k, vt)


def _merge_kernel(oa_ref, ob_ref, oc_ref, g0_ref, g1_ref, g2_ref, x_ref, wb_ref, wo_ref, o_ref):
    z = None
    for n, (o_r, g_r) in enumerate(((oa_ref, g0_ref), (ob_ref, g1_ref), (oc_ref, g2_ref))):
        br = jnp.dot(o_r[...], wb_ref[n], preferred_element_type=F32)
        gate = 1.0 / (1.0 + jnp.exp(-g_r[...].astype(F32)))
        z = gate * br if z is None else z + gate * br
    o_ref[...] = x_ref[...] + jnp.dot(z.astype(BF16), wo_ref[...], preferred_element_type=F32)


def _merge(oa, ob, oc, proj2, x2, wb, wo, tm):
    t, d = x2.shape
    bw = oa.shape[1]
    row = lambda w: pl.BlockSpec((tm, w), lambda i: (i, 0))
    gate = lambda n: pl.BlockSpec((tm, d), lambda i: (i, COL_G * LANES // d + n))
    return pl.pallas_call(
        _merge_kernel,
        grid=(t // tm,),
        in_specs=[row(bw), row(bw), row(bw), gate(0), gate(1), gate(2), row(d),
                  pl.BlockSpec(wb.shape, lambda i: (0, 0, 0)), pl.BlockSpec(wo.shape, lambda i: (0, 0))],
        out_specs=row(d),
        out_shape=jax.ShapeDtypeStruct((t, d), F32),
        compiler_params=_cp(("parallel",)),
        name="branch_merge",
    )(oa, ob, oc, proj2, proj2, proj2, x2, wb, wo)


def _cross_kernel(x_ref, g_ref, wq_ref, kbd_ref, vbd_ref, wo_ref, o_ref, *, n_mem):
    x = x_ref[...]
    h = _rms(x, g_ref[...], 1.0 / x.shape[-1]).astype(BF16)
    q = (jnp.dot(h, wq_ref[...], preferred_element_type=F32) * (X_DH ** -0.5 * LOG2E)).astype(BF16)
    s = jnp.dot(q, kbd_ref[...], preferred_element_type=F32)
    ps = []
    for hh in range(X_HEADS):
        sh = s[:, hh * n_mem:(hh + 1) * n_mem]
        p = jnp.exp2(sh - jnp.max(sh, axis=-1, keepdims=True))
        ps.append((p / jnp.sum(p, axis=-1, keepdims=True)).astype(BF16))
    o = jnp.dot(jnp.concatenate(ps, axis=1), vbd_ref[...], preferred_element_type=F32)
    o_ref[...] = x + jnp.dot(o.astype(BF16), wo_ref[...], preferred_element_type=F32)


def _cross(x3, g, wq, kbd, vbd, wo, tm):
    b, s, d = x3.shape
    n_mem = kbd.shape[2] // X_HEADS
    full = lambda a: pl.BlockSpec(a.shape, lambda bb, i: (0, 0))
    return pl.pallas_call(
        functools.partial(_cross_kernel, n_mem=n_mem),
        grid=(b, s // tm),
        in_specs=[pl.BlockSpec((None, tm, d), lambda bb, i: (bb, i, 0)), full(g), full(wq),
                  pl.BlockSpec((None,) + kbd.shape[1:], lambda bb, i: (bb, 0, 0)),
                  pl.BlockSpec((None,) + vbd.shape[1:], lambda bb, i: (bb, 0, 0)), full(wo)],
        out_specs=pl.BlockSpec((None, tm, d), lambda bb, i: (bb, i, 0)),
        out_shape=jax.ShapeDtypeStruct((b, s, d), F32),
        compiler_params=_cp(("parallel", "parallel")),
        name="cross_attn",
    )(x3, g, wq, kbd, vbd, wo)


def _block_diag_kv(kv):
    b, m, _ = kv.shape
    kv = kv.reshape(b, m, 2, X_HEADS, X_DH)
    eye = jnp.eye(X_HEADS, dtype=kv.dtype)
    kt = kv[:, :, 0].transpose(0, 2, 3, 1)
    kbd = (kt[:, :, :, None, :] * eye[None, :, None, :, None]).reshape(b, X_HEADS * X_DH, X_HEADS * m)
    vt = kv[:, :, 1].transpose(0, 2, 1, 3)
    vbd = (vt[:, :, :, None, :] * eye[None, :, None, :, None]).reshape(b, X_HEADS * m, X_HEADS * X_DH)
    return kbd, vbd


def _top_half(x):
    return lax.bitcast_convert_type(lax.bitcast_convert_type(x, jnp.uint32) & jnp.uint32(0xFFFF0000), F32)


def _router_kernel(x_ref, g_ref, w_ref, b_ref, h_out, r_out):
    x = x_ref[...]
    h = _rms(x, g_ref[...], 1.0 / x.shape[-1])
    h_out[...] = h.astype(h_out.dtype)
    h_top = _top_half(h)
    h_hi = h_top.astype(BF16)
    h_lo = (h - h_top).astype(BF16)
    both = jnp.dot(h_hi, w_ref[...], preferred_element_type=F32)
    logits = (both[:, :LANES] + both[:, LANES:] + jnp.dot(h_lo, w_ref[:, :LANES], preferred_element_type=F32)
              + b_ref[...])
    lane = lax.broadcasted_iota(jnp.int32, logits.shape, 1)
    lane_f = lane.astype(F32)
    big = jnp.float32(4 * LANES)

    def top(vals, mask):
        mv = jnp.max(jnp.where(mask, vals, -jnp.inf), axis=-1, keepdims=True)
        idx = jnp.min(jnp.where(mask & (vals == mv), lane_f, big), axis=-1, keepdims=True)
        return mv, idx

    g_mask = lane < N_GROUPS
    g_max, g_idx = top(logits, g_mask)
    p_g = 1.0 / jnp.sum(jnp.where(g_mask, jnp.exp(logits - g_max), 0.0), axis=-1, keepdims=True)
    first = N_GROUPS + g_idx * EXPERTS_PER_GROUP
    e_mask = (lane_f >= first) & (lane_f < first + EXPERTS_PER_GROUP)
    v0, i0 = top(logits, e_mask)
    v1, i1 = top(logits, e_mask & (lane_f != i0))
    e1 = jnp.exp(v1 - v0)
    w0 = p_g / (1.0 + e1)
    w1 = p_g * e1 / (1.0 + e1)
    out = jnp.where(lane == 0, i0 - N_GROUPS, 0.0)
    out = jnp.where(lane == 1, i1 - N_GROUPS, out)
    out = jnp.where(lane == 2, w0, out)
    out = jnp.where(lane == 3, w1, out)
    r_out[...] = out


def _router(x2, g, w, bias, tm):
    t, d = x2.shape
    full = lambda a: pl.BlockSpec(a.shape, lambda i: (0, 0))
    return pl.pallas_call(
        _router_kernel,
        grid=(t // tm,),
        in_specs=[pl.BlockSpec((tm, d), lambda i: (i, 0)), full(g), full(w), full(bias)],
        out_specs=[pl.BlockSpec((tm, d), lambda i: (i, 0)), pl.BlockSpec((tm, LANES), lambda i: (i, 0))],
        out_shape=[jax.ShapeDtypeStruct((t, d), BF16), jax.ShapeDtypeStruct((t, LANES), F32)],
        compiler_params=_cp(("parallel",)),
        name="moe_router",
    )(x2, g, w, bias)


def _expert_kernel(blk_e_ref, n_used_ref, x_ref, w1_ref, w3_ref, w2_ref, o_ref, w13_s, w2_s):
    i = pl.program_id(0)
    used = i < n_used_ref[0]
    de = w1_ref.shape[1]

    @pl.when(used & ((i == 0) | (blk_e_ref[i] != blk_e_ref[jnp.maximum(i - 1, 0)])))
    def _():
        w13_s[:, :de] = w1_ref[...].astype(BF16)
        w13_s[:, de:] = w3_ref[...].astype(BF16)
        w2_s[...] = w2_ref[...].astype(BF16)

    @pl.when(used)
    def _():
        hid = jnp.dot(x_ref[...], w13_s[...], preferred_element_type=F32)
        a = hid[:, :de]
        act = (a / (1.0 + jnp.exp(-a))) * hid[:, de:]
        o_ref[...] = jnp.dot(act.astype(BF16), w2_s[...], preferred_element_type=F32).astype(o_ref.dtype)

    @pl.when(jnp.logical_not(used))
    def _():
        o_ref[...] = jnp.zeros(o_ref.shape, o_ref.dtype)


def _experts(blk_e, n_used, xr, w1, w3, w2, layer, rows_per_block):
    rows, d = xr.shape
    de = w1.shape[3]
    weight = lambda w: pl.BlockSpec((None, None) + w.shape[2:], lambda i, be, nu: (layer, be[i], 0, 0))
    grid_spec = pltpu.PrefetchScalarGridSpec(
        num_scalar_prefetch=2,
        grid=(rows // rows_per_block,),
        in_specs=[pl.BlockSpec((rows_per_block, d), lambda i, be, nu: (i, 0)), weight(w1), weight(w3), weight(w2)],
        out_specs=pl.BlockSpec((rows_per_block, d), lambda i, be, nu: (i, 0)),
        scratch_shapes=[pltpu.VMEM((d, 2 * de), BF16), pltpu.VMEM((de, d), BF16)],
    )
    return pl.pallas_call(
        _expert_kernel,
        grid_spec=grid_spec,
        out_shape=jax.ShapeDtypeStruct((rows, d), BF16),
        compiler_params=_cp(("arbitrary",)),
        name="moe_experts",
    )(blk_e, n_used, xr, w1, w3, w2)


def _combine_kernel(x_ref, y0_ref, y1_ref, r_ref, g_ref, o_ref, *, final_norm):
    r = r_ref[...]
    y = x_ref[...] + r[:, 2:3] * y0_ref[...].astype(F32) + r[:, 3:4] * y1_ref[...].astype(F32)
    if final_norm:
        y = _rms(y, g_ref[...], 1.0 / y.shape[-1])
    o_ref[...] = y


def _combine(x2, y0, y1, route, g, final_norm, tm):
    t, d = x2.shape
    row = lambda w: pl.BlockSpec((tm, w), lambda i: (i, 0))
    return pl.pallas_call(
        functools.partial(_combine_kernel, final_norm=final_norm),
        grid=(t // tm,),
        in_specs=[row(d), row(d), row(d), row(LANES), pl.BlockSpec((1, d), lambda i: (0, 0))],
        out_specs=row(d),
        out_shape=jax.ShapeDtypeStruct((t, d), F32),
        compiler_params=_cp(("parallel",)),
        name="moe_combine",
    )(x2, y0, y1, route, g)


def _dispatch(route, rows_per_block):
    t = route.shape[0]
    eid = route[:, :TOP_K].astype(jnp.int32).reshape(-1)
    n = eid.shape[0]
    order = jnp.argsort(eid).astype(jnp.int32)
    rank = jnp.argsort(order).astype(jnp.int32)
    experts = jnp.arange(N_EXPERTS, dtype=jnp.int32)
    counts = jnp.sum((eid[:, None] == experts[None, :]).astype(jnp.int32), axis=0)
    start = jnp.cumsum(counts) - counts
    padded = (counts + rows_per_block - 1) // rows_per_block * rows_per_block
    pend = jnp.cumsum(padded)
    pstart = pend - padded
    dest = (rank + (pstart - start)[eid]).reshape(t, TOP_K)
    n_blocks = n // rows_per_block + N_EXPERTS
    blk_first = jnp.arange(n_blocks, dtype=jnp.int32) * rows_per_block
    blk_e = jnp.minimum(jnp.sum((pend[None, :] <= blk_first[:, None]).astype(jnp.int32), axis=1), N_EXPERTS - 1)
    off = (blk_first - pstart[blk_e])[:, None] + jnp.arange(rows_per_block, dtype=jnp.int32)[None, :]
    src = jnp.clip(start[blk_e][:, None] + off, 0, n - 1)
    row_tok = jnp.where(off < counts[blk_e][:, None], order[src] // TOP_K, 0).reshape(-1)
    n_used = (pend[-1] // rows_per_block).astype(jnp.int32).reshape(1)
    return row_tok, dest, blk_e, n_used


def _rot_cols(w):
    half = w.shape[-1] // 2
    return jnp.concatenate([-w[..., half:], w[..., :half]], axis=-1)


def _pack_w_in(w):
    d = w.shape[0]
    blk = A_HEADS * 2 * A_DH
    n_ab = 6 * blk
    cq = w[:, n_ab:n_ab + C_Q_RANK]
    ckv = w[:, n_ab + C_Q_RANK:n_ab + C_Q_RANK + C_KV_RANK]
    ckr = w[:, n_ab + C_Q_RANK + C_KV_RANK:n_ab + C_Q_RANK + C_KV_RANK + C_ROPE]
    gates = w[:, n_ab + C_Q_RANK + C_KV_RANK + C_ROPE:]
    z = lambda n: jnp.zeros((d, n), w.dtype)
    tail = LANES - C_NOPE - C_ROPE
    packed = jnp.concatenate([w[:, :2 * blk], w[:, 3 * blk:5 * blk], cq, z(LANES), ckv,
                              z(C_NOPE), ckr, z(tail), z(C_NOPE), _rot_cols(ckr), z(tail), gates], axis=1)
    return packed.astype(BF16), w[:, 2 * blk:3 * blk], w[:, 5 * blk:6 * blk]


def _pack_w_uq(w):
    wq = w.reshape(C_Q_RANK, C_HEADS, C_NOPE + C_ROPE)
    pad_rows = 4 * LANES - C_Q_RANK
    tail = LANES - C_NOPE - C_ROPE
    q1 = jnp.pad(wq, ((0, pad_rows), (0, 0), (0, tail))).reshape(4 * LANES, C_HEADS * LANES)
    q2 = jnp.pad(_rot_cols(wq[:, :, C_NOPE:]), ((0, pad_rows), (0, 0), (C_NOPE, tail))).reshape(4 * LANES, C_HEADS * LANES)
    return q1.astype(BF16), q2.astype(BF16)


def _pack_w_ukv(w):
    wkv = w.reshape(C_KV_RANK, C_HEADS, C_NOPE + C_DV)
    wk = jnp.pad(wkv[:, :, :C_NOPE], ((0, 0), (0, 0), (0, LANES - C_NOPE))).reshape(C_KV_RANK, C_HEADS * LANES)
    wv = wkv[:, :, C_NOPE:].reshape(C_KV_RANK, C_HEADS * C_DV)
    return wk.astype(BF16), wv


def kernel(x, mem, positions, mix_norm_g, w_in, diff_lambda, diff_subln_g, mla_q_norm_g, w_uq, mla_kv_norm_g, w_ukv, w_branch, w_out, cross_norm_g, mem_norm_g, w_xq, w_xkv, w_xo, ffn_norm_g, w_group, b_group, w_router, b_router, w1, w3, w2, final_norm_g):
    b, s, d = x.shape
    depth = w_in.shape[0]
    t = b * s
    n_mem = mem.shape[1]
    tl = _tiles(s)
    assert PROJ_COLS == COL_G * LANES + N_BRANCHES * d and s % tl["t_b"] == 0 and tl["t_b"] == tl["tq_a"]

    pos_f = positions.astype(F32)
    posq_row = pos_f.reshape(b, 1, s)
    posk_rep = jnp.broadcast_to(pos_f[:, :, None], (b, s, LANES))
    cos, sin = _rope_tables(pos_f.reshape(t, 1), tl["tm_tok"])
    slopes_a2 = jnp.exp2(-8.0 * jnp.arange(1, A_HEADS + 1, dtype=F32) / A_HEADS) * LOG2E
    reach = max((w // (2 * dl)) * dl for w, dl in B_PATTERNS)
    b_bias = _b_bias_tables(tl["t_b"], 2 * (-(-reach // tl["t_b"])) + 1)
    mem2 = mem.reshape(b * n_mem, d)

    x2 = x.reshape(t, d)
    for l in range(depth):
        w_main, w_av, w_bv = _pack_w_in(w_in[l])
        proj2, vt_a, vt_b = _in_proj(x2, mix_norm_g[l], w_main, *_vt_weights(w_av, 2 * A_DH), *_vt_weights(w_bv, B_DH),
                                     tl["tm_proj"], tl["tn_proj"])
        proj3 = proj2.reshape(b, s, PROJ_COLS)
        lq = diff_lambda[l].astype(F32)
        lam_init = 0.8 - 0.6 * math.exp(-0.3 * l)
        lam = (jnp.exp(jnp.sum(lq[0] * lq[1])) - jnp.exp(jnp.sum(lq[2] * lq[3])) + lam_init).reshape(1)
        norms = _block_norms(proj3, tl["t_b"])
        oa = _attn_a(proj3, vt_a, norms, posq_row, posk_rep, lam, slopes_a2, diff_subln_g[l], 1.0 - lam_init,
                     tl["tq_a"], tl["tk_a"])
        ob = _attn_b(proj3, vt_b, norms, b_bias, tl["t_b"])
        wq1, wq2 = _pack_w_uq(w_uq[l])
        wk, wv = _pack_w_ukv(w_ukv[l])
        gq = jnp.pad(mla_q_norm_g[l], (0, 4 * LANES - C_Q_RANK)).reshape(1, 4 * LANES)
        qc, kc, vt_c = _mla_prep(proj2, cos, sin, gq, mla_kv_norm_g[l].reshape(1, C_KV_RANK),
                                 wq1, wq2, wk, *_vt_weights(wv, C_DV), tl["tm_tok"])
        oc = _attn_c(qc.reshape(b, s, -1), kc.reshape(b, s, -1), vt_c, tl["tq_c"], tl["tk_c"])
        x2 = _merge(oa.reshape(t, -1), ob.reshape(t, -1), oc.reshape(t, -1), proj2, x2,
                    w_branch[l].astype(BF16), w_out[l].astype(BF16), tl["tm_tok"])
        kv = _norm_matmul(mem2, mem_norm_g[l], w_xkv[l].astype(BF16), min(1024, b * n_mem), w_xkv.shape[2])
        kbd, vbd = _block_diag_kv(kv.reshape(b, n_mem, -1))
        x2 = _cross(x2.reshape(b, s, d), cross_norm_g[l].reshape(1, d), w_xq[l].astype(BF16), kbd, vbd,
                    w_xo[l].astype(BF16), tl["tm_tok"]).reshape(t, d)
        w_r = jnp.pad(jnp.concatenate([w_group[l], w_router[l]], axis=1), ((0, 0), (0, LANES - N_GROUPS - N_EXPERTS)))
        b_r = jnp.pad(jnp.concatenate([b_group[l], b_router[l]]), (0, LANES - N_GROUPS - N_EXPERTS)).reshape(1, LANES)
        w_r2 = jnp.concatenate([_top_half(w_r).astype(BF16), (w_r - _top_half(w_r)).astype(BF16)], axis=1)
        h, route = _router(x2, ffn_norm_g[l].reshape(1, d), w_r2, b_r, tl["tm_moe"])
        row_tok, dest, blk_e, n_used = _dispatch(route, tl["moe_rows"])
        yr = _experts(blk_e, n_used, h[row_tok], w1, w3, w2, l, tl["moe_rows"])
        x2 = _combine(x2, yr[dest[:, 0]], yr[dest[:, 1]], route, final_norm_g.reshape(1, d),
                      l == depth - 1, tl["tm_moe"])
    return x2.reshape(b, s, d)
```

```python
import functools
import math

import jax
import jax.numpy as jnp
from jax import lax
from jax.experimental import pallas as pl
from jax.experimental.pallas import tpu as pltpu

F32 = jnp.float32
BF16 = jnp.bfloat16

LANES = 128
NORM_EPS = 1e-6
LOG2E = math.log2(math.e)
NEG_BIG = -1e30

A_HEADS, A_DH = 4, 64
B_HEADS, B_DH = 8, 64
B_PATTERNS = ((128, 1), (512, 4), (2048, 16))
C_HEADS, C_Q_RANK, C_KV_RANK, C_NOPE, C_ROPE, C_DV = 8, 384, 256, 64, 32, 64
ROPE_THETA = 10000.0
N_BRANCHES = 3
X_HEADS, X_DH = 4, 64
N_GROUPS, EXPERTS_PER_GROUP, TOP_K = 4, 8, 2
N_EXPERTS = N_GROUPS * EXPERTS_PER_GROUP

COL_A = 0
COL_B = 8
COL_CQ = 16
COL_CKV = 20
COL_R1 = 22
COL_R2 = 23
COL_G = 24
PROJ_COLS = 48 * LANES

VMEM_LIMIT = 48 * 1024 * 1024


def _tiles(seq):
    return dict(
        tm_proj=min(1024, seq), tn_proj=1024,
        tm_tok=min(512, seq),
        tm_moe=min(1024, seq),
        tq_a=min(512, seq), tk_a=min(512, seq),
        t_b=min(512, seq),
        tq_c=min(512, seq), tk_c=min(512, seq),
        moe_rows=512,
    )


def _cp(sem):
    return pltpu.CompilerParams(dimension_semantics=sem, vmem_limit_bytes=VMEM_LIMIT)


def _rms(x, g, inv_n):
    ms = jnp.sum(x * x, axis=-1, keepdims=True) * inv_n
    return x * lax.rsqrt(ms + NORM_EPS) * g


def _norm_matmul_kernel(x_ref, g_ref, w_ref, o_ref, h_ref, *, inv_n):
    @pl.when(pl.program_id(1) == 0)
    def _():
        h_ref[...] = _rms(x_ref[...].astype(F32), g_ref[...], inv_n).astype(BF16)

    o_ref[...] = jnp.dot(h_ref[...], w_ref[...], preferred_element_type=F32).astype(o_ref.dtype)


def _norm_matmul(x, g, w, tm, tn, out_dtype=BF16):
    m, k = x.shape
    n = w.shape[1]
    return pl.pallas_call(
        functools.partial(_norm_matmul_kernel, inv_n=1.0 / k),
        grid=(m // tm, n // tn),
        in_specs=[pl.BlockSpec((tm, k), lambda i, j: (i, 0)),
                  pl.BlockSpec((1, k), lambda i, j: (0, 0)),
                  pl.BlockSpec((k, tn), lambda i, j: (0, j))],
        out_specs=pl.BlockSpec((tm, tn), lambda i, j: (i, j)),
        out_shape=jax.ShapeDtypeStruct((m, n), out_dtype),
        scratch_shapes=[pltpu.VMEM((tm, k), BF16)],
        compiler_params=_cp(("parallel", "arbitrary")),
        name="norm_matmul",
    )(x, g.reshape(1, k), w)


def _in_proj_kernel(x_ref, g_ref, w_ref, wta_ref, wtb_ref, ona_ref, onb_ref, o_ref, vta_ref, vtb_ref, h_ref, *, inv_n):
    @pl.when(pl.program_id(1) == 0)
    def _():
        h = _rms(x_ref[...].astype(F32), g_ref[...], inv_n).astype(BF16)
        h_ref[...] = h
        vta_ref[...] = (_kq(wta_ref[...], h) + ona_ref[...]).astype(vta_ref.dtype)
        vtb_ref[...] = (_kq(wtb_ref[...], h) + onb_ref[...]).astype(vtb_ref.dtype)

    o_ref[...] = jnp.dot(h_ref[...], w_ref[...], preferred_element_type=F32).astype(o_ref.dtype)


def _in_proj(x, g, w, wta, ona, wtb, onb, tm, tn):
    m, k = x.shape
    n = w.shape[1]
    full = lambda a: pl.BlockSpec(a.shape, lambda i, j: (0, 0))
    return pl.pallas_call(
        functools.partial(_in_proj_kernel, inv_n=1.0 / k),
        grid=(m // tm, n // tn),
        in_specs=[pl.BlockSpec((tm, k), lambda i, j: (i, 0)),
                  pl.BlockSpec((1, k), lambda i, j: (0, 0)),
                  pl.BlockSpec((k, tn), lambda i, j: (0, j)),
                  full(wta), full(wtb), full(ona), full(onb)],
        out_specs=[pl.BlockSpec((tm, tn), lambda i, j: (i, j)),
                   pl.BlockSpec((wta.shape[0], tm), lambda i, j: (0, i)),
                   pl.BlockSpec((wtb.shape[0], tm), lambda i, j: (0, i))],
        out_shape=[jax.ShapeDtypeStruct((m, n), BF16),
                   jax.ShapeDtypeStruct((wta.shape[0], m), BF16),
                   jax.ShapeDtypeStruct((wtb.shape[0], m), BF16)],
        scratch_shapes=[pltpu.VMEM((tm, k), BF16)],
        compiler_params=_cp(("parallel", "arbitrary")),
        name="in_proj",
    )(x, g.reshape(1, k), w, wta, wtb, ona, onb)


ONES_ROWS = 16


def _vt_weights(wv, dv):
    k, w = wv.shape
    wt = jnp.pad(wv.T.reshape(w // dv, dv, k), ((0, 0), (0, ONES_ROWS), (0, 0))).reshape(-1, k)
    ones = jnp.zeros((w // dv, dv + ONES_ROWS, 1), F32).at[:, dv, 0].set(1.0).reshape(-1, 1)
    return wt.astype(BF16), ones


def _split_q(q_ref, qs_ref, scale):
    q = q_ref[...].astype(F32) * scale
    lane = lax.broadcasted_iota(jnp.int32, q.shape, 1)
    qs_ref[0] = jnp.where(lane < LANES // 2, q, 0.0).astype(BF16)
    qs_ref[1] = jnp.where(lane >= LANES // 2, q, 0.0).astype(BF16)


def _kq(k, q):
    return lax.dot_general(k, q, (((1,), (1,)), ((), ())), preferred_element_type=F32)


def _put_scores(s, s_ref, cm_ref, slot, c):
    s_ref[slot, c] = s
    cm_ref[slot, c] = jnp.max(s, axis=0, keepdims=True)


def _update(vt_of, s_ref, cm_ref, m_ref, acc_ref, slot):
    for c in range(2):
        m_prev = m_ref[c]
        m_new = jnp.maximum(m_prev, cm_ref[slot, c])
        alpha = jnp.exp2(m_prev - m_new)
        p = jnp.exp2(s_ref[slot, c] - m_new).astype(BF16)
        acc_ref[c] = alpha * acc_ref[c] + jnp.dot(vt_of(c), p, preferred_element_type=F32)
        m_ref[c] = m_new


def _init_state(m_ref, acc_ref):
    m_ref[...] = jnp.full(m_ref.shape, NEG_BIG, F32)
    acc_ref[...] = jnp.zeros(acc_ref.shape, F32)


def _pipelined(n, scores, update, unroll):
    scores(0, 0)

    def body(jj, carry):
        j = unroll * jj
        for u in range(unroll):
            scores(j + u + 1, (u + 1) % 2)
            update(j + u, u % 2)
        return carry

    looped = (n - 1) // unroll
    lax.fori_loop(0, looped, body, 0)
    for j in range(unroll * looped, n):
        if j + 1 < n:
            scores(j + 1, (j + 1) % 2)
        update(j, j % 2)


def _pipelined_list(n, chunk, scores, update):
    scores(chunk(0), 0)

    def body(unroll, first):
        def run(jj, carry):
            t = first + unroll * jj
            for u in range(unroll):
                scores(chunk(t + u + 1), (u + 1) % 2)
                update(chunk(t + u), u % 2)
            return carry
        return run

    quads = (n - 1) // 4
    lax.fori_loop(0, quads, body(4, 0), 0)
    pairs = (n - 1 - 4 * quads) // 2
    lax.fori_loop(0, pairs, body(2, 4 * quads), 0)
    left = n - 4 * quads - 2 * pairs

    @pl.when(left == 2)
    def _():
        scores(chunk(n - 1), 1)
        update(chunk(n - 2), 0)
        update(chunk(n - 1), 1)

    @pl.when(left == 1)
    def _():
        update(chunk(n - 1), 0)


def _normalized(acc_ref, c):
    dv = acc_ref.shape[1] - ONES_ROWS
    return acc_ref[c, :dv, :] / acc_ref[c, dv:dv + 1, :]


def _attn_scratch(tq, tk, dv):
    return [pltpu.VMEM((2, 1, tq), F32), pltpu.VMEM((2, dv + ONES_ROWS, tq), F32),
            pltpu.VMEM((2, 2, tk, tq), F32), pltpu.VMEM((2, 2, 1, tq), F32)]


SKIP_MARGIN = 128.0


def _block_norms_kernel(qk_ref, sel_ref, o_ref):
    x = qk_ref[...].astype(F32)
    lane = lax.broadcasted_iota(jnp.int32, x.shape, 1)
    is_query = (lane // (A_HEADS * 2 * A_DH)) % 2 == 0
    x = jnp.where(is_query, (x * (A_DH ** -0.5 * LOG2E)).astype(BF16).astype(F32), x)
    sums = jnp.dot(x * x, sel_ref[...], preferred_element_type=F32)
    o_ref[...] = jnp.sqrt(jnp.max(sums, axis=0, keepdims=True))


def _block_norms(proj3, t):
    b, s, _ = proj3.shape
    assert COL_B == COL_A + 2 * A_HEADS and A_DH == B_DH
    w = 4 * A_HEADS * 2 * A_DH
    sel = (jnp.arange(w)[:, None] // A_DH == jnp.arange(LANES)[None, :]).astype(BF16)
    norms = pl.pallas_call(
        _block_norms_kernel,
        grid=(b, s // t),
        in_specs=[pl.BlockSpec((None, t, w), lambda bb, i: (bb, i, COL_A * LANES // w)),
                  pl.BlockSpec((w, LANES), lambda bb, i: (0, 0))],
        out_specs=pl.BlockSpec((None, None, 1, LANES), lambda bb, i: (bb, i, 0, 0)),
        out_shape=jax.ShapeDtypeStruct((b, s // t, 1, LANES), F32),
        compiler_params=_cp(("parallel", "parallel")),
        name="block_norms",
    )(proj3, sel)
    return norms[:, :, 0, :w // A_DH].reshape(b, s // t, 4, 2 * A_HEADS)


def _position_bounds(pos_f, t):
    b, s = pos_f.shape
    p = pos_f.reshape(b, s // t, t)
    qlo, qhi = jnp.min(p, -1)[:, :, None], jnp.max(p, -1)[:, :, None]
    klo, khi = jnp.min(p, -1)[:, None, :], jnp.max(p, -1)[:, None, :]
    dmin = jnp.maximum(jnp.maximum(klo - qhi, qlo - khi), 0.0)
    dmax = jnp.maximum(khi - qlo, qhi - klo)
    cover = jnp.max(jnp.min(jnp.abs(p[:, :, :, None] - p[:, :, None, :]), axis=-1), axis=-1)
    j = jnp.arange(s // t)
    return dmin, jnp.where(j[:, None] == j[None, :], jnp.minimum(dmax, cover[:, :, None]), dmax)


def _a_chunk_lists(norms, bounds, slopes2):
    dmin, reach = bounds
    b, nk = dmin.shape[0], dmin.shape[2]
    norms = norms[:, :, :2].reshape(b, nk, 2, A_HEADS, 2).max(axis=-1)
    qn = norms[:, :, 0].transpose(0, 2, 1)
    kn = norms[:, :, 1].transpose(0, 2, 1)
    qk = 1.01 * qn[:, :, :, None] * kn[:, :, None, :] + 1.0
    sl = slopes2[None, :, None, None]
    lower = -qk - sl * reach[:, None]
    upper = qk - sl * dmin[:, None]
    j = jnp.arange(nk, dtype=jnp.int32)
    first = jnp.argmax(lower, axis=-1).astype(jnp.int32)[..., None]
    keep = (upper >= jnp.max(lower, axis=-1, keepdims=True) - SKIP_MARGIN) | (j == first)
    lst = jnp.argsort(jnp.where(j == first, -1, jnp.where(keep, j, nk + j)), axis=-1).astype(jnp.int32)
    cnt = jnp.sum(keep.astype(jnp.int32), axis=-1)
    return lst.reshape(-1), cnt.reshape(-1)


def _attn_a_kernel(lst_ref, cnt_ref, lam_ref, slope_ref, q_ref, k_ref, vt_ref, pq_ref, pk_ref, g_ref, o_ref,
                   qs_ref, m_ref, acc_ref, s_ref, cm_ref, *, tk, post_scale):
    tq = q_ref.shape[0]
    nk = k_ref.shape[0] // tk
    blk = (pl.program_id(0) * pl.num_programs(1) + pl.program_id(1)) * pl.num_programs(2) + pl.program_id(2)
    sl2 = slope_ref[pl.program_id(1)]
    _split_q(q_ref, qs_ref, A_DH ** -0.5 * LOG2E)
    _init_state(m_ref, acc_ref)
    pq = pq_ref[...] * sl2

    def scores(j, slot):
        ks = pl.multiple_of(j * tk, tk)
        k = k_ref[pl.ds(ks, tk), :]
        pk = pk_ref[pl.ds(ks, tk), :] * sl2
        bias = jnp.abs(jnp.concatenate([pk] * (tq // LANES), axis=1) - pq)
        for c in range(2):
            _put_scores(_kq(k, qs_ref[c]) - bias, s_ref, cm_ref, slot, c)

    def update(j, slot):
        ks = pl.multiple_of(j * tk, tk)
        _update(lambda c: vt_ref[:, pl.ds(ks, tk)], s_ref, cm_ref, m_ref, acc_ref, slot)

    _pipelined_list(cnt_ref[blk], lambda t: lst_ref[blk * nk + t], scores, update)
    o = (_normalized(acc_ref, 0) - lam_ref[0] * _normalized(acc_ref, 1)).T
    o_ref[...] = (_rms(o, g_ref[...], 1.0 / LANES) * post_scale).astype(o_ref.dtype)


def _attn_a(proj, vt, norms, bounds, posq_row, posk_rep, lam, slopes2, g, post_scale, tq, tk):
    b, s, _ = proj.shape
    assert tq == tk
    lst, cnt = _a_chunk_lists(norms, bounds, slopes2)
    smem = pl.BlockSpec(memory_space=pltpu.SMEM)
    grid_spec = pltpu.PrefetchScalarGridSpec(
        num_scalar_prefetch=2,
        grid=(b, A_HEADS, s // tq),
        in_specs=[smem, smem,
                  pl.BlockSpec((None, tq, LANES), lambda bb, h, i, *_: (bb, i, COL_A + h)),
                  pl.BlockSpec((None, s, LANES), lambda bb, h, i, *_: (bb, 0, COL_A + A_HEADS + h)),
                  pl.BlockSpec((vt.shape[0] // A_HEADS, s), lambda bb, h, i, *_: (h, bb)),
                  pl.BlockSpec((None, 1, tq), lambda bb, h, i, *_: (bb, 0, i)),
                  pl.BlockSpec((None, s, LANES), lambda bb, h, i, *_: (bb, 0, 0)),
                  pl.BlockSpec((1, LANES), lambda bb, h, i, *_: (0, 0))],
        out_specs=pl.BlockSpec((None, tq, LANES), lambda bb, h, i, *_: (bb, i, h)),
        scratch_shapes=[pltpu.VMEM((2, tq, LANES), BF16)] + _attn_scratch(tq, tk, LANES),
    )
    return pl.pallas_call(
        functools.partial(_attn_a_kernel, tk=tk, post_scale=post_scale),
        grid_spec=grid_spec,
        out_shape=jax.ShapeDtypeStruct((b, s, A_HEADS * LANES), BF16),
        compiler_params=_cp(("parallel", "parallel", "arbitrary")),
        name="attn_diff",
    )(lst, cnt, lam, slopes2, proj, proj, vt, posq_row, posk_rep, g.reshape(1, LANES))


def _b_chunk_lists(norms, nband, t):
    b, nq = norms.shape[:2]
    half = nband // 2
    norms = norms[:, :, 2:]
    jj = jnp.arange(nband, dtype=jnp.int32)
    kb = jnp.arange(nq, dtype=jnp.int32)[:, None] + jj[None, :] - half
    inside = (kb >= 0) & (kb < nq)
    qk = 1.01 * norms[:, :, None, 0, :] * norms[:, jnp.clip(kb, 0, nq - 1), 1, :] + 1.0
    dmin = jnp.where(jj == half, 0, (jnp.abs(jj - half) - 1) * t + 1).astype(F32)
    slopes2 = jnp.exp2(-8.0 * jnp.arange(1, B_HEADS + 1, dtype=F32) / B_HEADS) * LOG2E
    upper = qk - slopes2 * dmin[:, None] + math.log2(len(B_PATTERNS))
    keep = upper >= -qk[:, :, half:half + 1, :] - SKIP_MARGIN
    keep = jnp.any(keep.reshape(b, nq, nband, B_HEADS // 2, 2), axis=-1).transpose(0, 3, 1, 2)
    keep = (keep & inside) | (jj == half)
    lst = jnp.argsort(jnp.where(jj == half, -1, jnp.where(keep, jj, nband + jj)), axis=-1).astype(jnp.int32)
    return lst.reshape(-1), jnp.sum(keep.astype(jnp.int32), axis=-1).reshape(-1)


def _attn_b_kernel(lst_ref, cnt_ref, q_ref, k_ref, vt_ref, bias_ref, o_ref, qs_ref, m_ref, acc_ref, s_ref, cm_ref,
                   *, nband):
    t = q_ref.shape[0]
    rows = vt_ref.shape[0] // 2
    i = pl.program_id(2)
    blk = (pl.program_id(0) * pl.num_programs(1) + pl.program_id(1)) * pl.num_programs(2) + i
    _split_q(q_ref, qs_ref, B_DH ** -0.5 * LOG2E)
    _init_state(m_ref, acc_ref)

    def start(jj):
        return pl.multiple_of((i + jj - nband // 2) * t, t)

    def scores(jj, slot):
        k = k_ref[pl.ds(start(jj), t), :]
        for c in range(2):
            _put_scores(_kq(k, qs_ref[c]) + bias_ref[c, jj], s_ref, cm_ref, slot, c)

    def update(jj, slot):
        ks = start(jj)
        _update(lambda c: vt_ref[c * rows:(c + 1) * rows, pl.ds(ks, t)], s_ref, cm_ref, m_ref, acc_ref, slot)

    _pipelined_list(cnt_ref[blk], lambda n: lst_ref[blk * nband + n], scores, update)
    o = jnp.concatenate([_normalized(acc_ref, 0), _normalized(acc_ref, 1)], axis=0)
    o_ref[...] = o.T.astype(o_ref.dtype)


def _b_bias_tables(t, nband):
    half = nband // 2
    r = jnp.arange(t, dtype=jnp.int32)[None, :, None]
    c = jnp.arange(t, dtype=jnp.int32)[None, None, :]
    jj = jnp.arange(nband, dtype=jnp.int32)[:, None, None]
    ao = jnp.abs((jj - half) * t + r - c)
    mult = jnp.zeros(ao.shape, jnp.int32)
    for window, dilation in B_PATTERNS:
        reach = (window // (2 * dilation)) * dilation
        mult = mult + ((ao % dilation == 0) & (ao <= reach)).astype(jnp.int32)
    slopes = jnp.exp2(-8.0 * jnp.arange(1, B_HEADS + 1, dtype=F32) / B_HEADS) * LOG2E
    bias = jnp.log2(jnp.maximum(mult, 1).astype(F32))[None] - slopes[:, None, None, None] * ao.astype(F32)[None]
    bias = jnp.where((mult > 0)[None], bias, NEG_BIG)
    return bias.reshape(B_HEADS // 2, 2, nband, t, t)


def _attn_b(proj, vt, norms, bias, t):
    b, s, _ = proj.shape
    nband = bias.shape[2]
    npair = B_HEADS // 2
    lst, cnt = _b_chunk_lists(norms, nband, t)
    grid_spec = pltpu.PrefetchScalarGridSpec(
        num_scalar_prefetch=2,
        grid=(b, npair, s // t),
        in_specs=[pl.BlockSpec((None, t, LANES), lambda bb, p, i, *_: (bb, i, COL_B + p)),
                  pl.BlockSpec((None, s, LANES), lambda bb, p, i, *_: (bb, 0, COL_B + npair + p)),
                  pl.BlockSpec((vt.shape[0] // npair, s), lambda bb, p, i, *_: (p, bb)),
                  pl.BlockSpec((None,) + bias.shape[1:], lambda bb, p, i, *_: (p, 0, 0, 0, 0))],
        out_specs=pl.BlockSpec((None, t, LANES), lambda bb, p, i, *_: (bb, i, p)),
        scratch_shapes=[pltpu.VMEM((2, t, LANES), BF16)] + _attn_scratch(t, t, B_DH),
    )
    return pl.pallas_call(
        functools.partial(_attn_b_kernel, nband=nband),
        grid_spec=grid_spec,
        out_shape=jax.ShapeDtypeStruct((b, s, npair * LANES), BF16),
        compiler_params=_cp(("parallel", "parallel", "arbitrary")),
        name="attn_dilated",
    )(lst, cnt, proj, proj, vt, bias)


def _rope_table_kernel(pos_ref, invf_ref, c_ref, s_ref):
    ang = pos_ref[...] * invf_ref[...]
    c_ref[...] = jnp.cos(ang)
    s_ref[...] = jnp.sin(ang)


def _rope_tables(pos_col, tm):
    t = pos_col.shape[0]
    inv = ROPE_THETA ** (-jnp.arange(0, C_ROPE, 2, dtype=F32) / C_ROPE)
    invf = jnp.concatenate([jnp.zeros((C_NOPE,), F32), inv, inv,
                            jnp.zeros((LANES - C_NOPE - C_ROPE,), F32)]).reshape(1, LANES)
    spec = pl.BlockSpec((tm, LANES), lambda i: (i, 0))
    return pl.pallas_call(
        _rope_table_kernel,
        grid=(t // tm,),
        in_specs=[pl.BlockSpec((tm, 1), lambda i: (i, 0)), pl.BlockSpec((1, LANES), lambda i: (0, 0))],
        out_specs=[spec, spec],
        out_shape=[jax.ShapeDtypeStruct((t, LANES), F32)] * 2,
        compiler_params=_cp(("parallel",)),
        name="rope_tables",
    )(pos_col, invf)


def _mla_prep_kernel(cq_ref, ckv_ref, r1_ref, r2_ref, c_ref, s_ref, gq_ref, gkv_ref,
                     wq1_ref, wq2_ref, wk_ref, wvt_ref, onv_ref, q_out, k_out, vt_out):
    qn = _rms(cq_ref[...].astype(F32), gq_ref[...], 1.0 / C_Q_RANK).astype(BF16)
    kvn = _rms(ckv_ref[...].astype(F32), gkv_ref[...], 1.0 / C_KV_RANK).astype(BF16)
    q1 = jnp.dot(qn, wq1_ref[...], preferred_element_type=F32)
    q2 = jnp.dot(qn, wq2_ref[...], preferred_element_type=F32)
    k1 = jnp.dot(kvn, wk_ref[...], preferred_element_type=F32)
    vt_out[...] = (_kq(wvt_ref[...], kvn) + onv_ref[...]).astype(vt_out.dtype)
    cos = c_ref[...]
    sin = s_ref[...]
    k_rope = r1_ref[...].astype(F32) * cos + r2_ref[...].astype(F32) * sin
    scale = (C_NOPE + C_ROPE) ** -0.5 * LOG2E
    for h in range(C_HEADS):
        sl = slice(h * LANES, (h + 1) * LANES)
        q_out[:, sl] = ((q1[:, sl] * cos + q2[:, sl] * sin) * scale).astype(q_out.dtype)
        k_out[:, sl] = (k1[:, sl] + k_rope).astype(k_out.dtype)


def _mla_prep(proj2, cos, sin, gq, gkv, wq1, wq2, wk, wvt, onv, tm):
    t = proj2.shape[0]
    full = lambda a: pl.BlockSpec(a.shape, lambda i: (0, 0))
    row = lambda w: pl.BlockSpec((tm, w), lambda i: (i, 0))
    return pl.pallas_call(
        _mla_prep_kernel,
        grid=(t // tm,),
        in_specs=[pl.BlockSpec((tm, 4 * LANES), lambda i: (i, COL_CQ // 4)),
                  pl.BlockSpec((tm, 2 * LANES), lambda i: (i, COL_CKV // 2)),
                  pl.BlockSpec((tm, LANES), lambda i: (i, COL_R1)),
                  pl.BlockSpec((tm, LANES), lambda i: (i, COL_R2)),
                  row(LANES), row(LANES), full(gq), full(gkv), full(wq1), full(wq2), full(wk), full(wvt), full(onv)],
        out_specs=[row(C_HEADS * LANES), row(C_HEADS * LANES), pl.BlockSpec((wvt.shape[0], tm), lambda i: (0, i))],
        out_shape=[jax.ShapeDtypeStruct((t, C_HEADS * LANES), BF16),
                   jax.ShapeDtypeStruct((t, C_HEADS * LANES), BF16),
                   jax.ShapeDtypeStruct((wvt.shape[0], t), BF16)],
        compiler_params=_cp(("parallel",)),
        name="mla_prep",
    )(proj2, proj2, proj2, proj2, cos, sin, gq, gkv, wq1, wq2, wk, wvt, onv)


def _attn_c_kernel(q_ref, k_ref, vt_ref, o_ref, m_ref, acc_ref, s_ref, cm_ref, *, tk):
    rows = vt_ref.shape[0] // 2
    _init_state(m_ref, acc_ref)

    def scores(j, slot):
        ks = pl.multiple_of(j * tk, tk)
        for c in range(2):
            sl = slice(c * LANES, (c + 1) * LANES)
            _put_scores(_kq(k_ref[pl.ds(ks, tk), sl], q_ref[:, sl]), s_ref, cm_ref, slot, c)

    def update(j, slot):
        ks = pl.multiple_of(j * tk, tk)
        _update(lambda c: vt_ref[c * rows:(c + 1) * rows, pl.ds(ks, tk)], s_ref, cm_ref, m_ref, acc_ref, slot)

    _pipelined(k_ref.shape[0] // tk, scores, update, unroll=4)
    o = jnp.concatenate([_normalized(acc_ref, 0), _normalized(acc_ref, 1)], axis=0)
    o_ref[...] = o.T.astype(o_ref.dtype)


def _attn_c(q, k, vt, tq, tk):
    b, s, _ = q.shape
    npair = C_HEADS // 2
    return pl.pallas_call(
        functools.partial(_attn_c_kernel, tk=tk),
        grid=(b, npair, s // tq),
        in_specs=[pl.BlockSpec((None, tq, 2 * LANES), lambda bb, p, i: (bb, i, p)),
                  pl.BlockSpec((None, s, 2 * LANES), lambda bb, p, i: (bb, 0, p)),
                  pl.BlockSpec((vt.shape[0] // npair, s), lambda bb, p, i: (p, bb))],
        out_specs=pl.BlockSpec((None, tq, LANES), lambda bb, p, i: (bb, i, p)),
        out_shape=jax.ShapeDtypeStruct((b, s, npair * LANES), BF16),
        scratch_shapes=_attn_scratch(tq, tk, C_DV),
        compiler_params=_cp(("parallel", "parallel", "arbitrary")),
        name="attn_latent",
    )(q, k, vt)


def _merge_kernel(oa_ref, ob_ref, oc_ref, g0_ref, g1_ref, g2_ref, x_ref, wb_ref, wo_ref, o_ref):
    z = None
    for n, (o_r, g_r) in enumerate(((oa_ref, g0_ref), (ob_ref, g1_ref), (oc_ref, g2_ref))):
        br = jnp.dot(o_r[...], wb_ref[n], preferred_element_type=F32)
        gate = 1.0 / (1.0 + jnp.exp(-g_r[...].astype(F32)))
        z = gate * br if z is None else z + gate * br
    o_ref[...] = x_ref[...] + jnp.dot(z.astype(BF16), wo_ref[...], preferred_element_type=F32)


def _merge(oa, ob, oc, proj2, x2, wb, wo, tm):
    t, d = x2.shape
    bw = oa.shape[1]
    row = lambda w: pl.BlockSpec((tm, w), lambda i: (i, 0))
    gate = lambda n: pl.BlockSpec((tm, d), lambda i: (i, COL_G * LANES // d + n))
    return pl.pallas_call(
        _merge_kernel,
        grid=(t // tm,),
        in_specs=[row(bw), row(bw), row(bw), gate(0), gate(1), gate(2), row(d),
                  pl.BlockSpec(wb.shape, lambda i: (0, 0, 0)), pl.BlockSpec(wo.shape, lambda i: (0, 0))],
        out_specs=row(d),
        out_shape=jax.ShapeDtypeStruct((t, d), F32),
        compiler_params=_cp(("parallel",)),
        name="branch_merge",
    )(oa, ob, oc, proj2, proj2, proj2, x2, wb, wo)


def _cross_kernel(x_ref, g_ref, wq_ref, kbd_ref, vbd_ref, wo_ref, o_ref, *, n_mem):
    x = x_ref[...]
    h = _rms(x, g_ref[...], 1.0 / x.shape[-1]).astype(BF16)
    q = (jnp.dot(h, wq_ref[...], preferred_element_type=F32) * (X_DH ** -0.5 * LOG2E)).astype(BF16)
    s = jnp.dot(q, kbd_ref[...], preferred_element_type=F32)
    ps = []
    for hh in range(X_HEADS):
        sh = s[:, hh * n_mem:(hh + 1) * n_mem]
        p = jnp.exp2(sh - jnp.max(sh, axis=-1, keepdims=True))
        ps.append((p / jnp.sum(p, axis=-1, keepdims=True)).astype(BF16))
    o = jnp.dot(jnp.concatenate(ps, axis=1), vbd_ref[...], preferred_element_type=F32)
    o_ref[...] = x + jnp.dot(o.astype(BF16), wo_ref[...], preferred_element_type=F32)


def _cross(x3, g, wq, kbd, vbd, wo, tm):
    b, s, d = x3.shape
    n_mem = kbd.shape[2] // X_HEADS
    full = lambda a: pl.BlockSpec(a.shape, lambda bb, i: (0, 0))
    return pl.pallas_call(
        functools.partial(_cross_kernel, n_mem=n_mem),
        grid=(b, s // tm),
        in_specs=[pl.BlockSpec((None, tm, d), lambda bb, i: (bb, i, 0)), full(g), full(wq),
                  pl.BlockSpec((None,) + kbd.shape[1:], lambda bb, i: (bb, 0, 0)),
                  pl.BlockSpec((None,) + vbd.shape[1:], lambda bb, i: (bb, 0, 0)), full(wo)],
        out_specs=pl.BlockSpec((None, tm, d), lambda bb, i: (bb, i, 0)),
        out_shape=jax.ShapeDtypeStruct((b, s, d), F32),
        compiler_params=_cp(("parallel", "parallel")),
        name="cross_attn",
    )(x3, g, wq, kbd, vbd, wo)


def _block_diag_kv(kv):
    b, m, _ = kv.shape
    kv = kv.reshape(b, m, 2, X_HEADS, X_DH)
    eye = jnp.eye(X_HEADS, dtype=kv.dtype)
    kt = kv[:, :, 0].transpose(0, 2, 3, 1)
    kbd = (kt[:, :, :, None, :] * eye[None, :, None, :, None]).reshape(b, X_HEADS * X_DH, X_HEADS * m)
    vt = kv[:, :, 1].transpose(0, 2, 1, 3)
    vbd = (vt[:, :, :, None, :] * eye[None, :, None, :, None]).reshape(b, X_HEADS * m, X_HEADS * X_DH)
    return kbd, vbd


def _top_half(x):
    return lax.bitcast_convert_type(lax.bitcast_convert_type(x, jnp.uint32) & jnp.uint32(0xFFFF0000), F32)


def _router_kernel(x_ref, g_ref, w_ref, b_ref, h_out, r_out):
    x = x_ref[...]
    h = _rms(x, g_ref[...], 1.0 / x.shape[-1])
    h_out[...] = h.astype(h_out.dtype)
    h_top = _top_half(h)
    h_hi = h_top.astype(BF16)
    h_lo = (h - h_top).astype(BF16)
    both = jnp.dot(h_hi, w_ref[...], preferred_element_type=F32)
    logits = (both[:, :LANES] + both[:, LANES:] + jnp.dot(h_lo, w_ref[:, :LANES], preferred_element_type=F32)
              + b_ref[...])
    lane = lax.broadcasted_iota(jnp.int32, logits.shape, 1)
    lane_f = lane.astype(F32)
    big = jnp.float32(4 * LANES)

    def top(vals, mask):
        mv = jnp.max(jnp.where(mask, vals, -jnp.inf), axis=-1, keepdims=True)
        idx = jnp.min(jnp.where(mask & (vals == mv), lane_f, big), axis=-1, keepdims=True)
        return mv, idx

    g_mask = lane < N_GROUPS
    g_max, g_idx = top(logits, g_mask)
    p_g = 1.0 / jnp.sum(jnp.where(g_mask, jnp.exp(logits - g_max), 0.0), axis=-1, keepdims=True)
    first = N_GROUPS + g_idx * EXPERTS_PER_GROUP
    e_mask = (lane_f >= first) & (lane_f < first + EXPERTS_PER_GROUP)
    v0, i0 = top(logits, e_mask)
    v1, i1 = top(logits, e_mask & (lane_f != i0))
    e1 = jnp.exp(v1 - v0)
    w0 = p_g / (1.0 + e1)
    w1 = p_g * e1 / (1.0 + e1)
    out = jnp.where(lane == 0, i0 - N_GROUPS, 0.0)
    out = jnp.where(lane == 1, i1 - N_GROUPS, out)
    out = jnp.where(lane == 2, w0, out)
    out = jnp.where(lane == 3, w1, out)
    r_out[...] = out


def _router(x2, g, w, bias, tm):
    t, d = x2.shape
    full = lambda a: pl.BlockSpec(a.shape, lambda i: (0, 0))
    return pl.pallas_call(
        _router_kernel,
        grid=(t // tm,),
        in_specs=[pl.BlockSpec((tm, d), lambda i: (i, 0)), full(g), full(w), full(bias)],
        out_specs=[pl.BlockSpec((tm, d), lambda i: (i, 0)), pl.BlockSpec((tm, LANES), lambda i: (i, 0))],
        out_shape=[jax.ShapeDtypeStruct((t, d), BF16), jax.ShapeDtypeStruct((t, LANES), F32)],
        compiler_params=_cp(("parallel",)),
        name="moe_router",
    )(x2, g, w, bias)


def _expert_kernel(blk_e_ref, n_used_ref, x_ref, w1_ref, w3_ref, w2_ref, o_ref, w13_s, w2_s):
    i = pl.program_id(0)
    used = i < n_used_ref[0]
    de = w1_ref.shape[1]

    @pl.when(used & ((i == 0) | (blk_e_ref[i] != blk_e_ref[jnp.maximum(i - 1, 0)])))
    def _():
        w13_s[:, :de] = w1_ref[...].astype(BF16)
        w13_s[:, de:] = w3_ref[...].astype(BF16)
        w2_s[...] = w2_ref[...].astype(BF16)

    @pl.when(used)
    def _():
        hid = jnp.dot(x_ref[...], w13_s[...], preferred_element_type=F32)
        a = hid[:, :de]
        act = (a / (1.0 + jnp.exp(-a))) * hid[:, de:]
        o_ref[...] = jnp.dot(act.astype(BF16), w2_s[...], preferred_element_type=F32).astype(o_ref.dtype)

    @pl.when(jnp.logical_not(used))
    def _():
        o_ref[...] = jnp.zeros(o_ref.shape, o_ref.dtype)


def _experts(blk_e, n_used, xr, w1, w3, w2, layer, rows_per_block):
    rows, d = xr.shape
    de = w1.shape[3]
    weight = lambda w: pl.BlockSpec((None, None) + w.shape[2:], lambda i, be, nu: (layer, be[i], 0, 0))
    grid_spec = pltpu.PrefetchScalarGridSpec(
        num_scalar_prefetch=2,
        grid=(rows // rows_per_block,),
        in_specs=[pl.BlockSpec((rows_per_block, d), lambda i, be, nu: (i, 0)), weight(w1), weight(w3), weight(w2)],
        out_specs=pl.BlockSpec((rows_per_block, d), lambda i, be, nu: (i, 0)),
        scratch_shapes=[pltpu.VMEM((d, 2 * de), BF16), pltpu.VMEM((de, d), BF16)],
    )
    return pl.pallas_call(
        _expert_kernel,
        grid_spec=grid_spec,
        out_shape=jax.ShapeDtypeStruct((rows, d), BF16),
        compiler_params=_cp(("arbitrary",)),
        name="moe_experts",
    )(blk_e, n_used, xr, w1, w3, w2)


def _combine_kernel(x_ref, y0_ref, y1_ref, r_ref, g_ref, o_ref, *, final_norm):
    r = r_ref[...]
    y = x_ref[...] + r[:, 2:3] * y0_ref[...].astype(F32) + r[:, 3:4] * y1_ref[...].astype(F32)
    if final_norm:
        y = _rms(y, g_ref[...], 1.0 / y.shape[-1])
    o_ref[...] = y


def _combine(x2, y0, y1, route, g, final_norm, tm):
    t, d = x2.shape
    row = lambda w: pl.BlockSpec((tm, w), lambda i: (i, 0))
    return pl.pallas_call(
        functools.partial(_combine_kernel, final_norm=final_norm),
        grid=(t // tm,),
        in_specs=[row(d), row(d), row(d), row(LANES), pl.BlockSpec((1, d), lambda i: (0, 0))],
        out_specs=row(d),
        out_shape=jax.ShapeDtypeStruct((t, d), F32),
        compiler_params=_cp(("parallel",)),
        name="moe_combine",
    )(x2, y0, y1, route, g)


def _dispatch(route, rows_per_block):
    t = route.shape[0]
    eid = route[:, :TOP_K].astype(jnp.int32).reshape(-1)
    n = eid.shape[0]
    order = jnp.argsort(eid).astype(jnp.int32)
    rank = jnp.argsort(order).astype(jnp.int32)
    experts = jnp.arange(N_EXPERTS, dtype=jnp.int32)
    counts = jnp.sum((eid[:, None] == experts[None, :]).astype(jnp.int32), axis=0)
    start = jnp.cumsum(counts) - counts
    padded = (counts + rows_per_block - 1) // rows_per_block * rows_per_block
    pend = jnp.cumsum(padded)
    pstart = pend - padded
    dest = (rank + (pstart - start)[eid]).reshape(t, TOP_K)
    n_blocks = n // rows_per_block + N_EXPERTS
    blk_first = jnp.arange(n_blocks, dtype=jnp.int32) * rows_per_block
    blk_e = jnp.minimum(jnp.sum((pend[None, :] <= blk_first[:, None]).astype(jnp.int32), axis=1), N_EXPERTS - 1)
    off = (blk_first - pstart[blk_e])[:, None] + jnp.arange(rows_per_block, dtype=jnp.int32)[None, :]
    src = jnp.clip(start[blk_e][:, None] + off, 0, n - 1)
    row_tok = jnp.where(off < counts[blk_e][:, None], order[src] // TOP_K, 0).reshape(-1)
    n_used = (pend[-1] // rows_per_block).astype(jnp.int32).reshape(1)
    return row_tok, dest, blk_e, n_used


def _rot_cols(w):
    half = w.shape[-1] // 2
    return jnp.concatenate([-w[..., half:], w[..., :half]], axis=-1)


def _pack_w_in(w):
    d = w.shape[0]
    blk = A_HEADS * 2 * A_DH
    n_ab = 6 * blk
    cq = w[:, n_ab:n_ab + C_Q_RANK]
    ckv = w[:, n_ab + C_Q_RANK:n_ab + C_Q_RANK + C_KV_RANK]
    ckr = w[:, n_ab + C_Q_RANK + C_KV_RANK:n_ab + C_Q_RANK + C_KV_RANK + C_ROPE]
    gates = w[:, n_ab + C_Q_RANK + C_KV_RANK + C_ROPE:]
    z = lambda n: jnp.zeros((d, n), w.dtype)
    tail = LANES - C_NOPE - C_ROPE
    packed = jnp.concatenate([w[:, :2 * blk], w[:, 3 * blk:5 * blk], cq, z(LANES), ckv,
                              z(C_NOPE), ckr, z(tail), z(C_NOPE), _rot_cols(ckr), z(tail), gates], axis=1)
    return packed.astype(BF16), w[:, 2 * blk:3 * blk], w[:, 5 * blk:6 * blk]


def _pack_w_uq(w):
    wq = w.reshape(C_Q_RANK, C_HEADS, C_NOPE + C_ROPE)
    pad_rows = 4 * LANES - C_Q_RANK
    tail = LANES - C_NOPE - C_ROPE
    q1 = jnp.pad(wq, ((0, pad_rows), (0, 0), (0, tail))).reshape(4 * LANES, C_HEADS * LANES)
    q2 = jnp.pad(_rot_cols(wq[:, :, C_NOPE:]), ((0, pad_rows), (0, 0), (C_NOPE, tail))).reshape(4 * LANES, C_HEADS * LANES)
    return q1.astype(BF16), q2.astype(BF16)


def _pack_w_ukv(w):
    wkv = w.reshape(C_KV_RANK, C_HEADS, C_NOPE + C_DV)
    wk = jnp.pad(wkv[:, :, :C_NOPE], ((0, 0), (0, 0), (0, LANES - C_NOPE))).reshape(C_KV_RANK, C_HEADS * LANES)
    wv = wkv[:, :, C_NOPE:].reshape(C_KV_RANK, C_HEADS * C_DV)
    return wk.astype(BF16), wv


def kernel(x, mem, positions, mix_norm_g, w_in, diff_lambda, diff_subln_g, mla_q_norm_g, w_uq, mla_kv_norm_g, w_ukv, w_branch, w_out, cross_norm_g, mem_norm_g, w_xq, w_xkv, w_xo, ffn_norm_g, w_group, b_group, w_router, b_router, w1, w3, w2, final_norm_g):
    b, s, d = x.shape
    depth = w_in.shape[0]
    t = b * s
    n_mem = mem.shape[1]
    tl = _tiles(s)
    assert PROJ_COLS == COL_G * LANES + N_BRANCHES * d and s % tl["t_b"] == 0 and tl["t_b"] == tl["tq_a"]

    pos_f = positions.astype(F32)
    posq_row = pos_f.reshape(b, 1, s)
    posk_rep = jnp.broadcast_to(pos_f[:, :, None], (b, s, LANES))
    pos_bounds = _position_bounds(pos_f, tl["tq_a"])
    cos, sin = _rope_tables(pos_f.reshape(t, 1), tl["tm_tok"])
    slopes_a2 = jnp.exp2(-8.0 * jnp.arange(1, A_HEADS + 1, dtype=F32) / A_HEADS) * LOG2E
    reach = max((w // (2 * dl)) * dl for w, dl in B_PATTERNS)
    b_bias = _b_bias_tables(tl["t_b"], 2 * (-(-reach // tl["t_b"])) + 1)
    mem2 = mem.reshape(b * n_mem, d)

    x2 = x.reshape(t, d)
    for l in range(depth):
        w_main, w_av, w_bv = _pack_w_in(w_in[l])
        proj2, vt_a, vt_b = _in_proj(x2, mix_norm_g[l], w_main, *_vt_weights(w_av, 2 * A_DH), *_vt_weights(w_bv, B_DH),
                                     tl["tm_proj"], tl["tn_proj"])
        proj3 = proj2.reshape(b, s, PROJ_COLS)
        lq = diff_lambda[l].astype(F32)
        lam_init = 0.8 - 0.6 * math.exp(-0.3 * l)
        lam = (jnp.exp(jnp.sum(lq[0] * lq[1])) - jnp.exp(jnp.sum(lq[2] * lq[3])) + lam_init).reshape(1)
        norms = _block_norms(proj3, tl["t_b"])
        oa = _attn_a(proj3, vt_a, norms, pos_bounds, posq_row, posk_rep, lam, slopes_a2, diff_subln_g[l], 1.0 - lam_init,
                     tl["tq_a"], tl["tk_a"])
        ob = _attn_b(proj3, vt_b, norms, b_bias, tl["t_b"])
        wq1, wq2 = _pack_w_uq(w_uq[l])
        wk, wv = _pack_w_ukv(w_ukv[l])
        gq = jnp.pad(mla_q_norm_g[l], (0, 4 * LANES - C_Q_RANK)).reshape(1, 4 * LANES)
        qc, kc, vt_c = _mla_prep(proj2, cos, sin, gq, mla_kv_norm_g[l].reshape(1, C_KV_RANK),
                                 wq1, wq2, wk, *_vt_weights(wv, C_DV), tl["tm_tok"])
        oc = _attn_c(qc.reshape(b, s, -1), kc.reshape(b, s, -1), vt_c, tl["tq_c"], tl["tk_c"])
        x2 = _merge(oa.reshape(t, -1), ob.reshape(t, -1), oc.reshape(t, -1), proj2, x2,
                    w_branch[l].astype(BF16), w_out[l].astype(BF16), tl["tm_tok"])
        kv = _norm_matmul(mem2, mem_norm_g[l], w_xkv[l].astype(BF16), min(1024, b * n_mem), w_xkv.shape[2])
        kbd, vbd = _block_diag_kv(kv.reshape(b, n_mem, -1))
        x2 = _cross(x2.reshape(b, s, d), cross_norm_g[l].reshape(1, d), w_xq[l].astype(BF16), kbd, vbd,
                    w_xo[l].astype(BF16), tl["tm_tok"]).reshape(t, d)
        w_r = jnp.pad(jnp.concatenate([w_group[l], w_router[l]], axis=1), ((0, 0), (0, LANES - N_GROUPS - N_EXPERTS)))
        b_r = jnp.pad(jnp.concatenate([b_group[l], b_router[l]]), (0, LANES - N_GROUPS - N_EXPERTS)).reshape(1, LANES)
        w_r2 = jnp.concatenate([_top_half(w_r).astype(BF16), (w_r - _top_half(w_r)).astype(BF16)], axis=1)
        h, route = _router(x2, ffn_norm_g[l].reshape(1, d), w_r2, b_r, tl["tm_moe"])
        row_tok, dest, blk_e, n_used = _dispatch(route, tl["moe_rows"])
        yr = _experts(blk_e, n_used, h[row_tok], w1, w3, w2, l, tl["moe_rows"])
        x2 = _combine(x2, yr[dest[:, 0]], yr[dest[:, 1]], route, final_norm_g.reshape(1, d),
                      l == depth - 1, tl["tm_moe"])
    return x2.reshape(b, s, d)
```

```python
import functools
import math

import jax
import jax.numpy as jnp
from jax import lax
from jax.experimental import pallas as pl
from jax.experimental.pallas import tpu as pltpu

F32 = jnp.float32
BF16 = jnp.bfloat16

LANES = 128
NORM_EPS = 1e-6
LOG2E = math.log2(math.e)
NEG_BIG = -1e30

A_HEADS, A_DH = 4, 64
B_HEADS, B_DH = 8, 64
B_PATTERNS = ((128, 1), (512, 4), (2048, 16))
C_HEADS, C_Q_RANK, C_KV_RANK, C_NOPE, C_ROPE, C_DV = 8, 384, 256, 64, 32, 64
ROPE_THETA = 10000.0
N_BRANCHES = 3
X_HEADS, X_DH = 4, 64
N_GROUPS, EXPERTS_PER_GROUP, TOP_K = 4, 8, 2
N_EXPERTS = N_GROUPS * EXPERTS_PER_GROUP

COL_A = 0
COL_B = 8
COL_CQ = 16
COL_CKV = 20
COL_R1 = 22
COL_R2 = 23
COL_G = 24
PROJ_COLS = 48 * LANES

VMEM_LIMIT = 48 * 1024 * 1024


def _tiles(seq):
    return dict(
        tm_proj=min(1024, seq), tn_proj=2048,
        tm_tok=min(512, seq),
        tm_moe=min(1024, seq),
        tq_a=min(512, seq), tk_a=min(512, seq),
        t_b=min(512, seq),
        tq_c=min(512, seq), tk_c=min(512, seq),
        moe_rows=512,
    )


def _cp(sem):
    return pltpu.CompilerParams(dimension_semantics=sem, vmem_limit_bytes=VMEM_LIMIT)


def _rms(x, g, inv_n):
    ms = jnp.sum(x * x, axis=-1, keepdims=True) * inv_n
    return x * lax.rsqrt(ms + NORM_EPS) * g


def _norm_matmul_kernel(x_ref, g_ref, w_ref, o_ref, h_ref, *, inv_n):
    @pl.when(pl.program_id(1) == 0)
    def _():
        h_ref[...] = _rms(x_ref[...].astype(F32), g_ref[...], inv_n).astype(BF16)

    o_ref[...] = jnp.dot(h_ref[...], w_ref[...], preferred_element_type=F32).astype(o_ref.dtype)


def _norm_matmul(x, g, w, tm, tn, out_dtype=BF16):
    m, k = x.shape
    n = w.shape[1]
    return pl.pallas_call(
        functools.partial(_norm_matmul_kernel, inv_n=1.0 / k),
        grid=(m // tm, n // tn),
        in_specs=[pl.BlockSpec((tm, k), lambda i, j: (i, 0)),
                  pl.BlockSpec((1, k), lambda i, j: (0, 0)),
                  pl.BlockSpec((k, tn), lambda i, j: (0, j))],
        out_specs=pl.BlockSpec((tm, tn), lambda i, j: (i, j)),
        out_shape=jax.ShapeDtypeStruct((m, n), out_dtype),
        scratch_shapes=[pltpu.VMEM((tm, k), BF16)],
        compiler_params=_cp(("parallel", "arbitrary")),
        name="norm_matmul",
    )(x, g.reshape(1, k), w)


def _in_proj_kernel(x_ref, g_ref, w_ref, wta_ref, wtb_ref, ona_ref, onb_ref, o_ref, vta_ref, vtb_ref, h_ref, *, inv_n):
    @pl.when(pl.program_id(1) == 0)
    def _():
        h = _rms(x_ref[...].astype(F32), g_ref[...], inv_n).astype(BF16)
        h_ref[...] = h
        vta_ref[...] = (_kq(wta_ref[...], h) + ona_ref[...]).astype(vta_ref.dtype)
        vtb_ref[...] = (_kq(wtb_ref[...], h) + onb_ref[...]).astype(vtb_ref.dtype)

    o_ref[...] = jnp.dot(h_ref[...], w_ref[...], preferred_element_type=F32).astype(o_ref.dtype)


def _in_proj(x, g, w, wta, ona, wtb, onb, tm, tn):
    m, k = x.shape
    n = w.shape[1]
    full = lambda a: pl.BlockSpec(a.shape, lambda i, j: (0, 0))
    return pl.pallas_call(
        functools.partial(_in_proj_kernel, inv_n=1.0 / k),
        grid=(m // tm, n // tn),
        in_specs=[pl.BlockSpec((tm, k), lambda i, j: (i, 0)),
                  pl.BlockSpec((1, k), lambda i, j: (0, 0)),
                  pl.BlockSpec((k, tn), lambda i, j: (0, j)),
                  full(wta), full(wtb), full(ona), full(onb)],
        out_specs=[pl.BlockSpec((tm, tn), lambda i, j: (i, j)),
                   pl.BlockSpec((wta.shape[0], tm), lambda i, j: (0, i)),
                   pl.BlockSpec((wtb.shape[0], tm), lambda i, j: (0, i))],
        out_shape=[jax.ShapeDtypeStruct((m, n), BF16),
                   jax.ShapeDtypeStruct((wta.shape[0], m), BF16),
                   jax.ShapeDtypeStruct((wtb.shape[0], m), BF16)],
        scratch_shapes=[pltpu.VMEM((tm, k), BF16)],
        compiler_params=_cp(("parallel", "arbitrary")),
        name="in_proj",
    )(x, g.reshape(1, k), w, wta, wtb, ona, onb)


ONES_ROWS = 16


def _vt_weights(wv, dv):
    k, w = wv.shape
    wt = jnp.pad(wv.T.reshape(w // dv, dv, k), ((0, 0), (0, ONES_ROWS), (0, 0))).reshape(-1, k)
    ones = jnp.zeros((w // dv, dv + ONES_ROWS, 1), F32).at[:, dv, 0].set(1.0).reshape(-1, 1)
    return wt.astype(BF16), ones


def _split_q(q_ref, qs_ref, scale):
    q = q_ref[...].astype(F32) * scale
    lane = lax.broadcasted_iota(jnp.int32, q.shape, 1)
    qs_ref[0] = jnp.where(lane < LANES // 2, q, 0.0).astype(BF16)
    qs_ref[1] = jnp.where(lane >= LANES // 2, q, 0.0).astype(BF16)


def _kq(k, q):
    return lax.dot_general(k, q, (((1,), (1,)), ((), ())), preferred_element_type=F32)


def _put_scores(s, s_ref, cm_ref, slot, c):
    s_ref[slot, c] = s
    cm_ref[slot, c] = jnp.max(s, axis=0, keepdims=True)


def _update(vt_of, s_ref, cm_ref, m_ref, acc_ref, slot):
    for c in range(2):
        m_prev = m_ref[c]
        m_new = jnp.maximum(m_prev, cm_ref[slot, c])
        alpha = jnp.exp2(m_prev - m_new)
        p = jnp.exp2(s_ref[slot, c] - m_new).astype(BF16)
        acc_ref[c] = alpha * acc_ref[c] + jnp.dot(vt_of(c), p, preferred_element_type=F32)
        m_ref[c] = m_new


def _init_state(m_ref, acc_ref):
    m_ref[...] = jnp.full(m_ref.shape, NEG_BIG, F32)
    acc_ref[...] = jnp.zeros(acc_ref.shape, F32)


def _pipelined(n, scores, update, unroll):
    scores(0, 0)

    def body(jj, carry):
        j = unroll * jj
        for u in range(unroll):
            scores(j + u + 1, (u + 1) % 2)
            update(j + u, u % 2)
        return carry

    looped = (n - 1) // unroll
    lax.fori_loop(0, looped, body, 0)
    for j in range(unroll * looped, n):
        if j + 1 < n:
            scores(j + 1, (j + 1) % 2)
        update(j, j % 2)


def _pipelined_list(n, chunk, scores, update):
    scores(chunk(0), 0)

    def body(unroll, first):
        def run(jj, carry):
            t = first + unroll * jj
            for u in range(unroll):
                scores(chunk(t + u + 1), (u + 1) % 2)
                update(chunk(t + u), u % 2)
            return carry
        return run

    quads = (n - 1) // 4
    lax.fori_loop(0, quads, body(4, 0), 0)
    pairs = (n - 1 - 4 * quads) // 2
    lax.fori_loop(0, pairs, body(2, 4 * quads), 0)
    left = n - 4 * quads - 2 * pairs

    @pl.when(left == 2)
    def _():
        scores(chunk(n - 1), 1)
        update(chunk(n - 2), 0)
        update(chunk(n - 1), 1)

    @pl.when(left == 1)
    def _():
        update(chunk(n - 1), 0)


def _normalized(acc_ref, c):
    dv = acc_ref.shape[1] - ONES_ROWS
    return acc_ref[c, :dv, :] / acc_ref[c, dv:dv + 1, :]


def _attn_scratch(tq, tk, dv):
    return [pltpu.VMEM((2, 1, tq), F32), pltpu.VMEM((2, dv + ONES_ROWS, tq), F32),
            pltpu.VMEM((2, 2, tk, tq), F32), pltpu.VMEM((2, 2, 1, tq), F32)]


SKIP_MARGIN = 128.0


def _block_norms_kernel(qk_ref, sel_ref, o_ref):
    x = qk_ref[...].astype(F32)
    lane = lax.broadcasted_iota(jnp.int32, x.shape, 1)
    is_query = (lane // (A_HEADS * 2 * A_DH)) % 2 == 0
    x = jnp.where(is_query, (x * (A_DH ** -0.5 * LOG2E)).astype(BF16).astype(F32), x)
    sums = jnp.dot(x * x, sel_ref[...], preferred_element_type=F32)
    o_ref[...] = jnp.sqrt(jnp.max(sums, axis=0, keepdims=True))


def _block_norms(proj3, t):
    b, s, _ = proj3.shape
    assert COL_B == COL_A + 2 * A_HEADS and A_DH == B_DH
    w = 4 * A_HEADS * 2 * A_DH
    sel = (jnp.arange(w)[:, None] // A_DH == jnp.arange(LANES)[None, :]).astype(BF16)
    norms = pl.pallas_call(
        _block_norms_kernel,
        grid=(b, s // t),
        in_specs=[pl.BlockSpec((None, t, w), lambda bb, i: (bb, i, COL_A * LANES // w)),
                  pl.BlockSpec((w, LANES), lambda bb, i: (0, 0))],
        out_specs=pl.BlockSpec((None, None, 1, LANES), lambda bb, i: (bb, i, 0, 0)),
        out_shape=jax.ShapeDtypeStruct((b, s // t, 1, LANES), F32),
        compiler_params=_cp(("parallel", "parallel")),
        name="block_norms",
    )(proj3, sel)
    return norms[:, :, 0, :w // A_DH].reshape(b, s // t, 4, 2 * A_HEADS)


def _position_bounds(pos_f, t):
    b, s = pos_f.shape
    p = pos_f.reshape(b, s // t, t)
    qlo, qhi = jnp.min(p, -1)[:, :, None], jnp.max(p, -1)[:, :, None]
    klo, khi = jnp.min(p, -1)[:, None, :], jnp.max(p, -1)[:, None, :]
    dmin = jnp.maximum(jnp.maximum(klo - qhi, qlo - khi), 0.0)
    dmax = jnp.maximum(khi - qlo, qhi - klo)
    cover = jnp.max(jnp.min(jnp.abs(p[:, :, :, None] - p[:, :, None, :]), axis=-1), axis=-1)
    j = jnp.arange(s // t)
    return dmin, jnp.where(j[:, None] == j[None, :], jnp.minimum(dmax, cover[:, :, None]), dmax)


def _a_chunk_lists(norms, bounds, slopes2):
    dmin, reach = bounds
    b, nk = dmin.shape[0], dmin.shape[2]
    norms = norms[:, :, :2].reshape(b, nk, 2, A_HEADS, 2).max(axis=-1)
    qn = norms[:, :, 0].transpose(0, 2, 1)
    kn = norms[:, :, 1].transpose(0, 2, 1)
    qk = 1.01 * qn[:, :, :, None] * kn[:, :, None, :] + 1.0
    sl = slopes2[None, :, None, None]
    lower = -qk - sl * reach[:, None]
    upper = qk - sl * dmin[:, None]
    j = jnp.arange(nk, dtype=jnp.int32)
    first = jnp.argmax(lower, axis=-1).astype(jnp.int32)[..., None]
    keep = (upper >= jnp.max(lower, axis=-1, keepdims=True) - SKIP_MARGIN) | (j == first)
    lst = jnp.argsort(jnp.where(j == first, -1, jnp.where(keep, j, nk + j)), axis=-1).astype(jnp.int32)
    cnt = jnp.sum(keep.astype(jnp.int32), axis=-1)
    return lst.reshape(-1), cnt.reshape(-1)


def _attn_a_kernel(lst_ref, cnt_ref, lam_ref, slope_ref, q_ref, k_ref, vt_ref, pq_ref, pk_ref, g_ref, o_ref,
                   qs_ref, m_ref, acc_ref, s_ref, cm_ref, *, tk, post_scale):
    tq = q_ref.shape[0]
    nk = k_ref.shape[0] // tk
    blk = (pl.program_id(0) * pl.num_programs(1) + pl.program_id(1)) * pl.num_programs(2) + pl.program_id(2)
    sl2 = slope_ref[pl.program_id(1)]
    _split_q(q_ref, qs_ref, A_DH ** -0.5 * LOG2E)
    _init_state(m_ref, acc_ref)
    pq = pq_ref[...] * sl2

    def scores(j, slot):
        ks = pl.multiple_of(j * tk, tk)
        k = k_ref[pl.ds(ks, tk), :]
        pk = pk_ref[pl.ds(ks, tk), :] * sl2
        bias = jnp.abs(jnp.concatenate([pk] * (tq // LANES), axis=1) - pq)
        for c in range(2):
            _put_scores(_kq(k, qs_ref[c]) - bias, s_ref, cm_ref, slot, c)

    def update(j, slot):
        ks = pl.multiple_of(j * tk, tk)
        _update(lambda c: vt_ref[:, pl.ds(ks, tk)], s_ref, cm_ref, m_ref, acc_ref, slot)

    _pipelined_list(cnt_ref[blk], lambda t: lst_ref[blk * nk + t], scores, update)
    o = (_normalized(acc_ref, 0) - lam_ref[0] * _normalized(acc_ref, 1)).T
    o_ref[...] = (_rms(o, g_ref[...], 1.0 / LANES) * post_scale).astype(o_ref.dtype)


def _attn_a(proj, vt, norms, bounds, posq_row, posk_rep, lam, slopes2, g, post_scale, tq, tk):
    b, s, _ = proj.shape
    assert tq == tk
    lst, cnt = _a_chunk_lists(norms, bounds, slopes2)
    smem = pl.BlockSpec(memory_space=pltpu.SMEM)
    grid_spec = pltpu.PrefetchScalarGridSpec(
        num_scalar_prefetch=2,
        grid=(b, A_HEADS, s // tq),
        in_specs=[smem, smem,
                  pl.BlockSpec((None, tq, LANES), lambda bb, h, i, *_: (bb, i, COL_A + h)),
                  pl.BlockSpec((None, s, LANES), lambda bb, h, i, *_: (bb, 0, COL_A + A_HEADS + h)),
                  pl.BlockSpec((vt.shape[0] // A_HEADS, s), lambda bb, h, i, *_: (h, bb)),
                  pl.BlockSpec((None, 1, tq), lambda bb, h, i, *_: (bb, 0, i)),
                  pl.BlockSpec((None, s, LANES), lambda bb, h, i, *_: (bb, 0, 0)),
                  pl.BlockSpec((1, LANES), lambda bb, h, i, *_: (0, 0))],
        out_specs=pl.BlockSpec((None, tq, LANES), lambda bb, h, i, *_: (bb, i, h)),
        scratch_shapes=[pltpu.VMEM((2, tq, LANES), BF16)] + _attn_scratch(tq, tk, LANES),
    )
    return pl.pallas_call(
        functools.partial(_attn_a_kernel, tk=tk, post_scale=post_scale),
        grid_spec=grid_spec,
        out_shape=jax.ShapeDtypeStruct((b, s, A_HEADS * LANES), BF16),
        compiler_params=_cp(("parallel", "parallel", "arbitrary")),
        name="attn_diff",
    )(lst, cnt, lam, slopes2, proj, proj, vt, posq_row, posk_rep, g.reshape(1, LANES))


def _b_chunk_lists(norms, nband, t):
    b, nq = norms.shape[:2]
    half = nband // 2
    norms = norms[:, :, 2:]
    jj = jnp.arange(nband, dtype=jnp.int32)
    kb = jnp.arange(nq, dtype=jnp.int32)[:, None] + jj[None, :] - half
    inside = (kb >= 0) & (kb < nq)
    qk = 1.01 * norms[:, :, None, 0, :] * norms[:, jnp.clip(kb, 0, nq - 1), 1, :] + 1.0
    dmin = jnp.where(jj == half, 0, (jnp.abs(jj - half) - 1) * t + 1).astype(F32)
    slopes2 = jnp.exp2(-8.0 * jnp.arange(1, B_HEADS + 1, dtype=F32) / B_HEADS) * LOG2E
    upper = qk - slopes2 * dmin[:, None] + math.log2(len(B_PATTERNS))
    keep = upper >= -qk[:, :, half:half + 1, :] - SKIP_MARGIN
    keep = jnp.any(keep.reshape(b, nq, nband, B_HEADS // 2, 2), axis=-1).transpose(0, 3, 1, 2)
    keep = (keep & inside) | (jj == half)
    lst = jnp.argsort(jnp.where(jj == half, -1, jnp.where(keep, jj, nband + jj)), axis=-1).astype(jnp.int32)
    return lst.reshape(-1), jnp.sum(keep.astype(jnp.int32), axis=-1).reshape(-1)


def _attn_b_kernel(lst_ref, cnt_ref, q_ref, k_ref, vt_ref, bias_ref, o_ref, qs_ref, m_ref, acc_ref, s_ref, cm_ref,
                   *, nband):
    t = q_ref.shape[0]
    rows = vt_ref.shape[0] // 2
    i = pl.program_id(2)
    blk = (pl.program_id(0) * pl.num_programs(1) + pl.program_id(1)) * pl.num_programs(2) + i
    _split_q(q_ref, qs_ref, B_DH ** -0.5 * LOG2E)
    _init_state(m_ref, acc_ref)

    def start(jj):
        return pl.multiple_of((i + jj - nband // 2) * t, t)

    def scores(jj, slot):
        k = k_ref[pl.ds(start(jj), t), :]
        for c in range(2):
            _put_scores(_kq(k, qs_ref[c]) + bias_ref[c, jj], s_ref, cm_ref, slot, c)

    def update(jj, slot):
        ks = start(jj)
        _update(lambda c: vt_ref[c * rows:(c + 1) * rows, pl.ds(ks, t)], s_ref, cm_ref, m_ref, acc_ref, slot)

    _pipelined_list(cnt_ref[blk], lambda n: lst_ref[blk * nband + n], scores, update)
    o = jnp.concatenate([_normalized(acc_ref, 0), _normalized(acc_ref, 1)], axis=0)
    o_ref[...] = o.T.astype(o_ref.dtype)


def _b_bias_tables(t, nband):
    half = nband // 2
    r = jnp.arange(t, dtype=jnp.int32)[None, :, None]
    c = jnp.arange(t, dtype=jnp.int32)[None, None, :]
    jj = jnp.arange(nband, dtype=jnp.int32)[:, None, None]
    ao = jnp.abs((jj - half) * t + r - c)
    mult = jnp.zeros(ao.shape, jnp.int32)
    for window, dilation in B_PATTERNS:
        reach = (window // (2 * dilation)) * dilation
        mult = mult + ((ao % dilation == 0) & (ao <= reach)).astype(jnp.int32)
    slopes = jnp.exp2(-8.0 * jnp.arange(1, B_HEADS + 1, dtype=F32) / B_HEADS) * LOG2E
    bias = jnp.log2(jnp.maximum(mult, 1).astype(F32))[None] - slopes[:, None, None, None] * ao.astype(F32)[None]
    bias = jnp.where((mult > 0)[None], bias, NEG_BIG)
    return bias.reshape(B_HEADS // 2, 2, nband, t, t)


def _attn_b(proj, vt, norms, bias, t):
    b, s, _ = proj.shape
    nband = bias.shape[2]
    npair = B_HEADS // 2
    lst, cnt = _b_chunk_lists(norms, nband, t)
    grid_spec = pltpu.PrefetchScalarGridSpec(
        num_scalar_prefetch=2,
        grid=(b, npair, s // t),
        in_specs=[pl.BlockSpec((None, t, LANES), lambda bb, p, i, *_: (bb, i, COL_B + p)),
                  pl.BlockSpec((None, s, LANES), lambda bb, p, i, *_: (bb, 0, COL_B + npair + p)),
                  pl.BlockSpec((vt.shape[0] // npair, s), lambda bb, p, i, *_: (p, bb)),
                  pl.BlockSpec((None,) + bias.shape[1:], lambda bb, p, i, *_: (p, 0, 0, 0, 0))],
        out_specs=pl.BlockSpec((None, t, LANES), lambda bb, p, i, *_: (bb, i, p)),
        scratch_shapes=[pltpu.VMEM((2, t, LANES), BF16)] + _attn_scratch(t, t, B_DH),
    )
    return pl.pallas_call(
        functools.partial(_attn_b_kernel, nband=nband),
        grid_spec=grid_spec,
        out_shape=jax.ShapeDtypeStruct((b, s, npair * LANES), BF16),
        compiler_params=_cp(("parallel", "parallel", "arbitrary")),
        name="attn_dilated",
    )(lst, cnt, proj, proj, vt, bias)


def _rope_table_kernel(pos_ref, invf_ref, c_ref, s_ref):
    ang = pos_ref[...] * invf_ref[...]
    c_ref[...] = jnp.cos(ang)
    s_ref[...] = jnp.sin(ang)


def _rope_tables(pos_col, tm):
    t = pos_col.shape[0]
    inv = ROPE_THETA ** (-jnp.arange(0, C_ROPE, 2, dtype=F32) / C_ROPE)
    invf = jnp.concatenate([jnp.zeros((C_NOPE,), F32), inv, inv,
                            jnp.zeros((LANES - C_NOPE - C_ROPE,), F32)]).reshape(1, LANES)
    spec = pl.BlockSpec((tm, LANES), lambda i: (i, 0))
    return pl.pallas_call(
        _rope_table_kernel,
        grid=(t // tm,),
        in_specs=[pl.BlockSpec((tm, 1), lambda i: (i, 0)), pl.BlockSpec((1, LANES), lambda i: (0, 0))],
        out_specs=[spec, spec],
        out_shape=[jax.ShapeDtypeStruct((t, LANES), F32)] * 2,
        compiler_params=_cp(("parallel",)),
        name="rope_tables",
    )(pos_col, invf)


def _mla_prep_kernel(cq_ref, ckv_ref, r1_ref, r2_ref, c_ref, s_ref, gq_ref, gkv_ref,
                     wq1_ref, wq2_ref, wk_ref, wvt_ref, onv_ref, q_out, k_out, vt_out):
    qn = _rms(cq_ref[...].astype(F32), gq_ref[...], 1.0 / C_Q_RANK).astype(BF16)
    kvn = _rms(ckv_ref[...].astype(F32), gkv_ref[...], 1.0 / C_KV_RANK).astype(BF16)
    q1 = jnp.dot(qn, wq1_ref[...], preferred_element_type=F32)
    q2 = jnp.dot(qn, wq2_ref[...], preferred_element_type=F32)
    k1 = jnp.dot(kvn, wk_ref[...], preferred_element_type=F32)
    vt_out[...] = (_kq(wvt_ref[...], kvn) + onv_ref[...]).astype(vt_out.dtype)
    cos = c_ref[...]
    sin = s_ref[...]
    k_rope = r1_ref[...].astype(F32) * cos + r2_ref[...].astype(F32) * sin
    scale = (C_NOPE + C_ROPE) ** -0.5 * LOG2E
    for h in range(C_HEADS):
        sl = slice(h * LANES, (h + 1) * LANES)
        q_out[:, sl] = ((q1[:, sl] * cos + q2[:, sl] * sin) * scale).astype(q_out.dtype)
        k_out[:, sl] = (k1[:, sl] + k_rope).astype(k_out.dtype)


def _mla_prep(proj2, cos, sin, gq, gkv, wq1, wq2, wk, wvt, onv, tm):
    t = proj2.shape[0]
    full = lambda a: pl.BlockSpec(a.shape, lambda i: (0, 0))
    row = lambda w: pl.BlockSpec((tm, w), lambda i: (i, 0))
    return pl.pallas_call(
        _mla_prep_kernel,
        grid=(t // tm,),
        in_specs=[pl.BlockSpec((tm, 4 * LANES), lambda i: (i, COL_CQ // 4)),
                  pl.BlockSpec((tm, 2 * LANES), lambda i: (i, COL_CKV // 2)),
                  pl.BlockSpec((tm, LANES), lambda i: (i, COL_R1)),
                  pl.BlockSpec((tm, LANES), lambda i: (i, COL_R2)),
                  row(LANES), row(LANES), full(gq), full(gkv), full(wq1), full(wq2), full(wk), full(wvt), full(onv)],
        out_specs=[row(C_HEADS * LANES), row(C_HEADS * LANES), pl.BlockSpec((wvt.shape[0], tm), lambda i: (0, i))],
        out_shape=[jax.ShapeDtypeStruct((t, C_HEADS * LANES), BF16),
                   jax.ShapeDtypeStruct((t, C_HEADS * LANES), BF16),
                   jax.ShapeDtypeStruct((wvt.shape[0], t), BF16)],
        compiler_params=_cp(("parallel",)),
        name="mla_prep",
    )(proj2, proj2, proj2, proj2, cos, sin, gq, gkv, wq1, wq2, wk, wvt, onv)


def _attn_c_kernel(q_ref, k_ref, vt_ref, o_ref, m_ref, acc_ref, s_ref, cm_ref, *, tk):
    rows = vt_ref.shape[0] // 2
    _init_state(m_ref, acc_ref)

    def scores(j, slot):
        ks = pl.multiple_of(j * tk, tk)
        for c in range(2):
            sl = slice(c * LANES, (c + 1) * LANES)
            _put_scores(_kq(k_ref[pl.ds(ks, tk), sl], q_ref[:, sl]), s_ref, cm_ref, slot, c)

    def update(j, slot):
        ks = pl.multiple_of(j * tk, tk)
        _update(lambda c: vt_ref[c * rows:(c + 1) * rows, pl.ds(ks, tk)], s_ref, cm_ref, m_ref, acc_ref, slot)

    _pipelined(k_ref.shape[0] // tk, scores, update, unroll=4)
    o = jnp.concatenate([_normalized(acc_ref, 0), _normalized(acc_ref, 1)], axis=0)
    o_ref[...] = o.T.astype(o_ref.dtype)


def _attn_c(q, k, vt, tq, tk):
    b, s, _ = q.shape
    npair = C_HEADS // 2
    return pl.pallas_call(
        functools.partial(_attn_c_kernel, tk=tk),
        grid=(b, npair, s // tq),
        in_specs=[pl.BlockSpec((None, tq, 2 * LANES), lambda bb, p, i: (bb, i, p)),
                  pl.BlockSpec((None, s, 2 * LANES), lambda bb, p, i: (bb, 0, p)),
                  pl.BlockSpec((vt.shape[0] // npair, s), lambda bb, p, i: (p, bb))],
        out_specs=pl.BlockSpec((None, tq, LANES), lambda bb, p, i: (bb, i, p)),
        out_shape=jax.ShapeDtypeStruct((b, s, npair * LANES), BF16),
        scratch_shapes=_attn_scratch(tq, tk, C_DV),
        compiler_params=_cp(("parallel", "parallel", "arbitrary")),
        name="attn_latent",
    )(q, k, vt)


def _merge_kernel(oa_ref, ob_ref, oc_ref, g0_ref, g1_ref, g2_ref, x_ref, wb_ref, wo_ref, o_ref):
    z = None
    for n, (o_r, g_r) in enumerate(((oa_ref, g0_ref), (ob_ref, g1_ref), (oc_ref, g2_ref))):
        br = jnp.dot(o_r[...], wb_ref[n], preferred_element_type=F32)
        gate = 1.0 / (1.0 + jnp.exp(-g_r[...].astype(F32)))
        z = gate * br if z is None else z + gate * br
    o_ref[...] = x_ref[...] + jnp.dot(z.astype(BF16), wo_ref[...], preferred_element_type=F32)


def _merge(oa, ob, oc, proj2, x2, wb, wo, tm):
    t, d = x2.shape
    bw = oa.shape[1]
    row = lambda w: pl.BlockSpec((tm, w), lambda i: (i, 0))
    gate = lambda n: pl.BlockSpec((tm, d), lambda i: (i, COL_G * LANES // d + n))
    return pl.pallas_call(
        _merge_kernel,
        grid=(t // tm,),
        in_specs=[row(bw), row(bw), row(bw), gate(0), gate(1), gate(2), row(d),
                  pl.BlockSpec(wb.shape, lambda i: (0, 0, 0)), pl.BlockSpec(wo.shape, lambda i: (0, 0))],
        out_specs=row(d),
        out_shape=jax.ShapeDtypeStruct((t, d), F32),
        compiler_params=_cp(("parallel",)),
        name="branch_merge",
    )(oa, ob, oc, proj2, proj2, proj2, x2, wb, wo)


def _cross_kernel(x_ref, g_ref, wq_ref, kbd_ref, vbd_ref, wo_ref, o_ref, *, n_mem):
    x = x_ref[...]
    h = _rms(x, g_ref[...], 1.0 / x.shape[-1]).astype(BF16)
    q = (jnp.dot(h, wq_ref[...], preferred_element_type=F32) * (X_DH ** -0.5 * LOG2E)).astype(BF16)
    s = jnp.dot(q, kbd_ref[...], preferred_element_type=F32)
    ps = []
    for hh in range(X_HEADS):
        sh = s[:, hh * n_mem:(hh + 1) * n_mem]
        p = jnp.exp2(sh - jnp.max(sh, axis=-1, keepdims=True))
        ps.append((p / jnp.sum(p, axis=-1, keepdims=True)).astype(BF16))
    o = jnp.dot(jnp.concatenate(ps, axis=1), vbd_ref[...], preferred_element_type=F32)
    o_ref[...] = x + jnp.dot(o.astype(BF16), wo_ref[...], preferred_element_type=F32)


def _cross(x3, g, wq, kbd, vbd, wo, tm):
    b, s, d = x3.shape
    n_mem = kbd.shape[2] // X_HEADS
    full = lambda a: pl.BlockSpec(a.shape, lambda bb, i: (0, 0))
    return pl.pallas_call(
        functools.partial(_cross_kernel, n_mem=n_mem),
        grid=(b, s // tm),
        in_specs=[pl.BlockSpec((None, tm, d), lambda bb, i: (bb, i, 0)), full(g), full(wq),
                  pl.BlockSpec((None,) + kbd.shape[1:], lambda bb, i: (bb, 0, 0)),
                  pl.BlockSpec((None,) + vbd.shape[1:], lambda bb, i: (bb, 0, 0)), full(wo)],
        out_specs=pl.BlockSpec((None, tm, d), lambda bb, i: (bb, i, 0)),
        out_shape=jax.ShapeDtypeStruct((b, s, d), F32),
        compiler_params=_cp(("parallel", "parallel")),
        name="cross_attn",
    )(x3, g, wq, kbd, vbd, wo)


def _block_diag_kv(kv):
    b, m, _ = kv.shape
    kv = kv.reshape(b, m, 2, X_HEADS, X_DH)
    eye = jnp.eye(X_HEADS, dtype=kv.dtype)
    kt = kv[:, :, 0].transpose(0, 2, 3, 1)
    kbd = (kt[:, :, :, None, :] * eye[None, :, None, :, None]).reshape(b, X_HEADS * X_DH, X_HEADS * m)
    vt = kv[:, :, 1].transpose(0, 2, 1, 3)
    vbd = (vt[:, :, :, None, :] * eye[None, :, None, :, None]).reshape(b, X_HEADS * m, X_HEADS * X_DH)
    return kbd, vbd


def _top_half(x):
    return lax.bitcast_convert_type(lax.bitcast_convert_type(x, jnp.uint32) & jnp.uint32(0xFFFF0000), F32)


def _router_kernel(x_ref, g_ref, w_ref, b_ref, h_out, r_out):
    x = x_ref[...]
    h = _rms(x, g_ref[...], 1.0 / x.shape[-1])
    h_out[...] = h.astype(h_out.dtype)
    h_top = _top_half(h)
    h_hi = h_top.astype(BF16)
    h_lo = (h - h_top).astype(BF16)
    both = jnp.dot(h_hi, w_ref[...], preferred_element_type=F32)
    logits = (both[:, :LANES] + both[:, LANES:] + jnp.dot(h_lo, w_ref[:, :LANES], preferred_element_type=F32)
              + b_ref[...])
    lane = lax.broadcasted_iota(jnp.int32, logits.shape, 1)
    lane_f = lane.astype(F32)
    big = jnp.float32(4 * LANES)

    def top(vals, mask):
        mv = jnp.max(jnp.where(mask, vals, -jnp.inf), axis=-1, keepdims=True)
        idx = jnp.min(jnp.where(mask & (vals == mv), lane_f, big), axis=-1, keepdims=True)
        return mv, idx

    g_mask = lane < N_GROUPS
    g_max, g_idx = top(logits, g_mask)
    p_g = 1.0 / jnp.sum(jnp.where(g_mask, jnp.exp(logits - g_max), 0.0), axis=-1, keepdims=True)
    first = N_GROUPS + g_idx * EXPERTS_PER_GROUP
    e_mask = (lane_f >= first) & (lane_f < first + EXPERTS_PER_GROUP)
    v0, i0 = top(logits, e_mask)
    v1, i1 = top(logits, e_mask & (lane_f != i0))
    e1 = jnp.exp(v1 - v0)
    w0 = p_g / (1.0 + e1)
    w1 = p_g * e1 / (1.0 + e1)
    out = jnp.where(lane == 0, i0 - N_GROUPS, 0.0)
    out = jnp.where(lane == 1, i1 - N_GROUPS, out)
    out = jnp.where(lane == 2, w0, out)
    out = jnp.where(lane == 3, w1, out)
    r_out[...] = out


def _router(x2, g, w, bias, tm):
    t, d = x2.shape
    full = lambda a: pl.BlockSpec(a.shape, lambda i: (0, 0))
    return pl.pallas_call(
        _router_kernel,
        grid=(t // tm,),
        in_specs=[pl.BlockSpec((tm, d), lambda i: (i, 0)), full(g), full(w), full(bias)],
        out_specs=[pl.BlockSpec((tm, d), lambda i: (i, 0)), pl.BlockSpec((tm, LANES), lambda i: (i, 0))],
        out_shape=[jax.ShapeDtypeStruct((t, d), BF16), jax.ShapeDtypeStruct((t, LANES), F32)],
        compiler_params=_cp(("parallel",)),
        name="moe_router",
    )(x2, g, w, bias)


def _expert_kernel(blk_e_ref, n_used_ref, x_ref, w1_ref, w3_ref, w2_ref, o_ref, w13_s, w2_s):
    i = pl.program_id(0)
    used = i < n_used_ref[0]
    de = w1_ref.shape[1]

    @pl.when(used & ((i == 0) | (blk_e_ref[i] != blk_e_ref[jnp.maximum(i - 1, 0)])))
    def _():
        w13_s[:, :de] = w1_ref[...].astype(BF16)
        w13_s[:, de:] = w3_ref[...].astype(BF16)
        w2_s[...] = w2_ref[...].astype(BF16)

    @pl.when(used)
    def _():
        hid = jnp.dot(x_ref[...], w13_s[...], preferred_element_type=F32)
        a = hid[:, :de]
        act = (a / (1.0 + jnp.exp(-a))) * hid[:, de:]
        o_ref[...] = jnp.dot(act.astype(BF16), w2_s[...], preferred_element_type=F32).astype(o_ref.dtype)

    @pl.when(jnp.logical_not(used))
    def _():
        o_ref[...] = jnp.zeros(o_ref.shape, o_ref.dtype)


def _experts(blk_e, n_used, xr, w1, w3, w2, layer, rows_per_block):
    rows, d = xr.shape
    de = w1.shape[3]
    weight = lambda w: pl.BlockSpec((None, None) + w.shape[2:], lambda i, be, nu: (layer, be[i], 0, 0))
    grid_spec = pltpu.PrefetchScalarGridSpec(
        num_scalar_prefetch=2,
        grid=(rows // rows_per_block,),
        in_specs=[pl.BlockSpec((rows_per_block, d), lambda i, be, nu: (i, 0)), weight(w1), weight(w3), weight(w2)],
        out_specs=pl.BlockSpec((rows_per_block, d), lambda i, be, nu: (i, 0)),
        scratch_shapes=[pltpu.VMEM((d, 2 * de), BF16), pltpu.VMEM((de, d), BF16)],
    )
    return pl.pallas_call(
        _expert_kernel,
        grid_spec=grid_spec,
        out_shape=jax.ShapeDtypeStruct((rows, d), BF16),
        compiler_params=_cp(("arbitrary",)),
        name="moe_experts",
    )(blk_e, n_used, xr, w1, w3, w2)


def _combine_kernel(x_ref, y0_ref, y1_ref, r_ref, g_ref, o_ref, *, final_norm):
    r = r_ref[...]
    y = x_ref[...] + r[:, 2:3] * y0_ref[...].astype(F32) + r[:, 3:4] * y1_ref[...].astype(F32)
    if final_norm:
        y = _rms(y, g_ref[...], 1.0 / y.shape[-1])
    o_ref[...] = y


def _combine(x2, y0, y1, route, g, final_norm, tm):
    t, d = x2.shape
    row = lambda w: pl.BlockSpec((tm, w), lambda i: (i, 0))
    return pl.pallas_call(
        functools.partial(_combine_kernel, final_norm=final_norm),
        grid=(t // tm,),
        in_specs=[row(d), row(d), row(d), row(LANES), pl.BlockSpec((1, d), lambda i: (0, 0))],
        out_specs=row(d),
        out_shape=jax.ShapeDtypeStruct((t, d), F32),
        compiler_params=_cp(("parallel",)),
        name="moe_combine",
    )(x2, y0, y1, route, g)


def _dispatch(route, rows_per_block):
    t = route.shape[0]
    eid = route[:, :TOP_K].astype(jnp.int32).reshape(-1)
    n = eid.shape[0]
    order = jnp.argsort(eid).astype(jnp.int32)
    rank = jnp.argsort(order).astype(jnp.int32)
    experts = jnp.arange(N_EXPERTS, dtype=jnp.int32)
    counts = jnp.sum((eid[:, None] == experts[None, :]).astype(jnp.int32), axis=0)
    start = jnp.cumsum(counts) - counts
    padded = (counts + rows_per_block - 1) // rows_per_block * rows_per_block
    pend = jnp.cumsum(padded)
    pstart = pend - padded
    dest = (rank + (pstart - start)[eid]).reshape(t, TOP_K)
    n_blocks = n // rows_per_block + N_EXPERTS
    blk_first = jnp.arange(n_blocks, dtype=jnp.int32) * rows_per_block
    blk_e = jnp.minimum(jnp.sum((pend[None, :] <= blk_first[:, None]).astype(jnp.int32), axis=1), N_EXPERTS - 1)
    off = (blk_first - pstart[blk_e])[:, None] + jnp.arange(rows_per_block, dtype=jnp.int32)[None, :]
    src = jnp.clip(start[blk_e][:, None] + off, 0, n - 1)
    row_tok = jnp.where(off < counts[blk_e][:, None], order[src] // TOP_K, 0).reshape(-1)
    n_used = (pend[-1] // rows_per_block).astype(jnp.int32).reshape(1)
    return row_tok, dest, blk_e, n_used


def _rot_cols(w):
    half = w.shape[-1] // 2
    return jnp.concatenate([-w[..., half:], w[..., :half]], axis=-1)


def _pack_w_in(w):
    d = w.shape[0]
    blk = A_HEADS * 2 * A_DH
    n_ab = 6 * blk
    cq = w[:, n_ab:n_ab + C_Q_RANK]
    ckv = w[:, n_ab + C_Q_RANK:n_ab + C_Q_RANK + C_KV_RANK]
    ckr = w[:, n_ab + C_Q_RANK + C_KV_RANK:n_ab + C_Q_RANK + C_KV_RANK + C_ROPE]
    gates = w[:, n_ab + C_Q_RANK + C_KV_RANK + C_ROPE:]
    z = lambda n: jnp.zeros((d, n), w.dtype)
    tail = LANES - C_NOPE - C_ROPE
    packed = jnp.concatenate([w[:, :2 * blk], w[:, 3 * blk:5 * blk], cq, z(LANES), ckv,
                              z(C_NOPE), ckr, z(tail), z(C_NOPE), _rot_cols(ckr), z(tail), gates], axis=1)
    return packed.astype(BF16), w[:, 2 * blk:3 * blk], w[:, 5 * blk:6 * blk]


def _pack_w_uq(w):
    wq = w.reshape(C_Q_RANK, C_HEADS, C_NOPE + C_ROPE)
    pad_rows = 4 * LANES - C_Q_RANK
    tail = LANES - C_NOPE - C_ROPE
    q1 = jnp.pad(wq, ((0, pad_rows), (0, 0), (0, tail))).reshape(4 * LANES, C_HEADS * LANES)
    q2 = jnp.pad(_rot_cols(wq[:, :, C_NOPE:]), ((0, pad_rows), (0, 0), (C_NOPE, tail))).reshape(4 * LANES, C_HEADS * LANES)
    return q1.astype(BF16), q2.astype(BF16)


def _pack_w_ukv(w):
    wkv = w.reshape(C_KV_RANK, C_HEADS, C_NOPE + C_DV)
    wk = jnp.pad(wkv[:, :, :C_NOPE], ((0, 0), (0, 0), (0, LANES - C_NOPE))).reshape(C_KV_RANK, C_HEADS * LANES)
    wv = wkv[:, :, C_NOPE:].reshape(C_KV_RANK, C_HEADS * C_DV)
    return wk.astype(BF16), wv


def kernel(x, mem, positions, mix_norm_g, w_in, diff_lambda, diff_subln_g, mla_q_norm_g, w_uq, mla_kv_norm_g, w_ukv, w_branch, w_out, cross_norm_g, mem_norm_g, w_xq, w_xkv, w_xo, ffn_norm_g, w_group, b_group, w_router, b_router, w1, w3, w2, final_norm_g):
    b, s, d = x.shape
    depth = w_in.shape[0]
    t = b * s
    n_mem = mem.shape[1]
    tl = _tiles(s)
    assert PROJ_COLS == COL_G * LANES + N_BRANCHES * d and s % tl["t_b"] == 0 and tl["t_b"] == tl["tq_a"]

    pos_f = positions.astype(F32)
    posq_row = pos_f.reshape(b, 1, s)
    posk_rep = jnp.broadcast_to(pos_f[:, :, None], (b, s, LANES))
    pos_bounds = _position_bounds(pos_f, tl["tq_a"])
    cos, sin = _rope_tables(pos_f.reshape(t, 1), tl["tm_tok"])
    slopes_a2 = jnp.exp2(-8.0 * jnp.arange(1, A_HEADS + 1, dtype=F32) / A_HEADS) * LOG2E
    reach = max((w // (2 * dl)) * dl for w, dl in B_PATTERNS)
    b_bias = _b_bias_tables(tl["t_b"], 2 * (-(-reach // tl["t_b"])) + 1)
    mem2 = mem.reshape(b * n_mem, d)

    x2 = x.reshape(t, d)
    for l in range(depth):
        w_main, w_av, w_bv = _pack_w_in(w_in[l])
        proj2, vt_a, vt_b = _in_proj(x2, mix_norm_g[l], w_main, *_vt_weights(w_av, 2 * A_DH), *_vt_weights(w_bv, B_DH),
                                     tl["tm_proj"], tl["tn_proj"])
        proj3 = proj2.reshape(b, s, PROJ_COLS)
        lq = diff_lambda[l].astype(F32)
        lam_init = 0.8 - 0.6 * math.exp(-0.3 * l)
        lam = (jnp.exp(jnp.sum(lq[0] * lq[1])) - jnp.exp(jnp.sum(lq[2] * lq[3])) + lam_init).reshape(1)
        norms = _block_norms(proj3, tl["t_b"])
        oa = _attn_a(proj3, vt_a, norms, pos_bounds, posq_row, posk_rep, lam, slopes_a2, diff_subln_g[l], 1.0 - lam_init,
                     tl["tq_a"], tl["tk_a"])
        ob = _attn_b(proj3, vt_b, norms, b_bias, tl["t_b"])
        wq1, wq2 = _pack_w_uq(w_uq[l])
        wk, wv = _pack_w_ukv(w_ukv[l])
        gq = jnp.pad(mla_q_norm_g[l], (0, 4 * LANES - C_Q_RANK)).reshape(1, 4 * LANES)
        qc, kc, vt_c = _mla_prep(proj2, cos, sin, gq, mla_kv_norm_g[l].reshape(1, C_KV_RANK),
                                 wq1, wq2, wk, *_vt_weights(wv, C_DV), tl["tm_tok"])
        oc = _attn_c(qc.reshape(b, s, -1), kc.reshape(b, s, -1), vt_c, tl["tq_c"], tl["tk_c"])
        x2 = _merge(oa.reshape(t, -1), ob.reshape(t, -1), oc.reshape(t, -1), proj2, x2,
                    w_branch[l].astype(BF16), w_out[l].astype(BF16), tl["tm_tok"])
        kv = _norm_matmul(mem2, mem_norm_g[l], w_xkv[l].astype(BF16), min(1024, b * n_mem), w_xkv.shape[2])
        kbd, vbd = _block_diag_kv(kv.reshape(b, n_mem, -1))
        x2 = _cross(x2.reshape(b, s, d), cross_norm_g[l].reshape(1, d), w_xq[l].astype(BF16), kbd, vbd,
                    w_xo[l].astype(BF16), tl["tm_tok"]).reshape(t, d)
        w_r = jnp.pad(jnp.concatenate([w_group[l], w_router[l]], axis=1), ((0, 0), (0, LANES - N_GROUPS - N_EXPERTS)))
        b_r = jnp.pad(jnp.concatenate([b_group[l], b_router[l]]), (0, LANES - N_GROUPS - N_EXPERTS)).reshape(1, LANES)
        w_r2 = jnp.concatenate([_top_half(w_r).astype(BF16), (w_r - _top_half(w_r)).astype(BF16)], axis=1)
        h, route = _router(x2, ffn_norm_g[l].reshape(1, d), w_r2, b_r, tl["tm_moe"])
        row_tok, dest, blk_e, n_used = _dispatch(route, tl["moe_rows"])
        yr = _experts(blk_e, n_used, h[row_tok], w1, w3, w2, l, tl["moe_rows"])
        x2 = _combine(x2, yr[dest[:, 0]], yr[dest[:, 1]], route, final_norm_g.reshape(1, d),
                      l == depth - 1, tl["tm_moe"])
    return x2.reshape(b, s, d)
```

```python
import functools
import math

import jax
import jax.numpy as jnp
from jax import lax
from jax.experimental import pallas as pl
from jax.experimental.pallas import tpu as pltpu

F32 = jnp.float32
BF16 = jnp.bfloat16

LANES = 128
NORM_EPS = 1e-6
LOG2E = math.log2(math.e)
NEG_BIG = -1e30

A_HEADS, A_DH = 4, 64
B_HEADS, B_DH = 8, 64
B_PATTERNS = ((128, 1), (512, 4), (2048, 16))
C_HEADS, C_Q_RANK, C_KV_RANK, C_NOPE, C_ROPE, C_DV = 8, 384, 256, 64, 32, 64
ROPE_THETA = 10000.0
N_BRANCHES = 3
X_HEADS, X_DH = 4, 64
N_GROUPS, EXPERTS_PER_GROUP, TOP_K = 4, 8, 2
N_EXPERTS = N_GROUPS * EXPERTS_PER_GROUP

COL_A = 0
COL_B = 8
COL_CQ = 16
COL_CKV = 20
COL_R1 = 22
COL_R2 = 23
COL_G = 24
PROJ_COLS = 48 * LANES

VMEM_LIMIT = 48 * 1024 * 1024


def _tiles(seq):
    return dict(
        tm_proj=min(1024, seq), tn_proj=3072,
        tm_tok=min(512, seq),
        tm_moe=min(1024, seq),
        tq_a=min(512, seq), tk_a=min(512, seq),
        t_b=min(512, seq),
        tq_c=min(512, seq), tk_c=min(512, seq),
        moe_rows=512,
    )


def _cp(sem):
    return pltpu.CompilerParams(dimension_semantics=sem, vmem_limit_bytes=VMEM_LIMIT)


def _rms(x, g, inv_n):
    ms = jnp.sum(x * x, axis=-1, keepdims=True) * inv_n
    return x * lax.rsqrt(ms + NORM_EPS) * g


def _norm_matmul_kernel(x_ref, g_ref, w_ref, o_ref, h_ref, *, inv_n):
    @pl.when(pl.program_id(1) == 0)
    def _():
        h_ref[...] = _rms(x_ref[...].astype(F32), g_ref[...], inv_n).astype(BF16)

    o_ref[...] = jnp.dot(h_ref[...], w_ref[...], preferred_element_type=F32).astype(o_ref.dtype)


def _norm_matmul(x, g, w, tm, tn, out_dtype=BF16):
    m, k = x.shape
    n = w.shape[1]
    return pl.pallas_call(
        functools.partial(_norm_matmul_kernel, inv_n=1.0 / k),
        grid=(m // tm, n // tn),
        in_specs=[pl.BlockSpec((tm, k), lambda i, j: (i, 0)),
                  pl.BlockSpec((1, k), lambda i, j: (0, 0)),
                  pl.BlockSpec((k, tn), lambda i, j: (0, j))],
        out_specs=pl.BlockSpec((tm, tn), lambda i, j: (i, j)),
        out_shape=jax.ShapeDtypeStruct((m, n), out_dtype),
        scratch_shapes=[pltpu.VMEM((tm, k), BF16)],
        compiler_params=_cp(("parallel", "arbitrary")),
        name="norm_matmul",
    )(x, g.reshape(1, k), w)


def _in_proj_kernel(x_ref, g_ref, w_ref, wta_ref, wtb_ref, ona_ref, onb_ref, o_ref, vta_ref, vtb_ref, h_ref, *, inv_n):
    @pl.when(pl.program_id(1) == 0)
    def _():
        h = _rms(x_ref[...].astype(F32), g_ref[...], inv_n).astype(BF16)
        h_ref[...] = h
        vta_ref[...] = (_kq(wta_ref[...], h) + ona_ref[...]).astype(vta_ref.dtype)
        vtb_ref[...] = (_kq(wtb_ref[...], h) + onb_ref[...]).astype(vtb_ref.dtype)

    o_ref[...] = jnp.dot(h_ref[...], w_ref[...], preferred_element_type=F32).astype(o_ref.dtype)


def _in_proj(x, g, w, wta, ona, wtb, onb, tm, tn):
    m, k = x.shape
    n = w.shape[1]
    full = lambda a: pl.BlockSpec(a.shape, lambda i, j: (0, 0))
    return pl.pallas_call(
        functools.partial(_in_proj_kernel, inv_n=1.0 / k),
        grid=(m // tm, n // tn),
        in_specs=[pl.BlockSpec((tm, k), lambda i, j: (i, 0)),
                  pl.BlockSpec((1, k), lambda i, j: (0, 0)),
                  pl.BlockSpec((k, tn), lambda i, j: (0, j)),
                  full(wta), full(wtb), full(ona), full(onb)],
        out_specs=[pl.BlockSpec((tm, tn), lambda i, j: (i, j)),
                   pl.BlockSpec((wta.shape[0], tm), lambda i, j: (0, i)),
                   pl.BlockSpec((wtb.shape[0], tm), lambda i, j: (0, i))],
        out_shape=[jax.ShapeDtypeStruct((m, n), BF16),
                   jax.ShapeDtypeStruct((wta.shape[0], m), BF16),
                   jax.ShapeDtypeStruct((wtb.shape[0], m), BF16)],
        scratch_shapes=[pltpu.VMEM((tm, k), BF16)],
        compiler_params=_cp(("parallel", "arbitrary")),
        name="in_proj",
    )(x, g.reshape(1, k), w, wta, wtb, ona, onb)


ONES_ROWS = 16


def _vt_weights(wv, dv):
    k, w = wv.shape
    wt = jnp.pad(wv.T.reshape(w // dv, dv, k), ((0, 0), (0, ONES_ROWS), (0, 0))).reshape(-1, k)
    ones = jnp.zeros((w // dv, dv + ONES_ROWS, 1), F32).at[:, dv, 0].set(1.0).reshape(-1, 1)
    return wt.astype(BF16), ones


def _split_q(q_ref, qs_ref, scale):
    q = q_ref[...].astype(F32) * scale
    lane = lax.broadcasted_iota(jnp.int32, q.shape, 1)
    qs_ref[0] = jnp.where(lane < LANES // 2, q, 0.0).astype(BF16)
    qs_ref[1] = jnp.where(lane >= LANES // 2, q, 0.0).astype(BF16)


def _kq(k, q):
    return lax.dot_general(k, q, (((1,), (1,)), ((), ())), preferred_element_type=F32)


def _put_scores(s, s_ref, cm_ref, slot, c):
    s_ref[slot, c] = s
    cm_ref[slot, c] = jnp.max(s, axis=0, keepdims=True)


def _update(vt_of, s_ref, cm_ref, m_ref, acc_ref, slot):
    for c in range(2):
        m_prev = m_ref[c]
        m_new = jnp.maximum(m_prev, cm_ref[slot, c])
        alpha = jnp.exp2(m_prev - m_new)
        p = jnp.exp2(s_ref[slot, c] - m_new).astype(BF16)
        acc_ref[c] = alpha * acc_ref[c] + jnp.dot(vt_of(c), p, preferred_element_type=F32)
        m_ref[c] = m_new


def _init_state(m_ref, acc_ref):
    m_ref[...] = jnp.full(m_ref.shape, NEG_BIG, F32)
    acc_ref[...] = jnp.zeros(acc_ref.shape, F32)


def _pipelined(n, scores, update, unroll):
    scores(0, 0)

    def body(jj, carry):
        j = unroll * jj
        for u in range(unroll):
            scores(j + u + 1, (u + 1) % 2)
            update(j + u, u % 2)
        return carry

    looped = (n - 1) // unroll
    lax.fori_loop(0, looped, body, 0)
    for j in range(unroll * looped, n):
        if j + 1 < n:
            scores(j + 1, (j + 1) % 2)
        update(j, j % 2)


def _pipelined_list(n, chunk, scores, update):
    scores(chunk(0), 0)

    def body(unroll, first):
        def run(jj, carry):
            t = first + unroll * jj
            for u in range(unroll):
                scores(chunk(t + u + 1), (u + 1) % 2)
                update(chunk(t + u), u % 2)
            return carry
        return run

    quads = (n - 1) // 4
    lax.fori_loop(0, quads, body(4, 0), 0)
    pairs = (n - 1 - 4 * quads) // 2
    lax.fori_loop(0, pairs, body(2, 4 * quads), 0)
    left = n - 4 * quads - 2 * pairs

    @pl.when(left == 2)
    def _():
        scores(chunk(n - 1), 1)
        update(chunk(n - 2), 0)
        update(chunk(n - 1), 1)

    @pl.when(left == 1)
    def _():
        update(chunk(n - 1), 0)


def _normalized(acc_ref, c):
    dv = acc_ref.shape[1] - ONES_ROWS
    return acc_ref[c, :dv, :] / acc_ref[c, dv:dv + 1, :]


def _attn_scratch(tq, tk, dv):
    return [pltpu.VMEM((2, 1, tq), F32), pltpu.VMEM((2, dv + ONES_ROWS, tq), F32),
            pltpu.VMEM((2, 2, tk, tq), F32), pltpu.VMEM((2, 2, 1, tq), F32)]


SKIP_MARGIN = 128.0


def _block_norms_kernel(qk_ref, sel_ref, o_ref):
    x = qk_ref[...].astype(F32)
    lane = lax.broadcasted_iota(jnp.int32, x.shape, 1)
    is_query = (lane // (A_HEADS * 2 * A_DH)) % 2 == 0
    x = jnp.where(is_query, (x * (A_DH ** -0.5 * LOG2E)).astype(BF16).astype(F32), x)
    sums = jnp.dot(x * x, sel_ref[...], preferred_element_type=F32)
    o_ref[...] = jnp.sqrt(jnp.max(sums, axis=0, keepdims=True))


def _block_norms(proj3, t):
    b, s, _ = proj3.shape
    assert COL_B == COL_A + 2 * A_HEADS and A_DH == B_DH
    w = 4 * A_HEADS * 2 * A_DH
    sel = (jnp.arange(w)[:, None] // A_DH == jnp.arange(LANES)[None, :]).astype(BF16)
    norms = pl.pallas_call(
        _block_norms_kernel,
        grid=(b, s // t),
        in_specs=[pl.BlockSpec((None, t, w), lambda bb, i: (bb, i, COL_A * LANES // w)),
                  pl.BlockSpec((w, LANES), lambda bb, i: (0, 0))],
        out_specs=pl.BlockSpec((None, None, 1, LANES), lambda bb, i: (bb, i, 0, 0)),
        out_shape=jax.ShapeDtypeStruct((b, s // t, 1, LANES), F32),
        compiler_params=_cp(("parallel", "parallel")),
        name="block_norms",
    )(proj3, sel)
    return norms[:, :, 0, :w // A_DH].reshape(b, s // t, 4, 2 * A_HEADS)


def _position_bounds(pos_f, t):
    b, s = pos_f.shape
    p = pos_f.reshape(b, s // t, t)
    qlo, qhi = jnp.min(p, -1)[:, :, None], jnp.max(p, -1)[:, :, None]
    klo, khi = jnp.min(p, -1)[:, None, :], jnp.max(p, -1)[:, None, :]
    dmin = jnp.maximum(jnp.maximum(klo - qhi, qlo - khi), 0.0)
    dmax = jnp.maximum(khi - qlo, qhi - klo)
    cover = jnp.max(jnp.min(jnp.abs(p[:, :, :, None] - p[:, :, None, :]), axis=-1), axis=-1)
    j = jnp.arange(s // t)
    return dmin, jnp.where(j[:, None] == j[None, :], jnp.minimum(dmax, cover[:, :, None]), dmax)


def _a_chunk_lists(norms, bounds, slopes2):
    dmin, reach = bounds
    b, nk = dmin.shape[0], dmin.shape[2]
    norms = norms[:, :, :2].reshape(b, nk, 2, A_HEADS, 2).max(axis=-1)
    qn = norms[:, :, 0].transpose(0, 2, 1)
    kn = norms[:, :, 1].transpose(0, 2, 1)
    qk = 1.01 * qn[:, :, :, None] * kn[:, :, None, :] + 1.0
    sl = slopes2[None, :, None, None]
    lower = -qk - sl * reach[:, None]
    upper = qk - sl * dmin[:, None]
    j = jnp.arange(nk, dtype=jnp.int32)
    first = jnp.argmax(lower, axis=-1).astype(jnp.int32)[..., None]
    keep = (upper >= jnp.max(lower, axis=-1, keepdims=True) - SKIP_MARGIN) | (j == first)
    lst = jnp.argsort(jnp.where(j == first, -1, jnp.where(keep, j, nk + j)), axis=-1).astype(jnp.int32)
    cnt = jnp.sum(keep.astype(jnp.int32), axis=-1)
    return lst.reshape(-1), cnt.reshape(-1)


def _attn_a_kernel(lst_ref, cnt_ref, lam_ref, slope_ref, q_ref, k_ref, vt_ref, pq_ref, pk_ref, g_ref, o_ref,
                   qs_ref, m_ref, acc_ref, s_ref, cm_ref, *, tk, post_scale):
    tq = q_ref.shape[0]
    nk = k_ref.shape[0] // tk
    blk = (pl.program_id(0) * pl.num_programs(1) + pl.program_id(1)) * pl.num_programs(2) + pl.program_id(2)
    sl2 = slope_ref[pl.program_id(1)]
    _split_q(q_ref, qs_ref, A_DH ** -0.5 * LOG2E)
    _init_state(m_ref, acc_ref)
    pq = pq_ref[...] * sl2

    def scores(j, slot):
        ks = pl.multiple_of(j * tk, tk)
        k = k_ref[pl.ds(ks, tk), :]
        pk = pk_ref[pl.ds(ks, tk), :] * sl2
        bias = jnp.abs(jnp.concatenate([pk] * (tq // LANES), axis=1) - pq)
        for c in range(2):
            _put_scores(_kq(k, qs_ref[c]) - bias, s_ref, cm_ref, slot, c)

    def update(j, slot):
        ks = pl.multiple_of(j * tk, tk)
        _update(lambda c: vt_ref[:, pl.ds(ks, tk)], s_ref, cm_ref, m_ref, acc_ref, slot)

    _pipelined_list(cnt_ref[blk], lambda t: lst_ref[blk * nk + t], scores, update)
    o = (_normalized(acc_ref, 0) - lam_ref[0] * _normalized(acc_ref, 1)).T
    o_ref[...] = (_rms(o, g_ref[...], 1.0 / LANES) * post_scale).astype(o_ref.dtype)


def _attn_a(proj, vt, norms, bounds, posq_row, posk_rep, lam, slopes2, g, post_scale, tq, tk):
    b, s, _ = proj.shape
    assert tq == tk
    lst, cnt = _a_chunk_lists(norms, bounds, slopes2)
    smem = pl.BlockSpec(memory_space=pltpu.SMEM)
    grid_spec = pltpu.PrefetchScalarGridSpec(
        num_scalar_prefetch=2,
        grid=(b, A_HEADS, s // tq),
        in_specs=[smem, smem,
                  pl.BlockSpec((None, tq, LANES), lambda bb, h, i, *_: (bb, i, COL_A + h)),
                  pl.BlockSpec((None, s, LANES), lambda bb, h, i, *_: (bb, 0, COL_A + A_HEADS + h)),
                  pl.BlockSpec((vt.shape[0] // A_HEADS, s), lambda bb, h, i, *_: (h, bb)),
                  pl.BlockSpec((None, 1, tq), lambda bb, h, i, *_: (bb, 0, i)),
                  pl.BlockSpec((None, s, LANES), lambda bb, h, i, *_: (bb, 0, 0)),
                  pl.BlockSpec((1, LANES), lambda bb, h, i, *_: (0, 0))],
        out_specs=pl.BlockSpec((None, tq, LANES), lambda bb, h, i, *_: (bb, i, h)),
        scratch_shapes=[pltpu.VMEM((2, tq, LANES), BF16)] + _attn_scratch(tq, tk, LANES),
    )
    return pl.pallas_call(
        functools.partial(_attn_a_kernel, tk=tk, post_scale=post_scale),
        grid_spec=grid_spec,
        out_shape=jax.ShapeDtypeStruct((b, s, A_HEADS * LANES), BF16),
        compiler_params=_cp(("parallel", "parallel", "arbitrary")),
        name="attn_diff",
    )(lst, cnt, lam, slopes2, proj, proj, vt, posq_row, posk_rep, g.reshape(1, LANES))


def _b_chunk_lists(norms, nband, t):
    b, nq = norms.shape[:2]
    half = nband // 2
    norms = norms[:, :, 2:]
    jj = jnp.arange(nband, dtype=jnp.int32)
    kb = jnp.arange(nq, dtype=jnp.int32)[:, None] + jj[None, :] - half
    inside = (kb >= 0) & (kb < nq)
    qk = 1.01 * norms[:, :, None, 0, :] * norms[:, jnp.clip(kb, 0, nq - 1), 1, :] + 1.0
    dmin = jnp.where(jj == half, 0, (jnp.abs(jj - half) - 1) * t + 1).astype(F32)
    slopes2 = jnp.exp2(-8.0 * jnp.arange(1, B_HEADS + 1, dtype=F32) / B_HEADS) * LOG2E
    upper = qk - slopes2 * dmin[:, None] + math.log2(len(B_PATTERNS))
    keep = upper >= -qk[:, :, half:half + 1, :] - SKIP_MARGIN
    keep = jnp.any(keep.reshape(b, nq, nband, B_HEADS // 2, 2), axis=-1).transpose(0, 3, 1, 2)
    keep = (keep & inside) | (jj == half)
    lst = jnp.argsort(jnp.where(jj == half, -1, jnp.where(keep, jj, nband + jj)), axis=-1).astype(jnp.int32)
    return lst.reshape(-1), jnp.sum(keep.astype(jnp.int32), axis=-1).reshape(-1)


def _attn_b_kernel(lst_ref, cnt_ref, q_ref, k_ref, vt_ref, bias_ref, o_ref, qs_ref, m_ref, acc_ref, s_ref, cm_ref,
                   *, nband):
    t = q_ref.shape[0]
    rows = vt_ref.shape[0] // 2
    i = pl.program_id(2)
    blk = (pl.program_id(0) * pl.num_programs(1) + pl.program_id(1)) * pl.num_programs(2) + i
    _split_q(q_ref, qs_ref, B_DH ** -0.5 * LOG2E)
    _init_state(m_ref, acc_ref)

    def start(jj):
        return pl.multiple_of((i + jj - nband // 2) * t, t)

    def scores(jj, slot):
        k = k_ref[pl.ds(start(jj), t), :]
        for c in range(2):
            _put_scores(_kq(k, qs_ref[c]) + bias_ref[c, jj], s_ref, cm_ref, slot, c)

    def update(jj, slot):
        ks = start(jj)
        _update(lambda c: vt_ref[c * rows:(c + 1) * rows, pl.ds(ks, t)], s_ref, cm_ref, m_ref, acc_ref, slot)

    _pipelined_list(cnt_ref[blk], lambda n: lst_ref[blk * nband + n], scores, update)
    o = jnp.concatenate([_normalized(acc_ref, 0), _normalized(acc_ref, 1)], axis=0)
    o_ref[...] = o.T.astype(o_ref.dtype)


def _b_bias_tables(t, nband):
    half = nband // 2
    r = jnp.arange(t, dtype=jnp.int32)[None, :, None]
    c = jnp.arange(t, dtype=jnp.int32)[None, None, :]
    jj = jnp.arange(nband, dtype=jnp.int32)[:, None, None]
    ao = jnp.abs((jj - half) * t + r - c)
    mult = jnp.zeros(ao.shape, jnp.int32)
    for window, dilation in B_PATTERNS:
        reach = (window // (2 * dilation)) * dilation
        mult = mult + ((ao % dilation == 0) & (ao <= reach)).astype(jnp.int32)
    slopes = jnp.exp2(-8.0 * jnp.arange(1, B_HEADS + 1, dtype=F32) / B_HEADS) * LOG2E
    bias = jnp.log2(jnp.maximum(mult, 1).astype(F32))[None] - slopes[:, None, None, None] * ao.astype(F32)[None]
    bias = jnp.where((mult > 0)[None], bias, NEG_BIG)
    return bias.reshape(B_HEADS // 2, 2, nband, t, t)


def _attn_b(proj, vt, norms, bias, t):
    b, s, _ = proj.shape
    nband = bias.shape[2]
    npair = B_HEADS // 2
    lst, cnt = _b_chunk_lists(norms, nband, t)
    grid_spec = pltpu.PrefetchScalarGridSpec(
        num_scalar_prefetch=2,
        grid=(b, npair, s // t),
        in_specs=[pl.BlockSpec((None, t, LANES), lambda bb, p, i, *_: (bb, i, COL_B + p)),
                  pl.BlockSpec((None, s, LANES), lambda bb, p, i, *_: (bb, 0, COL_B + npair + p)),
                  pl.BlockSpec((vt.shape[0] // npair, s), lambda bb, p, i, *_: (p, bb)),
                  pl.BlockSpec((None,) + bias.shape[1:], lambda bb, p, i, *_: (p, 0, 0, 0, 0))],
        out_specs=pl.BlockSpec((None, t, LANES), lambda bb, p, i, *_: (bb, i, p)),
        scratch_shapes=[pltpu.VMEM((2, t, LANES), BF16)] + _attn_scratch(t, t, B_DH),
    )
    return pl.pallas_call(
        functools.partial(_attn_b_kernel, nband=nband),
        grid_spec=grid_spec,
        out_shape=jax.ShapeDtypeStruct((b, s, npair * LANES), BF16),
        compiler_params=_cp(("parallel", "parallel", "arbitrary")),
        name="attn_dilated",
    )(lst, cnt, proj, proj, vt, bias)


def _rope_table_kernel(pos_ref, invf_ref, c_ref, s_ref):
    ang = pos_ref[...] * invf_ref[...]
    c_ref[...] = jnp.cos(ang)
    s_ref[...] = jnp.sin(ang)


def _rope_tables(pos_col, tm):
    t = pos_col.shape[0]
    inv = ROPE_THETA ** (-jnp.arange(0, C_ROPE, 2, dtype=F32) / C_ROPE)
    invf = jnp.concatenate([jnp.zeros((C_NOPE,), F32), inv, inv,
                            jnp.zeros((LANES - C_NOPE - C_ROPE,), F32)]).reshape(1, LANES)
    spec = pl.BlockSpec((tm, LANES), lambda i: (i, 0))
    return pl.pallas_call(
        _rope_table_kernel,
        grid=(t // tm,),
        in_specs=[pl.BlockSpec((tm, 1), lambda i: (i, 0)), pl.BlockSpec((1, LANES), lambda i: (0, 0))],
        out_specs=[spec, spec],
        out_shape=[jax.ShapeDtypeStruct((t, LANES), F32)] * 2,
        compiler_params=_cp(("parallel",)),
        name="rope_tables",
    )(pos_col, invf)


def _mla_prep_kernel(cq_ref, ckv_ref, r1_ref, r2_ref, c_ref, s_ref, gq_ref, gkv_ref,
                     wq1_ref, wq2_ref, wk_ref, wvt_ref, onv_ref, q_out, k_out, vt_out):
    qn = _rms(cq_ref[...].astype(F32), gq_ref[...], 1.0 / C_Q_RANK).astype(BF16)
    kvn = _rms(ckv_ref[...].astype(F32), gkv_ref[...], 1.0 / C_KV_RANK).astype(BF16)
    q1 = jnp.dot(qn, wq1_ref[...], preferred_element_type=F32)
    q2 = jnp.dot(qn, wq2_ref[...], preferred_element_type=F32)
    k1 = jnp.dot(kvn, wk_ref[...], preferred_element_type=F32)
    vt_out[...] = (_kq(wvt_ref[...], kvn) + onv_ref[...]).astype(vt_out.dtype)
    cos = c_ref[...]
    sin = s_ref[...]
    k_rope = r1_ref[...].astype(F32) * cos + r2_ref[...].astype(F32) * sin
    scale = (C_NOPE + C_ROPE) ** -0.5 * LOG2E
    for h in range(C_HEADS):
        sl = slice(h * LANES, (h + 1) * LANES)
        q_out[:, sl] = ((q1[:, sl] * cos + q2[:, sl] * sin) * scale).astype(q_out.dtype)
        k_out[:, sl] = (k1[:, sl] + k_rope).astype(k_out.dtype)


def _mla_prep(proj2, cos, sin, gq, gkv, wq1, wq2, wk, wvt, onv, tm):
    t = proj2.shape[0]
    full = lambda a: pl.BlockSpec(a.shape, lambda i: (0, 0))
    row = lambda w: pl.BlockSpec((tm, w), lambda i: (i, 0))
    return pl.pallas_call(
        _mla_prep_kernel,
        grid=(t // tm,),
        in_specs=[pl.BlockSpec((tm, 4 * LANES), lambda i: (i, COL_CQ // 4)),
                  pl.BlockSpec((tm, 2 * LANES), lambda i: (i, COL_CKV // 2)),
                  pl.BlockSpec((tm, LANES), lambda i: (i, COL_R1)),
                  pl.BlockSpec((tm, LANES), lambda i: (i, COL_R2)),
                  row(LANES), row(LANES), full(gq), full(gkv), full(wq1), full(wq2), full(wk), full(wvt), full(onv)],
        out_specs=[row(C_HEADS * LANES), row(C_HEADS * LANES), pl.BlockSpec((wvt.shape[0], tm), lambda i: (0, i))],
        out_shape=[jax.ShapeDtypeStruct((t, C_HEADS * LANES), BF16),
                   jax.ShapeDtypeStruct((t, C_HEADS * LANES), BF16),
                   jax.ShapeDtypeStruct((wvt.shape[0], t), BF16)],
        compiler_params=_cp(("parallel",)),
        name="mla_prep",
    )(proj2, proj2, proj2, proj2, cos, sin, gq, gkv, wq1, wq2, wk, wvt, onv)


def _attn_c_kernel(q_ref, k_ref, vt_ref, o_ref, m_ref, acc_ref, s_ref, cm_ref, *, tk):
    rows = vt_ref.shape[0] // 2
    _init_state(m_ref, acc_ref)

    def scores(j, slot):
        ks = pl.multiple_of(j * tk, tk)
        for c in range(2):
            sl = slice(c * LANES, (c + 1) * LANES)
            _put_scores(_kq(k_ref[pl.ds(ks, tk), sl], q_ref[:, sl]), s_ref, cm_ref, slot, c)

    def update(j, slot):
        ks = pl.multiple_of(j * tk, tk)
        _update(lambda c: vt_ref[c * rows:(c + 1) * rows, pl.ds(ks, tk)], s_ref, cm_ref, m_ref, acc_ref, slot)

    _pipelined(k_ref.shape[0] // tk, scores, update, unroll=4)
    o = jnp.concatenate([_normalized(acc_ref, 0), _normalized(acc_ref, 1)], axis=0)
    o_ref[...] = o.T.astype(o_ref.dtype)


def _attn_c(q, k, vt, tq, tk):
    b, s, _ = q.shape
    npair = C_HEADS // 2
    return pl.pallas_call(
        functools.partial(_attn_c_kernel, tk=tk),
        grid=(b, npair, s // tq),
        in_specs=[pl.BlockSpec((None, tq, 2 * LANES), lambda bb, p, i: (bb, i, p)),
                  pl.BlockSpec((None, s, 2 * LANES), lambda bb, p, i: (bb, 0, p)),
                  pl.BlockSpec((vt.shape[0] // npair, s), lambda bb, p, i: (p, bb))],
        out_specs=pl.BlockSpec((None, tq, LANES), lambda bb, p, i: (bb, i, p)),
        out_shape=jax.ShapeDtypeStruct((b, s, npair * LANES), BF16),
        scratch_shapes=_attn_scratch(tq, tk, C_DV),
        compiler_params=_cp(("parallel", "parallel", "arbitrary")),
        name="attn_latent",
    )(q, k, vt)


def _merge_kernel(oa_ref, ob_ref, oc_ref, g0_ref, g1_ref, g2_ref, x_ref, wb_ref, wo_ref, o_ref):
    z = None
    for n, (o_r, g_r) in enumerate(((oa_ref, g0_ref), (ob_ref, g1_ref), (oc_ref, g2_ref))):
        br = jnp.dot(o_r[...], wb_ref[n], preferred_element_type=F32)
        gate = 1.0 / (1.0 + jnp.exp(-g_r[...].astype(F32)))
        z = gate * br if z is None else z + gate * br
    o_ref[...] = x_ref[...] + jnp.dot(z.astype(BF16), wo_ref[...], preferred_element_type=F32)


def _merge(oa, ob, oc, proj2, x2, wb, wo, tm):
    t, d = x2.shape
    bw = oa.shape[1]
    row = lambda w: pl.BlockSpec((tm, w), lambda i: (i, 0))
    gate = lambda n: pl.BlockSpec((tm, d), lambda i: (i, COL_G * LANES // d + n))
    return pl.pallas_call(
        _merge_kernel,
        grid=(t // tm,),
        in_specs=[row(bw), row(bw), row(bw), gate(0), gate(1), gate(2), row(d),
                  pl.BlockSpec(wb.shape, lambda i: (0, 0, 0)), pl.BlockSpec(wo.shape, lambda i: (0, 0))],
        out_specs=row(d),
        out_shape=jax.ShapeDtypeStruct((t, d), F32),
        compiler_params=_cp(("parallel",)),
        name="branch_merge",
    )(oa, ob, oc, proj2, proj2, proj2, x2, wb, wo)


def _cross_kernel(x_ref, g_ref, wq_ref, kbd_ref, vbd_ref, wo_ref, o_ref, *, n_mem):
    x = x_ref[...]
    h = _rms(x, g_ref[...], 1.0 / x.shape[-1]).astype(BF16)
    q = (jnp.dot(h, wq_ref[...], preferred_element_type=F32) * (X_DH ** -0.5 * LOG2E)).astype(BF16)
    s = jnp.dot(q, kbd_ref[...], preferred_element_type=F32)
    ps = []
    for hh in range(X_HEADS):
        sh = s[:, hh * n_mem:(hh + 1) * n_mem]
        p = jnp.exp2(sh - jnp.max(sh, axis=-1, keepdims=True))
        ps.append((p / jnp.sum(p, axis=-1, keepdims=True)).astype(BF16))
    o = jnp.dot(jnp.concatenate(ps, axis=1), vbd_ref[...], preferred_element_type=F32)
    o_ref[...] = x + jnp.dot(o.astype(BF16), wo_ref[...], preferred_element_type=F32)


def _cross(x3, g, wq, kbd, vbd, wo, tm):
    b, s, d = x3.shape
    n_mem = kbd.shape[2] // X_HEADS
    full = lambda a: pl.BlockSpec(a.shape, lambda bb, i: (0, 0))
    return pl.pallas_call(
        functools.partial(_cross_kernel, n_mem=n_mem),
        grid=(b, s // tm),
        in_specs=[pl.BlockSpec((None, tm, d), lambda bb, i: (bb, i, 0)), full(g), full(wq),
                  pl.BlockSpec((None,) + kbd.shape[1:], lambda bb, i: (bb, 0, 0)),
                  pl.BlockSpec((None,) + vbd.shape[1:], lambda bb, i: (bb, 0, 0)), full(wo)],
        out_specs=pl.BlockSpec((None, tm, d), lambda bb, i: (bb, i, 0)),
        out_shape=jax.ShapeDtypeStruct((b, s, d), F32),
        compiler_params=_cp(("parallel", "parallel")),
        name="cross_attn",
    )(x3, g, wq, kbd, vbd, wo)


def _block_diag_kv(kv):
    b, m, _ = kv.shape
    kv = kv.reshape(b, m, 2, X_HEADS, X_DH)
    eye = jnp.eye(X_HEADS, dtype=kv.dtype)
    kt = kv[:, :, 0].transpose(0, 2, 3, 1)
    kbd = (kt[:, :, :, None, :] * eye[None, :, None, :, None]).reshape(b, X_HEADS * X_DH, X_HEADS * m)
    vt = kv[:, :, 1].transpose(0, 2, 1, 3)
    vbd = (vt[:, :, :, None, :] * eye[None, :, None, :, None]).reshape(b, X_HEADS * m, X_HEADS * X_DH)
    return kbd, vbd


def _top_half(x):
    return lax.bitcast_convert_type(lax.bitcast_convert_type(x, jnp.uint32) & jnp.uint32(0xFFFF0000), F32)


def _router_kernel(x_ref, g_ref, w_ref, b_ref, h_out, r_out):
    x = x_ref[...]
    h = _rms(x, g_ref[...], 1.0 / x.shape[-1])
    h_out[...] = h.astype(h_out.dtype)
    h_top = _top_half(h)
    h_hi = h_top.astype(BF16)
    h_lo = (h - h_top).astype(BF16)
    both = jnp.dot(h_hi, w_ref[...], preferred_element_type=F32)
    logits = (both[:, :LANES] + both[:, LANES:] + jnp.dot(h_lo, w_ref[:, :LANES], preferred_element_type=F32)
              + b_ref[...])
    lane = lax.broadcasted_iota(jnp.int32, logits.shape, 1)
    lane_f = lane.astype(F32)
    big = jnp.float32(4 * LANES)

    def top(vals, mask):
        mv = jnp.max(jnp.where(mask, vals, -jnp.inf), axis=-1, keepdims=True)
        idx = jnp.min(jnp.where(mask & (vals == mv), lane_f, big), axis=-1, keepdims=True)
        return mv, idx

    g_mask = lane < N_GROUPS
    g_max, g_idx = top(logits, g_mask)
    p_g = 1.0 / jnp.sum(jnp.where(g_mask, jnp.exp(logits - g_max), 0.0), axis=-1, keepdims=True)
    first = N_GROUPS + g_idx * EXPERTS_PER_GROUP
    e_mask = (lane_f >= first) & (lane_f < first + EXPERTS_PER_GROUP)
    v0, i0 = top(logits, e_mask)
    v1, i1 = top(logits, e_mask & (lane_f != i0))
    e1 = jnp.exp(v1 - v0)
    w0 = p_g / (1.0 + e1)
    w1 = p_g * e1 / (1.0 + e1)
    out = jnp.where(lane == 0, i0 - N_GROUPS, 0.0)
    out = jnp.where(lane == 1, i1 - N_GROUPS, out)
    out = jnp.where(lane == 2, w0, out)
    out = jnp.where(lane == 3, w1, out)
    r_out[...] = out


def _router(x2, g, w, bias, tm):
    t, d = x2.shape
    full = lambda a: pl.BlockSpec(a.shape, lambda i: (0, 0))
    return pl.pallas_call(
        _router_kernel,
        grid=(t // tm,),
        in_specs=[pl.BlockSpec((tm, d), lambda i: (i, 0)), full(g), full(w), full(bias)],
        out_specs=[pl.BlockSpec((tm, d), lambda i: (i, 0)), pl.BlockSpec((tm, LANES), lambda i: (i, 0))],
        out_shape=[jax.ShapeDtypeStruct((t, d), BF16), jax.ShapeDtypeStruct((t, LANES), F32)],
        compiler_params=_cp(("parallel",)),
        name="moe_router",
    )(x2, g, w, bias)


def _expert_kernel(blk_e_ref, n_used_ref, x_ref, w1_ref, w3_ref, w2_ref, o_ref, w13_s, w2_s):
    i = pl.program_id(0)
    used = i < n_used_ref[0]
    de = w1_ref.shape[1]

    @pl.when(used & ((i == 0) | (blk_e_ref[i] != blk_e_ref[jnp.maximum(i - 1, 0)])))
    def _():
        w13_s[:, :de] = w1_ref[...].astype(BF16)
        w13_s[:, de:] = w3_ref[...].astype(BF16)
        w2_s[...] = w2_ref[...].astype(BF16)

    @pl.when(used)
    def _():
        hid = jnp.dot(x_ref[...], w13_s[...], preferred_element_type=F32)
        a = hid[:, :de]
        act = (a / (1.0 + jnp.exp(-a))) * hid[:, de:]
        o_ref[...] = jnp.dot(act.astype(BF16), w2_s[...], preferred_element_type=F32).astype(o_ref.dtype)

    @pl.when(jnp.logical_not(used))
    def _():
        o_ref[...] = jnp.zeros(o_ref.shape, o_ref.dtype)


def _experts(blk_e, n_used, xr, w1, w3, w2, layer, rows_per_block):
    rows, d = xr.shape
    de = w1.shape[3]
    weight = lambda w: pl.BlockSpec((None, None) + w.shape[2:], lambda i, be, nu: (layer, be[i], 0, 0))
    grid_spec = pltpu.PrefetchScalarGridSpec(
        num_scalar_prefetch=2,
        grid=(rows // rows_per_block,),
        in_specs=[pl.BlockSpec((rows_per_block, d), lambda i, be, nu: (i, 0)), weight(w1), weight(w3), weight(w2)],
        out_specs=pl.BlockSpec((rows_per_block, d), lambda i, be, nu: (i, 0)),
        scratch_shapes=[pltpu.VMEM((d, 2 * de), BF16), pltpu.VMEM((de, d), BF16)],
    )
    return pl.pallas_call(
        _expert_kernel,
        grid_spec=grid_spec,
        out_shape=jax.ShapeDtypeStruct((rows, d), BF16),
        compiler_params=_cp(("arbitrary",)),
        name="moe_experts",
    )(blk_e, n_used, xr, w1, w3, w2)


def _combine_kernel(x_ref, y0_ref, y1_ref, r_ref, g_ref, o_ref, *, final_norm):
    r = r_ref[...]
    y = x_ref[...] + r[:, 2:3] * y0_ref[...].astype(F32) + r[:, 3:4] * y1_ref[...].astype(F32)
    if final_norm:
        y = _rms(y, g_ref[...], 1.0 / y.shape[-1])
    o_ref[...] = y


def _combine(x2, y0, y1, route, g, final_norm, tm):
    t, d = x2.shape
    row = lambda w: pl.BlockSpec((tm, w), lambda i: (i, 0))
    return pl.pallas_call(
        functools.partial(_combine_kernel, final_norm=final_norm),
        grid=(t // tm,),
        in_specs=[row(d), row(d), row(d), row(LANES), pl.BlockSpec((1, d), lambda i: (0, 0))],
        out_specs=row(d),
        out_shape=jax.ShapeDtypeStruct((t, d), F32),
        compiler_params=_cp(("parallel",)),
        name="moe_combine",
    )(x2, y0, y1, route, g)


def _dispatch(route, rows_per_block):
    t = route.shape[0]
    eid = route[:, :TOP_K].astype(jnp.int32).reshape(-1)
    n = eid.shape[0]
    order = jnp.argsort(eid).astype(jnp.int32)
    rank = jnp.argsort(order).astype(jnp.int32)
    experts = jnp.arange(N_EXPERTS, dtype=jnp.int32)
    counts = jnp.sum((eid[:, None] == experts[None, :]).astype(jnp.int32), axis=0)
    start = jnp.cumsum(counts) - counts
    padded = (counts + rows_per_block - 1) // rows_per_block * rows_per_block
    pend = jnp.cumsum(padded)
    pstart = pend - padded
    dest = (rank + (pstart - start)[eid]).reshape(t, TOP_K)
    n_blocks = n // rows_per_block + N_EXPERTS
    blk_first = jnp.arange(n_blocks, dtype=jnp.int32) * rows_per_block
    blk_e = jnp.minimum(jnp.sum((pend[None, :] <= blk_first[:, None]).astype(jnp.int32), axis=1), N_EXPERTS - 1)
    off = (blk_first - pstart[blk_e])[:, None] + jnp.arange(rows_per_block, dtype=jnp.int32)[None, :]
    src = jnp.clip(start[blk_e][:, None] + off, 0, n - 1)
    row_tok = jnp.where(off < counts[blk_e][:, None], order[src] // TOP_K, 0).reshape(-1)
    n_used = (pend[-1] // rows_per_block).astype(jnp.int32).reshape(1)
    return row_tok, dest, blk_e, n_used


def _rot_cols(w):
    half = w.shape[-1] // 2
    return jnp.concatenate([-w[..., half:], w[..., :half]], axis=-1)


def _pack_w_in(w):
    d = w.shape[0]
    blk = A_HEADS * 2 * A_DH
    n_ab = 6 * blk
    cq = w[:, n_ab:n_ab + C_Q_RANK]
    ckv = w[:, n_ab + C_Q_RANK:n_ab + C_Q_RANK + C_KV_RANK]
    ckr = w[:, n_ab + C_Q_RANK + C_KV_RANK:n_ab + C_Q_RANK + C_KV_RANK + C_ROPE]
    gates = w[:, n_ab + C_Q_RANK + C_KV_RANK + C_ROPE:]
    z = lambda n: jnp.zeros((d, n), w.dtype)
    tail = LANES - C_NOPE - C_ROPE
    packed = jnp.concatenate([w[:, :2 * blk], w[:, 3 * blk:5 * blk], cq, z(LANES), ckv,
                              z(C_NOPE), ckr, z(tail), z(C_NOPE), _rot_cols(ckr), z(tail), gates], axis=1)
    return packed.astype(BF16), w[:, 2 * blk:3 * blk], w[:, 5 * blk:6 * blk]


def _pack_w_uq(w):
    wq = w.reshape(C_Q_RANK, C_HEADS, C_NOPE + C_ROPE)
    pad_rows = 4 * LANES - C_Q_RANK
    tail = LANES - C_NOPE - C_ROPE
    q1 = jnp.pad(wq, ((0, pad_rows), (0, 0), (0, tail))).reshape(4 * LANES, C_HEADS * LANES)
    q2 = jnp.pad(_rot_cols(wq[:, :, C_NOPE:]), ((0, pad_rows), (0, 0), (C_NOPE, tail))).reshape(4 * LANES, C_HEADS * LANES)
    return q1.astype(BF16), q2.astype(BF16)


def _pack_w_ukv(w):
    wkv = w.reshape(C_KV_RANK, C_HEADS, C_NOPE + C_DV)
    wk = jnp.pad(wkv[:, :, :C_NOPE], ((0, 0), (0, 0), (0, LANES - C_NOPE))).reshape(C_KV_RANK, C_HEADS * LANES)
    wv = wkv[:, :, C_NOPE:].reshape(C_KV_RANK, C_HEADS * C_DV)
    return wk.astype(BF16), wv


def kernel(x, mem, positions, mix_norm_g, w_in, diff_lambda, diff_subln_g, mla_q_norm_g, w_uq, mla_kv_norm_g, w_ukv, w_branch, w_out, cross_norm_g, mem_norm_g, w_xq, w_xkv, w_xo, ffn_norm_g, w_group, b_group, w_router, b_router, w1, w3, w2, final_norm_g):
    b, s, d = x.shape
    depth = w_in.shape[0]
    t = b * s
    n_mem = mem.shape[1]
    tl = _tiles(s)
    assert PROJ_COLS == COL_G * LANES + N_BRANCHES * d and s % tl["t_b"] == 0 and tl["t_b"] == tl["tq_a"]

    pos_f = positions.astype(F32)
    posq_row = pos_f.reshape(b, 1, s)
    posk_rep = jnp.broadcast_to(pos_f[:, :, None], (b, s, LANES))
    pos_bounds = _position_bounds(pos_f, tl["tq_a"])
    cos, sin = _rope_tables(pos_f.reshape(t, 1), tl["tm_tok"])
    slopes_a2 = jnp.exp2(-8.0 * jnp.arange(1, A_HEADS + 1, dtype=F32) / A_HEADS) * LOG2E
    reach = max((w // (2 * dl)) * dl for w, dl in B_PATTERNS)
    b_bias = _b_bias_tables(tl["t_b"], 2 * (-(-reach // tl["t_b"])) + 1)
    mem2 = mem.reshape(b * n_mem, d)

    x2 = x.reshape(t, d)
    for l in range(depth):
        w_main, w_av, w_bv = _pack_w_in(w_in[l])
        proj2, vt_a, vt_b = _in_proj(x2, mix_norm_g[l], w_main, *_vt_weights(w_av, 2 * A_DH), *_vt_weights(w_bv, B_DH),
                                     tl["tm_proj"], tl["tn_proj"])
        proj3 = proj2.reshape(b, s, PROJ_COLS)
        lq = diff_lambda[l].astype(F32)
        lam_init = 0.8 - 0.6 * math.exp(-0.3 * l)
        lam = (jnp.exp(jnp.sum(lq[0] * lq[1])) - jnp.exp(jnp.sum(lq[2] * lq[3])) + lam_init).reshape(1)
        norms = _block_norms(proj3, tl["t_b"])
        oa = _attn_a(proj3, vt_a, norms, pos_bounds, posq_row, posk_rep, lam, slopes_a2, diff_subln_g[l], 1.0 - lam_init,
                     tl["tq_a"], tl["tk_a"])
        ob = _attn_b(proj3, vt_b, norms, b_bias, tl["t_b"])
        wq1, wq2 = _pack_w_uq(w_uq[l])
        wk, wv = _pack_w_ukv(w_ukv[l])
        gq = jnp.pad(mla_q_norm_g[l], (0, 4 * LANES - C_Q_RANK)).reshape(1, 4 * LANES)
        qc, kc, vt_c = _mla_prep(proj2, cos, sin, gq, mla_kv_norm_g[l].reshape(1, C_KV_RANK),
                                 wq1, wq2, wk, *_vt_weights(wv, C_DV), tl["tm_tok"])
        oc = _attn_c(qc.reshape(b, s, -1), kc.reshape(b, s, -1), vt_c, tl["tq_c"], tl["tk_c"])
        x2 = _merge(oa.reshape(t, -1), ob.reshape(t, -1), oc.reshape(t, -1), proj2, x2,
                    w_branch[l].astype(BF16), w_out[l].astype(BF16), tl["tm_tok"])
        kv = _norm_matmul(mem2, mem_norm_g[l], w_xkv[l].astype(BF16), min(1024, b * n_mem), w_xkv.shape[2])
        kbd, vbd = _block_diag_kv(kv.reshape(b, n_mem, -1))
        x2 = _cross(x2.reshape(b, s, d), cross_norm_g[l].reshape(1, d), w_xq[l].astype(BF16), kbd, vbd,
                    w_xo[l].astype(BF16), tl["tm_tok"]).reshape(t, d)
        w_r = jnp.pad(jnp.concatenate([w_group[l], w_router[l]], axis=1), ((0, 0), (0, LANES - N_GROUPS - N_EXPERTS)))
        b_r = jnp.pad(jnp.concatenate([b_group[l], b_router[l]]), (0, LANES - N_GROUPS - N_EXPERTS)).reshape(1, LANES)
        w_r2 = jnp.concatenate([_top_half(w_r).astype(BF16), (w_r - _top_half(w_r)).astype(BF16)], axis=1)
        h, route = _router(x2, ffn_norm_g[l].reshape(1, d), w_r2, b_r, tl["tm_moe"])
        row_tok, dest, blk_e, n_used = _dispatch(route, tl["moe_rows"])
        yr = _experts(blk_e, n_used, h[row_tok], w1, w3, w2, l, tl["moe_rows"])
        x2 = _combine(x2, yr[dest[:, 0]], yr[dest[:, 1]], route, final_norm_g.reshape(1, d),
                      l == depth - 1, tl["tm_moe"])
    return x2.reshape(b, s, d)
```
